```python
import jax, jax.numpy as jnp
from jax import lax
import numpy as np

D_MODEL = 2048
BATCH = 8
SEQ = 8192
DEPTH = 4

N_META = 16
FRONT = 128
N_PAD = FRONT - N_META

N_A_LAYERS = DEPTH // 2
N_B_LAYERS = DEPTH - N_A_LAYERS

ALPHA = (2.0 * DEPTH) ** 0.25
BETA = (8.0 * DEPTH) ** -0.25
LN_EPS = 1e-5

GLA_HEADS = 4
GLA_DK = D_MODEL // 2
GLA_DV = D_MODEL
GLA_HK = GLA_DK // GLA_HEADS
GLA_HV = GLA_DV // GLA_HEADS
GLA_RANK = 16
GLA_TAU = 16.0
GLA_CHUNK = 64
GLA_IN = 2 * GLA_DK + GLA_DV + GLA_RANK + GLA_DV

FOX_HEADS = 16
FOX_HD = D_MODEL // FOX_HEADS
FOX_BLOCK = 128
KV_OUT = 2 * D_MODEL + FOX_HEADS

D_FF = 5632
CONV_W = 3

kernel_name = "gla_fox_yoco_deepnorm_hybrid"


def layer_norm(x, g, b):
    xf = x.astype(jnp.float32)
    mu = xf.mean(-1, keepdims=True)
    var = jnp.square(xf - mu).mean(-1, keepdims=True)
    return ((xf - mu) * lax.rsqrt(var + LN_EPS) * g + b).astype(x.dtype)


def rms_norm(x, g):
    xf = x.astype(jnp.float32)
    y = xf * lax.rsqrt(jnp.mean(jnp.square(xf), -1, keepdims=True) + LN_EPS)
    return (y * g).astype(x.dtype)


def gla_mixer(x, valid, w_in, w_g2, b_g2, norm_g, w_out):
    B, L, _ = x.shape
    n_chunks = L // GLA_CHUNK
    proj = x @ w_in
    q, k, v, g_low, r = jnp.split(
        proj, [GLA_DK, 2 * GLA_DK, 2 * GLA_DK + GLA_DV, 2 * GLA_DK + GLA_DV + GLA_RANK], axis=-1)
    m = valid[None, :, None]
    log_a = jax.nn.log_sigmoid((g_low @ w_g2 + b_g2).astype(jnp.float32)) / GLA_TAU
    log_a = jnp.where(m, log_a, 0.0)
    k = jnp.where(m, k, 0)
    q = q * (GLA_HK ** -0.5)

    def to_chunks(t, hd):
        return t.astype(jnp.float32).reshape(B, n_chunks, GLA_CHUNK, GLA_HEADS, hd).transpose(1, 0, 3, 2, 4)

    qc, kc, gc = to_chunks(q, GLA_HK), to_chunks(k, GLA_HK), to_chunks(log_a, GLA_HK)
    vc = to_chunks(v, GLA_HV)
    causal = jnp.tril(jnp.ones((GLA_CHUNK, GLA_CHUNK), dtype=bool))

    def step(S, inp):
        q_c, k_c, v_c, g_c = inp
        b = jnp.cumsum(g_c, axis=2)
        b_last = b[:, :, -1:, :]
        inter = jnp.einsum('bhck,bhkv->bhcv', q_c * jnp.exp(b), S)
        diff = b[:, :, :, None, :] - b[:, :, None, :, :]
        decay = jnp.exp(jnp.where(causal[:, :, None], diff, -jnp.inf))
        att = jnp.einsum('bhik,bhjk,bhijk->bhij', q_c, k_c, decay)
        intra = jnp.einsum('bhij,bhjv->bhiv', att, v_c)
        S_new = (jnp.exp(b_last[:, :, 0, :])[..., None] * S
                 + jnp.einsum('bhck,bhcv->bhkv', k_c * jnp.exp(b_last - b), v_c))
        return S_new, inter + intra

    S0 = jnp.zeros((B, GLA_HEADS, GLA_HK, GLA_HV), jnp.float32)
    _, o = lax.scan(step, S0, (qc, kc, vc, gc))
    o = o.transpose(1, 0, 3, 2, 4).reshape(B, L, GLA_HEADS, GLA_HV).astype(x.dtype)
    o = rms_norm(o, norm_g).reshape(B, L, GLA_DV)
    o = o * jax.nn.silu(r)
    return o @ w_out


def shared_kv(x, valid, kv_w, kv_bf):
    B, L, _ = x.shape
    k, v, f_logit = jnp.split(x @ kv_w, [D_MODEL, 2 * D_MODEL], axis=-1)
    log_f = jax.nn.log_sigmoid((f_logit + kv_bf).astype(jnp.float32))
    log_f = jnp.where(valid[None, :, None], log_f, 0.0)
    c = jnp.cumsum(log_f, axis=1).transpose(0, 2, 1)
    kh = k.reshape(B, L, FOX_HEADS, FOX_HD).transpose(0, 2, 1, 3)
    vh = v.reshape(B, L, FOX_HEADS, FOX_HD).transpose(0, 2, 1, 3)
    return kh, vh, c


def fox_mixer(x, valid, kh, vh, c, w_in, w_out):
    B, L, _ = x.shape
    n_blocks = L // FOX_BLOCK
    q, og = jnp.split(x @ w_in, [D_MODEL], axis=-1)
    q = q.reshape(B, L, FOX_HEADS, FOX_HD).transpose(0, 2, 1, 3) * (FOX_HD ** -0.5)
    qb = q.reshape(B, FOX_HEADS, n_blocks, FOX_BLOCK, FOX_HD).transpose(2, 0, 1, 3, 4)
    cb = c.reshape(B, FOX_HEADS, n_blocks, FOX_BLOCK).transpose(2, 0, 1, 3)
    pos = jnp.arange(L)

    def block(args):
        i, q_i, c_i = args
        t = i * FOX_BLOCK + jnp.arange(FOX_BLOCK)
        logits = (jnp.einsum('bhqd,bhkd->bhqk', q_i, kh).astype(jnp.float32)
                  + c_i[..., None] - c[:, :, None, :])
        mask = (pos[None, :] <= t[:, None]) & (valid[None, :] | (pos[None, :] == t[:, None]))
        p = jax.nn.softmax(jnp.where(mask, logits, -jnp.inf), axis=-1).astype(vh.dtype)
        return jnp.einsum('bhqk,bhkd->bhqd', p, vh)

    o = lax.map(block, (jnp.arange(n_blocks), qb, cb))
    o = o.transpose(1, 0, 3, 2, 4).reshape(B, L, D_MODEL)
    o = o * jax.nn.sigmoid(og)
    return o @ w_out


def conv_ffn(x, valid, w_up, conv_w, conv_b, w_down):
    h = x @ w_up
    h = jnp.where(valid[None, :, None], h, 0)
    h = lax.conv_general_dilated(
        h, conv_w[:, None, :], window_strides=(1,), padding=[(CONV_W - 1, 0)],
        dimension_numbers=('NWC', 'WIO', 'NWC'), feature_group_count=2 * D_FF) + conv_b
    u, g = jnp.split(h, 2, axis=-1)
    return (jax.nn.silu(g) * u) @ w_down


def _fwd_setup_inputs(seed: int = 0) -> dict:
    key = jax.random.key(seed)
    ks = jax.random.split(key, 20)

    def nrm(k, shape, fan_in, scale=1.0):
        return jax.random.normal(k, shape, jnp.float32) * (fan_in ** -0.5) * scale

    return {
        "x": jax.random.normal(ks[0], (BATCH, SEQ, D_MODEL), jnp.float32),
        "meta": jax.random.normal(ks[1], (N_META, D_MODEL), jnp.float32),
        "ln_g": 1.0 + 0.02 * jax.random.normal(ks[2], (DEPTH, 2, D_MODEL), jnp.float32),
        "ln_b": 0.02 * jax.random.normal(ks[3], (DEPTH, 2, D_MODEL), jnp.float32),
        "gla_w_in": nrm(ks[4], (N_A_LAYERS, D_MODEL, GLA_IN), D_MODEL),
        "gla_w_g2": nrm(ks[5], (N_A_LAYERS, GLA_RANK, GLA_DK), GLA_RANK),
        "gla_b_g2": 0.1 * jax.random.normal(ks[6], (N_A_LAYERS, GLA_DK), jnp.float32),
        "gla_norm_g": 1.0 + 0.02 * jax.random.normal(ks[7], (N_A_LAYERS, GLA_HV), jnp.float32),
        "gla_w_out": nrm(ks[8], (N_A_LAYERS, GLA_DV, D_MODEL), GLA_DV, BETA),
        "kv_w": nrm(ks[9], (D_MODEL, KV_OUT), D_MODEL),
        "kv_bf": 2.0 + 0.5 * jax.random.normal(ks[10], (FOX_HEADS,), jnp.float32),
        "fox_w_in": nrm(ks[11], (N_B_LAYERS, D_MODEL, 2 * D_MODEL), D_MODEL),
        "fox_w_out": nrm(ks[12], (N_B_LAYERS, D_MODEL, D_MODEL), D_MODEL, BETA),
        "ffn_w_up": nrm(ks[13], (DEPTH, D_MODEL, 2 * D_FF), D_MODEL),
        "ffn_conv_w": nrm(ks[14], (DEPTH, CONV_W, 2 * D_FF), CONV_W),
        "ffn_conv_b": 0.02 * jax.random.normal(ks[15], (DEPTH, 2 * D_FF), jnp.float32),
        "ffn_w_down": nrm(ks[16], (DEPTH, D_FF, D_MODEL), D_FF, BETA),
    }


def _fwd_reference(x, meta, ln_g, ln_b, gla_w_in, gla_w_g2, gla_b_g2, gla_norm_g, gla_w_out,
              kv_w, kv_bf, fox_w_in, fox_w_out, ffn_w_up, ffn_conv_w, ffn_conv_b, ffn_w_down):
    B = x.shape[0]
    L = FRONT + x.shape[1]
    pad = jnp.zeros((B, N_PAD, D_MODEL), x.dtype)
    meta_b = jnp.broadcast_to(meta[None].astype(x.dtype), (B, N_META, D_MODEL))
    h = jnp.concatenate([pad, meta_b, x], axis=1)
    valid = jnp.arange(L) >= N_PAD

    kh = vh = c = None
    for l in range(DEPTH):
        if l < N_A_LAYERS:
            mix = gla_mixer(h, valid, gla_w_in[l], gla_w_g2[l], gla_b_g2[l], gla_norm_g[l], gla_w_out[l])
        else:
            if l == N_A_LAYERS:
                kh, vh, c = shared_kv(h, valid, kv_w, kv_bf)
            j = l - N_A_LAYERS
            mix = fox_mixer(h, valid, kh, vh, c, fox_w_in[j], fox_w_out[j])
        h = layer_norm(ALPHA * h + mix, ln_g[l, 0], ln_b[l, 0])
        ffn = conv_ffn(h, valid, ffn_w_up[l], ffn_conv_w[l], ffn_conv_b[l], ffn_w_down[l])
        h = layer_norm(ALPHA * h + ffn, ln_g[l, 1], ln_b[l, 1])
    return h[:, FRONT:]


import jax as _jax
import jax.numpy as _jnp

TWIN_FORMAT = 'train_step'
FWD_PARAMS = ['x', 'meta', 'ln_g', 'ln_b', 'gla_w_in', 'gla_w_g2', 'gla_b_g2', 'gla_norm_g', 'gla_w_out', 'kv_w', 'kv_bf', 'fox_w_in', 'fox_w_out', 'ffn_w_up', 'ffn_conv_w', 'ffn_conv_b', 'ffn_w_down']
TWIN_WEIGHTS = ['meta', 'ln_g', 'ln_b', 'gla_w_in', 'gla_w_g2', 'gla_b_g2', 'gla_norm_g', 'gla_w_out', 'kv_w', 'kv_bf', 'fox_w_in', 'fox_w_out', 'ffn_w_up', 'ffn_conv_w', 'ffn_conv_b', 'ffn_w_down']
TWIN_DIFF_INPUT = 'x'
TWIN_INPUTS = ['x', 'meta', 'ln_g', 'ln_b', 'gla_w_in', 'gla_w_g2', 'gla_b_g2', 'gla_norm_g', 'gla_w_out', 'kv_w', 'kv_bf', 'fox_w_in', 'fox_w_out', 'ffn_w_up', 'ffn_conv_w', 'ffn_conv_b', 'ffn_w_down', 'loss_target', 'm_meta', 'm_ln_g', 'm_ln_b', 'm_gla_w_in', 'm_gla_w_g2', 'm_gla_b_g2', 'm_gla_norm_g', 'm_gla_w_out', 'm_kv_w', 'm_kv_bf', 'm_fox_w_in', 'm_fox_w_out', 'm_ffn_w_up', 'm_ffn_conv_w', 'm_ffn_conv_b', 'm_ffn_w_down', 'v_meta', 'v_ln_g', 'v_ln_b', 'v_gla_w_in', 'v_gla_w_g2', 'v_gla_b_g2', 'v_gla_norm_g', 'v_gla_w_out', 'v_kv_w', 'v_kv_bf', 'v_fox_w_in', 'v_fox_w_out', 'v_ffn_w_up', 'v_ffn_conv_w', 'v_ffn_conv_b', 'v_ffn_w_down']
TWIN_OUTPUTS = ['loss', 'grad_x', 'grad_meta', 'grad_ln_g', 'grad_ln_b', 'grad_gla_w_in', 'grad_gla_w_g2', 'grad_gla_b_g2', 'grad_gla_norm_g', 'grad_gla_w_out', 'grad_kv_w', 'grad_kv_bf', 'grad_fox_w_in', 'grad_fox_w_out', 'grad_ffn_w_up', 'grad_ffn_conv_w', 'grad_ffn_conv_b', 'grad_ffn_w_down', 'delta_meta', 'delta_ln_g', 'delta_ln_b', 'delta_gla_w_in', 'delta_gla_w_g2', 'delta_gla_b_g2', 'delta_gla_norm_g', 'delta_gla_w_out', 'delta_kv_w', 'delta_kv_bf', 'delta_fox_w_in', 'delta_fox_w_out', 'delta_ffn_w_up', 'delta_ffn_conv_w', 'delta_ffn_conv_b', 'delta_ffn_w_down', 'new_m_meta', 'new_m_ln_g', 'new_m_ln_b', 'new_m_gla_w_in', 'new_m_gla_w_g2', 'new_m_gla_b_g2', 'new_m_gla_norm_g', 'new_m_gla_w_out', 'new_m_kv_w', 'new_m_kv_bf', 'new_m_fox_w_in', 'new_m_fox_w_out', 'new_m_ffn_w_up', 'new_m_ffn_conv_w', 'new_m_ffn_conv_b', 'new_m_ffn_w_down', 'new_v_meta', 'new_v_ln_g', 'new_v_ln_b', 'new_v_gla_w_in', 'new_v_gla_w_g2', 'new_v_gla_b_g2', 'new_v_gla_norm_g', 'new_v_gla_w_out', 'new_v_kv_w', 'new_v_kv_bf', 'new_v_fox_w_in', 'new_v_fox_w_out', 'new_v_ffn_w_up', 'new_v_ffn_conv_w', 'new_v_ffn_conv_b', 'new_v_ffn_w_down']
TWIN_LEAF_KINDS = {'loss': 'loss', 'grad_x': 'grad_x', 'grad_meta': 'grad_w', 'grad_ln_g': 'grad_w', 'grad_ln_b': 'grad_w', 'grad_gla_w_in': 'grad_w', 'grad_gla_w_g2': 'grad_w', 'grad_gla_b_g2': 'grad_w', 'grad_gla_norm_g': 'grad_w', 'grad_gla_w_out': 'grad_w', 'grad_kv_w': 'grad_w', 'grad_kv_bf': 'grad_w', 'grad_fox_w_in': 'grad_w', 'grad_fox_w_out': 'grad_w', 'grad_ffn_w_up': 'grad_w', 'grad_ffn_conv_w': 'grad_w', 'grad_ffn_conv_b': 'grad_w', 'grad_ffn_w_down': 'grad_w', 'delta_meta': 'delta_w', 'delta_ln_g': 'delta_w', 'delta_ln_b': 'delta_w', 'delta_gla_w_in': 'delta_w', 'delta_gla_w_g2': 'delta_w', 'delta_gla_b_g2': 'delta_w', 'delta_gla_norm_g': 'delta_w', 'delta_gla_w_out': 'delta_w', 'delta_kv_w': 'delta_w', 'delta_kv_bf': 'delta_w', 'delta_fox_w_in': 'delta_w', 'delta_fox_w_out': 'delta_w', 'delta_ffn_w_up': 'delta_w', 'delta_ffn_conv_w': 'delta_w', 'delta_ffn_conv_b': 'delta_w', 'delta_ffn_w_down': 'delta_w', 'new_m_meta': 'new_m', 'new_m_ln_g': 'new_m', 'new_m_ln_b': 'new_m', 'new_m_gla_w_in': 'new_m', 'new_m_gla_w_g2': 'new_m', 'new_m_gla_b_g2': 'new_m', 'new_m_gla_norm_g': 'new_m', 'new_m_gla_w_out': 'new_m', 'new_m_kv_w': 'new_m', 'new_m_kv_bf': 'new_m', 'new_m_fox_w_in': 'new_m', 'new_m_fox_w_out': 'new_m', 'new_m_ffn_w_up': 'new_m', 'new_m_ffn_conv_w': 'new_m', 'new_m_ffn_conv_b': 'new_m', 'new_m_ffn_w_down': 'new_m', 'new_v_meta': 'new_v', 'new_v_ln_g': 'new_v', 'new_v_ln_b': 'new_v', 'new_v_gla_w_in': 'new_v', 'new_v_gla_w_g2': 'new_v', 'new_v_gla_b_g2': 'new_v', 'new_v_gla_norm_g': 'new_v', 'new_v_gla_w_out': 'new_v', 'new_v_kv_w': 'new_v', 'new_v_kv_bf': 'new_v', 'new_v_fox_w_in': 'new_v', 'new_v_fox_w_out': 'new_v', 'new_v_ffn_w_up': 'new_v', 'new_v_ffn_conv_w': 'new_v', 'new_v_ffn_conv_b': 'new_v', 'new_v_ffn_w_down': 'new_v'}


def _forward(args):
    return _fwd_reference(*[args[k] for k in FWD_PARAMS])


def _output_shape():
    def fwd():
        inp = _fwd_setup_inputs(0)
        return _fwd_reference(*[inp[k] for k in FWD_PARAMS])
    out = _jax.eval_shape(fwd)
    return out.shape, out.dtype

N_MICROBATCH = 1
ADAM_LR = 0.001
ADAM_B1 = 0.9
ADAM_B2 = 0.999
ADAM_EPS = 1e-08
ADAM_WD = 0.01
ADAM_STEP = 10
PER_EXAMPLE_BATCH_AXIS = {'x': 0, 'loss_target': 0}
SHARED_INPUTS = []
_WEIGHT_DTYPES = {'meta': _jnp.float32, 'ln_g': _jnp.float32, 'ln_b': _jnp.float32, 'gla_w_in': _jnp.float32, 'gla_w_g2': _jnp.float32, 'gla_b_g2': _jnp.float32, 'gla_norm_g': _jnp.float32, 'gla_w_out': _jnp.float32, 'kv_w': _jnp.float32, 'kv_bf': _jnp.float32, 'fox_w_in': _jnp.float32, 'fox_w_out': _jnp.float32, 'ffn_w_up': _jnp.float32, 'ffn_conv_w': _jnp.float32, 'ffn_conv_b': _jnp.float32, 'ffn_w_down': _jnp.float32}
MOMENT_SCALE = {'meta': 2.144305e-03, 'ln_g': 1.138930e+01, 'ln_b': 6.854482e-01, 'gla_w_in': 2.319525e-02, 'gla_w_g2': 3.121034e-03, 'gla_b_g2': 1.317688e-02, 'gla_norm_g': 4.199025e-02, 'gla_w_out': 4.665209e-02, 'kv_w': 9.013259e-03, 'kv_bf': 5.473351e-02, 'fox_w_in': 4.230870e-03, 'fox_w_out': 1.692578e-02, 'ffn_w_up': 1.196416e-02, 'ffn_conv_w': 1.194772e-02, 'ffn_conv_b': 1.452249e-02, 'ffn_w_down': 4.650821e-02}


def _to_microbatches(a, axis):
    t = _jnp.moveaxis(a, axis, 0)
    t = t.reshape((N_MICROBATCH, t.shape[0] // N_MICROBATCH) + t.shape[1:])
    return _jnp.moveaxis(t, 1, axis + 1)


def setup_inputs(seed: int = 0) -> dict:
    inp = _fwd_setup_inputs(seed)
    key = _jax.random.fold_in(_jax.random.key(seed), 7919)
    shape, _ = _output_shape()
    out = dict(inp)
    out["loss_target"] = _jax.random.normal(_jax.random.fold_in(key, 0), shape, _jnp.float32)
    for i, name in enumerate(TWIN_WEIGHTS):
        w = inp[name].astype(_jnp.float32)
        if MOMENT_SCALE is None:
            s = _jnp.sqrt(_jnp.mean(_jnp.square(w)) + 1e-30)
        else:
            s = MOMENT_SCALE[name]
        km, kv = _jax.random.split(_jax.random.fold_in(key, i + 1))
        out[name] = w
        out["m_" + name] = s * _jax.random.normal(km, w.shape, _jnp.float32)
        out["v_" + name] = (s * s) * _jax.random.uniform(kv, w.shape, _jnp.float32, 0.5, 1.5)
    if N_MICROBATCH > 1:
        for name, axis in PER_EXAMPLE_BATCH_AXIS.items():
            out[name] = _to_microbatches(out[name], axis)
    return {'x': out['x'], 'meta': out['meta'], 'ln_g': out['ln_g'], 'ln_b': out['ln_b'], 'gla_w_in': out['gla_w_in'], 'gla_w_g2': out['gla_w_g2'], 'gla_b_g2': out['gla_b_g2'], 'gla_norm_g': out['gla_norm_g'], 'gla_w_out': out['gla_w_out'], 'kv_w': out['kv_w'], 'kv_bf': out['kv_bf'], 'fox_w_in': out['fox_w_in'], 'fox_w_out': out['fox_w_out'], 'ffn_w_up': out['ffn_w_up'], 'ffn_conv_w': out['ffn_conv_w'], 'ffn_conv_b': out['ffn_conv_b'], 'ffn_w_down': out['ffn_w_down'], 'loss_target': out['loss_target'], 'm_meta': out['m_meta'], 'm_ln_g': out['m_ln_g'], 'm_ln_b': out['m_ln_b'], 'm_gla_w_in': out['m_gla_w_in'], 'm_gla_w_g2': out['m_gla_w_g2'], 'm_gla_b_g2': out['m_gla_b_g2'], 'm_gla_norm_g': out['m_gla_norm_g'], 'm_gla_w_out': out['m_gla_w_out'], 'm_kv_w': out['m_kv_w'], 'm_kv_bf': out['m_kv_bf'], 'm_fox_w_in': out['m_fox_w_in'], 'm_fox_w_out': out['m_fox_w_out'], 'm_ffn_w_up': out['m_ffn_w_up'], 'm_ffn_conv_w': out['m_ffn_conv_w'], 'm_ffn_conv_b': out['m_ffn_conv_b'], 'm_ffn_w_down': out['m_ffn_w_down'], 'v_meta': out['v_meta'], 'v_ln_g': out['v_ln_g'], 'v_ln_b': out['v_ln_b'], 'v_gla_w_in': out['v_gla_w_in'], 'v_gla_w_g2': out['v_gla_w_g2'], 'v_gla_b_g2': out['v_gla_b_g2'], 'v_gla_norm_g': out['v_gla_norm_g'], 'v_gla_w_out': out['v_gla_w_out'], 'v_kv_w': out['v_kv_w'], 'v_kv_bf': out['v_kv_bf'], 'v_fox_w_in': out['v_fox_w_in'], 'v_fox_w_out': out['v_fox_w_out'], 'v_ffn_w_up': out['v_ffn_w_up'], 'v_ffn_conv_w': out['v_ffn_conv_w'], 'v_ffn_conv_b': out['v_ffn_conv_b'], 'v_ffn_w_down': out['v_ffn_w_down']}


def _loss(weights, diff, rest, loss_target):
    with _jax.named_scope("forward"):
        args = {**rest, TWIN_DIFF_INPUT: diff, **{k: w.astype(_WEIGHT_DTYPES[k]) for k, w in weights.items()}}
        y = _forward(args)
    with _jax.named_scope("loss_head"):
        err = _jnp.square(y.astype(_jnp.float32) - loss_target)
        return 0.5 * _jnp.sum(_jnp.mean(err, axis=-1)) if err.ndim else 0.5 * err


def _adamw(w, g, m, v):
    m = ADAM_B1 * m + (1.0 - ADAM_B1) * g
    v = ADAM_B2 * v + (1.0 - ADAM_B2) * _jnp.square(g)
    m_hat = m / (1.0 - ADAM_B1 ** ADAM_STEP)
    v_hat = v / (1.0 - ADAM_B2 ** ADAM_STEP)
    delta = -ADAM_LR * (m_hat / (_jnp.sqrt(v_hat) + ADAM_EPS) + ADAM_WD * w)
    return delta, m, v


def reference(x, meta, ln_g, ln_b, gla_w_in, gla_w_g2, gla_b_g2, gla_norm_g, gla_w_out, kv_w, kv_bf, fox_w_in, fox_w_out, ffn_w_up, ffn_conv_w, ffn_conv_b, ffn_w_down, loss_target, m_meta, m_ln_g, m_ln_b, m_gla_w_in, m_gla_w_g2, m_gla_b_g2, m_gla_norm_g, m_gla_w_out, m_kv_w, m_kv_bf, m_fox_w_in, m_fox_w_out, m_ffn_w_up, m_ffn_conv_w, m_ffn_conv_b, m_ffn_w_down, v_meta, v_ln_g, v_ln_b, v_gla_w_in, v_gla_w_g2, v_gla_b_g2, v_gla_norm_g, v_gla_w_out, v_kv_w, v_kv_bf, v_fox_w_in, v_fox_w_out, v_ffn_w_up, v_ffn_conv_w, v_ffn_conv_b, v_ffn_w_down):
    given = dict(x=x, meta=meta, ln_g=ln_g, ln_b=ln_b, gla_w_in=gla_w_in, gla_w_g2=gla_w_g2, gla_b_g2=gla_b_g2, gla_norm_g=gla_norm_g, gla_w_out=gla_w_out, kv_w=kv_w, kv_bf=kv_bf, fox_w_in=fox_w_in, fox_w_out=fox_w_out, ffn_w_up=ffn_w_up, ffn_conv_w=ffn_conv_w, ffn_conv_b=ffn_conv_b, ffn_w_down=ffn_w_down, loss_target=loss_target, m_meta=m_meta, m_ln_g=m_ln_g, m_ln_b=m_ln_b, m_gla_w_in=m_gla_w_in, m_gla_w_g2=m_gla_w_g2, m_gla_b_g2=m_gla_b_g2, m_gla_norm_g=m_gla_norm_g, m_gla_w_out=m_gla_w_out, m_kv_w=m_kv_w, m_kv_bf=m_kv_bf, m_fox_w_in=m_fox_w_in, m_fox_w_out=m_fox_w_out, m_ffn_w_up=m_ffn_w_up, m_ffn_conv_w=m_ffn_conv_w, m_ffn_conv_b=m_ffn_conv_b, m_ffn_w_down=m_ffn_w_down, v_meta=v_meta, v_ln_g=v_ln_g, v_ln_b=v_ln_b, v_gla_w_in=v_gla_w_in, v_gla_w_g2=v_gla_w_g2, v_gla_b_g2=v_gla_b_g2, v_gla_norm_g=v_gla_norm_g, v_gla_w_out=v_gla_w_out, v_kv_w=v_kv_w, v_kv_bf=v_kv_bf, v_fox_w_in=v_fox_w_in, v_fox_w_out=v_fox_w_out, v_ffn_w_up=v_ffn_w_up, v_ffn_conv_w=v_ffn_conv_w, v_ffn_conv_b=v_ffn_conv_b, v_ffn_w_down=v_ffn_w_down)
    weights = {n: given[n] for n in TWIN_WEIGHTS}
    shared = {n: given[n] for n in SHARED_INPUTS}
    per_example = {n: given[n] for n in ['x']}
    grad_fn = _jax.value_and_grad(_loss, argnums=(0, 1))

    def one_microbatch(ex, loss_target):
        ex = dict(ex)
        diff = ex.pop(TWIN_DIFF_INPUT)
        return grad_fn(weights, diff, {**shared, **ex}, loss_target)

    if N_MICROBATCH == 1:
        loss, (grad_w, grad_x) = one_microbatch(per_example, given["loss_target"])
    else:
        def body(carry, xs):
            loss_sum, grad_sum = carry
            l_k, (gw_k, gx_k) = one_microbatch(xs[0], xs[1])
            with _jax.named_scope("update"):
                return (loss_sum + l_k, _jax.tree.map(_jnp.add, grad_sum, gw_k)), gx_k

        init = (_jnp.zeros((), _jnp.float32), _jax.tree.map(_jnp.zeros_like, weights))
        (loss, grad_w), grad_x = _jax.lax.scan(body, init, (per_example, given["loss_target"]))
    with _jax.named_scope("update"):
        delta_w, new_m, new_v = {}, {}, {}
        for n in TWIN_WEIGHTS:
            delta_w[n], new_m[n], new_v[n] = _adamw(weights[n], grad_w[n], given["m_" + n], given["v_" + n])
    return (loss, grad_x, *[grad_w[n] for n in TWIN_WEIGHTS], *[delta_w[n] for n in TWIN_WEIGHTS],
            *[new_m[n] for n in TWIN_WEIGHTS], *[new_v[n] for n in TWIN_WEIGHTS])
```

```python
import functools

import numpy as np
import jax
import jax.numpy as jnp
from jax import lax
from jax.experimental import pallas as pl
from jax.experimental.pallas import tpu as pltpu

F32 = jnp.float32
BF16 = jnp.bfloat16
HIGHEST = lax.Precision.HIGHEST

DEPTH = 4
N_A_LAYERS = DEPTH // 2
N_META = 16
FRONT = 128
N_PAD = FRONT - N_META
ALPHA = (2.0 * DEPTH) ** 0.25
LN_EPS = 1e-5
GLA_HEADS = 4
GLA_RANK = 16
GLA_TAU = 16.0
GLA_CHUNK = 64
FOX_HD = 128
LANE = 128
ADAM_LR = 0.001
ADAM_B1 = 0.9
ADAM_B2 = 0.999
ADAM_EPS = 1e-08
ADAM_WD = 0.01
ADAM_STEP = 10
NEG = -1e30
VMEM_LIMIT = 50 * 1024 * 1024
MESH = pl.DeviceIdType.MESH
ANY = pl.BlockSpec(memory_space=pl.ANY)


def _tile(n, pref, align):
    best = None
    t = align
    while t <= min(n, pref):
        if n % t == 0:
            best = t
        t += align
    return best if best is not None else n


def _cp(*sem):
    return pltpu.CompilerParams(dimension_semantics=sem, vmem_limit_bytes=VMEM_LIMIT)


def _dot(a, b, ca, cb, precision=None):
    return lax.dot_general(a, b, (((ca,), (cb,)), ((), ())), precision=precision,
                           preferred_element_type=F32)


def _sigmoid(x):
    return 1.0 / (1.0 + jnp.exp(-x))


def _log_sigmoid(z):
    return jnp.minimum(z, 0.0) - jnp.log(1.0 + jnp.exp(-jnp.abs(z)))


def _rows(i, tr, n=None):
    n = tr if n is None else n
    return i * tr + lax.broadcasted_iota(jnp.int32, (n, 1), 0)


def _mm(a, b, *, ta=False, tb=False, out_dtype=F32, add=None, add_scale=1.0, name):
    if ta:
        K, M = a.shape
    else:
        M, K = a.shape
    if tb:
        N, K2 = b.shape
    else:
        K2, N = b.shape
    assert K == K2, (a.shape, b.shape, ta, tb)
    tm = _tile(M, 1024, LANE) if ta else _tile(M, 1040, 16)
    tn = _tile(N, 1024, LANE)
    tk = _tile(K, 2048, LANE if ((not ta) or tb) else 16)
    nk = K // tk
    ca = 0 if ta else 1
    cb = 1 if tb else 0

    def body(*refs):
        if add is None:
            a_ref, b_ref, o_ref = refs[:3]
            add_ref = None
        else:
            a_ref, b_ref, add_ref, o_ref = refs[:4]
        part = _dot(a_ref[...].astype(BF16), b_ref[...].astype(BF16), ca, cb)

        def finish(acc):
            if add_ref is not None:
                acc = acc + add_scale * add_ref[...]
            o_ref[...] = acc.astype(out_dtype)

        if nk == 1:
            finish(part)
        else:
            acc_ref = refs[-1]
            k = pl.program_id(2)

            @pl.when(k == 0)
            def _():
                acc_ref[...] = part

            @pl.when(k > 0)
            def _():
                acc_ref[...] += part

            @pl.when(k == nk - 1)
            def _():
                finish(acc_ref[...])

    a_spec = pl.BlockSpec((tk, tm), lambda i, j, k: (k, i)) if ta else pl.BlockSpec((tm, tk), lambda i, j, k: (i, k))
    b_spec = pl.BlockSpec((tn, tk), lambda i, j, k: (j, k)) if tb else pl.BlockSpec((tk, tn), lambda i, j, k: (k, j))
    o_spec = pl.BlockSpec((tm, tn), lambda i, j, k: (i, j))
    in_specs = [a_spec, b_spec] + ([o_spec] if add is not None else [])
    args = (a, b) + ((add,) if add is not None else ())
    return pl.pallas_call(
        body, name=name, grid=(M // tm, N // tn, nk), in_specs=in_specs, out_specs=o_spec,
        out_shape=jax.ShapeDtypeStruct((M, N), out_dtype),
        scratch_shapes=[pltpu.VMEM((tm, tn), F32)] if nk > 1 else [],
        compiler_params=_cp("parallel", "parallel", "arbitrary"),
    )(*args)


def _ln_stats(h, mix):
    z = ALPHA * h + mix
    mu = jnp.mean(z, axis=-1, keepdims=True)
    zc = z - mu
    var = jnp.mean(zc * zc, axis=-1, keepdims=True)
    rstd = lax.rsqrt(var + LN_EPS)
    return zc * rstd, rstd


def _ln_fwd(h, mix, g, b, *, name):
    L, D = h.shape
    tr = _tile(L, 160, 16)

    def body(h_ref, m_ref, g_ref, b_ref, o_ref, ob_ref):
        xhat, _ = _ln_stats(h_ref[...], m_ref[...])
        y = xhat * g_ref[...] + b_ref[...]
        o_ref[...] = y
        ob_ref[...] = y.astype(BF16)

    row = pl.BlockSpec((tr, D), lambda i: (i, 0))
    vec = pl.BlockSpec((1, D), lambda i: (0, 0))
    return pl.pallas_call(
        body, name=name, grid=(L // tr,), in_specs=[row, row, vec, vec], out_specs=[row, row],
        out_shape=[jax.ShapeDtypeStruct((L, D), F32), jax.ShapeDtypeStruct((L, D), BF16)],
        compiler_params=_cp("parallel"),
    )(h, mix, g, b)


def _ln_bwd(dy, h, mix, g, *, name):
    L, D = h.shape
    tr = _tile(L, 160, 16)

    def body(dy_ref, h_ref, m_ref, g_ref, dz_ref, dzb_ref, dg_ref, db_ref):
        i = pl.program_id(0)
        xhat, rstd = _ln_stats(h_ref[...], m_ref[...])
        dy = dy_ref[...]
        dxh = dy * g_ref[...]
        m1 = jnp.mean(dxh, axis=-1, keepdims=True)
        m2 = jnp.mean(dxh * xhat, axis=-1, keepdims=True)
        dz = rstd * (dxh - m1 - xhat * m2)
        dz_ref[...] = dz
        dzb_ref[...] = dz.astype(BF16)
        pg = jnp.sum(dy * xhat, axis=0, keepdims=True)
        pb = jnp.sum(dy, axis=0, keepdims=True)

        @pl.when(i == 0)
        def _():
            dg_ref[...] = pg
            db_ref[...] = pb

        @pl.when(i > 0)
        def _():
            dg_ref[...] += pg
            db_ref[...] += pb

    row = pl.BlockSpec((tr, D), lambda i: (i, 0))
    vec = pl.BlockSpec((1, D), lambda i: (0, 0))
    return pl.pallas_call(
        body, name=name, grid=(L // tr,), in_specs=[row, row, row, vec], out_specs=[row, row, vec, vec],
        out_shape=[jax.ShapeDtypeStruct((L, D), F32), jax.ShapeDtypeStruct((L, D), BF16),
                   jax.ShapeDtypeStruct((1, D), F32), jax.ShapeDtypeStruct((1, D), F32)],
        compiler_params=_cp("arbitrary"),
    )(dy, h, mix, g)


def _loss(h, target, *, name):
    L, D = h.shape
    tr = FRONT

    def body(h_ref, t_ref, acc_ref, dy_ref):
        i = pl.program_id(0)
        e = jnp.where(i >= 1, h_ref[...] - t_ref[...], 0.0)
        dy_ref[...] = e * (1.0 / D)
        part = 0.5 * jnp.sum(jnp.sum(e * e, axis=-1, keepdims=True) * (1.0 / D), axis=0, keepdims=True)

        @pl.when(i == 0)
        def _():
            acc_ref[...] = jnp.zeros_like(acc_ref)

        acc_ref[...] += jnp.broadcast_to(part, acc_ref.shape)

    return pl.pallas_call(
        body, name=name, grid=(L // tr,),
        in_specs=[pl.BlockSpec((tr, D), lambda i: (i, 0)),
                  pl.BlockSpec((tr, D), lambda i: (jnp.maximum(i - 1, 0), 0))],
        out_specs=[pl.BlockSpec((8, LANE), lambda i: (0, 0)), pl.BlockSpec((tr, D), lambda i: (i, 0))],
        out_shape=[jax.ShapeDtypeStruct((8, LANE), F32), jax.ShapeDtypeStruct((L, D), F32)],
        compiler_params=_cp("arbitrary"),
    )(h, target)


def _gla_gate_fwd(P, w2p, b2, gl_blk, *, name):
    L = P.shape[0]
    DK = w2p.shape[1]
    tr = _tile(L, 640, 16)

    def body(gl_ref, w_ref, b_ref, o_ref):
        i = pl.program_id(0)
        z = _dot(gl_ref[...].astype(BF16), w_ref[...], 1, 0) + b_ref[...]
        la = _log_sigmoid(z) * (1.0 / GLA_TAU)
        o_ref[...] = jnp.where(_rows(i, tr) >= N_PAD, la, 0.0)

    return pl.pallas_call(
        body, name=name, grid=(L // tr,),
        in_specs=[pl.BlockSpec((tr, LANE), lambda i: (i, gl_blk)),
                  pl.BlockSpec((LANE, DK), lambda i: (0, 0)), pl.BlockSpec((1, DK), lambda i: (0, 0))],
        out_specs=pl.BlockSpec((tr, DK), lambda i: (i, 0)),
        out_shape=jax.ShapeDtypeStruct((L, DK), F32), compiler_params=_cp("parallel"),
    )(P, w2p, b2)


def _gla_gate_bwd(dla, P, w2p, b2, gl_blk, *, name):
    L = P.shape[0]
    DK = w2p.shape[1]
    tr = _tile(L, 640, 16)

    def body(dla_ref, gl_ref, w_ref, b_ref, dgl_ref, dw_ref, db_ref):
        i = pl.program_id(0)
        glb = gl_ref[...].astype(BF16)
        z = _dot(glb, w_ref[...], 1, 0) + b_ref[...]
        dz = jnp.where(_rows(i, tr) >= N_PAD, dla_ref[...], 0.0) * (1.0 / GLA_TAU) * _sigmoid(-z)
        dzb = dz.astype(BF16)
        dgl_ref[...] = _dot(dzb, w_ref[...], 1, 1).astype(BF16)
        pw = _dot(glb, dzb, 0, 0)
        pb = jnp.sum(dz, axis=0, keepdims=True)

        @pl.when(i == 0)
        def _():
            dw_ref[...] = pw
            db_ref[...] = pb

        @pl.when(i > 0)
        def _():
            dw_ref[...] += pw
            db_ref[...] += pb

    return pl.pallas_call(
        body, name=name, grid=(L // tr,),
        in_specs=[pl.BlockSpec((tr, DK), lambda i: (i, 0)), pl.BlockSpec((tr, LANE), lambda i: (i, gl_blk)),
                  pl.BlockSpec((LANE, DK), lambda i: (0, 0)), pl.BlockSpec((1, DK), lambda i: (0, 0))],
        out_specs=[pl.BlockSpec((tr, LANE), lambda i: (i, 0)), pl.BlockSpec((LANE, DK), lambda i: (0, 0)),
                   pl.BlockSpec((1, DK), lambda i: (0, 0))],
        out_shape=[jax.ShapeDtypeStruct((L, LANE), BF16), jax.ShapeDtypeStruct((LANE, DK), F32),
                   jax.ShapeDtypeStruct((1, DK), F32)],
        compiler_params=_cp("arbitrary"),
    )(dla, P, w2p, b2)


def _chunk_terms(q, k, g, n, scale, HV):
    C = q.shape[0]
    ri = lax.broadcasted_iota(jnp.int32, (C, C), 0)
    ci = lax.broadcasted_iota(jnp.int32, (C, C), 1)
    tri = ri >= ci
    valid = _rows(n, C) >= N_PAD
    km = jnp.where(valid, k, 0.0)
    b = _dot(tri.astype(F32), g, 1, 0, precision=HIGHEST)
    bl_row = jnp.sum(g, axis=0, keepdims=True)
    bl_col = _dot(g, jnp.ones((C, HV), F32), 0, 0, precision=HIGHEST)
    eb = jnp.exp(b)
    enb = jnp.exp(-b)
    qe = q * scale * eb
    ke = km * enb
    ebl_row = jnp.exp(bl_row)
    kl = ke * ebl_row
    return dict(tri=tri, valid=valid, eb=eb, enb=enb, qe=qe, ke=ke, kl=kl, ebl_row=ebl_row,
                ebl_col=jnp.exp(bl_col), ri=ri, ci=ci)


def _gla_chunk_fwd(P, la, H, HK, HV, *, name):
    L = P.shape[0]
    C = GLA_CHUNK
    N = L // C
    scale = HK ** -0.5

    def body(q_ref, k_ref, v_ref, g_ref, o_ref, s_ref, S):
        n = pl.program_id(1)

        @pl.when(n == 0)
        def _():
            S[...] = jnp.zeros_like(S)

        S0 = S[...]
        s_ref[0, 0] = S0
        t = _chunk_terms(q_ref[...], k_ref[...], g_ref[...], n, scale, HV)
        vb = v_ref[...].astype(BF16)
        qeb = t["qe"].astype(BF16)
        inter = _dot(qeb, S0.astype(BF16), 1, 0)
        att = jnp.where(t["tri"], _dot(qeb, t["ke"].astype(BF16), 1, 1), 0.0)
        o_ref[...] = inter + _dot(att.astype(BF16), vb, 1, 0)
        S[...] = t["ebl_col"] * S0 + _dot(t["kl"].astype(BF16), vb, 0, 0)

    return pl.pallas_call(
        body, name=name, grid=(H, N),
        in_specs=[pl.BlockSpec((C, HK), lambda h, n: (n, h)), pl.BlockSpec((C, HK), lambda h, n: (n, H + h)),
                  pl.BlockSpec((C, HV), lambda h, n: (n, H + h)), pl.BlockSpec((C, HK), lambda h, n: (n, h))],
        out_specs=[pl.BlockSpec((C, HV), lambda h, n: (n, h)),
                   pl.BlockSpec((1, 1, HK, HV), lambda h, n: (h, n, 0, 0))],
        out_shape=[jax.ShapeDtypeStruct((L, H * HV), F32), jax.ShapeDtypeStruct((H, N, HK, HV), F32)],
        scratch_shapes=[pltpu.VMEM((HK, HV), F32)],
        compiler_params=_cp("parallel", "arbitrary"),
    )(P, P, P, la)


def _gla_chunk_bwd(P, la, S_all, do, H, HK, HV, *, name):
    L = P.shape[0]
    C = GLA_CHUNK
    N = L // C
    scale = HK ** -0.5

    def body(q_ref, k_ref, v_ref, g_ref, s_ref, do_ref, dq_ref, dk_ref, dv_ref, dg_ref, dS):
        step = pl.program_id(1)
        n = N - 1 - step

        @pl.when(step == 0)
        def _():
            dS[...] = jnp.zeros_like(dS)

        dS1 = dS[...]
        S0 = s_ref[0, 0]
        t = _chunk_terms(q_ref[...], k_ref[...], g_ref[...], n, scale, HV)
        tri, qe, ke, kl = t["tri"], t["qe"], t["ke"], t["kl"]
        vb = v_ref[...].astype(BF16)
        dob = do_ref[...].astype(BF16)
        qeb, keb, dSb = qe.astype(BF16), ke.astype(BF16), dS1.astype(BF16)
        dA = jnp.where(tri, _dot(dob, vb, 1, 1), 0.0).astype(BF16)
        A = jnp.where(tri, _dot(qeb, keb, 1, 1), 0.0).astype(BF16)
        dqe = _dot(dob, S0.astype(BF16), 1, 1) + _dot(dA, keb, 1, 0)
        dkl = _dot(vb, dSb, 1, 1)
        dke = _dot(dA, qeb, 0, 0) + dkl * t["ebl_row"]
        dv_ref[...] = (_dot(A, dob, 0, 0) + _dot(kl.astype(BF16), dSb, 1, 0)).astype(BF16)
        debl = (jnp.sum(_dot(jnp.ones((8, HV), F32), dS1 * S0, 1, 1, precision=HIGHEST), axis=0, keepdims=True) * 0.125
                + jnp.sum(dkl * ke, axis=0, keepdims=True))
        dbl = debl * t["ebl_row"]
        db = dqe * qe - dke * ke + jnp.where(lax.broadcasted_iota(jnp.int32, (C, 1), 0) == C - 1, dbl, 0.0)
        triu = (t["ci"] >= t["ri"]).astype(F32)
        dg = _dot(triu, db, 1, 0, precision=HIGHEST)
        dq_ref[...] = (dqe * t["eb"] * scale).astype(BF16)
        dk_ref[...] = jnp.where(t["valid"], dke * t["enb"], 0.0).astype(BF16)
        dg_ref[...] = dg
        dS[...] = t["ebl_col"] * dS1 + _dot(qeb, dob, 0, 0)

    rev = lambda h, s: (N - 1 - s, h)
    return pl.pallas_call(
        body, name=name, grid=(H, N),
        in_specs=[pl.BlockSpec((C, HK), rev), pl.BlockSpec((C, HK), lambda h, s: (N - 1 - s, H + h)),
                  pl.BlockSpec((C, HV), lambda h, s: (N - 1 - s, H + h)), pl.BlockSpec((C, HK), rev),
                  pl.BlockSpec((1, 1, HK, HV), lambda h, s: (h, N - 1 - s, 0, 0)), pl.BlockSpec((C, HV), rev)],
        out_specs=[pl.BlockSpec((C, HK), rev), pl.BlockSpec((C, HK), rev),
                   pl.BlockSpec((C, HV), rev), pl.BlockSpec((C, HK), rev)],
        out_shape=[jax.ShapeDtypeStruct((L, H * HK), BF16), jax.ShapeDtypeStruct((L, H * HK), BF16),
                   jax.ShapeDtypeStruct((L, H * HV), BF16), jax.ShapeDtypeStruct((L, H * HK), F32)],
        scratch_shapes=[pltpu.VMEM((HK, HV), F32)],
        compiler_params=_cp("parallel", "arbitrary"),
    )(P, P, P, la, S_all, do)


def _silu_parts(x):
    s = _sigmoid(x)
    return x * s, s * (1.0 + x * (1.0 - s))


def _gla_post_fwd(o, P, ng, H, HV, r_blk, *, name):
    L, DV = o.shape
    tr = _tile(L, 320, 16)

    def body(o_ref, r_ref, g_ref, out_ref):
        for hd in range(H):
            sl = slice(hd * HV, (hd + 1) * HV)
            oh = o_ref[:, sl]
            rr = lax.rsqrt(jnp.mean(oh * oh, axis=-1, keepdims=True) + LN_EPS)
            silu, _ = _silu_parts(r_ref[:, sl])
            out_ref[:, sl] = (oh * rr * g_ref[...] * silu).astype(BF16)

    return pl.pallas_call(
        body, name=name, grid=(L // tr,),
        in_specs=[pl.BlockSpec((tr, DV), lambda i: (i, 0)), pl.BlockSpec((tr, DV), lambda i: (i, r_blk)),
                  pl.BlockSpec((1, HV), lambda i: (0, 0))],
        out_specs=pl.BlockSpec((tr, DV), lambda i: (i, 0)),
        out_shape=jax.ShapeDtypeStruct((L, DV), BF16), compiler_params=_cp("parallel"),
    )(o, P, ng)


def _gla_post_bwd(dgated, o, P, ng, H, HV, r_blk, *, name):
    L, DV = o.shape
    tr = _tile(L, 320, 16)

    def body(d_ref, o_ref, r_ref, g_ref, do_ref, dr_ref, dng_ref):
        i = pl.program_id(0)
        png = jnp.zeros((1, HV), F32)
        for hd in range(H):
            sl = slice(hd * HV, (hd + 1) * HV)
            oh = o_ref[:, sl]
            d = d_ref[:, sl]
            rr = lax.rsqrt(jnp.mean(oh * oh, axis=-1, keepdims=True) + LN_EPS)
            yh = oh * rr
            silu, dsilu = _silu_parts(r_ref[:, sl])
            dn = d * silu
            dr_ref[:, sl] = (d * yh * g_ref[...] * dsilu).astype(BF16)
            png = png + jnp.sum(dn * yh, axis=0, keepdims=True)
            dyh = dn * g_ref[...]
            do_ref[:, sl] = rr * (dyh - yh * jnp.mean(dyh * yh, axis=-1, keepdims=True))

        @pl.when(i == 0)
        def _():
            dng_ref[...] = png

        @pl.when(i > 0)
        def _():
            dng_ref[...] += png

    row = pl.BlockSpec((tr, DV), lambda i: (i, 0))
    return pl.pallas_call(
        body, name=name, grid=(L // tr,),
        in_specs=[row, row, pl.BlockSpec((tr, DV), lambda i: (i, r_blk)), pl.BlockSpec((1, HV), lambda i: (0, 0))],
        out_specs=[row, row, pl.BlockSpec((1, HV), lambda i: (0, 0))],
        out_shape=[jax.ShapeDtypeStruct((L, DV), F32), jax.ShapeDtypeStruct((L, DV), BF16),
                   jax.ShapeDtypeStruct((1, HV), F32)],
        compiler_params=_cp("arbitrary"),
    )(dgated, o, P, ng)


def _shift_down(x, halo, s):
    if s == 0:
        return x
    tr = x.shape[0]
    xx = jnp.concatenate([halo, x], axis=0)
    return pltpu.roll(xx, s, axis=0)[8:8 + tr]


def _shift_up(x, halo, s):
    if s == 0:
        return x
    tr = x.shape[0]
    xx = jnp.concatenate([x, halo], axis=0)
    return pltpu.roll(xx, tr + 8 - s, axis=0)[0:tr]


def _conv_taps(x_ref, halo_ref, i, tr):
    x = jnp.where(_rows(i, tr) >= N_PAD, x_ref[...], 0.0)
    halo = jnp.where(i * tr - 8 + lax.broadcasted_iota(jnp.int32, (8, 1), 0) >= N_PAD, halo_ref[...], 0.0)
    return [_shift_down(x, halo, s) for s in range(3)]


def _conv_apply(taps, w_ref, b_ref):
    return taps[2] * w_ref[0:1, :] + taps[1] * w_ref[1:2, :] + taps[0] * w_ref[2:3, :] + b_ref[...]


def _conv_specs(tr, tc):
    blk = pl.BlockSpec((tr, tc), lambda j, i: (i, j))
    halo = pl.BlockSpec((8, tc), lambda j, i: (jnp.maximum(i * (tr // 8) - 1, 0), j))
    w = pl.BlockSpec((3, tc), lambda j, i: (0, j))
    b = pl.BlockSpec((1, tc), lambda j, i: (0, j))
    return blk, halo, w, b


def _conv_act_fwd(Uu, Ug, wu, wg, bu, bg, *, name):
    L, DFF = Uu.shape
    tr = _tile(L, 320, 16)
    tc = _tile(DFF, 512, LANE)

    def body(xu_ref, hu_ref, xg_ref, hg_ref, wu_ref, wg_ref, bu_ref, bg_ref, o_ref):
        i = pl.program_id(1)
        u = _conv_apply(_conv_taps(xu_ref, hu_ref, i, tr), wu_ref, bu_ref)
        g = _conv_apply(_conv_taps(xg_ref, hg_ref, i, tr), wg_ref, bg_ref)
        o_ref[...] = (_silu_parts(g)[0] * u).astype(BF16)

    blk, halo, w, b = _conv_specs(tr, tc)
    return pl.pallas_call(
        body, name=name, grid=(DFF // tc, L // tr),
        in_specs=[blk, halo, blk, halo, w, w, b, b], out_specs=blk,
        out_shape=jax.ShapeDtypeStruct((L, DFF), BF16), compiler_params=_cp("parallel", "parallel"),
    )(Uu, Uu, Ug, Ug, wu, wg, bu, bg)


def _conv_act_bwd(Uu, Ug, wu, wg, bu, bg, dA, *, name):
    L, DFF = Uu.shape
    tr = _tile(L, 320, 16)
    tc = _tile(DFF, 512, LANE)

    def body(xu_ref, hu_ref, xg_ref, hg_ref, wu_ref, wg_ref, bu_ref, bg_ref, da_ref,
             du_ref, dg_ref, dwu_ref, dwg_ref, dbu_ref, dbg_ref):
        i = pl.program_id(1)
        tu = _conv_taps(xu_ref, hu_ref, i, tr)
        tg = _conv_taps(xg_ref, hg_ref, i, tr)
        u = _conv_apply(tu, wu_ref, bu_ref)
        g = _conv_apply(tg, wg_ref, bg_ref)
        silu, dsilu = _silu_parts(g)
        da = da_ref[...]
        du = da * silu
        dg = da * u * dsilu
        du_ref[...] = du
        dg_ref[...] = dg

        @pl.when(i == 0)
        def _():
            dwu_ref[...] = jnp.zeros_like(dwu_ref)
            dwg_ref[...] = jnp.zeros_like(dwg_ref)
            dbu_ref[...] = jnp.zeros_like(dbu_ref)
            dbg_ref[...] = jnp.zeros_like(dbg_ref)

        for j in range(3):
            dwu_ref[j:j + 1, :] += jnp.sum(du * tu[2 - j], axis=0, keepdims=True)
            dwg_ref[j:j + 1, :] += jnp.sum(dg * tg[2 - j], axis=0, keepdims=True)
        dbu_ref[...] += jnp.sum(du, axis=0, keepdims=True)
        dbg_ref[...] += jnp.sum(dg, axis=0, keepdims=True)

    blk, halo, w, b = _conv_specs(tr, tc)
    return pl.pallas_call(
        body, name=name, grid=(DFF // tc, L // tr),
        in_specs=[blk, halo, blk, halo, w, w, b, b, blk], out_specs=[blk, blk, w, w, b, b],
        out_shape=[jax.ShapeDtypeStruct((L, DFF), F32), jax.ShapeDtypeStruct((L, DFF), F32),
                   jax.ShapeDtypeStruct((3, DFF), F32), jax.ShapeDtypeStruct((3, DFF), F32),
                   jax.ShapeDtypeStruct((1, DFF), F32), jax.ShapeDtypeStruct((1, DFF), F32)],
        compiler_params=_cp("parallel", "arbitrary"),
    )(Uu, Uu, Ug, Ug, wu, wg, bu, bg, dA)


def _conv_in_bwd(dh, w, *, name):
    L, DFF = dh.shape
    tr = _tile(L, 320, 16)
    tc = _tile(DFF, 512, LANE)
    nb8 = L // 8

    def body(x_ref, halo_ref, w_ref, o_ref):
        i = pl.program_id(1)
        x = x_ref[...]
        halo = jnp.where((i + 1) * tr + lax.broadcasted_iota(jnp.int32, (8, 1), 0) < L, halo_ref[...], 0.0)
        d = (x * w_ref[2:3, :] + _shift_up(x, halo, 1) * w_ref[1:2, :] + _shift_up(x, halo, 2) * w_ref[0:1, :])
        o_ref[...] = jnp.where(_rows(i, tr) >= N_PAD, d, 0.0).astype(BF16)

    blk = pl.BlockSpec((tr, tc), lambda j, i: (i, j))
    halo = pl.BlockSpec((8, tc), lambda j, i: (jnp.minimum((i + 1) * (tr // 8), nb8 - 1), j))
    return pl.pallas_call(
        body, name=name, grid=(DFF // tc, L // tr),
        in_specs=[blk, halo, pl.BlockSpec((3, tc), lambda j, i: (0, j))], out_specs=blk,
        out_shape=jax.ShapeDtypeStruct((L, DFF), BF16), compiler_params=_cp("parallel", "parallel"),
    )(dh, dh, w)


def _fox_c_fwd(f, bf, *, name):
    L = f.shape[0]
    tr = _tile(L, 320, 16)

    def body(f_ref, b_ref, c_ref, carry):
        i = pl.program_id(0)

        @pl.when(i == 0)
        def _():
            carry[...] = jnp.zeros_like(carry)

        lf = jnp.where(_rows(i, tr) >= N_PAD, _log_sigmoid(f_ref[...] + b_ref[...]), 0.0)
        tri = (lax.broadcasted_iota(jnp.int32, (tr, tr), 0) >= lax.broadcasted_iota(jnp.int32, (tr, tr), 1)).astype(F32)
        c_ref[...] = _dot(tri, lf, 1, 0, precision=HIGHEST) + carry[...]
        carry[...] += jnp.sum(lf, axis=0, keepdims=True)

    return pl.pallas_call(
        body, name=name, grid=(L // tr,),
        in_specs=[pl.BlockSpec((tr, LANE), lambda i: (i, 0)), pl.BlockSpec((1, LANE), lambda i: (0, 0))],
        out_specs=pl.BlockSpec((tr, LANE), lambda i: (i, 0)),
        out_shape=jax.ShapeDtypeStruct((L, LANE), F32), scratch_shapes=[pltpu.VMEM((1, LANE), F32)],
        compiler_params=_cp("arbitrary"),
    )(f, bf)


def _fox_c_bwd(dc, f, bf, *, name):
    L = f.shape[0]
    tr = _tile(L, 320, 16)
    nb = L // tr

    def body(dc_ref, f_ref, b_ref, df_ref, db_ref, carry):
        s = pl.program_id(0)
        i = nb - 1 - s

        @pl.when(s == 0)
        def _():
            carry[...] = jnp.zeros_like(carry)
            db_ref[...] = jnp.zeros_like(db_ref)

        dc = dc_ref[...]
        triu = (lax.broadcasted_iota(jnp.int32, (tr, tr), 1) >= lax.broadcasted_iota(jnp.int32, (tr, tr), 0)).astype(F32)
        dlf = _dot(triu, dc, 1, 0, precision=HIGHEST) + carry[...]
        carry[...] += jnp.sum(dc, axis=0, keepdims=True)
        df = jnp.where(_rows(i, tr) >= N_PAD, dlf, 0.0) * _sigmoid(-(f_ref[...] + b_ref[...]))
        df_ref[...] = df.astype(BF16)
        db_ref[...] += jnp.sum(df, axis=0, keepdims=True)

    rev = pl.BlockSpec((tr, LANE), lambda s: (nb - 1 - s, 0))
    vec = pl.BlockSpec((1, LANE), lambda s: (0, 0))
    return pl.pallas_call(
        body, name=name, grid=(nb,), in_specs=[rev, rev, vec], out_specs=[rev, vec],
        out_shape=[jax.ShapeDtypeStruct((L, LANE), BF16), jax.ShapeDtypeStruct((1, LANE), F32)],
        scratch_shapes=[pltpu.VMEM((1, LANE), F32)], compiler_params=_cp("arbitrary"),
    )(dc, f, bf)


def _fox_logits(q_ref, k_ref, cc_ref, cr_ref, qi, kj, t):
    qs = (q_ref[...] * (FOX_HD ** -0.5)).astype(BF16)
    s = _dot(qs, k_ref[...], 1, 1) + cc_ref[0] - cr_ref[0]
    ti = qi * t + lax.broadcasted_iota(jnp.int32, (t, t), 0)
    si = kj * t + lax.broadcasted_iota(jnp.int32, (t, t), 1)
    mask = (si <= ti) & ((si >= N_PAD) | (si == ti))
    return qs, s, mask


def _fox_attn_fwd(QO, KV, ccol, crow, H, *, name):
    L = QO.shape[0]
    hd = FOX_HD
    t = _tile(L, 640, LANE)
    nb = L // t

    def body(q_ref, k_ref, v_ref, cc_ref, cr_ref, o_ref, lse_ref, m_s, l_s, acc):
        qi, kj = pl.program_id(1), pl.program_id(2)

        @pl.when(kj == 0)
        def _():
            m_s[...] = jnp.full_like(m_s, NEG)
            l_s[...] = jnp.zeros_like(l_s)
            acc[...] = jnp.zeros_like(acc)

        @pl.when(kj <= qi)
        def _():
            _, s, mask = _fox_logits(q_ref, k_ref, cc_ref, cr_ref, qi, kj, t)
            s = jnp.where(mask, s, NEG)
            m_new = jnp.maximum(m_s[...], jnp.max(s, axis=-1, keepdims=True))
            p = jnp.where(mask, jnp.exp(s - m_new), 0.0)
            alpha = jnp.exp(m_s[...] - m_new)
            l_s[...] = alpha * l_s[...] + jnp.sum(p, axis=-1, keepdims=True)
            acc[...] = alpha * acc[...] + _dot(p.astype(BF16), v_ref[...], 1, 0)
            m_s[...] = m_new

        @pl.when(kj == nb - 1)
        def _():
            o_ref[...] = acc[...] / l_s[...]
            lse_ref[0] = m_s[...] + jnp.log(l_s[...])

    kmap = lambda h, qi, kj: (jnp.minimum(kj, qi), h)
    return pl.pallas_call(
        body, name=name, grid=(H, nb, nb),
        in_specs=[pl.BlockSpec((t, hd), lambda h, qi, kj: (qi, h)), pl.BlockSpec((t, hd), kmap),
                  pl.BlockSpec((t, hd), lambda h, qi, kj: (jnp.minimum(kj, qi), H + h)),
                  pl.BlockSpec((1, t, 1), lambda h, qi, kj: (h, qi, 0)),
                  pl.BlockSpec((1, 1, t), lambda h, qi, kj: (h, 0, jnp.minimum(kj, qi)))],
        out_specs=[pl.BlockSpec((t, hd), lambda h, qi, kj: (qi, h)),
                   pl.BlockSpec((1, t, 1), lambda h, qi, kj: (h, qi, 0))],
        out_shape=[jax.ShapeDtypeStruct((L, H * hd), F32), jax.ShapeDtypeStruct((H, L, 1), F32)],
        scratch_shapes=[pltpu.VMEM((t, 1), F32), pltpu.VMEM((t, 1), F32), pltpu.VMEM((t, hd), F32)],
        compiler_params=_cp("parallel", "parallel", "arbitrary"),
    )(QO, KV, KV, ccol, crow)


def _fox_ds(q_ref, k_ref, v_ref, cc_ref, cr_ref, lse_ref, do_ref, o_ref, qi, kj, t):
    qs, s, mask = _fox_logits(q_ref, k_ref, cc_ref, cr_ref, qi, kj, t)
    p = jnp.where(mask, jnp.exp(s - lse_ref[0]), 0.0)
    do = do_ref[...]
    dob = do.astype(BF16)
    dp = _dot(dob, v_ref[...], 1, 1)
    delta = jnp.sum(do * o_ref[...], axis=-1, keepdims=True)
    ds = p * (dp - delta)
    return qs, p, ds, dob


def _fox_attn_bwd_dq(QO, KV, ccol, crow, lse, do, o, H, *, name):
    L = QO.shape[0]
    hd = FOX_HD
    t = _tile(L, 640, LANE)
    nb = L // t

    def body(q_ref, k_ref, v_ref, cc_ref, cr_ref, lse_ref, do_ref, o_ref, dq_ref, dc_ref, dq_acc, dc_acc):
        qi, kj = pl.program_id(1), pl.program_id(2)

        @pl.when(kj == 0)
        def _():
            dq_acc[...] = jnp.zeros_like(dq_acc)
            dc_acc[...] = jnp.zeros_like(dc_acc)

        @pl.when(kj <= qi)
        def _():
            _, _, ds, _ = _fox_ds(q_ref, k_ref, v_ref, cc_ref, cr_ref, lse_ref, do_ref, o_ref, qi, kj, t)
            dq_acc[...] += _dot(ds.astype(BF16), k_ref[...], 1, 0)
            dc_acc[...] += jnp.sum(ds, axis=-1, keepdims=True)

        @pl.when(kj == nb - 1)
        def _():
            dq_ref[...] = (dq_acc[...] * (hd ** -0.5)).astype(BF16)
            dc_ref[0] = dc_acc[...]

    qb = pl.BlockSpec((t, hd), lambda h, qi, kj: (qi, h))
    col = pl.BlockSpec((1, t, 1), lambda h, qi, kj: (h, qi, 0))
    return pl.pallas_call(
        body, name=name, grid=(H, nb, nb),
        in_specs=[qb, pl.BlockSpec((t, hd), lambda h, qi, kj: (jnp.minimum(kj, qi), h)),
                  pl.BlockSpec((t, hd), lambda h, qi, kj: (jnp.minimum(kj, qi), H + h)),
                  col, pl.BlockSpec((1, 1, t), lambda h, qi, kj: (h, 0, jnp.minimum(kj, qi))), col, qb, qb],
        out_specs=[qb, col],
        out_shape=[jax.ShapeDtypeStruct((L, H * hd), BF16), jax.ShapeDtypeStruct((H, L, 1), F32)],
        scratch_shapes=[pltpu.VMEM((t, hd), F32), pltpu.VMEM((t, 1), F32)],
        compiler_params=_cp("parallel", "parallel", "arbitrary"),
    )(QO, KV, KV, ccol, crow, lse, do, o)


def _fox_attn_bwd_dkv(QO, KV, ccol, crow, lse, do, o, init, H, *, name):
    L = QO.shape[0]
    hd = FOX_HD
    t = _tile(L, 640, LANE)
    nb = L // t

    def body(q_ref, k_ref, v_ref, cc_ref, cr_ref, lse_ref, do_ref, o_ref, dk0_ref, dv0_ref, dc0_ref,
             dk_ref, dv_ref, dc_ref):
        kj, qi = pl.program_id(1), pl.program_id(2)

        @pl.when(qi == 0)
        def _():
            dk_ref[...] = dk0_ref[...]
            dv_ref[...] = dv0_ref[...]
            dc_ref[...] = dc0_ref[...]

        @pl.when(qi >= kj)
        def _():
            qs, p, ds, dob = _fox_ds(q_ref, k_ref, v_ref, cc_ref, cr_ref, lse_ref, do_ref, o_ref, qi, kj, t)
            dv_ref[...] += _dot(p.astype(BF16), dob, 0, 0)
            dk_ref[...] += _dot(ds.astype(BF16), qs, 0, 0)
            dc_ref[0] -= jnp.sum(ds, axis=0, keepdims=True)

    qmap = lambda h, kj, qi: (jnp.maximum(qi, kj), h)
    qb = pl.BlockSpec((t, hd), qmap)
    col = pl.BlockSpec((1, t, 1), lambda h, kj, qi: (h, jnp.maximum(qi, kj), 0))
    kb = pl.BlockSpec((t, hd), lambda h, kj, qi: (kj, h))
    rowb = pl.BlockSpec((1, 1, t), lambda h, kj, qi: (h, 0, kj))
    return pl.pallas_call(
        body, name=name, grid=(H, nb, nb),
        in_specs=[qb, kb, pl.BlockSpec((t, hd), lambda h, kj, qi: (kj, H + h)), col, rowb, col, qb, qb,
                  kb, kb, rowb],
        out_specs=[kb, kb, rowb],
        out_shape=[jax.ShapeDtypeStruct((L, H * hd), F32), jax.ShapeDtypeStruct((L, H * hd), F32),
                   jax.ShapeDtypeStruct((H, 1, L), F32)],
        compiler_params=_cp("parallel", "parallel", "arbitrary"),
    )(QO, KV, KV, ccol, crow, lse, do, o, *init)


def _fox_gate_fwd(o, QO, *, name):
    L, D = o.shape
    tr = _tile(L, 320, 16)

    def body(o_ref, g_ref, out_ref):
        out_ref[...] = (o_ref[...] * _sigmoid(g_ref[...])).astype(BF16)

    row = pl.BlockSpec((tr, D), lambda i: (i, 0))
    return pl.pallas_call(
        body, name=name, grid=(L // tr,), in_specs=[row, pl.BlockSpec((tr, D), lambda i: (i, 1))], out_specs=row,
        out_shape=jax.ShapeDtypeStruct((L, D), BF16), compiler_params=_cp("parallel"),
    )(o, QO)


def _fox_gate_bwd(d, o, QO, *, name):
    L, D = o.shape
    tr = _tile(L, 320, 16)

    def body(d_ref, o_ref, g_ref, do_ref, dg_ref):
        s = _sigmoid(g_ref[...])
        d = d_ref[...]
        do_ref[...] = d * s
        dg_ref[...] = (d * o_ref[...] * s * (1.0 - s)).astype(BF16)

    row = pl.BlockSpec((tr, D), lambda i: (i, 0))
    return pl.pallas_call(
        body, name=name, grid=(L // tr,), in_specs=[row, row, pl.BlockSpec((tr, D), lambda i: (i, 1))],
        out_specs=[row, row],
        out_shape=[jax.ShapeDtypeStruct((L, D), F32), jax.ShapeDtypeStruct((L, D), BF16)],
        compiler_params=_cp("parallel"),
    )(d, o, QO)


def _row_tile(R, C, n_arrays):
    budget = VMEM_LIMIT // (3 * n_arrays * 4 * max(C, LANE))
    return _tile(R, max(8, budget // 8 * 8), 8)


def _sum_parts(parts, *, name):
    R, C = parts[0].shape
    tr = _row_tile(R, C, len(parts) + 1)

    def body(*refs):
        acc = refs[0][...].astype(F32)
        for r in refs[1:-1]:
            acc = acc + r[...].astype(F32)
        refs[-1][...] = acc

    blk = pl.BlockSpec((tr, C), lambda i: (i, 0))
    return pl.pallas_call(
        body, name=name, grid=(R // tr,), in_specs=[blk] * len(parts), out_specs=blk,
        out_shape=jax.ShapeDtypeStruct((R, C), F32), compiler_params=_cp("parallel"),
    )(*parts)


def _sum_slots(x, *, name):
    S, R, C = x.shape
    tr = _row_tile(R, C, S + 1)

    def body(x_ref, o_ref):
        acc = x_ref[0].astype(F32)
        for s in range(1, S):
            acc = acc + x_ref[s].astype(F32)
        o_ref[...] = acc

    return pl.pallas_call(
        body, name=name, grid=(R // tr,), in_specs=[pl.BlockSpec((S, tr, C), lambda i: (0, i, 0))],
        out_specs=pl.BlockSpec((tr, C), lambda i: (i, 0)),
        out_shape=jax.ShapeDtypeStruct((R, C), F32), compiler_params=_cp("parallel"),
    )(x)


def _adamw(w, m, v, gparts, *, name):
    R, C = w.shape
    tr = _row_tile(R, C, 7 + len(gparts))
    ng = len(gparts)

    def body(*refs):
        w_ref, m_ref, v_ref = refs[:3]
        g = refs[3][...]
        for r in refs[4:3 + ng]:
            g = g + r[...]
        g_ref, d_ref, nm_ref, nv_ref = refs[3 + ng:]
        nm = ADAM_B1 * m_ref[...] + (1.0 - ADAM_B1) * g
        nv = ADAM_B2 * v_ref[...] + (1.0 - ADAM_B2) * (g * g)
        m_hat = nm / (1.0 - ADAM_B1 ** ADAM_STEP)
        v_hat = nv / (1.0 - ADAM_B2 ** ADAM_STEP)
        g_ref[...] = g
        d_ref[...] = -ADAM_LR * (m_hat / (jnp.sqrt(v_hat) + ADAM_EPS) + ADAM_WD * w_ref[...])
        nm_ref[...] = nm
        nv_ref[...] = nv

    blk = pl.BlockSpec((tr, C), lambda i: (i, 0))
    return pl.pallas_call(
        body, name=name, grid=(R // tr,), in_specs=[blk] * (3 + ng), out_specs=[blk] * 4,
        out_shape=[jax.ShapeDtypeStruct((R, C), F32)] * 4, compiler_params=_cp("parallel"),
    )(w, m, v, *gparts)


def _chip_peers():
    x, y, c = lax.axis_index("x"), lax.axis_index("y"), lax.axis_index("c")
    return (x, y, c), [(1 - x, y), (x, 1 - y), (1 - x, 1 - y)]


def _gather_chips(arrs, *, name):
    n = len(arrs)

    def body(*refs):
        ins, outs = refs[:n], refs[n:2 * n]
        send, recv, loc = refs[2 * n:]
        (x, y, c), chips = _chip_peers()
        copies = []
        for a in range(n):
            own = pltpu.make_async_copy(ins[a], outs[a].at[2 * x + y], loc.at[a])
            own.start()
            copies.append(own)
            for j, (px, py) in enumerate(chips):
                cp = pltpu.make_async_remote_copy(
                    src_ref=ins[a], dst_ref=outs[a].at[2 * x + y], send_sem=send.at[a, j], recv_sem=recv.at[a, j],
                    device_id=(px, py, c), device_id_type=MESH)
                cp.start()
                copies.append(cp)
        for cp in copies:
            cp.wait()

    return pl.pallas_call(
        body, name=name, in_specs=[ANY] * n, out_specs=[ANY] * n,
        out_shape=[jax.ShapeDtypeStruct((4,) + a.shape, a.dtype) for a in arrs],
        scratch_shapes=[pltpu.SemaphoreType.DMA((n, 3)), pltpu.SemaphoreType.DMA((n, 3)), pltpu.SemaphoreType.DMA((n,))],
    )(*arrs)


def _scatter_chips(arrs, *, name):
    n = len(arrs)

    def body(*refs):
        ins, outs = refs[:n], refs[n:2 * n]
        send, recv = refs[2 * n:]
        (x, y, c), chips = _chip_peers()
        copies = []
        for a in range(n):
            for j, (px, py) in enumerate(chips):
                cp = pltpu.make_async_remote_copy(
                    src_ref=ins[a].at[2 * px + py], dst_ref=outs[a].at[j], send_sem=send.at[a, j],
                    recv_sem=recv.at[a, j], device_id=(px, py, c), device_id_type=MESH)
                cp.start()
                copies.append(cp)
        for cp in copies:
            cp.wait()

    return pl.pallas_call(
        body, name=name, in_specs=[ANY] * n, out_specs=[ANY] * n,
        out_shape=[jax.ShapeDtypeStruct((3,) + a.shape[1:], a.dtype) for a in arrs],
        scratch_shapes=[pltpu.SemaphoreType.DMA((n, 3)), pltpu.SemaphoreType.DMA((n, 3))],
    )(*arrs)


def _swap_sibling(arrs, *, name):
    n = len(arrs)

    def body(*refs):
        ins, outs = refs[:n], refs[n:2 * n]
        send, recv = refs[2 * n:]
        x, y, c = lax.axis_index("x"), lax.axis_index("y"), lax.axis_index("c")
        copies = []
        for a in range(n):
            cp = pltpu.make_async_remote_copy(
                src_ref=ins[a], dst_ref=outs[a], send_sem=send.at[a], recv_sem=recv.at[a],
                device_id=(x, y, 1 - c), device_id_type=MESH)
            cp.start()
            copies.append(cp)
        for cp in copies:
            cp.wait()

    return pl.pallas_call(
        body, name=name, in_specs=[ANY] * n, out_specs=[ANY] * n,
        out_shape=[jax.ShapeDtypeStruct(a.shape, a.dtype) for a in arrs],
        scratch_shapes=[pltpu.SemaphoreType.DMA((n,)), pltpu.SemaphoreType.DMA((n,))],
    )(*arrs)


def _gather_all(a, *, name):
    def body(in_ref, out_ref, send, recv, loc):
        x, y, c = lax.axis_index("x"), lax.axis_index("y"), lax.axis_index("c")
        me = 4 * x + 2 * y + c
        own = pltpu.make_async_copy(in_ref, out_ref.at[me], loc)
        own.start()
        copies = [own]
        for j in range(1, 8):
            fx, fy, fc = (j >> 2) & 1, (j >> 1) & 1, j & 1
            peer = (x ^ fx, y ^ fy, c ^ fc)
            cp = pltpu.make_async_remote_copy(
                src_ref=in_ref, dst_ref=out_ref.at[me], send_sem=send.at[j - 1], recv_sem=recv.at[j - 1],
                device_id=peer, device_id_type=MESH)
            cp.start()
            copies.append(cp)
        for cp in copies:
            cp.wait()

    return pl.pallas_call(
        body, name=name, in_specs=[ANY], out_specs=ANY,
        out_shape=jax.ShapeDtypeStruct((8,) + a.shape, a.dtype),
        scratch_shapes=[pltpu.SemaphoreType.DMA((7,)), pltpu.SemaphoreType.DMA((7,)), pltpu.SemaphoreType.DMA(())],
    )(a)


def _pack(arrs):
    flat = jnp.concatenate([a.astype(F32).reshape(-1) for a in arrs])
    n = flat.shape[0]
    pad = (-n) % (8 * LANE)
    return jnp.pad(flat, (0, pad)).reshape(-1, LANE)


def _unpack(buf, shapes):
    flat = buf.reshape(-1)
    out, off = [], 0
    for s in shapes:
        n = int(np.prod(s))
        out.append(flat[off:off + n].reshape(s))
        off += n
    return out


def _to_shards(g, axis):
    parts = jnp.split(g, 4, axis=axis)
    return jnp.stack([p.reshape(-1, p.shape[-1]) for p in parts])


def kernel(x, meta, ln_g, ln_b, gla_w_in, gla_w_g2, gla_b_g2, gla_norm_g, gla_w_out, kv_w, kv_bf, fox_w_in, fox_w_out, ffn_w_up, ffn_conv_w, ffn_conv_b, ffn_w_down, loss_target, m_meta, m_ln_g, m_ln_b, m_gla_w_in, m_gla_w_g2, m_gla_b_g2, m_gla_norm_g, m_gla_w_out, m_kv_w, m_kv_bf, m_fox_w_in, m_fox_w_out, m_ffn_w_up, m_ffn_conv_w, m_ffn_conv_b, m_ffn_w_down, v_meta, v_ln_g, v_ln_b, v_gla_w_in, v_gla_w_g2, v_gla_b_g2, v_gla_norm_g, v_gla_w_out, v_kv_w, v_kv_bf, v_fox_w_in, v_fox_w_out, v_ffn_w_up, v_ffn_conv_w, v_ffn_conv_b, v_ffn_w_down):
    D = x.shape[-1]
    L = x.shape[1] + FRONT
    HG = GLA_HEADS
    DK, DV = D // 2, D
    HK, HV = DK // HG, DV // HG
    HF = D // FOX_HD
    DFF = ffn_w_down.shape[1] * 4
    chip = 2 * lax.axis_index("x") + lax.axis_index("y")

    big_names = ["gla_w_in", "gla_w_out", "kv_w", "fox_w_in", "fox_w_out", "ffn_w_up", "ffn_w_down"]
    big = dict(gla_w_in=gla_w_in, gla_w_out=gla_w_out, kv_w=kv_w, fox_w_in=fox_w_in, fox_w_out=fox_w_out,
               ffn_w_up=ffn_w_up, ffn_w_down=ffn_w_down)
    big_axis = dict(gla_w_in=2, gla_w_out=1, kv_w=1, fox_w_in=2, fox_w_out=1, ffn_w_up=2, ffn_w_down=1)
    small_names = ["meta", "ln_g", "ln_b", "gla_w_g2", "gla_b_g2", "gla_norm_g", "ffn_conv_w"]
    small = dict(meta=meta, ln_g=ln_g, ln_b=ln_b, gla_w_g2=gla_w_g2, gla_b_g2=gla_b_g2, gla_norm_g=gla_norm_g,
                 ffn_conv_w=ffn_conv_w)
    small_shapes = [small[k].shape for k in small_names]
    gathered = _gather_chips([big[k].astype(BF16) for k in big_names] + [_pack([small[k] for k in small_names])],
                             name="gather_weights")
    full = {k: jnp.concatenate([gathered[i][s] for s in range(4)], axis=big_axis[k]) for i, k in enumerate(big_names)}
    sm_sh = [_unpack(gathered[-1][s], small_shapes) for s in range(4)]
    fs = {k: jnp.concatenate([sm_sh[s][i] for s in range(4)], axis=-1) for i, k in enumerate(small_names)}

    pad_cols = lambda w: jnp.pad(w, ((0, 0), (0, LANE - w.shape[1])))
    W_P, W_Pgl = [], []
    for l in range(N_A_LAYERS):
        w = full["gla_w_in"][l]
        W_P.append(jnp.concatenate([w[:, :2 * DK + DV], w[:, 2 * DK + DV + GLA_RANK:],
                                    pad_cols(w[:, 2 * DK + DV:2 * DK + DV + GLA_RANK])], axis=1))
    w2p = [jnp.pad(fs["gla_w_g2"][l], ((0, LANE - GLA_RANK), (0, 0))).astype(BF16) for l in range(N_A_LAYERS)]
    W_kv = full["kv_w"][:, :2 * D]
    W_f = pad_cols(full["kv_w"][:, 2 * D:])
    W_kvf = jnp.concatenate([W_kv, W_f], axis=1)
    bf_pad = jnp.pad(kv_bf, (0, LANE - HF)).reshape(1, LANE)
    W_u = [full["ffn_w_up"][l][:, :DFF] for l in range(DEPTH)]
    W_g = [full["ffn_w_up"][l][:, DFF:] for l in range(DEPTH)]
    cw_u = [fs["ffn_conv_w"][l][:, :DFF] for l in range(DEPTH)]
    cw_g = [fs["ffn_conv_w"][l][:, DFF:] for l in range(DEPTH)]
    cb_u = [ffn_conv_b[l][None, :DFF] for l in range(DEPTH)]
    cb_g = [ffn_conv_b[l][None, DFF:] for l in range(DEPTH)]
    gl_blk = (2 * DK + 2 * DV) // LANE
    r_blk = (2 * DK + DV) // DV

    h = jnp.concatenate([jnp.concatenate([jnp.zeros((N_PAD, D), F32), fs["meta"]], axis=0), x[0]], axis=0)
    hb = h.astype(BF16)
    saved = []
    kvs = None
    for l in range(DEPTH):
        s = dict(h=h, hb=hb)
        if l < N_A_LAYERS:
            s["P"] = _mm(hb, W_P[l], name=f"gla_in_{l}")
            s["la"] = _gla_gate_fwd(s["P"], w2p[l], fs["gla_b_g2"][l][None], gl_blk, name=f"gla_gate_{l}")
            s["o"], s["S"] = _gla_chunk_fwd(s["P"], s["la"], HG, HK, HV, name=f"gla_chunk_{l}")
            s["gated"] = _gla_post_fwd(s["o"], s["P"], fs["gla_norm_g"][l][None], HG, HV, r_blk, name=f"gla_post_{l}")
            s["mix"] = _mm(s["gated"], full["gla_w_out"][l], name=f"gla_out_{l}")
        else:
            j = l - N_A_LAYERS
            if kvs is None:
                KV = _mm(hb, W_kv, out_dtype=BF16, name="kv_proj")
                f = _mm(hb, W_f, name="kv_gate_proj")
                c = _fox_c_fwd(f, bf_pad, name="fox_c")
                cT = c[:, :HF].T
                kvs = dict(KV=KV, f=f, ccol=cT[:, :, None], crow=cT[:, None, :], hb=hb)
            s["QO"] = _mm(hb, full["fox_w_in"][j], name=f"fox_in_{j}")
            s["o"], s["lse"] = _fox_attn_fwd(s["QO"], kvs["KV"], kvs["ccol"], kvs["crow"], HF, name=f"fox_attn_{j}")
            s["gated"] = _fox_gate_fwd(s["o"], s["QO"], name=f"fox_gate_{j}")
            s["mix"] = _mm(s["gated"], full["fox_w_out"][j], name=f"fox_out_{j}")
        s["h1"], s["h1b"] = _ln_fwd(h, s["mix"], fs["ln_g"][l, 0][None], fs["ln_b"][l, 0][None], name=f"ln_a_{l}")
        s["Uu"] = _mm(s["h1b"], W_u[l], name=f"ffn_up_u_{l}")
        s["Ug"] = _mm(s["h1b"], W_g[l], name=f"ffn_up_g_{l}")
        s["a"] = _conv_act_fwd(s["Uu"], s["Ug"], cw_u[l], cw_g[l], cb_u[l], cb_g[l], name=f"ffn_conv_{l}")
        s["ffn"] = _mm(s["a"], full["ffn_w_down"][l], name=f"ffn_down_{l}")
        h, hb = _ln_fwd(s["h1"], s["ffn"], fs["ln_g"][l, 1][None], fs["ln_b"][l, 1][None], name=f"ln_b_{l}")
        saved.append(s)

    loss_acc, dh = _loss(h, loss_target[0], name="loss")

    gW = {}
    d_ln_g = [[None, None] for _ in range(DEPTH)]
    d_ln_b = [[None, None] for _ in range(DEPTH)]
    d_cw, d_cb = [None] * DEPTH, [None] * DEPTH
    d_wg2, d_bg2, d_ng = [None] * N_A_LAYERS, [None] * N_A_LAYERS, [None] * N_A_LAYERS
    zeros_ld = jnp.zeros((L, D), F32)
    dkv = (zeros_ld, zeros_ld, jnp.zeros((HF, 1, L), F32))
    dcq_sum = None
    for l in reversed(range(DEPTH)):
        s = saved[l]
        dz, dzb, d_ln_g[l][1], d_ln_b[l][1] = _ln_bwd(dh, s["h1"], s["ffn"], fs["ln_g"][l, 1][None], name=f"ln_b_bwd_{l}")
        dA = _mm(dzb, full["ffn_w_down"][l], tb=True, name=f"ffn_down_dx_{l}")
        gW[("ffn_w_down", l)] = _mm(s["a"], dzb, ta=True, out_dtype=BF16, name=f"ffn_down_dw_{l}")
        dcu, dcg, dwu, dwg, dbu, dbg = _conv_act_bwd(s["Uu"], s["Ug"], cw_u[l], cw_g[l], cb_u[l], cb_g[l], dA,
                                                     name=f"ffn_conv_bwd_{l}")
        d_cw[l] = jnp.concatenate([dwu, dwg], axis=1)
        d_cb[l] = jnp.concatenate([dbu, dbg], axis=1)[0]
        dUu = _conv_in_bwd(dcu, cw_u[l], name=f"ffn_conv_dx_u_{l}")
        dUg = _conv_in_bwd(dcg, cw_g[l], name=f"ffn_conv_dx_g_{l}")
        gW[("ffn_w_up", l)] = jnp.concatenate(
            [_mm(s["h1b"], dUu, ta=True, out_dtype=BF16, name=f"ffn_up_dw_u_{l}"),
             _mm(s["h1b"], dUg, ta=True, out_dtype=BF16, name=f"ffn_up_dw_g_{l}")], axis=1)
        dh1 = _mm(dUu, W_u[l], tb=True, add=dz, add_scale=ALPHA, name=f"ffn_up_dx_u_{l}")
        dh1 = _mm(dUg, W_g[l], tb=True, add=dh1, name=f"ffn_up_dx_g_{l}")
        dz, dzb, d_ln_g[l][0], d_ln_b[l][0] = _ln_bwd(dh1, s["h"], s["mix"], fs["ln_g"][l, 0][None], name=f"ln_a_bwd_{l}")
        if l < N_A_LAYERS:
            dgated = _mm(dzb, full["gla_w_out"][l], tb=True, name=f"gla_out_dx_{l}")
            gW[("gla_w_out", l)] = _mm(s["gated"], dzb, ta=True, out_dtype=BF16, name=f"gla_out_dw_{l}")
            do, drb, d_ng[l] = _gla_post_bwd(dgated, s["o"], s["P"], fs["gla_norm_g"][l][None], HG, HV, r_blk,
                                             name=f"gla_post_bwd_{l}")
            dq, dk, dvb, dla = _gla_chunk_bwd(s["P"], s["la"], s["S"], do, HG, HK, HV, name=f"gla_chunk_bwd_{l}")
            dglb, dw2, d_bg2[l] = _gla_gate_bwd(dla, s["P"], w2p[l], fs["gla_b_g2"][l][None], gl_blk,
                                               name=f"gla_gate_bwd_{l}")
            d_wg2[l] = dw2[:GLA_RANK]
            dP = jnp.concatenate([dq, dk, dvb, drb, dglb], axis=1)
            gP = _mm(s["hb"], dP, ta=True, out_dtype=BF16, name=f"gla_in_dw_{l}")
            gW[("gla_w_in", l)] = jnp.concatenate(
                [gP[:, :2 * DK + DV], gP[:, 2 * DK + 2 * DV:2 * DK + 2 * DV + GLA_RANK], gP[:, 2 * DK + DV:2 * DK + 2 * DV]],
                axis=1)
            dh = _mm(dP, W_P[l], tb=True, add=dz, add_scale=ALPHA, name=f"gla_in_dx_{l}")
        else:
            j = l - N_A_LAYERS
            dgo = _mm(dzb, full["fox_w_out"][j], tb=True, name=f"fox_out_dx_{j}")
            gW[("fox_w_out", j)] = _mm(s["gated"], dzb, ta=True, out_dtype=BF16, name=f"fox_out_dw_{j}")
            do, dogb = _fox_gate_bwd(dgo, s["o"], s["QO"], name=f"fox_gate_bwd_{j}")
            dqb, dcq = _fox_attn_bwd_dq(s["QO"], kvs["KV"], kvs["ccol"], kvs["crow"], s["lse"], do, s["o"], HF,
                                        name=f"fox_attn_dq_{j}")
            dkv = _fox_attn_bwd_dkv(s["QO"], kvs["KV"], kvs["ccol"], kvs["crow"], s["lse"], do, s["o"], dkv, HF,
                                    name=f"fox_attn_dkv_{j}")
            dcq_sum = dcq if dcq_sum is None else dcq_sum + dcq
            dQO = jnp.concatenate([dqb, dogb], axis=1)
            gW[("fox_w_in", j)] = _mm(s["hb"], dQO, ta=True, out_dtype=BF16, name=f"fox_in_dw_{j}")
            dh = _mm(dQO, full["fox_w_in"][j], tb=True, add=dz, add_scale=ALPHA, name=f"fox_in_dx_{j}")
            if j == 0:
                dc = (dcq_sum[:, :, 0] + dkv[2][:, 0, :]).T
                dc = jnp.pad(dc, ((0, 0), (0, LANE - HF)))
                dfb, d_bf = _fox_c_bwd(dc, kvs["f"], bf_pad, name="fox_c_bwd")
                dKVF = jnp.concatenate([dkv[0].astype(BF16), dkv[1].astype(BF16), dfb], axis=1)
                gkv = _mm(kvs["hb"], dKVF, ta=True, out_dtype=BF16, name="kv_dw")
                gW[("kv_w", 0)] = gkv[:, :2 * D + HF]
                dh = _mm(dKVF, W_kvf, tb=True, add=dh, name="kv_dx")

    stack = lambda k, n: jnp.stack([gW[(k, i)] for i in range(n)])
    gfull = dict(gla_w_in=stack("gla_w_in", N_A_LAYERS), gla_w_out=stack("gla_w_out", N_A_LAYERS), kv_w=gW[("kv_w", 0)],
                 fox_w_in=stack("fox_w_in", DEPTH - N_A_LAYERS), fox_w_out=stack("fox_w_out", DEPTH - N_A_LAYERS),
                 ffn_w_up=stack("ffn_w_up", DEPTH), ffn_w_down=stack("ffn_w_down", DEPTH))
    g4 = [_to_shards(gfull[k], big_axis[k]) for k in big_names]
    recv = _scatter_chips(g4, name="scatter_grads")
    partial = []
    for i, k in enumerate(big_names):
        own = lax.dynamic_index_in_dim(g4[i], chip, axis=0, keepdims=False)
        partial.append(_sum_parts([own, recv[i][0], recv[i][1], recv[i][2]], name=f"sum_chips_{k}"))
    other = _swap_sibling(partial, name="swap_sibling")
    moments = dict(gla_w_in=(m_gla_w_in, v_gla_w_in), gla_w_out=(m_gla_w_out, v_gla_w_out), kv_w=(m_kv_w, v_kv_w),
                   fox_w_in=(m_fox_w_in, v_fox_w_in), fox_w_out=(m_fox_w_out, v_fox_w_out),
                   ffn_w_up=(m_ffn_w_up, v_ffn_w_up), ffn_w_down=(m_ffn_w_down, v_ffn_w_down))
    res = {}
    for i, k in enumerate(big_names):
        w = big[k]
        sh = w.shape
        flat = lambda a: a.reshape(-1, sh[-1])
        outs = _adamw(flat(w), flat(moments[k][0]), flat(moments[k][1]), [partial[i], other[i]], name=f"adamw_{k}")
        res[k] = [o.reshape(sh) for o in outs]

    dmeta = dh[N_PAD:FRONT]
    sg = dict(meta=dmeta,
              ln_g=jnp.stack([jnp.concatenate(d_ln_g[l], axis=0) for l in range(DEPTH)]),
              ln_b=jnp.stack([jnp.concatenate(d_ln_b[l], axis=0) for l in range(DEPTH)]),
              gla_w_g2=jnp.stack(d_wg2), gla_b_g2=jnp.stack([d[0] for d in d_bg2]),
              gla_norm_g=jnp.stack([d[0] for d in d_ng]), ffn_conv_w=jnp.stack(d_cw),
              kv_bf=d_bf[0, :HF], ffn_conv_b=jnp.stack(d_cb), loss=loss_acc[0, :1])
    sg_names = small_names + ["kv_bf", "ffn_conv_b", "loss"]
    sg_shapes = [sg[k].shape for k in sg_names]
    red = _sum_slots(_gather_all(_pack([sg[k] for k in sg_names]), name="gather_small_grads"), name="sum_small_grads")
    red = dict(zip(sg_names, _unpack(red, sg_shapes)))
    loss = red["loss"][0]
    loc = {}
    for k in small_names:
        wdt = small[k].shape[-1]
        loc[k] = lax.dynamic_slice_in_dim(red[k], chip * wdt, wdt, axis=red[k].ndim - 1)
    loc["kv_bf"] = red["kv_bf"]
    loc["ffn_conv_b"] = red["ffn_conv_b"]
    sm_all = small_names + ["kv_bf", "ffn_conv_b"]
    sw = dict(small, kv_bf=kv_bf, ffn_conv_b=ffn_conv_b)
    sm_m = dict(meta=m_meta, ln_g=m_ln_g, ln_b=m_ln_b, gla_w_g2=m_gla_w_g2, gla_b_g2=m_gla_b_g2, gla_norm_g=m_gla_norm_g,
                ffn_conv_w=m_ffn_conv_w, kv_bf=m_kv_bf, ffn_conv_b=m_ffn_conv_b)
    sm_v = dict(meta=v_meta, ln_g=v_ln_g, ln_b=v_ln_b, gla_w_g2=v_gla_w_g2, gla_b_g2=v_gla_b_g2, gla_norm_g=v_gla_norm_g,
                ffn_conv_w=v_ffn_conv_w, kv_bf=v_kv_bf, ffn_conv_b=v_ffn_conv_b)
    shapes_loc = [sw[k].shape for k in sm_all]
    outs = _adamw(_pack([sw[k] for k in sm_all]), _pack([sm_m[k] for k in sm_all]), _pack([sm_v[k] for k in sm_all]),
                  [_pack([loc[k] for k in sm_all])], name="adamw_small")
    outs = [_unpack(o, shapes_loc) for o in outs]
    for i, k in enumerate(sm_all):
        res[k] = [outs[q][i] for q in range(4)]

    order = ["meta", "ln_g", "ln_b", "gla_w_in", "gla_w_g2", "gla_b_g2", "gla_norm_g", "gla_w_out", "kv_w", "kv_bf",
             "fox_w_in", "fox_w_out", "ffn_w_up", "ffn_conv_w", "ffn_conv_b", "ffn_w_down"]
    grad_x = dh[FRONT:][None]
    return (loss, grad_x, *[res[k][0] for k in order], *[res[k][1] for k in order], *[res[k][2] for k in order],
            *[res[k][3] for k in order])
```

```python
import functools

import numpy as np
import jax
import jax.numpy as jnp
from jax import lax
from jax.experimental import pallas as pl
from jax.experimental.pallas import tpu as pltpu

F32 = jnp.float32
BF16 = jnp.bfloat16
HIGHEST = lax.Precision.HIGHEST

DEPTH = 4
N_A_LAYERS = DEPTH // 2
N_META = 16
FRONT = 128
N_PAD = FRONT - N_META
ALPHA = (2.0 * DEPTH) ** 0.25
LN_EPS = 1e-5
GLA_HEADS = 4
GLA_RANK = 16
GLA_TAU = 16.0
GLA_CHUNK = 64
FOX_HD = 128
LANE = 128
ADAM_LR = 0.001
ADAM_B1 = 0.9
ADAM_B2 = 0.999
ADAM_EPS = 1e-08
ADAM_WD = 0.01
ADAM_STEP = 10
NEG = -(2.0 ** 100)
VMEM_LIMIT = 50 * 1024 * 1024
MESH = pl.DeviceIdType.MESH
ANY = pl.BlockSpec(memory_space=pl.ANY)


def _tile(n, pref, align):
    best = None
    t = align
    while t <= min(n, pref):
        if n % t == 0:
            best = t
        t += align
    return best if best is not None else n


def _cp(*sem):
    return pltpu.CompilerParams(dimension_semantics=sem, vmem_limit_bytes=VMEM_LIMIT)


def _dot(a, b, ca, cb, precision=None):
    return lax.dot_general(a, b, (((ca,), (cb,)), ((), ())), precision=precision,
                           preferred_element_type=F32)


def _sigmoid(x):
    return 1.0 / (1.0 + jnp.exp(-x))


def _log_sigmoid(z):
    return jnp.minimum(z, 0.0) - jnp.log(1.0 + jnp.exp(-jnp.abs(z)))


def _rows(i, tr, n=None):
    n = tr if n is None else n
    return i * tr + lax.broadcasted_iota(jnp.int32, (n, 1), 0)


def _mm(a, b, *, ta=False, tb=False, out_dtype=F32, add=None, add_scale=1.0, name):
    if ta:
        K, M = a.shape
    else:
        M, K = a.shape
    if tb:
        N, K2 = b.shape
    else:
        K2, N = b.shape
    assert K == K2, (a.shape, b.shape, ta, tb)
    tm = _tile(M, 1024, LANE) if ta else _tile(M, 1040, 16)
    tn = _tile(N, 1024, LANE)
    tk = _tile(K, 2048, LANE if ((not ta) or tb) else 16)
    nk = K // tk
    ca = 0 if ta else 1
    cb = 1 if tb else 0

    def body(*refs):
        if add is None:
            a_ref, b_ref, o_ref = refs[:3]
            add_ref = None
        else:
            a_ref, b_ref, add_ref, o_ref = refs[:4]
        part = _dot(a_ref[...].astype(BF16), b_ref[...].astype(BF16), ca, cb)

        def finish(acc):
            if add_ref is not None:
                acc = acc + add_scale * add_ref[...]
            o_ref[...] = acc.astype(out_dtype)

        if nk == 1:
            finish(part)
        else:
            acc_ref = refs[-1]
            k = pl.program_id(2)

            @pl.when(k == 0)
            def _():
                acc_ref[...] = part

            @pl.when(k > 0)
            def _():
                acc_ref[...] += part

            @pl.when(k == nk - 1)
            def _():
                finish(acc_ref[...])

    a_spec = pl.BlockSpec((tk, tm), lambda i, j, k: (k, i)) if ta else pl.BlockSpec((tm, tk), lambda i, j, k: (i, k))
    b_spec = pl.BlockSpec((tn, tk), lambda i, j, k: (j, k)) if tb else pl.BlockSpec((tk, tn), lambda i, j, k: (k, j))
    o_spec = pl.BlockSpec((tm, tn), lambda i, j, k: (i, j))
    in_specs = [a_spec, b_spec] + ([o_spec] if add is not None else [])
    args = (a, b) + ((add,) if add is not None else ())
    return pl.pallas_call(
        body, name=name, grid=(M // tm, N // tn, nk), in_specs=in_specs, out_specs=o_spec,
        out_shape=jax.ShapeDtypeStruct((M, N), out_dtype),
        scratch_shapes=[pltpu.VMEM((tm, tn), F32)] if nk > 1 else [],
        compiler_params=_cp("parallel", "parallel", "arbitrary"),
    )(*args)


def _ln_stats(h, mix):
    z = ALPHA * h + mix
    mu = jnp.mean(z, axis=-1, keepdims=True)
    zc = z - mu
    var = jnp.mean(zc * zc, axis=-1, keepdims=True)
    rstd = lax.rsqrt(var + LN_EPS)
    return zc * rstd, rstd


def _ln_fwd(h, mix, g, b, *, name):
    L, D = h.shape
    tr = _tile(L, 160, 16)

    def body(h_ref, m_ref, g_ref, b_ref, o_ref, ob_ref):
        xhat, _ = _ln_stats(h_ref[...], m_ref[...])
        y = xhat * g_ref[...] + b_ref[...]
        o_ref[...] = y
        ob_ref[...] = y.astype(BF16)

    row = pl.BlockSpec((tr, D), lambda i: (i, 0))
    vec = pl.BlockSpec((1, D), lambda i: (0, 0))
    return pl.pallas_call(
        body, name=name, grid=(L // tr,), in_specs=[row, row, vec, vec], out_specs=[row, row],
        out_shape=[jax.ShapeDtypeStruct((L, D), F32), jax.ShapeDtypeStruct((L, D), BF16)],
        compiler_params=_cp("parallel"),
    )(h, mix, g, b)


def _ln_bwd(dy, h, mix, g, *, name):
    L, D = h.shape
    tr = _tile(L, 160, 16)

    def body(dy_ref, h_ref, m_ref, g_ref, dz_ref, dzb_ref, dg_ref, db_ref):
        i = pl.program_id(0)
        xhat, rstd = _ln_stats(h_ref[...], m_ref[...])
        dy = dy_ref[...]
        dxh = dy * g_ref[...]
        m1 = jnp.mean(dxh, axis=-1, keepdims=True)
        m2 = jnp.mean(dxh * xhat, axis=-1, keepdims=True)
        dz = rstd * (dxh - m1 - xhat * m2)
        dz_ref[...] = dz
        dzb_ref[...] = dz.astype(BF16)
        pg = jnp.sum(dy * xhat, axis=0, keepdims=True)
        pb = jnp.sum(dy, axis=0, keepdims=True)

        @pl.when(i == 0)
        def _():
            dg_ref[...] = pg
            db_ref[...] = pb

        @pl.when(i > 0)
        def _():
            dg_ref[...] += pg
            db_ref[...] += pb

    row = pl.BlockSpec((tr, D), lambda i: (i, 0))
    vec = pl.BlockSpec((1, D), lambda i: (0, 0))
    return pl.pallas_call(
        body, name=name, grid=(L // tr,), in_specs=[row, row, row, vec], out_specs=[row, row, vec, vec],
        out_shape=[jax.ShapeDtypeStruct((L, D), F32), jax.ShapeDtypeStruct((L, D), BF16),
                   jax.ShapeDtypeStruct((1, D), F32), jax.ShapeDtypeStruct((1, D), F32)],
        compiler_params=_cp("arbitrary"),
    )(dy, h, mix, g)


def _loss(h, target, *, name):
    L, D = h.shape
    tr = FRONT

    def body(h_ref, t_ref, acc_ref, dy_ref):
        i = pl.program_id(0)
        e = jnp.where(i >= 1, h_ref[...] - t_ref[...], 0.0)
        dy_ref[...] = e * (1.0 / D)
        part = 0.5 * jnp.sum(jnp.sum(e * e, axis=-1, keepdims=True) * (1.0 / D), axis=0, keepdims=True)

        @pl.when(i == 0)
        def _():
            acc_ref[...] = jnp.zeros_like(acc_ref)

        acc_ref[...] += jnp.broadcast_to(part, acc_ref.shape)

    return pl.pallas_call(
        body, name=name, grid=(L // tr,),
        in_specs=[pl.BlockSpec((tr, D), lambda i: (i, 0)),
                  pl.BlockSpec((tr, D), lambda i: (jnp.maximum(i - 1, 0), 0))],
        out_specs=[pl.BlockSpec((8, LANE), lambda i: (0, 0)), pl.BlockSpec((tr, D), lambda i: (i, 0))],
        out_shape=[jax.ShapeDtypeStruct((8, LANE), F32), jax.ShapeDtypeStruct((L, D), F32)],
        compiler_params=_cp("arbitrary"),
    )(h, target)


def _gla_gate_fwd(P, w2p, b2, gl_blk, *, name):
    L = P.shape[0]
    DK = w2p.shape[1]
    tr = _tile(L, 640, 16)

    def body(gl_ref, w_ref, b_ref, o_ref):
        i = pl.program_id(0)
        z = _dot(gl_ref[...].astype(BF16), w_ref[...], 1, 0) + b_ref[...]
        la = _log_sigmoid(z) * (1.0 / GLA_TAU)
        o_ref[...] = jnp.where(_rows(i, tr) >= N_PAD, la, 0.0)

    return pl.pallas_call(
        body, name=name, grid=(L // tr,),
        in_specs=[pl.BlockSpec((tr, LANE), lambda i: (i, gl_blk)),
                  pl.BlockSpec((LANE, DK), lambda i: (0, 0)), pl.BlockSpec((1, DK), lambda i: (0, 0))],
        out_specs=pl.BlockSpec((tr, DK), lambda i: (i, 0)),
        out_shape=jax.ShapeDtypeStruct((L, DK), F32), compiler_params=_cp("parallel"),
    )(P, w2p, b2)


def _gla_gate_bwd(dla, P, w2p, b2, gl_blk, *, name):
    L = P.shape[0]
    DK = w2p.shape[1]
    tr = _tile(L, 640, 16)

    def body(dla_ref, gl_ref, w_ref, b_ref, dgl_ref, dw_ref, db_ref):
        i = pl.program_id(0)
        glb = gl_ref[...].astype(BF16)
        z = _dot(glb, w_ref[...], 1, 0) + b_ref[...]
        dz = jnp.where(_rows(i, tr) >= N_PAD, dla_ref[...], 0.0) * (1.0 / GLA_TAU) * _sigmoid(-z)
        dzb = dz.astype(BF16)
        dgl_ref[...] = _dot(dzb, w_ref[...], 1, 1).astype(BF16)
        pw = _dot(glb, dzb, 0, 0)
        pb = jnp.sum(dz, axis=0, keepdims=True)

        @pl.when(i == 0)
        def _():
            dw_ref[...] = pw
            db_ref[...] = pb

        @pl.when(i > 0)
        def _():
            dw_ref[...] += pw
            db_ref[...] += pb

    return pl.pallas_call(
        body, name=name, grid=(L // tr,),
        in_specs=[pl.BlockSpec((tr, DK), lambda i: (i, 0)), pl.BlockSpec((tr, LANE), lambda i: (i, gl_blk)),
                  pl.BlockSpec((LANE, DK), lambda i: (0, 0)), pl.BlockSpec((1, DK), lambda i: (0, 0))],
        out_specs=[pl.BlockSpec((tr, LANE), lambda i: (i, 0)), pl.BlockSpec((LANE, DK), lambda i: (0, 0)),
                   pl.BlockSpec((1, DK), lambda i: (0, 0))],
        out_shape=[jax.ShapeDtypeStruct((L, LANE), BF16), jax.ShapeDtypeStruct((LANE, DK), F32),
                   jax.ShapeDtypeStruct((1, DK), F32)],
        compiler_params=_cp("arbitrary"),
    )(dla, P, w2p, b2)


def _chunk_terms(q, k, g, n, scale, HV):
    C = q.shape[0]
    ri = lax.broadcasted_iota(jnp.int32, (C, C), 0)
    ci = lax.broadcasted_iota(jnp.int32, (C, C), 1)
    tri = ri >= ci
    valid = _rows(n, C) >= N_PAD
    km = jnp.where(valid, k, 0.0)
    b = _dot(tri.astype(F32), g, 1, 0, precision=HIGHEST)
    bl_row = jnp.sum(g, axis=0, keepdims=True)
    bl_col = _dot(g, jnp.ones((C, HV), F32), 0, 0, precision=HIGHEST)
    eb = jnp.exp(b)
    enb = jnp.exp(-b)
    qe = q * scale * eb
    ke = km * enb
    ebl_row = jnp.exp(bl_row)
    kl = ke * ebl_row
    return dict(tri=tri, valid=valid, eb=eb, enb=enb, qe=qe, ke=ke, kl=kl, ebl_row=ebl_row,
                ebl_col=jnp.exp(bl_col), ri=ri, ci=ci)


def _gla_chunk_fwd(P, la, H, HK, HV, *, name):
    L = P.shape[0]
    C = GLA_CHUNK
    N = L // C
    scale = HK ** -0.5

    def body(q_ref, k_ref, v_ref, g_ref, o_ref, s_ref, S):
        n = pl.program_id(1)

        @pl.when(n == 0)
        def _():
            S[...] = jnp.zeros_like(S)

        S0 = S[...]
        s_ref[0, 0] = S0
        t = _chunk_terms(q_ref[...], k_ref[...], g_ref[...], n, scale, HV)
        vb = v_ref[...].astype(BF16)
        qeb = t["qe"].astype(BF16)
        inter = _dot(qeb, S0.astype(BF16), 1, 0)
        att = jnp.where(t["tri"], _dot(qeb, t["ke"].astype(BF16), 1, 1), 0.0)
        o_ref[...] = inter + _dot(att.astype(BF16), vb, 1, 0)
        S[...] = t["ebl_col"] * S0 + _dot(t["kl"].astype(BF16), vb, 0, 0)

    return pl.pallas_call(
        body, name=name, grid=(H, N),
        in_specs=[pl.BlockSpec((C, HK), lambda h, n: (n, h)), pl.BlockSpec((C, HK), lambda h, n: (n, H + h)),
                  pl.BlockSpec((C, HV), lambda h, n: (n, H + h)), pl.BlockSpec((C, HK), lambda h, n: (n, h))],
        out_specs=[pl.BlockSpec((C, HV), lambda h, n: (n, h)),
                   pl.BlockSpec((1, 1, HK, HV), lambda h, n: (h, n, 0, 0))],
        out_shape=[jax.ShapeDtypeStruct((L, H * HV), F32), jax.ShapeDtypeStruct((H, N, HK, HV), F32)],
        scratch_shapes=[pltpu.VMEM((HK, HV), F32)],
        compiler_params=_cp("parallel", "arbitrary"),
    )(P, P, P, la)


def _gla_chunk_bwd(P, la, S_all, do, H, HK, HV, *, name):
    L = P.shape[0]
    C = GLA_CHUNK
    N = L // C
    scale = HK ** -0.5

    def body(q_ref, k_ref, v_ref, g_ref, s_ref, do_ref, dq_ref, dk_ref, dv_ref, dg_ref, dS):
        step = pl.program_id(1)
        n = N - 1 - step

        @pl.when(step == 0)
        def _():
            dS[...] = jnp.zeros_like(dS)

        dS1 = dS[...]
        S0 = s_ref[0, 0]
        t = _chunk_terms(q_ref[...], k_ref[...], g_ref[...], n, scale, HV)
        tri, qe, ke, kl = t["tri"], t["qe"], t["ke"], t["kl"]
        vb = v_ref[...].astype(BF16)
        dob = do_ref[...].astype(BF16)
        qeb, keb, dSb = qe.astype(BF16), ke.astype(BF16), dS1.astype(BF16)
        dA = jnp.where(tri, _dot(dob, vb, 1, 1), 0.0).astype(BF16)
        A = jnp.where(tri, _dot(qeb, keb, 1, 1), 0.0).astype(BF16)
        dqe = _dot(dob, S0.astype(BF16), 1, 1) + _dot(dA, keb, 1, 0)
        dkl = _dot(vb, dSb, 1, 1)
        dke = _dot(dA, qeb, 0, 0) + dkl * t["ebl_row"]
        dv_ref[...] = (_dot(A, dob, 0, 0) + _dot(kl.astype(BF16), dSb, 1, 0)).astype(BF16)
        debl = (jnp.sum(_dot(jnp.ones((8, HV), F32), dS1 * S0, 1, 1, precision=HIGHEST), axis=0, keepdims=True) * 0.125
                + jnp.sum(dkl * ke, axis=0, keepdims=True))
        dbl = debl * t["ebl_row"]
        db = dqe * qe - dke * ke + jnp.where(lax.broadcasted_iota(jnp.int32, (C, 1), 0) == C - 1, dbl, 0.0)
        triu = (t["ci"] >= t["ri"]).astype(F32)
        dg = _dot(triu, db, 1, 0, precision=HIGHEST)
        dq_ref[...] = (dqe * t["eb"] * scale).astype(BF16)
        dk_ref[...] = jnp.where(t["valid"], dke * t["enb"], 0.0).astype(BF16)
        dg_ref[...] = dg
        dS[...] = t["ebl_col"] * dS1 + _dot(qeb, dob, 0, 0)

    rev = lambda h, s: (N - 1 - s, h)
    return pl.pallas_call(
        body, name=name, grid=(H, N),
        in_specs=[pl.BlockSpec((C, HK), rev), pl.BlockSpec((C, HK), lambda h, s: (N - 1 - s, H + h)),
                  pl.BlockSpec((C, HV), lambda h, s: (N - 1 - s, H + h)), pl.BlockSpec((C, HK), rev),
                  pl.BlockSpec((1, 1, HK, HV), lambda h, s: (h, N - 1 - s, 0, 0)), pl.BlockSpec((C, HV), rev)],
        out_specs=[pl.BlockSpec((C, HK), rev), pl.BlockSpec((C, HK), rev),
                   pl.BlockSpec((C, HV), rev), pl.BlockSpec((C, HK), rev)],
        out_shape=[jax.ShapeDtypeStruct((L, H * HK), BF16), jax.ShapeDtypeStruct((L, H * HK), BF16),
                   jax.ShapeDtypeStruct((L, H * HV), BF16), jax.ShapeDtypeStruct((L, H * HK), F32)],
        scratch_shapes=[pltpu.VMEM((HK, HV), F32)],
        compiler_params=_cp("parallel", "arbitrary"),
    )(P, P, P, la, S_all, do)


def _silu_parts(x):
    s = _sigmoid(x)
    return x * s, s * (1.0 + x * (1.0 - s))


def _gla_post_fwd(o, P, ng, H, HV, r_blk, *, name):
    L, DV = o.shape
    tr = _tile(L, 320, 16)

    def body(o_ref, r_ref, g_ref, out_ref):
        for hd in range(H):
            sl = slice(hd * HV, (hd + 1) * HV)
            oh = o_ref[:, sl]
            rr = lax.rsqrt(jnp.mean(oh * oh, axis=-1, keepdims=True) + LN_EPS)
            silu, _ = _silu_parts(r_ref[:, sl])
            out_ref[:, sl] = (oh * rr * g_ref[...] * silu).astype(BF16)

    return pl.pallas_call(
        body, name=name, grid=(L // tr,),
        in_specs=[pl.BlockSpec((tr, DV), lambda i: (i, 0)), pl.BlockSpec((tr, DV), lambda i: (i, r_blk)),
                  pl.BlockSpec((1, HV), lambda i: (0, 0))],
        out_specs=pl.BlockSpec((tr, DV), lambda i: (i, 0)),
        out_shape=jax.ShapeDtypeStruct((L, DV), BF16), compiler_params=_cp("parallel"),
    )(o, P, ng)


def _gla_post_bwd(dgated, o, P, ng, H, HV, r_blk, *, name):
    L, DV = o.shape
    tr = _tile(L, 320, 16)

    def body(d_ref, o_ref, r_ref, g_ref, do_ref, dr_ref, dng_ref):
        i = pl.program_id(0)
        png = jnp.zeros((1, HV), F32)
        for hd in range(H):
            sl = slice(hd * HV, (hd + 1) * HV)
            oh = o_ref[:, sl]
            d = d_ref[:, sl]
            rr = lax.rsqrt(jnp.mean(oh * oh, axis=-1, keepdims=True) + LN_EPS)
            yh = oh * rr
            silu, dsilu = _silu_parts(r_ref[:, sl])
            dn = d * silu
            dr_ref[:, sl] = (d * yh * g_ref[...] * dsilu).astype(BF16)
            png = png + jnp.sum(dn * yh, axis=0, keepdims=True)
            dyh = dn * g_ref[...]
            do_ref[:, sl] = rr * (dyh - yh * jnp.mean(dyh * yh, axis=-1, keepdims=True))

        @pl.when(i == 0)
        def _():
            dng_ref[...] = png

        @pl.when(i > 0)
        def _():
            dng_ref[...] += png

    row = pl.BlockSpec((tr, DV), lambda i: (i, 0))
    return pl.pallas_call(
        body, name=name, grid=(L // tr,),
        in_specs=[row, row, pl.BlockSpec((tr, DV), lambda i: (i, r_blk)), pl.BlockSpec((1, HV), lambda i: (0, 0))],
        out_specs=[row, row, pl.BlockSpec((1, HV), lambda i: (0, 0))],
        out_shape=[jax.ShapeDtypeStruct((L, DV), F32), jax.ShapeDtypeStruct((L, DV), BF16),
                   jax.ShapeDtypeStruct((1, HV), F32)],
        compiler_params=_cp("arbitrary"),
    )(dgated, o, P, ng)


def _shift_down(x, halo, s):
    if s == 0:
        return x
    tr = x.shape[0]
    xx = jnp.concatenate([halo, x], axis=0)
    return pltpu.roll(xx, s, axis=0)[8:8 + tr]


def _shift_up(x, halo, s):
    if s == 0:
        return x
    tr = x.shape[0]
    xx = jnp.concatenate([x, halo], axis=0)
    return pltpu.roll(xx, tr + 8 - s, axis=0)[0:tr]


def _conv_taps(x_ref, halo_ref, i, tr):
    x = jnp.where(_rows(i, tr) >= N_PAD, x_ref[...], 0.0)
    halo = jnp.where(i * tr - 8 + lax.broadcasted_iota(jnp.int32, (8, 1), 0) >= N_PAD, halo_ref[...], 0.0)
    return [_shift_down(x, halo, s) for s in range(3)]


def _conv_apply(taps, w_ref, b_ref):
    return taps[2] * w_ref[0:1, :] + taps[1] * w_ref[1:2, :] + taps[0] * w_ref[2:3, :] + b_ref[...]


def _conv_specs(tr, tc):
    blk = pl.BlockSpec((tr, tc), lambda j, i: (i, j))
    halo = pl.BlockSpec((8, tc), lambda j, i: (jnp.maximum(i * (tr // 8) - 1, 0), j))
    w = pl.BlockSpec((3, tc), lambda j, i: (0, j))
    b = pl.BlockSpec((1, tc), lambda j, i: (0, j))
    return blk, halo, w, b


def _conv_act_fwd(Uu, Ug, wu, wg, bu, bg, *, name):
    L, DFF = Uu.shape
    tr = _tile(L, 320, 16)
    tc = _tile(DFF, 512, LANE)

    def body(xu_ref, hu_ref, xg_ref, hg_ref, wu_ref, wg_ref, bu_ref, bg_ref, o_ref):
        i = pl.program_id(1)
        u = _conv_apply(_conv_taps(xu_ref, hu_ref, i, tr), wu_ref, bu_ref)
        g = _conv_apply(_conv_taps(xg_ref, hg_ref, i, tr), wg_ref, bg_ref)
        o_ref[...] = (_silu_parts(g)[0] * u).astype(BF16)

    blk, halo, w, b = _conv_specs(tr, tc)
    return pl.pallas_call(
        body, name=name, grid=(DFF // tc, L // tr),
        in_specs=[blk, halo, blk, halo, w, w, b, b], out_specs=blk,
        out_shape=jax.ShapeDtypeStruct((L, DFF), BF16), compiler_params=_cp("parallel", "parallel"),
    )(Uu, Uu, Ug, Ug, wu, wg, bu, bg)


def _conv_act_bwd(Uu, Ug, wu, wg, bu, bg, dA, *, name):
    L, DFF = Uu.shape
    tr = _tile(L, 320, 16)
    tc = _tile(DFF, 512, LANE)

    def body(xu_ref, hu_ref, xg_ref, hg_ref, wu_ref, wg_ref, bu_ref, bg_ref, da_ref,
             du_ref, dg_ref, dwu_ref, dwg_ref, dbu_ref, dbg_ref):
        i = pl.program_id(1)
        tu = _conv_taps(xu_ref, hu_ref, i, tr)
        tg = _conv_taps(xg_ref, hg_ref, i, tr)
        u = _conv_apply(tu, wu_ref, bu_ref)
        g = _conv_apply(tg, wg_ref, bg_ref)
        silu, dsilu = _silu_parts(g)
        da = da_ref[...]
        du = da * silu
        dg = da * u * dsilu
        du_ref[...] = du
        dg_ref[...] = dg

        @pl.when(i == 0)
        def _():
            dwu_ref[...] = jnp.zeros_like(dwu_ref)
            dwg_ref[...] = jnp.zeros_like(dwg_ref)
            dbu_ref[...] = jnp.zeros_like(dbu_ref)
            dbg_ref[...] = jnp.zeros_like(dbg_ref)

        for j in range(3):
            dwu_ref[j:j + 1, :] += jnp.sum(du * tu[2 - j], axis=0, keepdims=True)
            dwg_ref[j:j + 1, :] += jnp.sum(dg * tg[2 - j], axis=0, keepdims=True)
        dbu_ref[...] += jnp.sum(du, axis=0, keepdims=True)
        dbg_ref[...] += jnp.sum(dg, axis=0, keepdims=True)

    blk, halo, w, b = _conv_specs(tr, tc)
    return pl.pallas_call(
        body, name=name, grid=(DFF // tc, L // tr),
        in_specs=[blk, halo, blk, halo, w, w, b, b, blk], out_specs=[blk, blk, w, w, b, b],
        out_shape=[jax.ShapeDtypeStruct((L, DFF), F32), jax.ShapeDtypeStruct((L, DFF), F32),
                   jax.ShapeDtypeStruct((3, DFF), F32), jax.ShapeDtypeStruct((3, DFF), F32),
                   jax.ShapeDtypeStruct((1, DFF), F32), jax.ShapeDtypeStruct((1, DFF), F32)],
        compiler_params=_cp("parallel", "arbitrary"),
    )(Uu, Uu, Ug, Ug, wu, wg, bu, bg, dA)


def _conv_in_bwd(dh, w, *, name):
    L, DFF = dh.shape
    tr = _tile(L, 320, 16)
    tc = _tile(DFF, 512, LANE)
    nb8 = L // 8

    def body(x_ref, halo_ref, w_ref, o_ref):
        i = pl.program_id(1)
        x = x_ref[...]
        halo = jnp.where((i + 1) * tr + lax.broadcasted_iota(jnp.int32, (8, 1), 0) < L, halo_ref[...], 0.0)
        d = (x * w_ref[2:3, :] + _shift_up(x, halo, 1) * w_ref[1:2, :] + _shift_up(x, halo, 2) * w_ref[0:1, :])
        o_ref[...] = jnp.where(_rows(i, tr) >= N_PAD, d, 0.0).astype(BF16)

    blk = pl.BlockSpec((tr, tc), lambda j, i: (i, j))
    halo = pl.BlockSpec((8, tc), lambda j, i: (jnp.minimum((i + 1) * (tr // 8), nb8 - 1), j))
    return pl.pallas_call(
        body, name=name, grid=(DFF // tc, L // tr),
        in_specs=[blk, halo, pl.BlockSpec((3, tc), lambda j, i: (0, j))], out_specs=blk,
        out_shape=jax.ShapeDtypeStruct((L, DFF), BF16), compiler_params=_cp("parallel", "parallel"),
    )(dh, dh, w)


def _fox_c_fwd(f, bf, *, name):
    L = f.shape[0]
    tr = _tile(L, 320, 16)

    def body(f_ref, b_ref, c_ref, carry):
        i = pl.program_id(0)

        @pl.when(i == 0)
        def _():
            carry[...] = jnp.zeros_like(carry)

        lf = jnp.where(_rows(i, tr) >= N_PAD, _log_sigmoid(f_ref[...] + b_ref[...]), 0.0)
        tri = (lax.broadcasted_iota(jnp.int32, (tr, tr), 0) >= lax.broadcasted_iota(jnp.int32, (tr, tr), 1)).astype(F32)
        c_ref[...] = _dot(tri, lf, 1, 0, precision=HIGHEST) + carry[...]
        carry[...] += jnp.sum(lf, axis=0, keepdims=True)

    return pl.pallas_call(
        body, name=name, grid=(L // tr,),
        in_specs=[pl.BlockSpec((tr, LANE), lambda i: (i, 0)), pl.BlockSpec((1, LANE), lambda i: (0, 0))],
        out_specs=pl.BlockSpec((tr, LANE), lambda i: (i, 0)),
        out_shape=jax.ShapeDtypeStruct((L, LANE), F32), scratch_shapes=[pltpu.VMEM((1, LANE), F32)],
        compiler_params=_cp("arbitrary"),
    )(f, bf)


def _fox_c_bwd(dcs, f, bf, *, name):
    L = f.shape[0]
    tr = _tile(L, 320, 16)
    nb = L // tr
    nd = len(dcs)

    def body(*refs):
        f_ref, b_ref, df_ref, db_ref, carry = refs[nd:]
        s = pl.program_id(0)
        i = nb - 1 - s

        @pl.when(s == 0)
        def _():
            carry[...] = jnp.zeros_like(carry)
            db_ref[...] = jnp.zeros_like(db_ref)

        dc = refs[0][...]
        for r in refs[1:nd]:
            dc = dc + r[...]
        triu = (lax.broadcasted_iota(jnp.int32, (tr, tr), 1) >= lax.broadcasted_iota(jnp.int32, (tr, tr), 0)).astype(F32)
        dlf = _dot(triu, dc, 1, 0, precision=HIGHEST) + carry[...]
        carry[...] += jnp.sum(dc, axis=0, keepdims=True)
        df = jnp.where(_rows(i, tr) >= N_PAD, dlf, 0.0) * _sigmoid(-(f_ref[...] + b_ref[...]))
        df_ref[...] = df.astype(BF16)
        db_ref[...] += jnp.sum(df, axis=0, keepdims=True)

    rev = pl.BlockSpec((tr, LANE), lambda s: (nb - 1 - s, 0))
    vec = pl.BlockSpec((1, LANE), lambda s: (0, 0))
    return pl.pallas_call(
        body, name=name, grid=(nb,), in_specs=[rev] * (nd + 1) + [vec], out_specs=[rev, vec],
        out_shape=[jax.ShapeDtypeStruct((L, LANE), BF16), jax.ShapeDtypeStruct((1, LANE), F32)],
        scratch_shapes=[pltpu.VMEM((1, LANE), F32)], compiler_params=_cp("arbitrary"),
    )(*dcs, f, bf)


AUG = 2 * FOX_HD


def _split3(x):
    hi = x.astype(BF16).astype(F32)
    r = x - hi
    mid = r.astype(BF16).astype(F32)
    return hi, mid, (r - mid).astype(BF16).astype(F32)


def _aug_lanes(n, vals):
    lane = lax.broadcasted_iota(jnp.int32, (n, FOX_HD), 1)
    out = jnp.zeros((n, FOX_HD), F32)
    for j, v in enumerate(vals):
        out = jnp.where(lane == j, v, out)
    return out


def _lane_col(x, j):
    lane = lax.broadcasted_iota(jnp.int32, x.shape, 1)
    return jnp.sum(jnp.where(lane == j, x, 0.0), axis=-1, keepdims=True)


def _fox_prep_q(QO, c, H, *, name):
    L = QO.shape[0]
    hd = FOX_HD
    tr = _tile(L, 320, 16)

    def body(q_ref, c_ref, o_ref):
        c = c_ref[...]
        for h in range(H):
            hi, mid, lo = _split3(_lane_col(c, h))
            o_ref[:, h * AUG:h * AUG + hd] = (q_ref[:, h * hd:(h + 1) * hd] * (hd ** -0.5)).astype(BF16)
            o_ref[:, h * AUG + hd:(h + 1) * AUG] = _aug_lanes(tr, [hi, mid, lo, 1.0, 1.0, 1.0]).astype(BF16)

    return pl.pallas_call(
        body, name=name, grid=(L // tr,),
        in_specs=[pl.BlockSpec((tr, H * hd), lambda i: (i, 0)), pl.BlockSpec((tr, LANE), lambda i: (i, 0))],
        out_specs=pl.BlockSpec((tr, H * AUG), lambda i: (i, 0)),
        out_shape=jax.ShapeDtypeStruct((L, H * AUG), BF16), compiler_params=_cp("parallel"),
    )(QO, c)


def _fox_prep_kv(KV, c, H, *, name):
    L = KV.shape[0]
    hd = FOX_HD
    tr = _tile(L, 320, 16)

    def body(k_ref, v_ref, c_ref, ko_ref, vo_ref):
        i = pl.program_id(0)
        c = c_ref[...]
        pad = _rows(i, tr) < N_PAD
        for h in range(H):
            hi, mid, lo = _split3(_lane_col(c, h))
            aug = _aug_lanes(tr, [1.0, 1.0, 1.0, jnp.where(pad, NEG, -hi), jnp.where(pad, 0.0, -mid),
                                  jnp.where(pad, 0.0, -lo)])
            ko_ref[:, h * AUG:h * AUG + hd] = k_ref[:, h * hd:(h + 1) * hd]
            ko_ref[:, h * AUG + hd:(h + 1) * AUG] = aug.astype(BF16)
            vo_ref[:, h * AUG:h * AUG + hd] = v_ref[:, h * hd:(h + 1) * hd]
            vo_ref[:, h * AUG + hd:(h + 1) * AUG] = jnp.ones((tr, hd), BF16)

    wide = pl.BlockSpec((tr, H * AUG), lambda i: (i, 0))
    return pl.pallas_call(
        body, name=name, grid=(L // tr,),
        in_specs=[pl.BlockSpec((tr, H * hd), lambda i: (i, 0)), pl.BlockSpec((tr, H * hd), lambda i: (i, 1)),
                  pl.BlockSpec((tr, LANE), lambda i: (i, 0))],
        out_specs=[wide, wide],
        out_shape=[jax.ShapeDtypeStruct((L, H * AUG), BF16)] * 2, compiler_params=_cp("parallel"),
    )(KV, KV, c)


def _fox_mask(qi, kj, t):
    ti = qi * t + lax.broadcasted_iota(jnp.int32, (t, t), 0)
    si = kj * t + lax.broadcasted_iota(jnp.int32, (t, t), 1)
    return (si <= ti) & ((si >= N_PAD) | (si == ti))


def _fox_attn_fwd(QA, KA, VA, H, *, name):
    L = QA.shape[0]
    hd = FOX_HD
    t = _tile(L, 640, LANE)
    nb = L // t

    def body(q_ref, k_ref, v_ref, o_ref, lse_ref, m_s, acc):
        qi, kj = pl.program_id(1), pl.program_id(2)

        @pl.when(kj == 0)
        def _():
            m_s[...] = jnp.full_like(m_s, NEG)
            acc[...] = jnp.zeros_like(acc)

        def step(masked):
            s = _dot(q_ref[...], k_ref[...], 1, 1)
            if masked:
                mask = _fox_mask(qi, kj, t)
                s = jnp.where(mask, s, NEG)
            m_new = jnp.maximum(m_s[...], jnp.max(s, axis=-1, keepdims=True))
            p = jnp.exp(s - m_new)
            if masked:
                p = jnp.where(mask, p, 0.0)
            alpha = jnp.exp(m_s[...] - m_new)
            acc[...] = alpha * acc[...] + _dot(p.astype(BF16), v_ref[...], 1, 0)
            m_s[...] = m_new

        @pl.when(kj < qi)
        def _():
            step(False)

        @pl.when(kj == qi)
        def _():
            step(True)

        @pl.when(kj == nb - 1)
        def _():
            a = acc[...]
            l = a[:, hd:]
            o_ref[...] = a[:, :hd] / l
            lse_ref[0] = m_s[...] + jnp.log(jnp.max(l, axis=-1, keepdims=True))

    kmap = lambda h, qi, kj: (jnp.minimum(kj, qi), h)
    return pl.pallas_call(
        body, name=name, grid=(H, nb, nb),
        in_specs=[pl.BlockSpec((t, AUG), lambda h, qi, kj: (qi, h)), pl.BlockSpec((t, AUG), kmap),
                  pl.BlockSpec((t, AUG), kmap)],
        out_specs=[pl.BlockSpec((t, hd), lambda h, qi, kj: (qi, h)),
                   pl.BlockSpec((1, t, 1), lambda h, qi, kj: (h, qi, 0))],
        out_shape=[jax.ShapeDtypeStruct((L, H * hd), F32), jax.ShapeDtypeStruct((H, L, 1), F32)],
        scratch_shapes=[pltpu.VMEM((t, 1), F32), pltpu.VMEM((t, AUG), F32)],
        compiler_params=_cp("parallel", "parallel", "arbitrary"),
    )(QA, KA, VA)


def _fox_attn_bwd(QA, KA, VA, DOA, lse, init, H, *, name):
    L = QA.shape[0]
    t = _tile(L, 640, LANE)
    nb = L // t

    def body(q_ref, k_ref, v_ref, do_ref, lse_ref, dk0_ref, dv0_ref, dq_ref, dk_ref, dv_ref):
        kj, qi = pl.program_id(1), pl.program_id(2)

        @pl.when((kj == 0) & (qi == 0))
        def _():
            dq_ref[...] = jnp.zeros_like(dq_ref)

        @pl.when(qi == 0)
        def _():
            dk_ref[...] = dk0_ref[...]
            dv_ref[...] = dv0_ref[...]

        def step(masked):
            q, k, doa = q_ref[...], k_ref[...], do_ref[...]
            p = jnp.exp(_dot(q, k, 1, 1) - lse_ref[0])
            if masked:
                p = jnp.where(_fox_mask(qi, kj, t), p, 0.0)
            pb = p.astype(BF16)
            ds = (p * _dot(doa, v_ref[...], 1, 1)).astype(BF16)
            dv_ref[...] += _dot(pb, doa, 0, 0)
            dk_ref[...] += _dot(ds, q, 0, 0)
            rows = pl.ds(pl.multiple_of(qi * t, t), t)
            dq_ref[rows, :] += _dot(ds, k, 1, 0)

        @pl.when(qi > kj)
        def _():
            step(False)

        @pl.when(qi == kj)
        def _():
            step(True)

    qb = pl.BlockSpec((t, AUG), lambda h, kj, qi: (jnp.maximum(qi, kj), h))
    kb = pl.BlockSpec((t, AUG), lambda h, kj, qi: (kj, h))
    return pl.pallas_call(
        body, name=name, grid=(H, nb, nb),
        in_specs=[qb, kb, kb, qb, pl.BlockSpec((1, t, 1), lambda h, kj, qi: (h, jnp.maximum(qi, kj), 0)), kb, kb],
        out_specs=[pl.BlockSpec((L, AUG), lambda h, kj, qi: (0, h)), kb, kb],
        out_shape=[jax.ShapeDtypeStruct((L, H * AUG), F32)] * 3,
        compiler_params=_cp("parallel", "arbitrary", "arbitrary"),
    )(QA, KA, VA, DOA, lse, *init)


def _fox_post_q(DQA, H, *, name):
    L = DQA.shape[0]
    hd = FOX_HD
    tr = _tile(L, 320, 16)

    def body(x_ref, dq_ref, dc_ref):
        lane = lax.broadcasted_iota(jnp.int32, (tr, LANE), 1)
        dc = jnp.zeros((tr, LANE), F32)
        for h in range(H):
            dq_ref[:, h * hd:(h + 1) * hd] = (x_ref[:, h * AUG:h * AUG + hd] * (hd ** -0.5)).astype(BF16)
            dc = jnp.where(lane == h, _lane_col(x_ref[:, h * AUG + hd:(h + 1) * AUG], 0), dc)
        dc_ref[...] = dc

    return pl.pallas_call(
        body, name=name, grid=(L // tr,), in_specs=[pl.BlockSpec((tr, H * AUG), lambda i: (i, 0))],
        out_specs=[pl.BlockSpec((tr, H * hd), lambda i: (i, 0)), pl.BlockSpec((tr, LANE), lambda i: (i, 0))],
        out_shape=[jax.ShapeDtypeStruct((L, H * hd), BF16), jax.ShapeDtypeStruct((L, LANE), F32)],
        compiler_params=_cp("parallel"),
    )(DQA)


def _fox_post_kv(DKA, DVA, H, *, name):
    L = DKA.shape[0]
    hd = FOX_HD
    tr = _tile(L, 320, 16)

    def body(k_ref, v_ref, o_ref, dc_ref):
        lane = lax.broadcasted_iota(jnp.int32, (tr, LANE), 1)
        dc = jnp.zeros((tr, LANE), F32)
        for h in range(H):
            o_ref[:, h * hd:(h + 1) * hd] = k_ref[:, h * AUG:h * AUG + hd].astype(BF16)
            o_ref[:, (H + h) * hd:(H + h + 1) * hd] = v_ref[:, h * AUG:h * AUG + hd].astype(BF16)
            dc = jnp.where(lane == h, -_lane_col(k_ref[:, h * AUG + hd:(h + 1) * AUG], 3), dc)
        dc_ref[...] = dc

    wide = pl.BlockSpec((tr, H * AUG), lambda i: (i, 0))
    return pl.pallas_call(
        body, name=name, grid=(L // tr,), in_specs=[wide, wide],
        out_specs=[pl.BlockSpec((tr, 2 * H * hd), lambda i: (i, 0)), pl.BlockSpec((tr, LANE), lambda i: (i, 0))],
        out_shape=[jax.ShapeDtypeStruct((L, 2 * H * hd), BF16), jax.ShapeDtypeStruct((L, LANE), F32)],
        compiler_params=_cp("parallel"),
    )(DKA, DVA)


def _fox_gate_fwd(o, QO, *, name):
    L, D = o.shape
    tr = _tile(L, 320, 16)

    def body(o_ref, g_ref, out_ref):
        out_ref[...] = (o_ref[...] * _sigmoid(g_ref[...])).astype(BF16)

    row = pl.BlockSpec((tr, D), lambda i: (i, 0))
    return pl.pallas_call(
        body, name=name, grid=(L // tr,), in_specs=[row, pl.BlockSpec((tr, D), lambda i: (i, 1))], out_specs=row,
        out_shape=jax.ShapeDtypeStruct((L, D), BF16), compiler_params=_cp("parallel"),
    )(o, QO)


def _fox_gate_bwd(d, o, QO, H, *, name):
    L, D = o.shape
    hd = FOX_HD
    tr = _tile(L, 320, 16)

    def body(d_ref, o_ref, g_ref, do_ref, dg_ref):
        for h in range(H):
            sl = slice(h * hd, (h + 1) * hd)
            s = _sigmoid(g_ref[:, sl])
            d = d_ref[:, sl]
            o = o_ref[:, sl]
            do = d * s
            dg_ref[:, sl] = (d * o * s * (1.0 - s)).astype(BF16)
            hi, mid, lo = _split3(-jnp.sum(do * o, axis=-1, keepdims=True))
            do_ref[:, h * AUG:h * AUG + hd] = do.astype(BF16)
            do_ref[:, h * AUG + hd:(h + 1) * AUG] = _aug_lanes(tr, [hi, mid, lo]).astype(BF16)

    row = pl.BlockSpec((tr, D), lambda i: (i, 0))
    return pl.pallas_call(
        body, name=name, grid=(L // tr,), in_specs=[row, row, pl.BlockSpec((tr, D), lambda i: (i, 1))],
        out_specs=[pl.BlockSpec((tr, H * AUG), lambda i: (i, 0)), row],
        out_shape=[jax.ShapeDtypeStruct((L, H * AUG), BF16), jax.ShapeDtypeStruct((L, D), BF16)],
        compiler_params=_cp("parallel"),
    )(d, o, QO)


def _row_tile(R, C, n_arrays):
    budget = VMEM_LIMIT // (3 * n_arrays * 4 * max(C, LANE))
    return _tile(R, max(8, budget // 8 * 8), 8)


def _sum_parts(parts, *, name):
    R, C = parts[0].shape
    tr = _row_tile(R, C, len(parts) + 1)

    def body(*refs):
        acc = refs[0][...].astype(F32)
        for r in refs[1:-1]:
            acc = acc + r[...].astype(F32)
        refs[-1][...] = acc

    blk = pl.BlockSpec((tr, C), lambda i: (i, 0))
    return pl.pallas_call(
        body, name=name, grid=(R // tr,), in_specs=[blk] * len(parts), out_specs=blk,
        out_shape=jax.ShapeDtypeStruct((R, C), F32), compiler_params=_cp("parallel"),
    )(*parts)


def _sum_slots(x, *, name):
    S, R, C = x.shape
    tr = _row_tile(R, C, S + 1)

    def body(x_ref, o_ref):
        acc = x_ref[0].astype(F32)
        for s in range(1, S):
            acc = acc + x_ref[s].astype(F32)
        o_ref[...] = acc

    return pl.pallas_call(
        body, name=name, grid=(R // tr,), in_specs=[pl.BlockSpec((S, tr, C), lambda i: (0, i, 0))],
        out_specs=pl.BlockSpec((tr, C), lambda i: (i, 0)),
        out_shape=jax.ShapeDtypeStruct((R, C), F32), compiler_params=_cp("parallel"),
    )(x)


def _adamw(w, m, v, gparts, *, name):
    R, C = w.shape
    tr = _row_tile(R, C, 7 + len(gparts))
    ng = len(gparts)

    def body(*refs):
        w_ref, m_ref, v_ref = refs[:3]
        g = refs[3][...]
        for r in refs[4:3 + ng]:
            g = g + r[...]
        g_ref, d_ref, nm_ref, nv_ref = refs[3 + ng:]
        nm = ADAM_B1 * m_ref[...] + (1.0 - ADAM_B1) * g
        nv = ADAM_B2 * v_ref[...] + (1.0 - ADAM_B2) * (g * g)
        m_hat = nm / (1.0 - ADAM_B1 ** ADAM_STEP)
        v_hat = nv / (1.0 - ADAM_B2 ** ADAM_STEP)
        g_ref[...] = g
        d_ref[...] = -ADAM_LR * (m_hat / (jnp.sqrt(v_hat) + ADAM_EPS) + ADAM_WD * w_ref[...])
        nm_ref[...] = nm
        nv_ref[...] = nv

    blk = pl.BlockSpec((tr, C), lambda i: (i, 0))
    return pl.pallas_call(
        body, name=name, grid=(R // tr,), in_specs=[blk] * (3 + ng), out_specs=[blk] * 4,
        out_shape=[jax.ShapeDtypeStruct((R, C), F32)] * 4, compiler_params=_cp("parallel"),
    )(w, m, v, *gparts)


def _chip_peers():
    x, y, c = lax.axis_index("x"), lax.axis_index("y"), lax.axis_index("c")
    return (x, y, c), [(1 - x, y), (x, 1 - y), (1 - x, 1 - y)]


def _gather_chips(arrs, *, name):
    n = len(arrs)
    hs = [a.shape[0] // 2 for a in arrs]

    def body(*refs):
        ins, outs = refs[:n], refs[n:2 * n]
        send, recv, loc = refs[2 * n:]
        (x, y, c), chips = _chip_peers()
        me = 2 * x + y

        def landing(a, chip_idx):
            return outs[a].at[chip_idx, pl.ds(c * hs[a], hs[a])]

        owns, ici, passed = [], [], []
        for a in range(n):
            own = pltpu.make_async_copy(ins[a], outs[a].at[me], loc.at[a])
            own.start()
            owns.append(own)
            for j, (px, py) in enumerate(chips):
                cp = pltpu.make_async_remote_copy(
                    src_ref=ins[a].at[pl.ds(c * hs[a], hs[a])], dst_ref=landing(a, me), send_sem=send.at[a, j],
                    recv_sem=recv.at[a, j], device_id=(px, py, c), device_id_type=MESH)
                cp.start()
                ici.append(cp)
        for a in range(n):
            for j, (px, py) in enumerate(chips):
                ici[3 * a + j].wait_recv()
                src = landing(a, 2 * px + py)
                cp = pltpu.make_async_remote_copy(
                    src_ref=src, dst_ref=src, send_sem=send.at[a, 3 + j], recv_sem=recv.at[a, 3 + j],
                    device_id=(x, y, 1 - c), device_id_type=MESH)
                cp.start()
                passed.append(cp)
        for cp in ici:
            cp.wait_send()
        for cp in passed + owns:
            cp.wait()

    return pl.pallas_call(
        body, name=name, in_specs=[ANY] * n, out_specs=[ANY] * n,
        out_shape=[jax.ShapeDtypeStruct((4,) + a.shape, a.dtype) for a in arrs],
        scratch_shapes=[pltpu.SemaphoreType.DMA((n, 6)), pltpu.SemaphoreType.DMA((n, 6)), pltpu.SemaphoreType.DMA((n,))],
    )(*arrs)


def _scatter_chips(arrs, *, name):
    n = len(arrs)

    def body(*refs):
        ins, outs = refs[:n], refs[n:2 * n]
        send, recv = refs[2 * n:]
        (x, y, c), chips = _chip_peers()
        copies = []
        for a in range(n):
            for j, (px, py) in enumerate(chips):
                cp = pltpu.make_async_remote_copy(
                    src_ref=ins[a].at[2 * px + py], dst_ref=outs[a].at[j], send_sem=send.at[a, j],
                    recv_sem=recv.at[a, j], device_id=(px, py, c), device_id_type=MESH)
                cp.start()
                copies.append(cp)
        for cp in copies:
            cp.wait()

    return pl.pallas_call(
        body, name=name, in_specs=[ANY] * n, out_specs=[ANY] * n,
        out_shape=[jax.ShapeDtypeStruct((3,) + a.shape[1:], a.dtype) for a in arrs],
        scratch_shapes=[pltpu.SemaphoreType.DMA((n, 3)), pltpu.SemaphoreType.DMA((n, 3))],
    )(*arrs)


def _swap_sibling(arrs, *, name):
    n = len(arrs)

    def body(*refs):
        ins, outs = refs[:n], refs[n:2 * n]
        send, recv = refs[2 * n:]
        x, y, c = lax.axis_index("x"), lax.axis_index("y"), lax.axis_index("c")
        copies = []
        for a in range(n):
            cp = pltpu.make_async_remote_copy(
                src_ref=ins[a], dst_ref=outs[a], send_sem=send.at[a], recv_sem=recv.at[a],
                device_id=(x, y, 1 - c), device_id_type=MESH)
            cp.start()
            copies.append(cp)
        for cp in copies:
            cp.wait()

    return pl.pallas_call(
        body, name=name, in_specs=[ANY] * n, out_specs=[ANY] * n,
        out_shape=[jax.ShapeDtypeStruct(a.shape, a.dtype) for a in arrs],
        scratch_shapes=[pltpu.SemaphoreType.DMA((n,)), pltpu.SemaphoreType.DMA((n,))],
    )(*arrs)


def _gather_all(a, *, name):
    def body(in_ref, out_ref, send, recv, loc):
        x, y, c = lax.axis_index("x"), lax.axis_index("y"), lax.axis_index("c")
        me = 4 * x + 2 * y + c
        own = pltpu.make_async_copy(in_ref, out_ref.at[me], loc)
        own.start()
        copies = [own]
        for j in range(1, 8):
            fx, fy, fc = (j >> 2) & 1, (j >> 1) & 1, j & 1
            peer = (x ^ fx, y ^ fy, c ^ fc)
            cp = pltpu.make_async_remote_copy(
                src_ref=in_ref, dst_ref=out_ref.at[me], send_sem=send.at[j - 1], recv_sem=recv.at[j - 1],
                device_id=peer, device_id_type=MESH)
            cp.start()
            copies.append(cp)
        for cp in copies:
            cp.wait()

    return pl.pallas_call(
        body, name=name, in_specs=[ANY], out_specs=ANY,
        out_shape=jax.ShapeDtypeStruct((8,) + a.shape, a.dtype),
        scratch_shapes=[pltpu.SemaphoreType.DMA((7,)), pltpu.SemaphoreType.DMA((7,)), pltpu.SemaphoreType.DMA(())],
    )(a)


def _pack(arrs):
    flat = jnp.concatenate([a.astype(F32).reshape(-1) for a in arrs])
    n = flat.shape[0]
    pad = (-n) % (16 * LANE)
    return jnp.pad(flat, (0, pad)).reshape(-1, LANE)


def _unpack(buf, shapes):
    flat = buf.reshape(-1)
    out, off = [], 0
    for s in shapes:
        n = int(np.prod(s))
        out.append(flat[off:off + n].reshape(s))
        off += n
    return out


def _to_shards(g, axis):
    parts = jnp.split(g, 4, axis=axis)
    return jnp.stack([p.reshape(-1, p.shape[-1]) for p in parts])


def kernel(x, meta, ln_g, ln_b, gla_w_in, gla_w_g2, gla_b_g2, gla_norm_g, gla_w_out, kv_w, kv_bf, fox_w_in, fox_w_out, ffn_w_up, ffn_conv_w, ffn_conv_b, ffn_w_down, loss_target, m_meta, m_ln_g, m_ln_b, m_gla_w_in, m_gla_w_g2, m_gla_b_g2, m_gla_norm_g, m_gla_w_out, m_kv_w, m_kv_bf, m_fox_w_in, m_fox_w_out, m_ffn_w_up, m_ffn_conv_w, m_ffn_conv_b, m_ffn_w_down, v_meta, v_ln_g, v_ln_b, v_gla_w_in, v_gla_w_g2, v_gla_b_g2, v_gla_norm_g, v_gla_w_out, v_kv_w, v_kv_bf, v_fox_w_in, v_fox_w_out, v_ffn_w_up, v_ffn_conv_w, v_ffn_conv_b, v_ffn_w_down):
    D = x.shape[-1]
    L = x.shape[1] + FRONT
    HG = GLA_HEADS
    DK, DV = D // 2, D
    HK, HV = DK // HG, DV // HG
    HF = D // FOX_HD
    DFF = ffn_w_down.shape[1] * 4
    chip = 2 * lax.axis_index("x") + lax.axis_index("y")

    big_names = ["gla_w_in", "gla_w_out", "kv_w", "fox_w_in", "fox_w_out", "ffn_w_up", "ffn_w_down"]
    big = dict(gla_w_in=gla_w_in, gla_w_out=gla_w_out, kv_w=kv_w, fox_w_in=fox_w_in, fox_w_out=fox_w_out,
               ffn_w_up=ffn_w_up, ffn_w_down=ffn_w_down)
    big_axis = dict(gla_w_in=2, gla_w_out=1, kv_w=1, fox_w_in=2, fox_w_out=1, ffn_w_up=2, ffn_w_down=1)
    small_names = ["meta", "ln_g", "ln_b", "gla_w_g2", "gla_b_g2", "gla_norm_g", "ffn_conv_w"]
    small = dict(meta=meta, ln_g=ln_g, ln_b=ln_b, gla_w_g2=gla_w_g2, gla_b_g2=gla_b_g2, gla_norm_g=gla_norm_g,
                 ffn_conv_w=ffn_conv_w)
    small_shapes = [small[k].shape for k in small_names]
    gathered = _gather_chips([big[k].astype(BF16) for k in big_names] + [_pack([small[k] for k in small_names])],
                             name="gather_weights")
    full = {k: jnp.concatenate([gathered[i][s] for s in range(4)], axis=big_axis[k]) for i, k in enumerate(big_names)}
    sm_sh = [_unpack(gathered[-1][s], small_shapes) for s in range(4)]
    fs = {k: jnp.concatenate([sm_sh[s][i] for s in range(4)], axis=-1) for i, k in enumerate(small_names)}

    pad_cols = lambda w: jnp.pad(w, ((0, 0), (0, LANE - w.shape[1])))
    W_P, W_Pgl = [], []
    for l in range(N_A_LAYERS):
        w = full["gla_w_in"][l]
        W_P.append(jnp.concatenate([w[:, :2 * DK + DV], w[:, 2 * DK + DV + GLA_RANK:],
                                    pad_cols(w[:, 2 * DK + DV:2 * DK + DV + GLA_RANK])], axis=1))
    w2p = [jnp.pad(fs["gla_w_g2"][l], ((0, LANE - GLA_RANK), (0, 0))).astype(BF16) for l in range(N_A_LAYERS)]
    W_kv = full["kv_w"][:, :2 * D]
    W_f = pad_cols(full["kv_w"][:, 2 * D:])
    W_kvf = jnp.concatenate([W_kv, W_f], axis=1)
    bf_pad = jnp.pad(kv_bf, (0, LANE - HF)).reshape(1, LANE)
    W_u = [full["ffn_w_up"][l][:, :DFF] for l in range(DEPTH)]
    W_g = [full["ffn_w_up"][l][:, DFF:] for l in range(DEPTH)]
    cw_u = [fs["ffn_conv_w"][l][:, :DFF] for l in range(DEPTH)]
    cw_g = [fs["ffn_conv_w"][l][:, DFF:] for l in range(DEPTH)]
    cb_u = [ffn_conv_b[l][None, :DFF] for l in range(DEPTH)]
    cb_g = [ffn_conv_b[l][None, DFF:] for l in range(DEPTH)]
    gl_blk = (2 * DK + 2 * DV) // LANE
    r_blk = (2 * DK + DV) // DV

    h = jnp.concatenate([jnp.concatenate([jnp.zeros((N_PAD, D), F32), fs["meta"]], axis=0), x[0]], axis=0)
    hb = h.astype(BF16)
    saved = []
    kvs = None
    for l in range(DEPTH):
        s = dict(h=h, hb=hb)
        if l < N_A_LAYERS:
            s["P"] = _mm(hb, W_P[l], name=f"gla_in_{l}")
            s["la"] = _gla_gate_fwd(s["P"], w2p[l], fs["gla_b_g2"][l][None], gl_blk, name=f"gla_gate_{l}")
            s["o"], s["S"] = _gla_chunk_fwd(s["P"], s["la"], HG, HK, HV, name=f"gla_chunk_{l}")
            s["gated"] = _gla_post_fwd(s["o"], s["P"], fs["gla_norm_g"][l][None], HG, HV, r_blk, name=f"gla_post_{l}")
            s["mix"] = _mm(s["gated"], full["gla_w_out"][l], name=f"gla_out_{l}")
        else:
            j = l - N_A_LAYERS
            if kvs is None:
                KV = _mm(hb, W_kv, out_dtype=BF16, name="kv_proj")
                f = _mm(hb, W_f, name="kv_gate_proj")
                c = _fox_c_fwd(f, bf_pad, name="fox_c")
                KA, VA = _fox_prep_kv(KV, c, HF, name="fox_prep_kv")
                kvs = dict(KA=KA, VA=VA, f=f, c=c, hb=hb)
            s["QO"] = _mm(hb, full["fox_w_in"][j], name=f"fox_in_{j}")
            s["QA"] = _fox_prep_q(s["QO"], kvs["c"], HF, name=f"fox_prep_q_{j}")
            s["o"], s["lse"] = _fox_attn_fwd(s["QA"], kvs["KA"], kvs["VA"], HF, name=f"fox_attn_{j}")
            s["gated"] = _fox_gate_fwd(s["o"], s["QO"], name=f"fox_gate_{j}")
            s["mix"] = _mm(s["gated"], full["fox_w_out"][j], name=f"fox_out_{j}")
        s["h1"], s["h1b"] = _ln_fwd(h, s["mix"], fs["ln_g"][l, 0][None], fs["ln_b"][l, 0][None], name=f"ln_a_{l}")
        s["Uu"] = _mm(s["h1b"], W_u[l], name=f"ffn_up_u_{l}")
        s["Ug"] = _mm(s["h1b"], W_g[l], name=f"ffn_up_g_{l}")
        s["a"] = _conv_act_fwd(s["Uu"], s["Ug"], cw_u[l], cw_g[l], cb_u[l], cb_g[l], name=f"ffn_conv_{l}")
        s["ffn"] = _mm(s["a"], full["ffn_w_down"][l], name=f"ffn_down_{l}")
        h, hb = _ln_fwd(s["h1"], s["ffn"], fs["ln_g"][l, 1][None], fs["ln_b"][l, 1][None], name=f"ln_b_{l}")
        saved.append(s)

    loss_acc, dh = _loss(h, loss_target[0], name="loss")

    gW = {}
    d_ln_g = [[None, None] for _ in range(DEPTH)]
    d_ln_b = [[None, None] for _ in range(DEPTH)]
    d_cw, d_cb = [None] * DEPTH, [None] * DEPTH
    d_wg2, d_bg2, d_ng = [None] * N_A_LAYERS, [None] * N_A_LAYERS, [None] * N_A_LAYERS
    dkv = (jnp.zeros((L, HF * AUG), F32), jnp.zeros((L, HF * AUG), F32))
    dcqs = []
    for l in reversed(range(DEPTH)):
        s = saved[l]
        dz, dzb, d_ln_g[l][1], d_ln_b[l][1] = _ln_bwd(dh, s["h1"], s["ffn"], fs["ln_g"][l, 1][None], name=f"ln_b_bwd_{l}")
        dA = _mm(dzb, full["ffn_w_down"][l], tb=True, name=f"ffn_down_dx_{l}")
        gW[("ffn_w_down", l)] = _mm(s["a"], dzb, ta=True, out_dtype=BF16, name=f"ffn_down_dw_{l}")
        dcu, dcg, dwu, dwg, dbu, dbg = _conv_act_bwd(s["Uu"], s["Ug"], cw_u[l], cw_g[l], cb_u[l], cb_g[l], dA,
                                                     name=f"ffn_conv_bwd_{l}")
        d_cw[l] = jnp.concatenate([dwu, dwg], axis=1)
        d_cb[l] = jnp.concatenate([dbu, dbg], axis=1)[0]
        dUu = _conv_in_bwd(dcu, cw_u[l], name=f"ffn_conv_dx_u_{l}")
        dUg = _conv_in_bwd(dcg, cw_g[l], name=f"ffn_conv_dx_g_{l}")
        gW[("ffn_w_up", l)] = jnp.concatenate(
            [_mm(s["h1b"], dUu, ta=True, out_dtype=BF16, name=f"ffn_up_dw_u_{l}"),
             _mm(s["h1b"], dUg, ta=True, out_dtype=BF16, name=f"ffn_up_dw_g_{l}")], axis=1)
        dh1 = _mm(dUu, W_u[l], tb=True, add=dz, add_scale=ALPHA, name=f"ffn_up_dx_u_{l}")
        dh1 = _mm(dUg, W_g[l], tb=True, add=dh1, name=f"ffn_up_dx_g_{l}")
        dz, dzb, d_ln_g[l][0], d_ln_b[l][0] = _ln_bwd(dh1, s["h"], s["mix"], fs["ln_g"][l, 0][None], name=f"ln_a_bwd_{l}")
        if l < N_A_LAYERS:
            dgated = _mm(dzb, full["gla_w_out"][l], tb=True, name=f"gla_out_dx_{l}")
            gW[("gla_w_out", l)] = _mm(s["gated"], dzb, ta=True, out_dtype=BF16, name=f"gla_out_dw_{l}")
            do, drb, d_ng[l] = _gla_post_bwd(dgated, s["o"], s["P"], fs["gla_norm_g"][l][None], HG, HV, r_blk,
                                             name=f"gla_post_bwd_{l}")
            dq, dk, dvb, dla = _gla_chunk_bwd(s["P"], s["la"], s["S"], do, HG, HK, HV, name=f"gla_chunk_bwd_{l}")
            dglb, dw2, d_bg2[l] = _gla_gate_bwd(dla, s["P"], w2p[l], fs["gla_b_g2"][l][None], gl_blk,
                                               name=f"gla_gate_bwd_{l}")
            d_wg2[l] = dw2[:GLA_RANK]
            dP = jnp.concatenate([dq, dk, dvb, drb, dglb], axis=1)
            gP = _mm(s["hb"], dP, ta=True, out_dtype=BF16, name=f"gla_in_dw_{l}")
            gW[("gla_w_in", l)] = jnp.concatenate(
                [gP[:, :2 * DK + DV], gP[:, 2 * DK + 2 * DV:2 * DK + 2 * DV + GLA_RANK], gP[:, 2 * DK + DV:2 * DK + 2 * DV]],
                axis=1)
            dh = _mm(dP, W_P[l], tb=True, add=dz, add_scale=ALPHA, name=f"gla_in_dx_{l}")
        else:
            j = l - N_A_LAYERS
            dgo = _mm(dzb, full["fox_w_out"][j], tb=True, name=f"fox_out_dx_{j}")
            gW[("fox_w_out", j)] = _mm(s["gated"], dzb, ta=True, out_dtype=BF16, name=f"fox_out_dw_{j}")
            DOA, dogb = _fox_gate_bwd(dgo, s["o"], s["QO"], HF, name=f"fox_gate_bwd_{j}")
            DQA, DKA, DVA = _fox_attn_bwd(s["QA"], kvs["KA"], kvs["VA"], DOA, s["lse"], dkv, HF, name=f"fox_attn_bwd_{j}")
            dkv = (DKA, DVA)
            dqb, dcq = _fox_post_q(DQA, HF, name=f"fox_post_q_{j}")
            dcqs.append(dcq)
            dQO = jnp.concatenate([dqb, dogb], axis=1)
            gW[("fox_w_in", j)] = _mm(s["hb"], dQO, ta=True, out_dtype=BF16, name=f"fox_in_dw_{j}")
            dh = _mm(dQO, full["fox_w_in"][j], tb=True, add=dz, add_scale=ALPHA, name=f"fox_in_dx_{j}")
            if j == 0:
                dkvb, dck = _fox_post_kv(DKA, DVA, HF, name="fox_post_kv")
                dfb, d_bf = _fox_c_bwd(dcqs + [dck], kvs["f"], bf_pad, name="fox_c_bwd")
                dKVF = jnp.concatenate([dkvb, dfb], axis=1)
                gkv = _mm(kvs["hb"], dKVF, ta=True, out_dtype=BF16, name="kv_dw")
                gW[("kv_w", 0)] = gkv[:, :2 * D + HF]
                dh = _mm(dKVF, W_kvf, tb=True, add=dh, name="kv_dx")

    stack = lambda k, n: jnp.stack([gW[(k, i)] for i in range(n)])
    gfull = dict(gla_w_in=stack("gla_w_in", N_A_LAYERS), gla_w_out=stack("gla_w_out", N_A_LAYERS), kv_w=gW[("kv_w", 0)],
                 fox_w_in=stack("fox_w_in", DEPTH - N_A_LAYERS), fox_w_out=stack("fox_w_out", DEPTH - N_A_LAYERS),
                 ffn_w_up=stack("ffn_w_up", DEPTH), ffn_w_down=stack("ffn_w_down", DEPTH))
    g4 = [_to_shards(gfull[k], big_axis[k]) for k in big_names]
    recv = _scatter_chips(g4, name="scatter_grads")
    partial = []
    for i, k in enumerate(big_names):
        own = lax.dynamic_index_in_dim(g4[i], chip, axis=0, keepdims=False)
        partial.append(_sum_parts([own, recv[i][0], recv[i][1], recv[i][2]], name=f"sum_chips_{k}"))
    other = _swap_sibling(partial, name="swap_sibling")
    moments = dict(gla_w_in=(m_gla_w_in, v_gla_w_in), gla_w_out=(m_gla_w_out, v_gla_w_out), kv_w=(m_kv_w, v_kv_w),
                   fox_w_in=(m_fox_w_in, v_fox_w_in), fox_w_out=(m_fox_w_out, v_fox_w_out),
                   ffn_w_up=(m_ffn_w_up, v_ffn_w_up), ffn_w_down=(m_ffn_w_down, v_ffn_w_down))
    res = {}
    for i, k in enumerate(big_names):
        w = big[k]
        sh = w.shape
        flat = lambda a: a.reshape(-1, sh[-1])
        outs = _adamw(flat(w), flat(moments[k][0]), flat(moments[k][1]), [partial[i], other[i]], name=f"adamw_{k}")
        res[k] = [o.reshape(sh) for o in outs]

    dmeta = dh[N_PAD:FRONT]
    sg = dict(meta=dmeta,
              ln_g=jnp.stack([jnp.concatenate(d_ln_g[l], axis=0) for l in range(DEPTH)]),
              ln_b=jnp.stack([jnp.concatenate(d_ln_b[l], axis=0) for l in range(DEPTH)]),
              gla_w_g2=jnp.stack(d_wg2), gla_b_g2=jnp.stack([d[0] for d in d_bg2]),
              gla_norm_g=jnp.stack([d[0] for d in d_ng]), ffn_conv_w=jnp.stack(d_cw),
              kv_bf=d_bf[0, :HF], ffn_conv_b=jnp.stack(d_cb), loss=loss_acc[0, :1])
    sg_names = small_names + ["kv_bf", "ffn_conv_b", "loss"]
    sg_shapes = [sg[k].shape for k in sg_names]
    red = _sum_slots(_gather_all(_pack([sg[k] for k in sg_names]), name="gather_small_grads"), name="sum_small_grads")
    red = dict(zip(sg_names, _unpack(red, sg_shapes)))
    loss = red["loss"][0]
    loc = {}
    for k in small_names:
        wdt = small[k].shape[-1]
        loc[k] = lax.dynamic_slice_in_dim(red[k], chip * wdt, wdt, axis=red[k].ndim - 1)
    loc["kv_bf"] = red["kv_bf"]
    loc["ffn_conv_b"] = red["ffn_conv_b"]
    sm_all = small_names + ["kv_bf", "ffn_conv_b"]
    sw = dict(small, kv_bf=kv_bf, ffn_conv_b=ffn_conv_b)
    sm_m = dict(meta=m_meta, ln_g=m_ln_g, ln_b=m_ln_b, gla_w_g2=m_gla_w_g2, gla_b_g2=m_gla_b_g2, gla_norm_g=m_gla_norm_g,
                ffn_conv_w=m_ffn_conv_w, kv_bf=m_kv_bf, ffn_conv_b=m_ffn_conv_b)
    sm_v = dict(meta=v_meta, ln_g=v_ln_g, ln_b=v_ln_b, gla_w_g2=v_gla_w_g2, gla_b_g2=v_gla_b_g2, gla_norm_g=v_gla_norm_g,
                ffn_conv_w=v_ffn_conv_w, kv_bf=v_kv_bf, ffn_conv_b=v_ffn_conv_b)
    shapes_loc = [sw[k].shape for k in sm_all]
    outs = _adamw(_pack([sw[k] for k in sm_all]), _pack([sm_m[k] for k in sm_all]), _pack([sm_v[k] for k in sm_all]),
                  [_pack([loc[k] for k in sm_all])], name="adamw_small")
    outs = [_unpack(o, shapes_loc) for o in outs]
    for i, k in enumerate(sm_all):
        res[k] = [outs[q][i] for q in range(4)]

    order = ["meta", "ln_g", "ln_b", "gla_w_in", "gla_w_g2", "gla_b_g2", "gla_norm_g", "gla_w_out", "kv_w", "kv_bf",
             "fox_w_in", "fox_w_out", "ffn_w_up", "ffn_conv_w", "ffn_conv_b", "ffn_w_down"]
    grad_x = dh[FRONT:][None]
    return (loss, grad_x, *[res[k][0] for k in order], *[res[k][1] for k in order], *[res[k][2] for k in order],
            *[res[k][3] for k in order])
```

```python
import functools

import numpy as np
import jax
import jax.numpy as jnp
from jax import lax
from jax.experimental import pallas as pl
from jax.experimental.pallas import tpu as pltpu

F32 = jnp.float32
BF16 = jnp.bfloat16
HIGHEST = lax.Precision.HIGHEST

DEPTH = 4
N_A_LAYERS = DEPTH // 2
N_META = 16
FRONT = 128
N_PAD = FRONT - N_META
ALPHA = (2.0 * DEPTH) ** 0.25
LN_EPS = 1e-5
GLA_HEADS = 4
GLA_RANK = 16
GLA_TAU = 16.0
GLA_CHUNK = 64
FOX_HD = 128
LANE = 128
ADAM_LR = 0.001
ADAM_B1 = 0.9
ADAM_B2 = 0.999
ADAM_EPS = 1e-08
ADAM_WD = 0.01
ADAM_STEP = 10
NEG = -(2.0 ** 100)
VMEM_LIMIT = 50 * 1024 * 1024
MESH = pl.DeviceIdType.MESH
ANY = pl.BlockSpec(memory_space=pl.ANY)


def _tile(n, pref, align):
    best = None
    t = align
    while t <= min(n, pref):
        if n % t == 0:
            best = t
        t += align
    return best if best is not None else n


def _cp(*sem):
    return pltpu.CompilerParams(dimension_semantics=sem, vmem_limit_bytes=VMEM_LIMIT)


def _dot(a, b, ca, cb, precision=None):
    return lax.dot_general(a, b, (((ca,), (cb,)), ((), ())), precision=precision,
                           preferred_element_type=F32)


def _sigmoid(x):
    return 1.0 / (1.0 + jnp.exp(-x))


def _log_sigmoid(z):
    return jnp.minimum(z, 0.0) - jnp.log(1.0 + jnp.exp(-jnp.abs(z)))


def _rows(i, tr, n=None):
    n = tr if n is None else n
    return i * tr + lax.broadcasted_iota(jnp.int32, (n, 1), 0)


def _mm(a, b, *, ta=False, tb=False, out_dtype=F32, add=None, add_scale=1.0, name):
    if ta:
        K, M = a.shape
    else:
        M, K = a.shape
    if tb:
        N, K2 = b.shape
    else:
        K2, N = b.shape
    assert K == K2, (a.shape, b.shape, ta, tb)
    tm = _tile(M, 1024, LANE) if ta else _tile(M, 1040, 16)
    tn = _tile(N, 1024, LANE)
    tk = _tile(K, 2048, LANE if ((not ta) or tb) else 16)
    nk = K // tk
    ca = 0 if ta else 1
    cb = 1 if tb else 0

    def body(*refs):
        if add is None:
            a_ref, b_ref, o_ref = refs[:3]
            add_ref = None
        else:
            a_ref, b_ref, add_ref, o_ref = refs[:4]
        part = _dot(a_ref[...].astype(BF16), b_ref[...].astype(BF16), ca, cb)

        def finish(acc):
            if add_ref is not None:
                acc = acc + add_scale * add_ref[...]
            o_ref[...] = acc.astype(out_dtype)

        if nk == 1:
            finish(part)
        else:
            acc_ref = refs[-1]
            k = pl.program_id(2)

            @pl.when(k == 0)
            def _():
                acc_ref[...] = part

            @pl.when(k > 0)
            def _():
                acc_ref[...] += part

            @pl.when(k == nk - 1)
            def _():
                finish(acc_ref[...])

    a_spec = pl.BlockSpec((tk, tm), lambda i, j, k: (k, i)) if ta else pl.BlockSpec((tm, tk), lambda i, j, k: (i, k))
    b_spec = pl.BlockSpec((tn, tk), lambda i, j, k: (j, k)) if tb else pl.BlockSpec((tk, tn), lambda i, j, k: (k, j))
    o_spec = pl.BlockSpec((tm, tn), lambda i, j, k: (i, j))
    in_specs = [a_spec, b_spec] + ([o_spec] if add is not None else [])
    args = (a, b) + ((add,) if add is not None else ())
    return pl.pallas_call(
        body, name=name, grid=(M // tm, N // tn, nk), in_specs=in_specs, out_specs=o_spec,
        out_shape=jax.ShapeDtypeStruct((M, N), out_dtype),
        scratch_shapes=[pltpu.VMEM((tm, tn), F32)] if nk > 1 else [],
        compiler_params=_cp("parallel", "parallel", "arbitrary"),
    )(*args)


def _ln_stats(h, mix):
    z = ALPHA * h + mix
    mu = jnp.mean(z, axis=-1, keepdims=True)
    zc = z - mu
    var = jnp.mean(zc * zc, axis=-1, keepdims=True)
    rstd = lax.rsqrt(var + LN_EPS)
    return zc * rstd, rstd


def _ln_fwd(h, mix, g, b, *, name):
    L, D = h.shape
    tr = _tile(L, 160, 16)

    def body(h_ref, m_ref, g_ref, b_ref, o_ref, ob_ref):
        xhat, _ = _ln_stats(h_ref[...], m_ref[...])
        y = xhat * g_ref[...] + b_ref[...]
        o_ref[...] = y
        ob_ref[...] = y.astype(BF16)

    row = pl.BlockSpec((tr, D), lambda i: (i, 0))
    vec = pl.BlockSpec((1, D), lambda i: (0, 0))
    return pl.pallas_call(
        body, name=name, grid=(L // tr,), in_specs=[row, row, vec, vec], out_specs=[row, row],
        out_shape=[jax.ShapeDtypeStruct((L, D), F32), jax.ShapeDtypeStruct((L, D), BF16)],
        compiler_params=_cp("parallel"),
    )(h, mix, g, b)


def _ln_bwd(dy, h, mix, g, *, name):
    L, D = h.shape
    tr = _tile(L, 160, 16)

    def body(dy_ref, h_ref, m_ref, g_ref, dz_ref, dzb_ref, dg_ref, db_ref):
        i = pl.program_id(0)
        xhat, rstd = _ln_stats(h_ref[...], m_ref[...])
        dy = dy_ref[...]
        dxh = dy * g_ref[...]
        m1 = jnp.mean(dxh, axis=-1, keepdims=True)
        m2 = jnp.mean(dxh * xhat, axis=-1, keepdims=True)
        dz = rstd * (dxh - m1 - xhat * m2)
        dz_ref[...] = dz
        dzb_ref[...] = dz.astype(BF16)
        pg = jnp.sum(dy * xhat, axis=0, keepdims=True)
        pb = jnp.sum(dy, axis=0, keepdims=True)

        @pl.when(i == 0)
        def _():
            dg_ref[...] = pg
            db_ref[...] = pb

        @pl.when(i > 0)
        def _():
            dg_ref[...] += pg
            db_ref[...] += pb

    row = pl.BlockSpec((tr, D), lambda i: (i, 0))
    vec = pl.BlockSpec((1, D), lambda i: (0, 0))
    return pl.pallas_call(
        body, name=name, grid=(L // tr,), in_specs=[row, row, row, vec], out_specs=[row, row, vec, vec],
        out_shape=[jax.ShapeDtypeStruct((L, D), F32), jax.ShapeDtypeStruct((L, D), BF16),
                   jax.ShapeDtypeStruct((1, D), F32), jax.ShapeDtypeStruct((1, D), F32)],
        compiler_params=_cp("arbitrary"),
    )(dy, h, mix, g)


def _loss(h, target, *, name):
    L, D = h.shape
    tr = FRONT

    def body(h_ref, t_ref, acc_ref, dy_ref):
        i = pl.program_id(0)
        e = jnp.where(i >= 1, h_ref[...] - t_ref[...], 0.0)
        dy_ref[...] = e * (1.0 / D)
        part = 0.5 * jnp.sum(jnp.sum(e * e, axis=-1, keepdims=True) * (1.0 / D), axis=0, keepdims=True)

        @pl.when(i == 0)
        def _():
            acc_ref[...] = jnp.zeros_like(acc_ref)

        acc_ref[...] += jnp.broadcast_to(part, acc_ref.shape)

    return pl.pallas_call(
        body, name=name, grid=(L // tr,),
        in_specs=[pl.BlockSpec((tr, D), lambda i: (i, 0)),
                  pl.BlockSpec((tr, D), lambda i: (jnp.maximum(i - 1, 0), 0))],
        out_specs=[pl.BlockSpec((8, LANE), lambda i: (0, 0)), pl.BlockSpec((tr, D), lambda i: (i, 0))],
        out_shape=[jax.ShapeDtypeStruct((8, LANE), F32), jax.ShapeDtypeStruct((L, D), F32)],
        compiler_params=_cp("arbitrary"),
    )(h, target)


def _gla_gate_fwd(P, w2p, b2, gl_blk, *, name):
    L = P.shape[0]
    DK = w2p.shape[1]
    tr = _tile(L, 640, 16)

    def body(gl_ref, w_ref, b_ref, o_ref):
        i = pl.program_id(0)
        z = _dot(gl_ref[...].astype(BF16), w_ref[...], 1, 0) + b_ref[...]
        la = _log_sigmoid(z) * (1.0 / GLA_TAU)
        o_ref[...] = jnp.where(_rows(i, tr) >= N_PAD, la, 0.0)

    return pl.pallas_call(
        body, name=name, grid=(L // tr,),
        in_specs=[pl.BlockSpec((tr, LANE), lambda i: (i, gl_blk)),
                  pl.BlockSpec((LANE, DK), lambda i: (0, 0)), pl.BlockSpec((1, DK), lambda i: (0, 0))],
        out_specs=pl.BlockSpec((tr, DK), lambda i: (i, 0)),
        out_shape=jax.ShapeDtypeStruct((L, DK), F32), compiler_params=_cp("parallel"),
    )(P, w2p, b2)


def _gla_gate_bwd(dla, P, w2p, b2, gl_blk, *, name):
    L = P.shape[0]
    DK = w2p.shape[1]
    tr = _tile(L, 640, 16)

    def body(dla_ref, gl_ref, w_ref, b_ref, dgl_ref, dw_ref, db_ref):
        i = pl.program_id(0)
        glb = gl_ref[...].astype(BF16)
        z = _dot(glb, w_ref[...], 1, 0) + b_ref[...]
        dz = jnp.where(_rows(i, tr) >= N_PAD, dla_ref[...], 0.0) * (1.0 / GLA_TAU) * _sigmoid(-z)
        dzb = dz.astype(BF16)
        dgl_ref[...] = _dot(dzb, w_ref[...], 1, 1).astype(BF16)
        pw = _dot(glb, dzb, 0, 0)
        pb = jnp.sum(dz, axis=0, keepdims=True)

        @pl.when(i == 0)
        def _():
            dw_ref[...] = pw
            db_ref[...] = pb

        @pl.when(i > 0)
        def _():
            dw_ref[...] += pw
            db_ref[...] += pb

    return pl.pallas_call(
        body, name=name, grid=(L // tr,),
        in_specs=[pl.BlockSpec((tr, DK), lambda i: (i, 0)), pl.BlockSpec((tr, LANE), lambda i: (i, gl_blk)),
                  pl.BlockSpec((LANE, DK), lambda i: (0, 0)), pl.BlockSpec((1, DK), lambda i: (0, 0))],
        out_specs=[pl.BlockSpec((tr, LANE), lambda i: (i, 0)), pl.BlockSpec((LANE, DK), lambda i: (0, 0)),
                   pl.BlockSpec((1, DK), lambda i: (0, 0))],
        out_shape=[jax.ShapeDtypeStruct((L, LANE), BF16), jax.ShapeDtypeStruct((LANE, DK), F32),
                   jax.ShapeDtypeStruct((1, DK), F32)],
        compiler_params=_cp("arbitrary"),
    )(dla, P, w2p, b2)


def _chunk_terms(q, k, g, n, scale, HV):
    C = q.shape[0]
    ri = lax.broadcasted_iota(jnp.int32, (C, C), 0)
    ci = lax.broadcasted_iota(jnp.int32, (C, C), 1)
    tri = ri >= ci
    valid = _rows(n, C) >= N_PAD
    km = jnp.where(valid, k, 0.0)
    b = _dot(tri.astype(F32), g, 1, 0, precision=HIGHEST)
    bl_row = jnp.sum(g, axis=0, keepdims=True)
    bl_col = _dot(g, jnp.ones((C, HV), F32), 0, 0, precision=HIGHEST)
    eb = jnp.exp(b)
    enb = jnp.exp(-b)
    qe = q * scale * eb
    ke = km * enb
    ebl_row = jnp.exp(bl_row)
    kl = ke * ebl_row
    return dict(tri=tri, valid=valid, eb=eb, enb=enb, qe=qe, ke=ke, kl=kl, ebl_row=ebl_row,
                ebl_col=jnp.exp(bl_col), ri=ri, ci=ci)


def _gla_chunk_fwd(P, la, H, HK, HV, *, name):
    L = P.shape[0]
    C = GLA_CHUNK
    N = L // C
    scale = HK ** -0.5

    def body(q_ref, k_ref, v_ref, g_ref, o_ref, s_ref, S):
        n = pl.program_id(1)

        @pl.when(n == 0)
        def _():
            S[...] = jnp.zeros_like(S)

        S0 = S[...]
        s_ref[0, 0] = S0
        t = _chunk_terms(q_ref[...], k_ref[...], g_ref[...], n, scale, HV)
        vb = v_ref[...].astype(BF16)
        qeb = t["qe"].astype(BF16)
        inter = _dot(qeb, S0.astype(BF16), 1, 0)
        att = jnp.where(t["tri"], _dot(qeb, t["ke"].astype(BF16), 1, 1), 0.0)
        o_ref[...] = inter + _dot(att.astype(BF16), vb, 1, 0)
        S[...] = t["ebl_col"] * S0 + _dot(t["kl"].astype(BF16), vb, 0, 0)

    return pl.pallas_call(
        body, name=name, grid=(H, N),
        in_specs=[pl.BlockSpec((C, HK), lambda h, n: (n, h)), pl.BlockSpec((C, HK), lambda h, n: (n, H + h)),
                  pl.BlockSpec((C, HV), lambda h, n: (n, H + h)), pl.BlockSpec((C, HK), lambda h, n: (n, h))],
        out_specs=[pl.BlockSpec((C, HV), lambda h, n: (n, h)),
                   pl.BlockSpec((1, 1, HK, HV), lambda h, n: (h, n, 0, 0))],
        out_shape=[jax.ShapeDtypeStruct((L, H * HV), F32), jax.ShapeDtypeStruct((H, N, HK, HV), F32)],
        scratch_shapes=[pltpu.VMEM((HK, HV), F32)],
        compiler_params=_cp("parallel", "arbitrary"),
    )(P, P, P, la)


def _gla_chunk_bwd(P, la, S_all, do, H, HK, HV, *, name):
    L = P.shape[0]
    C = GLA_CHUNK
    N = L // C
    scale = HK ** -0.5

    def body(q_ref, k_ref, v_ref, g_ref, s_ref, do_ref, dq_ref, dk_ref, dv_ref, dg_ref, dS):
        step = pl.program_id(1)
        n = N - 1 - step

        @pl.when(step == 0)
        def _():
            dS[...] = jnp.zeros_like(dS)

        dS1 = dS[...]
        S0 = s_ref[0, 0]
        t = _chunk_terms(q_ref[...], k_ref[...], g_ref[...], n, scale, HV)
        tri, qe, ke, kl = t["tri"], t["qe"], t["ke"], t["kl"]
        vb = v_ref[...].astype(BF16)
        dob = do_ref[...].astype(BF16)
        qeb, keb, dSb = qe.astype(BF16), ke.astype(BF16), dS1.astype(BF16)
        dA = jnp.where(tri, _dot(dob, vb, 1, 1), 0.0).astype(BF16)
        A = jnp.where(tri, _dot(qeb, keb, 1, 1), 0.0).astype(BF16)
        dqe = _dot(dob, S0.astype(BF16), 1, 1) + _dot(dA, keb, 1, 0)
        dkl = _dot(vb, dSb, 1, 1)
        dke = _dot(dA, qeb, 0, 0) + dkl * t["ebl_row"]
        dv_ref[...] = (_dot(A, dob, 0, 0) + _dot(kl.astype(BF16), dSb, 1, 0)).astype(BF16)
        debl = (jnp.sum(_dot(jnp.ones((8, HV), F32), dS1 * S0, 1, 1, precision=HIGHEST), axis=0, keepdims=True) * 0.125
                + jnp.sum(dkl * ke, axis=0, keepdims=True))
        dbl = debl * t["ebl_row"]
        db = dqe * qe - dke * ke + jnp.where(lax.broadcasted_iota(jnp.int32, (C, 1), 0) == C - 1, dbl, 0.0)
        triu = (t["ci"] >= t["ri"]).astype(F32)
        dg = _dot(triu, db, 1, 0, precision=HIGHEST)
        dq_ref[...] = (dqe * t["eb"] * scale).astype(BF16)
        dk_ref[...] = jnp.where(t["valid"], dke * t["enb"], 0.0).astype(BF16)
        dg_ref[...] = dg
        dS[...] = t["ebl_col"] * dS1 + _dot(qeb, dob, 0, 0)

    rev = lambda h, s: (N - 1 - s, h)
    return pl.pallas_call(
        body, name=name, grid=(H, N),
        in_specs=[pl.BlockSpec((C, HK), rev), pl.BlockSpec((C, HK), lambda h, s: (N - 1 - s, H + h)),
                  pl.BlockSpec((C, HV), lambda h, s: (N - 1 - s, H + h)), pl.BlockSpec((C, HK), rev),
                  pl.BlockSpec((1, 1, HK, HV), lambda h, s: (h, N - 1 - s, 0, 0)), pl.BlockSpec((C, HV), rev)],
        out_specs=[pl.BlockSpec((C, HK), rev), pl.BlockSpec((C, HK), rev),
                   pl.BlockSpec((C, HV), rev), pl.BlockSpec((C, HK), rev)],
        out_shape=[jax.ShapeDtypeStruct((L, H * HK), BF16), jax.ShapeDtypeStruct((L, H * HK), BF16),
                   jax.ShapeDtypeStruct((L, H * HV), BF16), jax.ShapeDtypeStruct((L, H * HK), F32)],
        scratch_shapes=[pltpu.VMEM((HK, HV), F32)],
        compiler_params=_cp("parallel", "arbitrary"),
    )(P, P, P, la, S_all, do)


def _silu_parts(x):
    s = _sigmoid(x)
    return x * s, s * (1.0 + x * (1.0 - s))


def _gla_post_fwd(o, P, ng, H, HV, r_blk, *, name):
    L, DV = o.shape
    tr = _tile(L, 320, 16)

    def body(o_ref, r_ref, g_ref, out_ref):
        for hd in range(H):
            sl = slice(hd * HV, (hd + 1) * HV)
            oh = o_ref[:, sl]
            rr = lax.rsqrt(jnp.mean(oh * oh, axis=-1, keepdims=True) + LN_EPS)
            silu, _ = _silu_parts(r_ref[:, sl])
            out_ref[:, sl] = (oh * rr * g_ref[...] * silu).astype(BF16)

    return pl.pallas_call(
        body, name=name, grid=(L // tr,),
        in_specs=[pl.BlockSpec((tr, DV), lambda i: (i, 0)), pl.BlockSpec((tr, DV), lambda i: (i, r_blk)),
                  pl.BlockSpec((1, HV), lambda i: (0, 0))],
        out_specs=pl.BlockSpec((tr, DV), lambda i: (i, 0)),
        out_shape=jax.ShapeDtypeStruct((L, DV), BF16), compiler_params=_cp("parallel"),
    )(o, P, ng)


def _gla_post_bwd(dgated, o, P, ng, H, HV, r_blk, *, name):
    L, DV = o.shape
    tr = _tile(L, 320, 16)

    def body(d_ref, o_ref, r_ref, g_ref, do_ref, dr_ref, dng_ref):
        i = pl.program_id(0)
        png = jnp.zeros((1, HV), F32)
        for hd in range(H):
            sl = slice(hd * HV, (hd + 1) * HV)
            oh = o_ref[:, sl]
            d = d_ref[:, sl]
            rr = lax.rsqrt(jnp.mean(oh * oh, axis=-1, keepdims=True) + LN_EPS)
            yh = oh * rr
            silu, dsilu = _silu_parts(r_ref[:, sl])
            dn = d * silu
            dr_ref[:, sl] = (d * yh * g_ref[...] * dsilu).astype(BF16)
            png = png + jnp.sum(dn * yh, axis=0, keepdims=True)
            dyh = dn * g_ref[...]
            do_ref[:, sl] = rr * (dyh - yh * jnp.mean(dyh * yh, axis=-1, keepdims=True))

        @pl.when(i == 0)
        def _():
            dng_ref[...] = png

        @pl.when(i > 0)
        def _():
            dng_ref[...] += png

    row = pl.BlockSpec((tr, DV), lambda i: (i, 0))
    return pl.pallas_call(
        body, name=name, grid=(L // tr,),
        in_specs=[row, row, pl.BlockSpec((tr, DV), lambda i: (i, r_blk)), pl.BlockSpec((1, HV), lambda i: (0, 0))],
        out_specs=[row, row, pl.BlockSpec((1, HV), lambda i: (0, 0))],
        out_shape=[jax.ShapeDtypeStruct((L, DV), F32), jax.ShapeDtypeStruct((L, DV), BF16),
                   jax.ShapeDtypeStruct((1, HV), F32)],
        compiler_params=_cp("arbitrary"),
    )(dgated, o, P, ng)


def _shift_down(x, halo, s):
    if s == 0:
        return x
    tr = x.shape[0]
    xx = jnp.concatenate([halo, x], axis=0)
    return pltpu.roll(xx, s, axis=0)[8:8 + tr]


def _shift_up(x, halo, s):
    if s == 0:
        return x
    tr = x.shape[0]
    xx = jnp.concatenate([x, halo], axis=0)
    return pltpu.roll(xx, tr + 8 - s, axis=0)[0:tr]


def _conv_taps(x_ref, halo_ref, i, tr):
    x = jnp.where(_rows(i, tr) >= N_PAD, x_ref[...], 0.0)
    halo = jnp.where(i * tr - 8 + lax.broadcasted_iota(jnp.int32, (8, 1), 0) >= N_PAD, halo_ref[...], 0.0)
    return [_shift_down(x, halo, s) for s in range(3)]


def _conv_apply(taps, w_ref, b_ref):
    return taps[2] * w_ref[0:1, :] + taps[1] * w_ref[1:2, :] + taps[0] * w_ref[2:3, :] + b_ref[...]


def _conv_specs(tr, tc):
    blk = pl.BlockSpec((tr, tc), lambda j, i: (i, j))
    halo = pl.BlockSpec((8, tc), lambda j, i: (jnp.maximum(i * (tr // 8) - 1, 0), j))
    w = pl.BlockSpec((3, tc), lambda j, i: (0, j))
    b = pl.BlockSpec((1, tc), lambda j, i: (0, j))
    return blk, halo, w, b


def _conv_act_fwd(Uu, Ug, wu, wg, bu, bg, *, name):
    L, DFF = Uu.shape
    tr = _tile(L, 320, 16)
    tc = _tile(DFF, 512, LANE)

    def body(xu_ref, hu_ref, xg_ref, hg_ref, wu_ref, wg_ref, bu_ref, bg_ref, o_ref):
        i = pl.program_id(1)
        u = _conv_apply(_conv_taps(xu_ref, hu_ref, i, tr), wu_ref, bu_ref)
        g = _conv_apply(_conv_taps(xg_ref, hg_ref, i, tr), wg_ref, bg_ref)
        o_ref[...] = (_silu_parts(g)[0] * u).astype(BF16)

    blk, halo, w, b = _conv_specs(tr, tc)
    return pl.pallas_call(
        body, name=name, grid=(DFF // tc, L // tr),
        in_specs=[blk, halo, blk, halo, w, w, b, b], out_specs=blk,
        out_shape=jax.ShapeDtypeStruct((L, DFF), BF16), compiler_params=_cp("parallel", "parallel"),
    )(Uu, Uu, Ug, Ug, wu, wg, bu, bg)


def _conv_act_bwd(Uu, Ug, wu, wg, bu, bg, dA, *, name):
    L, DFF = Uu.shape
    tr = _tile(L, 320, 16)
    tc = _tile(DFF, 512, LANE)

    def body(xu_ref, hu_ref, xg_ref, hg_ref, wu_ref, wg_ref, bu_ref, bg_ref, da_ref,
             du_ref, dg_ref, dwu_ref, dwg_ref, dbu_ref, dbg_ref):
        i = pl.program_id(1)
        tu = _conv_taps(xu_ref, hu_ref, i, tr)
        tg = _conv_taps(xg_ref, hg_ref, i, tr)
        u = _conv_apply(tu, wu_ref, bu_ref)
        g = _conv_apply(tg, wg_ref, bg_ref)
        silu, dsilu = _silu_parts(g)
        da = da_ref[...]
        du = da * silu
        dg = da * u * dsilu
        du_ref[...] = du
        dg_ref[...] = dg

        @pl.when(i == 0)
        def _():
            dwu_ref[...] = jnp.zeros_like(dwu_ref)
            dwg_ref[...] = jnp.zeros_like(dwg_ref)
            dbu_ref[...] = jnp.zeros_like(dbu_ref)
            dbg_ref[...] = jnp.zeros_like(dbg_ref)

        for j in range(3):
            dwu_ref[j:j + 1, :] += jnp.sum(du * tu[2 - j], axis=0, keepdims=True)
            dwg_ref[j:j + 1, :] += jnp.sum(dg * tg[2 - j], axis=0, keepdims=True)
        dbu_ref[...] += jnp.sum(du, axis=0, keepdims=True)
        dbg_ref[...] += jnp.sum(dg, axis=0, keepdims=True)

    blk, halo, w, b = _conv_specs(tr, tc)
    return pl.pallas_call(
        body, name=name, grid=(DFF // tc, L // tr),
        in_specs=[blk, halo, blk, halo, w, w, b, b, blk], out_specs=[blk, blk, w, w, b, b],
        out_shape=[jax.ShapeDtypeStruct((L, DFF), F32), jax.ShapeDtypeStruct((L, DFF), F32),
                   jax.ShapeDtypeStruct((3, DFF), F32), jax.ShapeDtypeStruct((3, DFF), F32),
                   jax.ShapeDtypeStruct((1, DFF), F32), jax.ShapeDtypeStruct((1, DFF), F32)],
        compiler_params=_cp("parallel", "arbitrary"),
    )(Uu, Uu, Ug, Ug, wu, wg, bu, bg, dA)


def _conv_in_bwd(dh, w, *, name):
    L, DFF = dh.shape
    tr = _tile(L, 320, 16)
    tc = _tile(DFF, 512, LANE)
    nb8 = L // 8

    def body(x_ref, halo_ref, w_ref, o_ref):
        i = pl.program_id(1)
        x = x_ref[...]
        halo = jnp.where((i + 1) * tr + lax.broadcasted_iota(jnp.int32, (8, 1), 0) < L, halo_ref[...], 0.0)
        d = (x * w_ref[2:3, :] + _shift_up(x, halo, 1) * w_ref[1:2, :] + _shift_up(x, halo, 2) * w_ref[0:1, :])
        o_ref[...] = jnp.where(_rows(i, tr) >= N_PAD, d, 0.0).astype(BF16)

    blk = pl.BlockSpec((tr, tc), lambda j, i: (i, j))
    halo = pl.BlockSpec((8, tc), lambda j, i: (jnp.minimum((i + 1) * (tr // 8), nb8 - 1), j))
    return pl.pallas_call(
        body, name=name, grid=(DFF // tc, L // tr),
        in_specs=[blk, halo, pl.BlockSpec((3, tc), lambda j, i: (0, j))], out_specs=blk,
        out_shape=jax.ShapeDtypeStruct((L, DFF), BF16), compiler_params=_cp("parallel", "parallel"),
    )(dh, dh, w)


def _fox_c_fwd(f, bf, *, name):
    L = f.shape[0]
    tr = _tile(L, 320, 16)

    def body(f_ref, b_ref, c_ref, carry):
        i = pl.program_id(0)

        @pl.when(i == 0)
        def _():
            carry[...] = jnp.zeros_like(carry)

        lf = jnp.where(_rows(i, tr) >= N_PAD, _log_sigmoid(f_ref[...] + b_ref[...]), 0.0)
        tri = (lax.broadcasted_iota(jnp.int32, (tr, tr), 0) >= lax.broadcasted_iota(jnp.int32, (tr, tr), 1)).astype(F32)
        c_ref[...] = _dot(tri, lf, 1, 0, precision=HIGHEST) + carry[...]
        carry[...] += jnp.sum(lf, axis=0, keepdims=True)

    return pl.pallas_call(
        body, name=name, grid=(L // tr,),
        in_specs=[pl.BlockSpec((tr, LANE), lambda i: (i, 0)), pl.BlockSpec((1, LANE), lambda i: (0, 0))],
        out_specs=pl.BlockSpec((tr, LANE), lambda i: (i, 0)),
        out_shape=jax.ShapeDtypeStruct((L, LANE), F32), scratch_shapes=[pltpu.VMEM((1, LANE), F32)],
        compiler_params=_cp("arbitrary"),
    )(f, bf)


def _fox_c_bwd(dcs, f, bf, *, name):
    L = f.shape[0]
    tr = _tile(L, 320, 16)
    nb = L // tr
    nd = len(dcs)

    def body(*refs):
        f_ref, b_ref, df_ref, db_ref, carry = refs[nd:]
        s = pl.program_id(0)
        i = nb - 1 - s

        @pl.when(s == 0)
        def _():
            carry[...] = jnp.zeros_like(carry)
            db_ref[...] = jnp.zeros_like(db_ref)

        dc = refs[0][...]
        for r in refs[1:nd]:
            dc = dc + r[...]
        triu = (lax.broadcasted_iota(jnp.int32, (tr, tr), 1) >= lax.broadcasted_iota(jnp.int32, (tr, tr), 0)).astype(F32)
        dlf = _dot(triu, dc, 1, 0, precision=HIGHEST) + carry[...]
        carry[...] += jnp.sum(dc, axis=0, keepdims=True)
        df = jnp.where(_rows(i, tr) >= N_PAD, dlf, 0.0) * _sigmoid(-(f_ref[...] + b_ref[...]))
        df_ref[...] = df.astype(BF16)
        db_ref[...] += jnp.sum(df, axis=0, keepdims=True)

    rev = pl.BlockSpec((tr, LANE), lambda s: (nb - 1 - s, 0))
    vec = pl.BlockSpec((1, LANE), lambda s: (0, 0))
    return pl.pallas_call(
        body, name=name, grid=(nb,), in_specs=[rev] * (nd + 1) + [vec], out_specs=[rev, vec],
        out_shape=[jax.ShapeDtypeStruct((L, LANE), BF16), jax.ShapeDtypeStruct((1, LANE), F32)],
        scratch_shapes=[pltpu.VMEM((1, LANE), F32)], compiler_params=_cp("arbitrary"),
    )(*dcs, f, bf)


AUG = 2 * FOX_HD
FOX_GROUP = 2
FOX_ROW_SPLIT = 5


def _split3(x):
    hi = x.astype(BF16).astype(F32)
    r = x - hi
    mid = r.astype(BF16).astype(F32)
    return hi, mid, (r - mid).astype(BF16).astype(F32)


def _aug_lanes(n, vals):
    lane = lax.broadcasted_iota(jnp.int32, (n, FOX_HD), 1)
    out = jnp.zeros((n, FOX_HD), F32)
    for j, v in enumerate(vals):
        out = jnp.where(lane == j, v, out)
    return out


def _lane_col(x, j):
    lane = lax.broadcasted_iota(jnp.int32, x.shape, 1)
    return jnp.sum(jnp.where(lane == j, x, 0.0), axis=-1, keepdims=True)


def _fox_prep_q(QO, c, H, *, name):
    L = QO.shape[0]
    hd = FOX_HD
    tr = _tile(L, 320, 16)

    def body(q_ref, c_ref, o_ref):
        c = c_ref[...]
        for h in range(H):
            hi, mid, lo = _split3(_lane_col(c, h))
            o_ref[:, h * AUG:h * AUG + hd] = (q_ref[:, h * hd:(h + 1) * hd] * (hd ** -0.5)).astype(BF16)
            o_ref[:, h * AUG + hd:(h + 1) * AUG] = _aug_lanes(tr, [hi, mid, lo, 1.0, 1.0, 1.0]).astype(BF16)

    return pl.pallas_call(
        body, name=name, grid=(L // tr,),
        in_specs=[pl.BlockSpec((tr, H * hd), lambda i: (i, 0)), pl.BlockSpec((tr, LANE), lambda i: (i, 0))],
        out_specs=pl.BlockSpec((tr, H * AUG), lambda i: (i, 0)),
        out_shape=jax.ShapeDtypeStruct((L, H * AUG), BF16), compiler_params=_cp("parallel"),
    )(QO, c)


def _fox_prep_kv(KV, c, H, *, name):
    L = KV.shape[0]
    hd = FOX_HD
    tr = _tile(L, 320, 16)

    def body(k_ref, v_ref, c_ref, ko_ref, vo_ref):
        i = pl.program_id(0)
        c = c_ref[...]
        pad = _rows(i, tr) < N_PAD
        for h in range(H):
            hi, mid, lo = _split3(_lane_col(c, h))
            aug = _aug_lanes(tr, [1.0, 1.0, 1.0, jnp.where(pad, NEG, -hi), jnp.where(pad, 0.0, -mid),
                                  jnp.where(pad, 0.0, -lo)])
            ko_ref[:, h * AUG:h * AUG + hd] = k_ref[:, h * hd:(h + 1) * hd]
            ko_ref[:, h * AUG + hd:(h + 1) * AUG] = aug.astype(BF16)
            vo_ref[:, h * AUG:h * AUG + hd] = v_ref[:, h * hd:(h + 1) * hd]
            vo_ref[:, h * AUG + hd:(h + 1) * AUG] = jnp.ones((tr, hd), BF16)

    wide = pl.BlockSpec((tr, H * AUG), lambda i: (i, 0))
    return pl.pallas_call(
        body, name=name, grid=(L // tr,),
        in_specs=[pl.BlockSpec((tr, H * hd), lambda i: (i, 0)), pl.BlockSpec((tr, H * hd), lambda i: (i, 1)),
                  pl.BlockSpec((tr, LANE), lambda i: (i, 0))],
        out_specs=[wide, wide],
        out_shape=[jax.ShapeDtypeStruct((L, H * AUG), BF16)] * 2, compiler_params=_cp("parallel"),
    )(KV, KV, c)


def _fox_mask(qi, kj, t):
    ti = qi * t + lax.broadcasted_iota(jnp.int32, (t, t), 0)
    si = kj * t + lax.broadcasted_iota(jnp.int32, (t, t), 1)
    return (si <= ti) & ((si >= N_PAD) | (si == ti))


def _fox_attn_fwd(QA, KT, VA, H, *, name):
    L = QA.shape[0]
    hd = FOX_HD
    t = _tile(L, 640, LANE)
    nb = L // t
    G = FOX_GROUP if H % FOX_GROUP == 0 else 1
    nr = FOX_ROW_SPLIT if t % (8 * FOX_ROW_SPLIT) == 0 else 1
    tr = t // nr

    def body(q_ref, k_ref, v_ref, o_ref, lse_ref, m_s, acc):
        qi, kj = pl.program_id(1), pl.program_id(2)

        @pl.when(kj == 0)
        def _():
            m_s[...] = jnp.full_like(m_s, NEG)
            acc[...] = jnp.zeros_like(acc)

        def step(masked):
            for g in range(G):
                cs = slice(g * AUG, (g + 1) * AUG)
                for r in range(nr):
                    rows = slice(r * tr, (r + 1) * tr)
                    s = _dot(q_ref[rows, cs], k_ref[cs, :], 1, 0)
                    if masked:
                        ti = qi * t + r * tr + lax.broadcasted_iota(jnp.int32, (tr, t), 0)
                        si = kj * t + lax.broadcasted_iota(jnp.int32, (tr, t), 1)
                        mask = (si <= ti) & ((si >= N_PAD) | (si == ti))
                        s = jnp.where(mask, s, NEG)
                    m_old = m_s[g, rows]
                    m_new = jnp.maximum(m_old, jnp.max(s, axis=-1, keepdims=True))
                    p = jnp.exp(s - m_new)
                    if masked:
                        p = jnp.where(mask, p, 0.0)
                    acc[g, rows] = jnp.exp(m_old - m_new) * acc[g, rows] + _dot(p.astype(BF16), v_ref[:, cs], 1, 0)
                    m_s[g, rows] = m_new

        @pl.when(kj < qi)
        def _():
            step(False)

        @pl.when(kj == qi)
        def _():
            step(True)

        @pl.when(kj == nb - 1)
        def _():
            for g in range(G):
                a = acc[g]
                l = a[:, hd:]
                o_ref[:, g * hd:(g + 1) * hd] = a[:, :hd] / l
                lse_ref[g] = m_s[g] + jnp.log(jnp.max(l, axis=-1, keepdims=True))

    return pl.pallas_call(
        body, name=name, grid=(H // G, nb, nb),
        in_specs=[pl.BlockSpec((t, G * AUG), lambda h, qi, kj: (qi, h)),
                  pl.BlockSpec((G * AUG, t), lambda h, qi, kj: (h, jnp.minimum(kj, qi))),
                  pl.BlockSpec((t, G * AUG), lambda h, qi, kj: (jnp.minimum(kj, qi), h))],
        out_specs=[pl.BlockSpec((t, G * hd), lambda h, qi, kj: (qi, h)),
                   pl.BlockSpec((G, t, 1), lambda h, qi, kj: (h, qi, 0))],
        out_shape=[jax.ShapeDtypeStruct((L, H * hd), F32), jax.ShapeDtypeStruct((H, L, 1), F32)],
        scratch_shapes=[pltpu.VMEM((G, t, 1), F32), pltpu.VMEM((G, t, AUG), F32)],
        compiler_params=_cp("parallel", "parallel", "arbitrary"),
    )(QA, KT, VA)


def _fox_attn_bwd(QA, KA, VA, DOA, lse, init, H, *, name):
    L = QA.shape[0]
    t = _tile(L, 640, LANE)
    nb = L // t

    def body(q_ref, k_ref, v_ref, do_ref, lse_ref, dk0_ref, dv0_ref, dq_ref, dk_ref, dv_ref):
        kj, qi = pl.program_id(1), pl.program_id(2)

        @pl.when((kj == 0) & (qi == 0))
        def _():
            dq_ref[...] = jnp.zeros_like(dq_ref)

        @pl.when(qi == 0)
        def _():
            dk_ref[...] = dk0_ref[...]
            dv_ref[...] = dv0_ref[...]

        def step(masked):
            q, k, doa = q_ref[...], k_ref[...], do_ref[...]
            p = jnp.exp(_dot(q, k, 1, 1) - lse_ref[0])
            if masked:
                p = jnp.where(_fox_mask(qi, kj, t), p, 0.0)
            pb = p.astype(BF16)
            ds = (p * _dot(doa, v_ref[...], 1, 1)).astype(BF16)
            dv_ref[...] += _dot(pb, doa, 0, 0)
            dk_ref[...] += _dot(ds, q, 0, 0)
            rows = pl.ds(pl.multiple_of(qi * t, t), t)
            dq_ref[rows, :] += _dot(ds, k, 1, 0)

        @pl.when(qi > kj)
        def _():
            step(False)

        @pl.when(qi == kj)
        def _():
            step(True)

    qb = pl.BlockSpec((t, AUG), lambda h, kj, qi: (jnp.maximum(qi, kj), h))
    kb = pl.BlockSpec((t, AUG), lambda h, kj, qi: (kj, h))
    return pl.pallas_call(
        body, name=name, grid=(H, nb, nb),
        in_specs=[qb, kb, kb, qb, pl.BlockSpec((1, t, 1), lambda h, kj, qi: (h, jnp.maximum(qi, kj), 0)), kb, kb],
        out_specs=[pl.BlockSpec((L, AUG), lambda h, kj, qi: (0, h)), kb, kb],
        out_shape=[jax.ShapeDtypeStruct((L, H * AUG), F32)] * 3,
        compiler_params=_cp("parallel", "arbitrary", "arbitrary"),
    )(QA, KA, VA, DOA, lse, *init)


def _fox_post_q(DQA, H, *, name):
    L = DQA.shape[0]
    hd = FOX_HD
    tr = _tile(L, 320, 16)

    def body(x_ref, dq_ref, dc_ref):
        lane = lax.broadcasted_iota(jnp.int32, (tr, LANE), 1)
        dc = jnp.zeros((tr, LANE), F32)
        for h in range(H):
            dq_ref[:, h * hd:(h + 1) * hd] = (x_ref[:, h * AUG:h * AUG + hd] * (hd ** -0.5)).astype(BF16)
            dc = jnp.where(lane == h, _lane_col(x_ref[:, h * AUG + hd:(h + 1) * AUG], 0), dc)
        dc_ref[...] = dc

    return pl.pallas_call(
        body, name=name, grid=(L // tr,), in_specs=[pl.BlockSpec((tr, H * AUG), lambda i: (i, 0))],
        out_specs=[pl.BlockSpec((tr, H * hd), lambda i: (i, 0)), pl.BlockSpec((tr, LANE), lambda i: (i, 0))],
        out_shape=[jax.ShapeDtypeStruct((L, H * hd), BF16), jax.ShapeDtypeStruct((L, LANE), F32)],
        compiler_params=_cp("parallel"),
    )(DQA)


def _fox_post_kv(DKA, DVA, H, *, name):
    L = DKA.shape[0]
    hd = FOX_HD
    tr = _tile(L, 320, 16)

    def body(k_ref, v_ref, o_ref, dc_ref):
        lane = lax.broadcasted_iota(jnp.int32, (tr, LANE), 1)
        dc = jnp.zeros((tr, LANE), F32)
        for h in range(H):
            o_ref[:, h * hd:(h + 1) * hd] = k_ref[:, h * AUG:h * AUG + hd].astype(BF16)
            o_ref[:, (H + h) * hd:(H + h + 1) * hd] = v_ref[:, h * AUG:h * AUG + hd].astype(BF16)
            dc = jnp.where(lane == h, -_lane_col(k_ref[:, h * AUG + hd:(h + 1) * AUG], 3), dc)
        dc_ref[...] = dc

    wide = pl.BlockSpec((tr, H * AUG), lambda i: (i, 0))
    return pl.pallas_call(
        body, name=name, grid=(L // tr,), in_specs=[wide, wide],
        out_specs=[pl.BlockSpec((tr, 2 * H * hd), lambda i: (i, 0)), pl.BlockSpec((tr, LANE), lambda i: (i, 0))],
        out_shape=[jax.ShapeDtypeStruct((L, 2 * H * hd), BF16), jax.ShapeDtypeStruct((L, LANE), F32)],
        compiler_params=_cp("parallel"),
    )(DKA, DVA)


def _fox_gate_fwd(o, QO, *, name):
    L, D = o.shape
    tr = _tile(L, 320, 16)

    def body(o_ref, g_ref, out_ref):
        out_ref[...] = (o_ref[...] * _sigmoid(g_ref[...])).astype(BF16)

    row = pl.BlockSpec((tr, D), lambda i: (i, 0))
    return pl.pallas_call(
        body, name=name, grid=(L // tr,), in_specs=[row, pl.BlockSpec((tr, D), lambda i: (i, 1))], out_specs=row,
        out_shape=jax.ShapeDtypeStruct((L, D), BF16), compiler_params=_cp("parallel"),
    )(o, QO)


def _fox_gate_bwd(d, o, QO, H, *, name):
    L, D = o.shape
    hd = FOX_HD
    tr = _tile(L, 320, 16)

    def body(d_ref, o_ref, g_ref, do_ref, dg_ref):
        for h in range(H):
            sl = slice(h * hd, (h + 1) * hd)
            s = _sigmoid(g_ref[:, sl])
            d = d_ref[:, sl]
            o = o_ref[:, sl]
            do = d * s
            dg_ref[:, sl] = (d * o * s * (1.0 - s)).astype(BF16)
            hi, mid, lo = _split3(-jnp.sum(do * o, axis=-1, keepdims=True))
            do_ref[:, h * AUG:h * AUG + hd] = do.astype(BF16)
            do_ref[:, h * AUG + hd:(h + 1) * AUG] = _aug_lanes(tr, [hi, mid, lo]).astype(BF16)

    row = pl.BlockSpec((tr, D), lambda i: (i, 0))
    return pl.pallas_call(
        body, name=name, grid=(L // tr,), in_specs=[row, row, pl.BlockSpec((tr, D), lambda i: (i, 1))],
        out_specs=[pl.BlockSpec((tr, H * AUG), lambda i: (i, 0)), row],
        out_shape=[jax.ShapeDtypeStruct((L, H * AUG), BF16), jax.ShapeDtypeStruct((L, D), BF16)],
        compiler_params=_cp("parallel"),
    )(d, o, QO)


def _row_tile(R, C, n_arrays):
    budget = VMEM_LIMIT // (3 * n_arrays * 4 * max(C, LANE))
    return _tile(R, max(16, budget // 16 * 16), 16)


def _sum_parts(parts, *, name, out_dtype=F32):
    R, C = parts[0].shape
    tr = _row_tile(R, C, len(parts) + 1)

    def body(*refs):
        acc = refs[0][...].astype(F32)
        for r in refs[1:-1]:
            acc = acc + r[...].astype(F32)
        refs[-1][...] = acc.astype(out_dtype)

    blk = pl.BlockSpec((tr, C), lambda i: (i, 0))
    return pl.pallas_call(
        body, name=name, grid=(R // tr,), in_specs=[blk] * len(parts), out_specs=blk,
        out_shape=jax.ShapeDtypeStruct((R, C), out_dtype), compiler_params=_cp("parallel"),
    )(*parts)


def _sum_slots(x, *, name):
    S, R, C = x.shape
    tr = _row_tile(R, C, S + 1)

    def body(x_ref, o_ref):
        acc = x_ref[0].astype(F32)
        for s in range(1, S):
            acc = acc + x_ref[s].astype(F32)
        o_ref[...] = acc

    return pl.pallas_call(
        body, name=name, grid=(R // tr,), in_specs=[pl.BlockSpec((S, tr, C), lambda i: (0, i, 0))],
        out_specs=pl.BlockSpec((tr, C), lambda i: (i, 0)),
        out_shape=jax.ShapeDtypeStruct((R, C), F32), compiler_params=_cp("parallel"),
    )(x)


def _adamw(w, m, v, gparts, *, name):
    R, C = w.shape
    tr = _row_tile(R, C, 7 + len(gparts))
    ng = len(gparts)

    def body(*refs):
        w_ref, m_ref, v_ref = refs[:3]
        g = refs[3][...]
        for r in refs[4:3 + ng]:
            g = g + r[...]
        g_ref, d_ref, nm_ref, nv_ref = refs[3 + ng:]
        nm = ADAM_B1 * m_ref[...] + (1.0 - ADAM_B1) * g
        nv = ADAM_B2 * v_ref[...] + (1.0 - ADAM_B2) * (g * g)
        m_hat = nm / (1.0 - ADAM_B1 ** ADAM_STEP)
        v_hat = nv / (1.0 - ADAM_B2 ** ADAM_STEP)
        g_ref[...] = g
        d_ref[...] = -ADAM_LR * (m_hat / (jnp.sqrt(v_hat) + ADAM_EPS) + ADAM_WD * w_ref[...])
        nm_ref[...] = nm
        nv_ref[...] = nv

    blk = pl.BlockSpec((tr, C), lambda i: (i, 0))
    return pl.pallas_call(
        body, name=name, grid=(R // tr,), in_specs=[blk] * (3 + ng), out_specs=[blk] * 4,
        out_shape=[jax.ShapeDtypeStruct((R, C), F32)] * 4, compiler_params=_cp("parallel"),
    )(w, m, v, *gparts)


def _chip_peers():
    x, y, c = lax.axis_index("x"), lax.axis_index("y"), lax.axis_index("c")
    return (x, y, c), [(1 - x, y), (x, 1 - y), (1 - x, 1 - y)]


def _gather_chips(arrs, *, name):
    n = len(arrs)
    hs = [a.shape[0] // 2 for a in arrs]

    def body(*refs):
        ins, outs = refs[:n], refs[n:2 * n]
        send, recv, loc = refs[2 * n:]
        (x, y, c), chips = _chip_peers()
        me = 2 * x + y

        def landing(a, chip_idx):
            return outs[a].at[chip_idx, pl.ds(c * hs[a], hs[a])]

        owns, ici, passed = [], [], []
        for a in range(n):
            own = pltpu.make_async_copy(ins[a], outs[a].at[me], loc.at[a])
            own.start()
            owns.append(own)
            for j, (px, py) in enumerate(chips):
                cp = pltpu.make_async_remote_copy(
                    src_ref=ins[a].at[pl.ds(c * hs[a], hs[a])], dst_ref=landing(a, me), send_sem=send.at[a, j],
                    recv_sem=recv.at[a, j], device_id=(px, py, c), device_id_type=MESH)
                cp.start()
                ici.append(cp)
        for a in range(n):
            for j, (px, py) in enumerate(chips):
                ici[3 * a + j].wait_recv()
                src = landing(a, 2 * px + py)
                cp = pltpu.make_async_remote_copy(
                    src_ref=src, dst_ref=src, send_sem=send.at[a, 3 + j], recv_sem=recv.at[a, 3 + j],
                    device_id=(x, y, 1 - c), device_id_type=MESH)
                cp.start()
                passed.append(cp)
        for cp in ici:
            cp.wait_send()
        for cp in passed + owns:
            cp.wait()

    return pl.pallas_call(
        body, name=name, in_specs=[ANY] * n, out_specs=[ANY] * n,
        out_shape=[jax.ShapeDtypeStruct((4,) + a.shape, a.dtype) for a in arrs],
        scratch_shapes=[pltpu.SemaphoreType.DMA((n, 6)), pltpu.SemaphoreType.DMA((n, 6)), pltpu.SemaphoreType.DMA((n,))],
    )(*arrs)


def _scatter_chips(arrs, *, name):
    n = len(arrs)

    def body(*refs):
        ins, outs = refs[:n], refs[n:2 * n]
        send, recv = refs[2 * n:]
        (x, y, c), chips = _chip_peers()
        copies = []
        for a in range(n):
            for j, (px, py) in enumerate(chips):
                cp = pltpu.make_async_remote_copy(
                    src_ref=ins[a].at[2 * px + py], dst_ref=outs[a].at[j], send_sem=send.at[a, j],
                    recv_sem=recv.at[a, j], device_id=(px, py, c), device_id_type=MESH)
                cp.start()
                copies.append(cp)
        for cp in copies:
            cp.wait()

    return pl.pallas_call(
        body, name=name, in_specs=[ANY] * n, out_specs=[ANY] * n,
        out_shape=[jax.ShapeDtypeStruct((3,) + a.shape[1:], a.dtype) for a in arrs],
        scratch_shapes=[pltpu.SemaphoreType.DMA((n, 3)), pltpu.SemaphoreType.DMA((n, 3))],
    )(*arrs)


def _send_other_halves(arrs, *, name):
    n = len(arrs)
    hs = [a.shape[1] // 2 for a in arrs]

    def body(*refs):
        ins, outs = refs[:n], refs[n:2 * n]
        send, recv = refs[2 * n:]
        x, y, c = lax.axis_index("x"), lax.axis_index("y"), lax.axis_index("c")
        copies = []
        for a in range(n):
            cp = pltpu.make_async_remote_copy(
                src_ref=ins[a].at[pl.ds(0, 4), pl.ds((1 - c) * hs[a], hs[a])], dst_ref=outs[a], send_sem=send.at[a],
                recv_sem=recv.at[a], device_id=(x, y, 1 - c), device_id_type=MESH)
            cp.start()
            copies.append(cp)
        for cp in copies:
            cp.wait()

    return pl.pallas_call(
        body, name=name, in_specs=[ANY] * n, out_specs=[ANY] * n,
        out_shape=[jax.ShapeDtypeStruct((4, h) + a.shape[2:], a.dtype) for a, h in zip(arrs, hs)],
        scratch_shapes=[pltpu.SemaphoreType.DMA((n,)), pltpu.SemaphoreType.DMA((n,))],
    )(*arrs)


def _join_halves(arrs, *, name):
    n = len(arrs)

    def body(*refs):
        ins, outs = refs[:n], refs[n:2 * n]
        send, recv, loc = refs[2 * n:]
        x, y, c = lax.axis_index("x"), lax.axis_index("y"), lax.axis_index("c")
        copies = []
        for a in range(n):
            h = ins[a].shape[0]
            dst = outs[a].at[pl.ds(c * h, h)]
            own = pltpu.make_async_copy(ins[a], dst, loc.at[a])
            own.start()
            cp = pltpu.make_async_remote_copy(
                src_ref=ins[a], dst_ref=dst, send_sem=send.at[a], recv_sem=recv.at[a],
                device_id=(x, y, 1 - c), device_id_type=MESH)
            cp.start()
            copies += [own, cp]
        for cp in copies:
            cp.wait()

    return pl.pallas_call(
        body, name=name, in_specs=[ANY] * n, out_specs=[ANY] * n,
        out_shape=[jax.ShapeDtypeStruct((2 * a.shape[0],) + a.shape[1:], a.dtype) for a in arrs],
        scratch_shapes=[pltpu.SemaphoreType.DMA((n,)), pltpu.SemaphoreType.DMA((n,)), pltpu.SemaphoreType.DMA((n,))],
    )(*arrs)


def _gather_all(a, *, name):
    def body(in_ref, out_ref, send, recv, loc):
        x, y, c = lax.axis_index("x"), lax.axis_index("y"), lax.axis_index("c")
        me = 4 * x + 2 * y + c
        own = pltpu.make_async_copy(in_ref, out_ref.at[me], loc)
        own.start()
        copies = [own]
        for j in range(1, 8):
            fx, fy, fc = (j >> 2) & 1, (j >> 1) & 1, j & 1
            peer = (x ^ fx, y ^ fy, c ^ fc)
            cp = pltpu.make_async_remote_copy(
                src_ref=in_ref, dst_ref=out_ref.at[me], send_sem=send.at[j - 1], recv_sem=recv.at[j - 1],
                device_id=peer, device_id_type=MESH)
            cp.start()
            copies.append(cp)
        for cp in copies:
            cp.wait()

    return pl.pallas_call(
        body, name=name, in_specs=[ANY], out_specs=ANY,
        out_shape=jax.ShapeDtypeStruct((8,) + a.shape, a.dtype),
        scratch_shapes=[pltpu.SemaphoreType.DMA((7,)), pltpu.SemaphoreType.DMA((7,)), pltpu.SemaphoreType.DMA(())],
    )(a)


def _pack(arrs):
    flat = jnp.concatenate([a.astype(F32).reshape(-1) for a in arrs])
    n = flat.shape[0]
    pad = (-n) % (16 * LANE)
    return jnp.pad(flat, (0, pad)).reshape(-1, LANE)


def _unpack(buf, shapes):
    flat = buf.reshape(-1)
    out, off = [], 0
    for s in shapes:
        n = int(np.prod(s))
        out.append(flat[off:off + n].reshape(s))
        off += n
    return out


def _to_shards(g, axis):
    parts = jnp.split(g, 4, axis=axis)
    return jnp.stack([p.reshape(-1, p.shape[-1]) for p in parts])


def kernel(x, meta, ln_g, ln_b, gla_w_in, gla_w_g2, gla_b_g2, gla_norm_g, gla_w_out, kv_w, kv_bf, fox_w_in, fox_w_out, ffn_w_up, ffn_conv_w, ffn_conv_b, ffn_w_down, loss_target, m_meta, m_ln_g, m_ln_b, m_gla_w_in, m_gla_w_g2, m_gla_b_g2, m_gla_norm_g, m_gla_w_out, m_kv_w, m_kv_bf, m_fox_w_in, m_fox_w_out, m_ffn_w_up, m_ffn_conv_w, m_ffn_conv_b, m_ffn_w_down, v_meta, v_ln_g, v_ln_b, v_gla_w_in, v_gla_w_g2, v_gla_b_g2, v_gla_norm_g, v_gla_w_out, v_kv_w, v_kv_bf, v_fox_w_in, v_fox_w_out, v_ffn_w_up, v_ffn_conv_w, v_ffn_conv_b, v_ffn_w_down):
    D = x.shape[-1]
    L = x.shape[1] + FRONT
    HG = GLA_HEADS
    DK, DV = D // 2, D
    HK, HV = DK // HG, DV // HG
    HF = D // FOX_HD
    DFF = ffn_w_down.shape[1] * 4
    chip = 2 * lax.axis_index("x") + lax.axis_index("y")

    big_names = ["gla_w_in", "gla_w_out", "kv_w", "fox_w_in", "fox_w_out", "ffn_w_up", "ffn_w_down"]
    big = dict(gla_w_in=gla_w_in, gla_w_out=gla_w_out, kv_w=kv_w, fox_w_in=fox_w_in, fox_w_out=fox_w_out,
               ffn_w_up=ffn_w_up, ffn_w_down=ffn_w_down)
    big_axis = dict(gla_w_in=2, gla_w_out=1, kv_w=1, fox_w_in=2, fox_w_out=1, ffn_w_up=2, ffn_w_down=1)
    small_names = ["meta", "ln_g", "ln_b", "gla_w_g2", "gla_b_g2", "gla_norm_g", "ffn_conv_w"]
    small = dict(meta=meta, ln_g=ln_g, ln_b=ln_b, gla_w_g2=gla_w_g2, gla_b_g2=gla_b_g2, gla_norm_g=gla_norm_g,
                 ffn_conv_w=ffn_conv_w)
    small_shapes = [small[k].shape for k in small_names]
    gathered = _gather_chips([big[k].astype(BF16) for k in big_names] + [_pack([small[k] for k in small_names])],
                             name="gather_weights")
    full = {k: jnp.concatenate([gathered[i][s] for s in range(4)], axis=big_axis[k]) for i, k in enumerate(big_names)}
    sm_sh = [_unpack(gathered[-1][s], small_shapes) for s in range(4)]
    fs = {k: jnp.concatenate([sm_sh[s][i] for s in range(4)], axis=-1) for i, k in enumerate(small_names)}

    pad_cols = lambda w: jnp.pad(w, ((0, 0), (0, LANE - w.shape[1])))
    W_P, W_Pgl = [], []
    for l in range(N_A_LAYERS):
        w = full["gla_w_in"][l]
        W_P.append(jnp.concatenate([w[:, :2 * DK + DV], w[:, 2 * DK + DV + GLA_RANK:],
                                    pad_cols(w[:, 2 * DK + DV:2 * DK + DV + GLA_RANK])], axis=1))
    w2p = [jnp.pad(fs["gla_w_g2"][l], ((0, LANE - GLA_RANK), (0, 0))).astype(BF16) for l in range(N_A_LAYERS)]
    W_kv = full["kv_w"][:, :2 * D]
    W_f = pad_cols(full["kv_w"][:, 2 * D:])
    W_kvf = jnp.concatenate([W_kv, W_f], axis=1)
    bf_pad = jnp.pad(kv_bf, (0, LANE - HF)).reshape(1, LANE)
    W_u = [full["ffn_w_up"][l][:, :DFF] for l in range(DEPTH)]
    W_g = [full["ffn_w_up"][l][:, DFF:] for l in range(DEPTH)]
    cw_u = [fs["ffn_conv_w"][l][:, :DFF] for l in range(DEPTH)]
    cw_g = [fs["ffn_conv_w"][l][:, DFF:] for l in range(DEPTH)]
    cb_u = [ffn_conv_b[l][None, :DFF] for l in range(DEPTH)]
    cb_g = [ffn_conv_b[l][None, DFF:] for l in range(DEPTH)]
    gl_blk = (2 * DK + 2 * DV) // LANE
    r_blk = (2 * DK + DV) // DV

    h = jnp.concatenate([jnp.concatenate([jnp.zeros((N_PAD, D), F32), fs["meta"]], axis=0), x[0]], axis=0)
    hb = h.astype(BF16)
    saved = []
    kvs = None
    for l in range(DEPTH):
        s = dict(h=h, hb=hb)
        if l < N_A_LAYERS:
            s["P"] = _mm(hb, W_P[l], name=f"gla_in_{l}")
            s["la"] = _gla_gate_fwd(s["P"], w2p[l], fs["gla_b_g2"][l][None], gl_blk, name=f"gla_gate_{l}")
            s["o"], s["S"] = _gla_chunk_fwd(s["P"], s["la"], HG, HK, HV, name=f"gla_chunk_{l}")
            s["gated"] = _gla_post_fwd(s["o"], s["P"], fs["gla_norm_g"][l][None], HG, HV, r_blk, name=f"gla_post_{l}")
            s["mix"] = _mm(s["gated"], full["gla_w_out"][l], name=f"gla_out_{l}")
        else:
            j = l - N_A_LAYERS
            if kvs is None:
                KV = _mm(hb, W_kv, out_dtype=BF16, name="kv_proj")
                f = _mm(hb, W_f, name="kv_gate_proj")
                c = _fox_c_fwd(f, bf_pad, name="fox_c")
                KA, VA = _fox_prep_kv(KV, c, HF, name="fox_prep_kv")
                kvs = dict(KA=KA, KT=KA.T, VA=VA, f=f, c=c, hb=hb)
            s["QO"] = _mm(hb, full["fox_w_in"][j], name=f"fox_in_{j}")
            s["QA"] = _fox_prep_q(s["QO"], kvs["c"], HF, name=f"fox_prep_q_{j}")
            s["o"], s["lse"] = _fox_attn_fwd(s["QA"], kvs["KT"], kvs["VA"], HF, name=f"fox_attn_{j}")
            s["gated"] = _fox_gate_fwd(s["o"], s["QO"], name=f"fox_gate_{j}")
            s["mix"] = _mm(s["gated"], full["fox_w_out"][j], name=f"fox_out_{j}")
        s["h1"], s["h1b"] = _ln_fwd(h, s["mix"], fs["ln_g"][l, 0][None], fs["ln_b"][l, 0][None], name=f"ln_a_{l}")
        s["Uu"] = _mm(s["h1b"], W_u[l], name=f"ffn_up_u_{l}")
        s["Ug"] = _mm(s["h1b"], W_g[l], name=f"ffn_up_g_{l}")
        s["a"] = _conv_act_fwd(s["Uu"], s["Ug"], cw_u[l], cw_g[l], cb_u[l], cb_g[l], name=f"ffn_conv_{l}")
        s["ffn"] = _mm(s["a"], full["ffn_w_down"][l], name=f"ffn_down_{l}")
        h, hb = _ln_fwd(s["h1"], s["ffn"], fs["ln_g"][l, 1][None], fs["ln_b"][l, 1][None], name=f"ln_b_{l}")
        saved.append(s)

    loss_acc, dh = _loss(h, loss_target[0], name="loss")

    gW = {}
    d_ln_g = [[None, None] for _ in range(DEPTH)]
    d_ln_b = [[None, None] for _ in range(DEPTH)]
    d_cw, d_cb = [None] * DEPTH, [None] * DEPTH
    d_wg2, d_bg2, d_ng = [None] * N_A_LAYERS, [None] * N_A_LAYERS, [None] * N_A_LAYERS
    dkv = (jnp.zeros((L, HF * AUG), F32), jnp.zeros((L, HF * AUG), F32))
    dcqs = []
    for l in reversed(range(DEPTH)):
        s = saved[l]
        dz, dzb, d_ln_g[l][1], d_ln_b[l][1] = _ln_bwd(dh, s["h1"], s["ffn"], fs["ln_g"][l, 1][None], name=f"ln_b_bwd_{l}")
        dA = _mm(dzb, full["ffn_w_down"][l], tb=True, name=f"ffn_down_dx_{l}")
        gW[("ffn_w_down", l)] = _mm(s["a"], dzb, ta=True, out_dtype=BF16, name=f"ffn_down_dw_{l}")
        dcu, dcg, dwu, dwg, dbu, dbg = _conv_act_bwd(s["Uu"], s["Ug"], cw_u[l], cw_g[l], cb_u[l], cb_g[l], dA,
                                                     name=f"ffn_conv_bwd_{l}")
        d_cw[l] = jnp.concatenate([dwu, dwg], axis=1)
        d_cb[l] = jnp.concatenate([dbu, dbg], axis=1)[0]
        dUu = _conv_in_bwd(dcu, cw_u[l], name=f"ffn_conv_dx_u_{l}")
        dUg = _conv_in_bwd(dcg, cw_g[l], name=f"ffn_conv_dx_g_{l}")
        gW[("ffn_w_up", l)] = jnp.concatenate(
            [_mm(s["h1b"], dUu, ta=True, out_dtype=BF16, name=f"ffn_up_dw_u_{l}"),
             _mm(s["h1b"], dUg, ta=True, out_dtype=BF16, name=f"ffn_up_dw_g_{l}")], axis=1)
        dh1 = _mm(dUu, W_u[l], tb=True, add=dz, add_scale=ALPHA, name=f"ffn_up_dx_u_{l}")
        dh1 = _mm(dUg, W_g[l], tb=True, add=dh1, name=f"ffn_up_dx_g_{l}")
        dz, dzb, d_ln_g[l][0], d_ln_b[l][0] = _ln_bwd(dh1, s["h"], s["mix"], fs["ln_g"][l, 0][None], name=f"ln_a_bwd_{l}")
        if l < N_A_LAYERS:
            dgated = _mm(dzb, full["gla_w_out"][l], tb=True, name=f"gla_out_dx_{l}")
            gW[("gla_w_out", l)] = _mm(s["gated"], dzb, ta=True, out_dtype=BF16, name=f"gla_out_dw_{l}")
            do, drb, d_ng[l] = _gla_post_bwd(dgated, s["o"], s["P"], fs["gla_norm_g"][l][None], HG, HV, r_blk,
                                             name=f"gla_post_bwd_{l}")
            dq, dk, dvb, dla = _gla_chunk_bwd(s["P"], s["la"], s["S"], do, HG, HK, HV, name=f"gla_chunk_bwd_{l}")
            dglb, dw2, d_bg2[l] = _gla_gate_bwd(dla, s["P"], w2p[l], fs["gla_b_g2"][l][None], gl_blk,
                                               name=f"gla_gate_bwd_{l}")
            d_wg2[l] = dw2[:GLA_RANK]
            dP = jnp.concatenate([dq, dk, dvb, drb, dglb], axis=1)
            gP = _mm(s["hb"], dP, ta=True, out_dtype=BF16, name=f"gla_in_dw_{l}")
            gW[("gla_w_in", l)] = jnp.concatenate(
                [gP[:, :2 * DK + DV], gP[:, 2 * DK + 2 * DV:2 * DK + 2 * DV + GLA_RANK], gP[:, 2 * DK + DV:2 * DK + 2 * DV]],
                axis=1)
            dh = _mm(dP, W_P[l], tb=True, add=dz, add_scale=ALPHA, name=f"gla_in_dx_{l}")
        else:
            j = l - N_A_LAYERS
            dgo = _mm(dzb, full["fox_w_out"][j], tb=True, name=f"fox_out_dx_{j}")
            gW[("fox_w_out", j)] = _mm(s["gated"], dzb, ta=True, out_dtype=BF16, name=f"fox_out_dw_{j}")
            DOA, dogb = _fox_gate_bwd(dgo, s["o"], s["QO"], HF, name=f"fox_gate_bwd_{j}")
            DQA, DKA, DVA = _fox_attn_bwd(s["QA"], kvs["KA"], kvs["VA"], DOA, s["lse"], dkv, HF, name=f"fox_attn_bwd_{j}")
            dkv = (DKA, DVA)
            dqb, dcq = _fox_post_q(DQA, HF, name=f"fox_post_q_{j}")
            dcqs.append(dcq)
            dQO = jnp.concatenate([dqb, dogb], axis=1)
            gW[("fox_w_in", j)] = _mm(s["hb"], dQO, ta=True, out_dtype=BF16, name=f"fox_in_dw_{j}")
            dh = _mm(dQO, full["fox_w_in"][j], tb=True, add=dz, add_scale=ALPHA, name=f"fox_in_dx_{j}")
            if j == 0:
                dkvb, dck = _fox_post_kv(DKA, DVA, HF, name="fox_post_kv")
                dfb, d_bf = _fox_c_bwd(dcqs + [dck], kvs["f"], bf_pad, name="fox_c_bwd")
                dKVF = jnp.concatenate([dkvb, dfb], axis=1)
                gkv = _mm(kvs["hb"], dKVF, ta=True, out_dtype=BF16, name="kv_dw")
                gW[("kv_w", 0)] = gkv[:, :2 * D + HF]
                dh = _mm(dKVF, W_kvf, tb=True, add=dh, name="kv_dx")

    stack = lambda k, n: jnp.stack([gW[(k, i)] for i in range(n)])
    gfull = dict(gla_w_in=stack("gla_w_in", N_A_LAYERS), gla_w_out=stack("gla_w_out", N_A_LAYERS), kv_w=gW[("kv_w", 0)],
                 fox_w_in=stack("fox_w_in", DEPTH - N_A_LAYERS), fox_w_out=stack("fox_w_out", DEPTH - N_A_LAYERS),
                 ffn_w_up=stack("ffn_w_up", DEPTH), ffn_w_down=stack("ffn_w_down", DEPTH))
    core = lax.axis_index("c")
    g4 = [_to_shards(gfull[k], big_axis[k]) for k in big_names]
    sib = _send_other_halves(g4, name="pair_exchange")
    pair = []
    for i, k in enumerate(big_names):
        _, R, C = g4[i].shape
        mine = lax.dynamic_slice_in_dim(g4[i], core * (R // 2), R // 2, axis=1)
        pair.append(_sum_parts([mine.reshape(2 * R, C), sib[i].reshape(2 * R, C)], out_dtype=BF16,
                               name=f"sum_pair_{k}").reshape(4, R // 2, C))
    recv = _scatter_chips(pair, name="scatter_grads")
    halves = []
    for i, k in enumerate(big_names):
        own = lax.dynamic_index_in_dim(pair[i], chip, axis=0, keepdims=False)
        halves.append(_sum_parts([own, recv[i][0], recv[i][1], recv[i][2]], name=f"sum_chips_{k}"))
    gsum = _join_halves(halves, name="join_halves")
    moments = dict(gla_w_in=(m_gla_w_in, v_gla_w_in), gla_w_out=(m_gla_w_out, v_gla_w_out), kv_w=(m_kv_w, v_kv_w),
                   fox_w_in=(m_fox_w_in, v_fox_w_in), fox_w_out=(m_fox_w_out, v_fox_w_out),
                   ffn_w_up=(m_ffn_w_up, v_ffn_w_up), ffn_w_down=(m_ffn_w_down, v_ffn_w_down))
    res = {}
    for i, k in enumerate(big_names):
        w = big[k]
        sh = w.shape
        flat = lambda a: a.reshape(-1, sh[-1])
        outs = _adamw(flat(w), flat(moments[k][0]), flat(moments[k][1]), [gsum[i]], name=f"adamw_{k}")
        res[k] = [o.reshape(sh) for o in outs]

    dmeta = dh[N_PAD:FRONT]
    sg = dict(meta=dmeta,
              ln_g=jnp.stack([jnp.concatenate(d_ln_g[l], axis=0) for l in range(DEPTH)]),
              ln_b=jnp.stack([jnp.concatenate(d_ln_b[l], axis=0) for l in range(DEPTH)]),
              gla_w_g2=jnp.stack(d_wg2), gla_b_g2=jnp.stack([d[0] for d in d_bg2]),
              gla_norm_g=jnp.stack([d[0] for d in d_ng]), ffn_conv_w=jnp.stack(d_cw),
              kv_bf=d_bf[0, :HF], ffn_conv_b=jnp.stack(d_cb), loss=loss_acc[0, :1])
    sg_names = small_names + ["kv_bf", "ffn_conv_b", "loss"]
    sg_shapes = [sg[k].shape for k in sg_names]
    red = _sum_slots(_gather_all(_pack([sg[k] for k in sg_names]), name="gather_small_grads"), name="sum_small_grads")
    red = dict(zip(sg_names, _unpack(red, sg_shapes)))
    loss = red["loss"][0]
    loc = {}
    for k in small_names:
        wdt = small[k].shape[-1]
        loc[k] = lax.dynamic_slice_in_dim(red[k], chip * wdt, wdt, axis=red[k].ndim - 1)
    loc["kv_bf"] = red["kv_bf"]
    loc["ffn_conv_b"] = red["ffn_conv_b"]
    sm_all = small_names + ["kv_bf", "ffn_conv_b"]
    sw = dict(small, kv_bf=kv_bf, ffn_conv_b=ffn_conv_b)
    sm_m = dict(meta=m_meta, ln_g=m_ln_g, ln_b=m_ln_b, gla_w_g2=m_gla_w_g2, gla_b_g2=m_gla_b_g2, gla_norm_g=m_gla_norm_g,
                ffn_conv_w=m_ffn_conv_w, kv_bf=m_kv_bf, ffn_conv_b=m_ffn_conv_b)
    sm_v = dict(meta=v_meta, ln_g=v_ln_g, ln_b=v_ln_b, gla_w_g2=v_gla_w_g2, gla_b_g2=v_gla_b_g2, gla_norm_g=v_gla_norm_g,
                ffn_conv_w=v_ffn_conv_w, kv_bf=v_kv_bf, ffn_conv_b=v_ffn_conv_b)
    shapes_loc = [sw[k].shape for k in sm_all]
    outs = _adamw(_pack([sw[k] for k in sm_all]), _pack([sm_m[k] for k in sm_all]), _pack([sm_v[k] for k in sm_all]),
                  [_pack([loc[k] for k in sm_all])], name="adamw_small")
    outs = [_unpack(o, shapes_loc) for o in outs]
    for i, k in enumerate(sm_all):
        res[k] = [outs[q][i] for q in range(4)]

    order = ["meta", "ln_g", "ln_b", "gla_w_in", "gla_w_g2", "gla_b_g2", "gla_norm_g", "gla_w_out", "kv_w", "kv_bf",
             "fox_w_in", "fox_w_out", "ffn_w_up", "ffn_conv_w", "ffn_conv_b", "ffn_w_down"]
    grad_x = dh[FRONT:][None]
    return (loss, grad_x, *[res[k][0] for k in order], *[res[k][1] for k in order], *[res[k][2] for k in order],
            *[res[k][3] for k in order])
```

```python
import functools

import numpy as np
import jax
import jax.numpy as jnp
from jax import lax
from jax.experimental import pallas as pl
from jax.experimental.pallas import tpu as pltpu

F32 = jnp.float32
BF16 = jnp.bfloat16
HIGHEST = lax.Precision.HIGHEST

DEPTH = 4
N_A_LAYERS = DEPTH // 2
N_META = 16
FRONT = 128
N_PAD = FRONT - N_META
ALPHA = (2.0 * DEPTH) ** 0.25
LN_EPS = 1e-5
GLA_HEADS = 4
GLA_RANK = 16
GLA_TAU = 16.0
GLA_CHUNK = 64
GLA_GROUP = 2
FOX_HD = 128
LANE = 128
ADAM_LR = 0.001
ADAM_B1 = 0.9
ADAM_B2 = 0.999
ADAM_EPS = 1e-08
ADAM_WD = 0.01
ADAM_STEP = 10
NEG = -(2.0 ** 100)
VMEM_LIMIT = 50 * 1024 * 1024
MESH = pl.DeviceIdType.MESH
ANY = pl.BlockSpec(memory_space=pl.ANY)


def _tile(n, pref, align):
    best = None
    t = align
    while t <= min(n, pref):
        if n % t == 0:
            best = t
        t += align
    return best if best is not None else n


def _cp(*sem):
    return pltpu.CompilerParams(dimension_semantics=sem, vmem_limit_bytes=VMEM_LIMIT)


def _dot(a, b, ca, cb, precision=None):
    return lax.dot_general(a, b, (((ca,), (cb,)), ((), ())), precision=precision,
                           preferred_element_type=F32)


def _sigmoid(x):
    return 1.0 / (1.0 + jnp.exp(-x))


def _log_sigmoid(z):
    return jnp.minimum(z, 0.0) - jnp.log(1.0 + jnp.exp(-jnp.abs(z)))


def _rows(i, tr, n=None):
    n = tr if n is None else n
    return i * tr + lax.broadcasted_iota(jnp.int32, (n, 1), 0)


def _mm(a, b, *, ta=False, tb=False, out_dtype=F32, add=None, add_scale=1.0, name):
    if ta:
        K, M = a.shape
    else:
        M, K = a.shape
    if tb:
        N, K2 = b.shape
    else:
        K2, N = b.shape
    assert K == K2, (a.shape, b.shape, ta, tb)
    tm = _tile(M, 1024, LANE) if ta else _tile(M, 1040, 16)
    tn = _tile(N, 1024, LANE)
    tk = _tile(K, 2048, LANE if ((not ta) or tb) else 16)
    nk = K // tk
    ca = 0 if ta else 1
    cb = 1 if tb else 0

    def body(*refs):
        if add is None:
            a_ref, b_ref, o_ref = refs[:3]
            add_ref = None
        else:
            a_ref, b_ref, add_ref, o_ref = refs[:4]
        part = _dot(a_ref[...].astype(BF16), b_ref[...].astype(BF16), ca, cb)

        def finish(acc):
            if add_ref is not None:
                acc = acc + add_scale * add_ref[...]
            o_ref[...] = acc.astype(out_dtype)

        if nk == 1:
            finish(part)
        else:
            acc_ref = refs[-1]
            k = pl.program_id(2)

            @pl.when(k == 0)
            def _():
                acc_ref[...] = part

            @pl.when(k > 0)
            def _():
                acc_ref[...] += part

            @pl.when(k == nk - 1)
            def _():
                finish(acc_ref[...])

    a_spec = pl.BlockSpec((tk, tm), lambda i, j, k: (k, i)) if ta else pl.BlockSpec((tm, tk), lambda i, j, k: (i, k))
    b_spec = pl.BlockSpec((tn, tk), lambda i, j, k: (j, k)) if tb else pl.BlockSpec((tk, tn), lambda i, j, k: (k, j))
    o_spec = pl.BlockSpec((tm, tn), lambda i, j, k: (i, j))
    in_specs = [a_spec, b_spec] + ([o_spec] if add is not None else [])
    args = (a, b) + ((add,) if add is not None else ())
    return pl.pallas_call(
        body, name=name, grid=(M // tm, N // tn, nk), in_specs=in_specs, out_specs=o_spec,
        out_shape=jax.ShapeDtypeStruct((M, N), out_dtype),
        scratch_shapes=[pltpu.VMEM((tm, tn), F32)] if nk > 1 else [],
        compiler_params=_cp("parallel", "parallel", "arbitrary"),
    )(*args)


def _ln_stats(h, mix):
    z = ALPHA * h + mix
    mu = jnp.mean(z, axis=-1, keepdims=True)
    zc = z - mu
    var = jnp.mean(zc * zc, axis=-1, keepdims=True)
    rstd = lax.rsqrt(var + LN_EPS)
    return zc * rstd, rstd


def _ln_fwd(h, mix, g, b, *, name):
    L, D = h.shape
    tr = _tile(L, 160, 16)

    def body(h_ref, m_ref, g_ref, b_ref, o_ref, ob_ref):
        xhat, _ = _ln_stats(h_ref[...], m_ref[...])
        y = xhat * g_ref[...] + b_ref[...]
        o_ref[...] = y
        ob_ref[...] = y.astype(BF16)

    row = pl.BlockSpec((tr, D), lambda i: (i, 0))
    vec = pl.BlockSpec((1, D), lambda i: (0, 0))
    return pl.pallas_call(
        body, name=name, grid=(L // tr,), in_specs=[row, row, vec, vec], out_specs=[row, row],
        out_shape=[jax.ShapeDtypeStruct((L, D), F32), jax.ShapeDtypeStruct((L, D), BF16)],
        compiler_params=_cp("parallel"),
    )(h, mix, g, b)


def _ln_bwd(dy, h, mix, g, *, name):
    L, D = h.shape
    tr = _tile(L, 160, 16)

    def body(dy_ref, h_ref, m_ref, g_ref, dz_ref, dzb_ref, dg_ref, db_ref):
        i = pl.program_id(0)
        xhat, rstd = _ln_stats(h_ref[...], m_ref[...])
        dy = dy_ref[...]
        dxh = dy * g_ref[...]
        m1 = jnp.mean(dxh, axis=-1, keepdims=True)
        m2 = jnp.mean(dxh * xhat, axis=-1, keepdims=True)
        dz = rstd * (dxh - m1 - xhat * m2)
        dz_ref[...] = dz
        dzb_ref[...] = dz.astype(BF16)
        pg = jnp.sum(dy * xhat, axis=0, keepdims=True)
        pb = jnp.sum(dy, axis=0, keepdims=True)

        @pl.when(i == 0)
        def _():
            dg_ref[...] = pg
            db_ref[...] = pb

        @pl.when(i > 0)
        def _():
            dg_ref[...] += pg
            db_ref[...] += pb

    row = pl.BlockSpec((tr, D), lambda i: (i, 0))
    vec = pl.BlockSpec((1, D), lambda i: (0, 0))
    return pl.pallas_call(
        body, name=name, grid=(L // tr,), in_specs=[row, row, row, vec], out_specs=[row, row, vec, vec],
        out_shape=[jax.ShapeDtypeStruct((L, D), F32), jax.ShapeDtypeStruct((L, D), BF16),
                   jax.ShapeDtypeStruct((1, D), F32), jax.ShapeDtypeStruct((1, D), F32)],
        compiler_params=_cp("arbitrary"),
    )(dy, h, mix, g)


def _loss(h, target, *, name):
    L, D = h.shape
    tr = FRONT

    def body(h_ref, t_ref, acc_ref, dy_ref):
        i = pl.program_id(0)
        e = jnp.where(i >= 1, h_ref[...] - t_ref[...], 0.0)
        dy_ref[...] = e * (1.0 / D)
        part = 0.5 * jnp.sum(jnp.sum(e * e, axis=-1, keepdims=True) * (1.0 / D), axis=0, keepdims=True)

        @pl.when(i == 0)
        def _():
            acc_ref[...] = jnp.zeros_like(acc_ref)

        acc_ref[...] += jnp.broadcast_to(part, acc_ref.shape)

    return pl.pallas_call(
        body, name=name, grid=(L // tr,),
        in_specs=[pl.BlockSpec((tr, D), lambda i: (i, 0)),
                  pl.BlockSpec((tr, D), lambda i: (jnp.maximum(i - 1, 0), 0))],
        out_specs=[pl.BlockSpec((8, LANE), lambda i: (0, 0)), pl.BlockSpec((tr, D), lambda i: (i, 0))],
        out_shape=[jax.ShapeDtypeStruct((8, LANE), F32), jax.ShapeDtypeStruct((L, D), F32)],
        compiler_params=_cp("arbitrary"),
    )(h, target)


def _gla_gate_fwd(P, w2p, b2, gl_blk, *, name):
    L = P.shape[0]
    DK = w2p.shape[1]
    tr = _tile(L, 640, 16)

    def body(gl_ref, w_ref, b_ref, o_ref):
        i = pl.program_id(0)
        z = _dot(gl_ref[...].astype(BF16), w_ref[...], 1, 0) + b_ref[...]
        la = _log_sigmoid(z) * (1.0 / GLA_TAU)
        o_ref[...] = jnp.where(_rows(i, tr) >= N_PAD, la, 0.0)

    return pl.pallas_call(
        body, name=name, grid=(L // tr,),
        in_specs=[pl.BlockSpec((tr, LANE), lambda i: (i, gl_blk)),
                  pl.BlockSpec((LANE, DK), lambda i: (0, 0)), pl.BlockSpec((1, DK), lambda i: (0, 0))],
        out_specs=pl.BlockSpec((tr, DK), lambda i: (i, 0)),
        out_shape=jax.ShapeDtypeStruct((L, DK), F32), compiler_params=_cp("parallel"),
    )(P, w2p, b2)


def _gla_gate_bwd(dla, P, w2p, b2, gl_blk, *, name):
    L = P.shape[0]
    DK = w2p.shape[1]
    tr = _tile(L, 640, 16)

    def body(dla_ref, gl_ref, w_ref, b_ref, dgl_ref, dw_ref, db_ref):
        i = pl.program_id(0)
        glb = gl_ref[...].astype(BF16)
        z = _dot(glb, w_ref[...], 1, 0) + b_ref[...]
        dz = jnp.where(_rows(i, tr) >= N_PAD, dla_ref[...], 0.0) * (1.0 / GLA_TAU) * _sigmoid(-z)
        dzb = dz.astype(BF16)
        dgl_ref[...] = _dot(dzb, w_ref[...], 1, 1).astype(BF16)
        pw = _dot(glb, dzb, 0, 0)
        pb = jnp.sum(dz, axis=0, keepdims=True)

        @pl.when(i == 0)
        def _():
            dw_ref[...] = pw
            db_ref[...] = pb

        @pl.when(i > 0)
        def _():
            dw_ref[...] += pw
            db_ref[...] += pb

    return pl.pallas_call(
        body, name=name, grid=(L // tr,),
        in_specs=[pl.BlockSpec((tr, DK), lambda i: (i, 0)), pl.BlockSpec((tr, LANE), lambda i: (i, gl_blk)),
                  pl.BlockSpec((LANE, DK), lambda i: (0, 0)), pl.BlockSpec((1, DK), lambda i: (0, 0))],
        out_specs=[pl.BlockSpec((tr, LANE), lambda i: (i, 0)), pl.BlockSpec((LANE, DK), lambda i: (0, 0)),
                   pl.BlockSpec((1, DK), lambda i: (0, 0))],
        out_shape=[jax.ShapeDtypeStruct((L, LANE), BF16), jax.ShapeDtypeStruct((LANE, DK), F32),
                   jax.ShapeDtypeStruct((1, DK), F32)],
        compiler_params=_cp("arbitrary"),
    )(dla, P, w2p, b2)


def _chunk_terms(q, k, g, n, scale, HV):
    C = q.shape[0]
    ri = lax.broadcasted_iota(jnp.int32, (C, C), 0)
    ci = lax.broadcasted_iota(jnp.int32, (C, C), 1)
    tri = ri >= ci
    valid = _rows(n, C) >= N_PAD
    km = jnp.where(valid, k, 0.0)
    b = _dot(tri.astype(F32), g, 1, 0, precision=HIGHEST)
    bl_row = jnp.sum(g, axis=0, keepdims=True)
    bl_col = _dot(g, jnp.ones((C, HV), F32), 0, 0, precision=HIGHEST)
    eb = jnp.exp(b)
    enb = jnp.exp(-b)
    qe = q * scale * eb
    ke = km * enb
    ebl_row = jnp.exp(bl_row)
    kl = ke * ebl_row
    return dict(tri=tri, valid=valid, eb=eb, enb=enb, qe=qe, ke=ke, kl=kl, ebl_row=ebl_row,
                ebl_col=jnp.exp(bl_col), ri=ri, ci=ci)


def _gla_chunk_fwd(P, la, H, HK, HV, *, name):
    L = P.shape[0]
    C = GLA_CHUNK
    N = L // C
    scale = HK ** -0.5

    G = GLA_GROUP if H % GLA_GROUP == 0 else 1
    HG = H // G

    def body(q_ref, k_ref, v_ref, g_ref, o_ref, s_ref, S):
        n = pl.program_id(1)

        @pl.when(n == 0)
        def _():
            S[...] = jnp.zeros_like(S)

        for g in range(G):
            ks, vs = slice(g * HK, (g + 1) * HK), slice(g * HV, (g + 1) * HV)
            S0 = S[g]
            s_ref[g, 0] = S0
            t = _chunk_terms(q_ref[:, ks], k_ref[:, ks], g_ref[:, ks], n, scale, HV)
            vb = v_ref[:, vs].astype(BF16)
            qeb = t["qe"].astype(BF16)
            inter = _dot(qeb, S0.astype(BF16), 1, 0)
            att = jnp.where(t["tri"], _dot(qeb, t["ke"].astype(BF16), 1, 1), 0.0)
            o_ref[:, vs] = inter + _dot(att.astype(BF16), vb, 1, 0)
            S[g] = t["ebl_col"] * S0 + _dot(t["kl"].astype(BF16), vb, 0, 0)

    return pl.pallas_call(
        body, name=name, grid=(HG, N),
        in_specs=[pl.BlockSpec((C, G * HK), lambda h, n: (n, h)), pl.BlockSpec((C, G * HK), lambda h, n: (n, HG + h)),
                  pl.BlockSpec((C, G * HV), lambda h, n: (n, HG + h)), pl.BlockSpec((C, G * HK), lambda h, n: (n, h))],
        out_specs=[pl.BlockSpec((C, G * HV), lambda h, n: (n, h)),
                   pl.BlockSpec((G, 1, HK, HV), lambda h, n: (h, n, 0, 0))],
        out_shape=[jax.ShapeDtypeStruct((L, H * HV), F32), jax.ShapeDtypeStruct((H, N, HK, HV), F32)],
        scratch_shapes=[pltpu.VMEM((G, HK, HV), F32)],
        compiler_params=_cp("parallel", "arbitrary"),
    )(P, P, P, la)


def _gla_chunk_bwd(P, la, S_all, do, H, HK, HV, *, name):
    L = P.shape[0]
    C = GLA_CHUNK
    N = L // C
    scale = HK ** -0.5

    G = 1
    HG = H // G

    def body(q_ref, k_ref, v_ref, g_ref, s_ref, do_ref, dq_ref, dk_ref, dv_ref, dg_ref, dS):
        step = pl.program_id(1)
        n = N - 1 - step

        @pl.when(step == 0)
        def _():
            dS[...] = jnp.zeros_like(dS)

        for g in range(G):
            ks, vs = slice(g * HK, (g + 1) * HK), slice(g * HV, (g + 1) * HV)
            dS1 = dS[g]
            S0 = s_ref[g, 0]
            t = _chunk_terms(q_ref[:, ks], k_ref[:, ks], g_ref[:, ks], n, scale, HV)
            tri, qe, ke, kl = t["tri"], t["qe"], t["ke"], t["kl"]
            vb = v_ref[:, vs].astype(BF16)
            dob = do_ref[:, vs].astype(BF16)
            qeb, keb, dSb = qe.astype(BF16), ke.astype(BF16), dS1.astype(BF16)
            dA = jnp.where(tri, _dot(dob, vb, 1, 1), 0.0).astype(BF16)
            A = jnp.where(tri, _dot(qeb, keb, 1, 1), 0.0).astype(BF16)
            dqe = _dot(dob, S0.astype(BF16), 1, 1) + _dot(dA, keb, 1, 0)
            dkl = _dot(vb, dSb, 1, 1)
            dke = _dot(dA, qeb, 0, 0) + dkl * t["ebl_row"]
            dv_ref[:, vs] = (_dot(A, dob, 0, 0) + _dot(kl.astype(BF16), dSb, 1, 0)).astype(BF16)
            debl = (jnp.sum(_dot(jnp.ones((8, HV), F32), dS1 * S0, 1, 1, precision=HIGHEST), axis=0, keepdims=True) * 0.125
                    + jnp.sum(dkl * ke, axis=0, keepdims=True))
            dbl = debl * t["ebl_row"]
            db = dqe * qe - dke * ke + jnp.where(lax.broadcasted_iota(jnp.int32, (C, 1), 0) == C - 1, dbl, 0.0)
            triu = (t["ci"] >= t["ri"]).astype(F32)
            dq_ref[:, ks] = (dqe * t["eb"] * scale).astype(BF16)
            dk_ref[:, ks] = jnp.where(t["valid"], dke * t["enb"], 0.0).astype(BF16)
            dg_ref[:, ks] = _dot(triu, db, 1, 0, precision=HIGHEST)
            dS[g] = t["ebl_col"] * dS1 + _dot(qeb, dob, 0, 0)

    rev = lambda h, s: (N - 1 - s, h)
    return pl.pallas_call(
        body, name=name, grid=(HG, N),
        in_specs=[pl.BlockSpec((C, G * HK), rev), pl.BlockSpec((C, G * HK), lambda h, s: (N - 1 - s, HG + h)),
                  pl.BlockSpec((C, G * HV), lambda h, s: (N - 1 - s, HG + h)), pl.BlockSpec((C, G * HK), rev),
                  pl.BlockSpec((G, 1, HK, HV), lambda h, s: (h, N - 1 - s, 0, 0)), pl.BlockSpec((C, G * HV), rev)],
        out_specs=[pl.BlockSpec((C, G * HK), rev), pl.BlockSpec((C, G * HK), rev),
                   pl.BlockSpec((C, G * HV), rev), pl.BlockSpec((C, G * HK), rev)],
        out_shape=[jax.ShapeDtypeStruct((L, H * HK), BF16), jax.ShapeDtypeStruct((L, H * HK), BF16),
                   jax.ShapeDtypeStruct((L, H * HV), BF16), jax.ShapeDtypeStruct((L, H * HK), F32)],
        scratch_shapes=[pltpu.VMEM((G, HK, HV), F32)],
        compiler_params=_cp("parallel", "arbitrary"),
    )(P, P, P, la, S_all, do)


def _silu_parts(x):
    s = _sigmoid(x)
    return x * s, s * (1.0 + x * (1.0 - s))


def _gla_post_fwd(o, P, ng, H, HV, r_blk, *, name):
    L, DV = o.shape
    tr = _tile(L, 320, 16)

    def body(o_ref, r_ref, g_ref, out_ref):
        for hd in range(H):
            sl = slice(hd * HV, (hd + 1) * HV)
            oh = o_ref[:, sl]
            rr = lax.rsqrt(jnp.mean(oh * oh, axis=-1, keepdims=True) + LN_EPS)
            silu, _ = _silu_parts(r_ref[:, sl])
            out_ref[:, sl] = (oh * rr * g_ref[...] * silu).astype(BF16)

    return pl.pallas_call(
        body, name=name, grid=(L // tr,),
        in_specs=[pl.BlockSpec((tr, DV), lambda i: (i, 0)), pl.BlockSpec((tr, DV), lambda i: (i, r_blk)),
                  pl.BlockSpec((1, HV), lambda i: (0, 0))],
        out_specs=pl.BlockSpec((tr, DV), lambda i: (i, 0)),
        out_shape=jax.ShapeDtypeStruct((L, DV), BF16), compiler_params=_cp("parallel"),
    )(o, P, ng)


def _gla_post_bwd(dgated, o, P, ng, H, HV, r_blk, *, name):
    L, DV = o.shape
    tr = _tile(L, 320, 16)

    def body(d_ref, o_ref, r_ref, g_ref, do_ref, dr_ref, dng_ref):
        i = pl.program_id(0)
        png = jnp.zeros((1, HV), F32)
        for hd in range(H):
            sl = slice(hd * HV, (hd + 1) * HV)
            oh = o_ref[:, sl]
            d = d_ref[:, sl]
            rr = lax.rsqrt(jnp.mean(oh * oh, axis=-1, keepdims=True) + LN_EPS)
            yh = oh * rr
            silu, dsilu = _silu_parts(r_ref[:, sl])
            dn = d * silu
            dr_ref[:, sl] = (d * yh * g_ref[...] * dsilu).astype(BF16)
            png = png + jnp.sum(dn * yh, axis=0, keepdims=True)
            dyh = dn * g_ref[...]
            do_ref[:, sl] = rr * (dyh - yh * jnp.mean(dyh * yh, axis=-1, keepdims=True))

        @pl.when(i == 0)
        def _():
            dng_ref[...] = png

        @pl.when(i > 0)
        def _():
            dng_ref[...] += png

    row = pl.BlockSpec((tr, DV), lambda i: (i, 0))
    return pl.pallas_call(
        body, name=name, grid=(L // tr,),
        in_specs=[row, row, pl.BlockSpec((tr, DV), lambda i: (i, r_blk)), pl.BlockSpec((1, HV), lambda i: (0, 0))],
        out_specs=[row, row, pl.BlockSpec((1, HV), lambda i: (0, 0))],
        out_shape=[jax.ShapeDtypeStruct((L, DV), F32), jax.ShapeDtypeStruct((L, DV), BF16),
                   jax.ShapeDtypeStruct((1, HV), F32)],
        compiler_params=_cp("arbitrary"),
    )(dgated, o, P, ng)


def _shift_down(x, halo, s):
    if s == 0:
        return x
    tr = x.shape[0]
    xx = jnp.concatenate([halo, x], axis=0)
    return pltpu.roll(xx, s, axis=0)[8:8 + tr]


def _shift_up(x, halo, s):
    if s == 0:
        return x
    tr = x.shape[0]
    xx = jnp.concatenate([x, halo], axis=0)
    return pltpu.roll(xx, tr + 8 - s, axis=0)[0:tr]


def _conv_taps(x_ref, halo_ref, i, tr):
    x = jnp.where(_rows(i, tr) >= N_PAD, x_ref[...], 0.0)
    halo = jnp.where(i * tr - 8 + lax.broadcasted_iota(jnp.int32, (8, 1), 0) >= N_PAD, halo_ref[...], 0.0)
    return [_shift_down(x, halo, s) for s in range(3)]


def _conv_apply(taps, w_ref, b_ref):
    return taps[2] * w_ref[0:1, :] + taps[1] * w_ref[1:2, :] + taps[0] * w_ref[2:3, :] + b_ref[...]


def _conv_specs(tr, tc):
    blk = pl.BlockSpec((tr, tc), lambda j, i: (i, j))
    halo = pl.BlockSpec((8, tc), lambda j, i: (jnp.maximum(i * (tr // 8) - 1, 0), j))
    w = pl.BlockSpec((3, tc), lambda j, i: (0, j))
    b = pl.BlockSpec((1, tc), lambda j, i: (0, j))
    return blk, halo, w, b


def _conv_act_fwd(Uu, Ug, wu, wg, bu, bg, *, name):
    L, DFF = Uu.shape
    tr = _tile(L, 320, 16)
    tc = _tile(DFF, 512, LANE)

    def body(xu_ref, hu_ref, xg_ref, hg_ref, wu_ref, wg_ref, bu_ref, bg_ref, o_ref):
        i = pl.program_id(1)
        u = _conv_apply(_conv_taps(xu_ref, hu_ref, i, tr), wu_ref, bu_ref)
        g = _conv_apply(_conv_taps(xg_ref, hg_ref, i, tr), wg_ref, bg_ref)
        o_ref[...] = (_silu_parts(g)[0] * u).astype(BF16)

    blk, halo, w, b = _conv_specs(tr, tc)
    return pl.pallas_call(
        body, name=name, grid=(DFF // tc, L // tr),
        in_specs=[blk, halo, blk, halo, w, w, b, b], out_specs=blk,
        out_shape=jax.ShapeDtypeStruct((L, DFF), BF16), compiler_params=_cp("parallel", "parallel"),
    )(Uu, Uu, Ug, Ug, wu, wg, bu, bg)


def _conv_act_bwd(Uu, Ug, wu, wg, bu, bg, dA, *, name):
    L, DFF = Uu.shape
    tr = _tile(L, 320, 16)
    tc = _tile(DFF, 512, LANE)

    def body(xu_ref, hu_ref, xg_ref, hg_ref, wu_ref, wg_ref, bu_ref, bg_ref, da_ref,
             du_ref, dg_ref, dwu_ref, dwg_ref, dbu_ref, dbg_ref):
        i = pl.program_id(1)
        tu = _conv_taps(xu_ref, hu_ref, i, tr)
        tg = _conv_taps(xg_ref, hg_ref, i, tr)
        u = _conv_apply(tu, wu_ref, bu_ref)
        g = _conv_apply(tg, wg_ref, bg_ref)
        silu, dsilu = _silu_parts(g)
        da = da_ref[...]
        du = da * silu
        dg = da * u * dsilu
        du_ref[...] = du
        dg_ref[...] = dg

        @pl.when(i == 0)
        def _():
            dwu_ref[...] = jnp.zeros_like(dwu_ref)
            dwg_ref[...] = jnp.zeros_like(dwg_ref)
            dbu_ref[...] = jnp.zeros_like(dbu_ref)
            dbg_ref[...] = jnp.zeros_like(dbg_ref)

        for j in range(3):
            dwu_ref[j:j + 1, :] += jnp.sum(du * tu[2 - j], axis=0, keepdims=True)
            dwg_ref[j:j + 1, :] += jnp.sum(dg * tg[2 - j], axis=0, keepdims=True)
        dbu_ref[...] += jnp.sum(du, axis=0, keepdims=True)
        dbg_ref[...] += jnp.sum(dg, axis=0, keepdims=True)

    blk, halo, w, b = _conv_specs(tr, tc)
    return pl.pallas_call(
        body, name=name, grid=(DFF // tc, L // tr),
        in_specs=[blk, halo, blk, halo, w, w, b, b, blk], out_specs=[blk, blk, w, w, b, b],
        out_shape=[jax.ShapeDtypeStruct((L, DFF), F32), jax.ShapeDtypeStruct((L, DFF), F32),
                   jax.ShapeDtypeStruct((3, DFF), F32), jax.ShapeDtypeStruct((3, DFF), F32),
                   jax.ShapeDtypeStruct((1, DFF), F32), jax.ShapeDtypeStruct((1, DFF), F32)],
        compiler_params=_cp("parallel", "arbitrary"),
    )(Uu, Uu, Ug, Ug, wu, wg, bu, bg, dA)


def _conv_in_bwd(dh, w, *, name):
    L, DFF = dh.shape
    tr = _tile(L, 320, 16)
    tc = _tile(DFF, 512, LANE)
    nb8 = L // 8

    def body(x_ref, halo_ref, w_ref, o_ref):
        i = pl.program_id(1)
        x = x_ref[...]
        halo = jnp.where((i + 1) * tr + lax.broadcasted_iota(jnp.int32, (8, 1), 0) < L, halo_ref[...], 0.0)
        d = (x * w_ref[2:3, :] + _shift_up(x, halo, 1) * w_ref[1:2, :] + _shift_up(x, halo, 2) * w_ref[0:1, :])
        o_ref[...] = jnp.where(_rows(i, tr) >= N_PAD, d, 0.0).astype(BF16)

    blk = pl.BlockSpec((tr, tc), lambda j, i: (i, j))
    halo = pl.BlockSpec((8, tc), lambda j, i: (jnp.minimum((i + 1) * (tr // 8), nb8 - 1), j))
    return pl.pallas_call(
        body, name=name, grid=(DFF // tc, L // tr),
        in_specs=[blk, halo, pl.BlockSpec((3, tc), lambda j, i: (0, j))], out_specs=blk,
        out_shape=jax.ShapeDtypeStruct((L, DFF), BF16), compiler_params=_cp("parallel", "parallel"),
    )(dh, dh, w)


def _fox_c_fwd(f, bf, *, name):
    L = f.shape[0]
    tr = _tile(L, 320, 16)

    def body(f_ref, b_ref, c_ref, carry):
        i = pl.program_id(0)

        @pl.when(i == 0)
        def _():
            carry[...] = jnp.zeros_like(carry)

        lf = jnp.where(_rows(i, tr) >= N_PAD, _log_sigmoid(f_ref[...] + b_ref[...]), 0.0)
        tri = (lax.broadcasted_iota(jnp.int32, (tr, tr), 0) >= lax.broadcasted_iota(jnp.int32, (tr, tr), 1)).astype(F32)
        c_ref[...] = _dot(tri, lf, 1, 0, precision=HIGHEST) + carry[...]
        carry[...] += jnp.sum(lf, axis=0, keepdims=True)

    return pl.pallas_call(
        body, name=name, grid=(L // tr,),
        in_specs=[pl.BlockSpec((tr, LANE), lambda i: (i, 0)), pl.BlockSpec((1, LANE), lambda i: (0, 0))],
        out_specs=pl.BlockSpec((tr, LANE), lambda i: (i, 0)),
        out_shape=jax.ShapeDtypeStruct((L, LANE), F32), scratch_shapes=[pltpu.VMEM((1, LANE), F32)],
        compiler_params=_cp("arbitrary"),
    )(f, bf)


def _fox_c_bwd(dcs, f, bf, *, name):
    L = f.shape[0]
    tr = _tile(L, 320, 16)
    nb = L // tr
    nd = len(dcs)

    def body(*refs):
        f_ref, b_ref, df_ref, db_ref, carry = refs[nd:]
        s = pl.program_id(0)
        i = nb - 1 - s

        @pl.when(s == 0)
        def _():
            carry[...] = jnp.zeros_like(carry)
            db_ref[...] = jnp.zeros_like(db_ref)

        dc = refs[0][...]
        for r in refs[1:nd]:
            dc = dc + r[...]
        triu = (lax.broadcasted_iota(jnp.int32, (tr, tr), 1) >= lax.broadcasted_iota(jnp.int32, (tr, tr), 0)).astype(F32)
        dlf = _dot(triu, dc, 1, 0, precision=HIGHEST) + carry[...]
        carry[...] += jnp.sum(dc, axis=0, keepdims=True)
        df = jnp.where(_rows(i, tr) >= N_PAD, dlf, 0.0) * _sigmoid(-(f_ref[...] + b_ref[...]))
        df_ref[...] = df.astype(BF16)
        db_ref[...] += jnp.sum(df, axis=0, keepdims=True)

    rev = pl.BlockSpec((tr, LANE), lambda s: (nb - 1 - s, 0))
    vec = pl.BlockSpec((1, LANE), lambda s: (0, 0))
    return pl.pallas_call(
        body, name=name, grid=(nb,), in_specs=[rev] * (nd + 1) + [vec], out_specs=[rev, vec],
        out_shape=[jax.ShapeDtypeStruct((L, LANE), BF16), jax.ShapeDtypeStruct((1, LANE), F32)],
        scratch_shapes=[pltpu.VMEM((1, LANE), F32)], compiler_params=_cp("arbitrary"),
    )(*dcs, f, bf)


AUG = 2 * FOX_HD
FOX_GROUP = 2
FOX_ROW_SPLIT = 5


def _split3(x):
    hi = x.astype(BF16).astype(F32)
    r = x - hi
    mid = r.astype(BF16).astype(F32)
    return hi, mid, (r - mid).astype(BF16).astype(F32)


def _aug_lanes(n, vals):
    lane = lax.broadcasted_iota(jnp.int32, (n, FOX_HD), 1)
    out = jnp.zeros((n, FOX_HD), F32)
    for j, v in enumerate(vals):
        out = jnp.where(lane == j, v, out)
    return out


def _lane_col(x, j):
    lane = lax.broadcasted_iota(jnp.int32, x.shape, 1)
    return jnp.sum(jnp.where(lane == j, x, 0.0), axis=-1, keepdims=True)


def _fox_prep_q(QO, c, H, *, name):
    L = QO.shape[0]
    hd = FOX_HD
    tr = _tile(L, 320, 16)

    def body(q_ref, c_ref, o_ref):
        c = c_ref[...]
        for h in range(H):
            hi, mid, lo = _split3(_lane_col(c, h))
            o_ref[:, h * AUG:h * AUG + hd] = (q_ref[:, h * hd:(h + 1) * hd] * (hd ** -0.5)).astype(BF16)
            o_ref[:, h * AUG + hd:(h + 1) * AUG] = _aug_lanes(tr, [hi, mid, lo, 1.0, 1.0, 1.0]).astype(BF16)

    return pl.pallas_call(
        body, name=name, grid=(L // tr,),
        in_specs=[pl.BlockSpec((tr, H * hd), lambda i: (i, 0)), pl.BlockSpec((tr, LANE), lambda i: (i, 0))],
        out_specs=pl.BlockSpec((tr, H * AUG), lambda i: (i, 0)),
        out_shape=jax.ShapeDtypeStruct((L, H * AUG), BF16), compiler_params=_cp("parallel"),
    )(QO, c)


def _fox_prep_kv(KV, c, H, *, name):
    L = KV.shape[0]
    hd = FOX_HD
    tr = _tile(L, 320, 16)

    def body(k_ref, v_ref, c_ref, ko_ref, vo_ref):
        i = pl.program_id(0)
        c = c_ref[...]
        pad = _rows(i, tr) < N_PAD
        for h in range(H):
            hi, mid, lo = _split3(_lane_col(c, h))
            aug = _aug_lanes(tr, [1.0, 1.0, 1.0, jnp.where(pad, NEG, -hi), jnp.where(pad, 0.0, -mid),
                                  jnp.where(pad, 0.0, -lo)])
            ko_ref[:, h * AUG:h * AUG + hd] = k_ref[:, h * hd:(h + 1) * hd]
            ko_ref[:, h * AUG + hd:(h + 1) * AUG] = aug.astype(BF16)
            vo_ref[:, h * AUG:h * AUG + hd] = v_ref[:, h * hd:(h + 1) * hd]
            vo_ref[:, h * AUG + hd:(h + 1) * AUG] = jnp.ones((tr, hd), BF16)

    wide = pl.BlockSpec((tr, H * AUG), lambda i: (i, 0))
    return pl.pallas_call(
        body, name=name, grid=(L // tr,),
        in_specs=[pl.BlockSpec((tr, H * hd), lambda i: (i, 0)), pl.BlockSpec((tr, H * hd), lambda i: (i, 1)),
                  pl.BlockSpec((tr, LANE), lambda i: (i, 0))],
        out_specs=[wide, wide],
        out_shape=[jax.ShapeDtypeStruct((L, H * AUG), BF16)] * 2, compiler_params=_cp("parallel"),
    )(KV, KV, c)


def _fox_mask(qi, kj, t):
    ti = qi * t + lax.broadcasted_iota(jnp.int32, (t, t), 0)
    si = kj * t + lax.broadcasted_iota(jnp.int32, (t, t), 1)
    return (si <= ti) & ((si >= N_PAD) | (si == ti))


def _fox_attn_fwd(QA, KT, VA, H, *, name):
    L = QA.shape[0]
    hd = FOX_HD
    t = _tile(L, 640, LANE)
    nb = L // t
    G = FOX_GROUP if H % FOX_GROUP == 0 else 1
    nr = FOX_ROW_SPLIT if t % (8 * FOX_ROW_SPLIT) == 0 else 1
    tr = t // nr

    def body(q_ref, k_ref, v_ref, o_ref, lse_ref, m_s, acc):
        qi, kj = pl.program_id(1), pl.program_id(2)

        @pl.when(kj == 0)
        def _():
            m_s[...] = jnp.full_like(m_s, NEG)
            acc[...] = jnp.zeros_like(acc)

        def step(masked):
            for g in range(G):
                cs = slice(g * AUG, (g + 1) * AUG)
                for r in range(nr):
                    rows = slice(r * tr, (r + 1) * tr)
                    s = _dot(q_ref[rows, cs], k_ref[cs, :], 1, 0)
                    if masked:
                        ti = qi * t + r * tr + lax.broadcasted_iota(jnp.int32, (tr, t), 0)
                        si = kj * t + lax.broadcasted_iota(jnp.int32, (tr, t), 1)
                        mask = (si <= ti) & ((si >= N_PAD) | (si == ti))
                        s = jnp.where(mask, s, NEG)
                    m_old = m_s[g, rows]
                    m_new = jnp.maximum(m_old, jnp.max(s, axis=-1, keepdims=True))
                    p = jnp.exp(s - m_new)
                    if masked:
                        p = jnp.where(mask, p, 0.0)
                    acc[g, rows] = jnp.exp(m_old - m_new) * acc[g, rows] + _dot(p.astype(BF16), v_ref[:, cs], 1, 0)
                    m_s[g, rows] = m_new

        @pl.when(kj < qi)
        def _():
            step(False)

        @pl.when(kj == qi)
        def _():
            step(True)

        @pl.when(kj == nb - 1)
        def _():
            for g in range(G):
                a = acc[g]
                l = a[:, hd:]
                o_ref[:, g * hd:(g + 1) * hd] = a[:, :hd] / l
                lse_ref[g] = m_s[g] + jnp.log(jnp.max(l, axis=-1, keepdims=True))

    return pl.pallas_call(
        body, name=name, grid=(H // G, nb, nb),
        in_specs=[pl.BlockSpec((t, G * AUG), lambda h, qi, kj: (qi, h)),
                  pl.BlockSpec((G * AUG, t), lambda h, qi, kj: (h, jnp.minimum(kj, qi))),
                  pl.BlockSpec((t, G * AUG), lambda h, qi, kj: (jnp.minimum(kj, qi), h))],
        out_specs=[pl.BlockSpec((t, G * hd), lambda h, qi, kj: (qi, h)),
                   pl.BlockSpec((G, t, 1), lambda h, qi, kj: (h, qi, 0))],
        out_shape=[jax.ShapeDtypeStruct((L, H * hd), F32), jax.ShapeDtypeStruct((H, L, 1), F32)],
        scratch_shapes=[pltpu.VMEM((G, t, 1), F32), pltpu.VMEM((G, t, AUG), F32)],
        compiler_params=_cp("parallel", "parallel", "arbitrary"),
    )(QA, KT, VA)


def _fox_attn_bwd(QA, KA, VA, DOA, lse, init, H, *, name):
    L = QA.shape[0]
    t = _tile(L, 640, LANE)
    nb = L // t

    def body(q_ref, k_ref, v_ref, do_ref, lse_ref, dk0_ref, dv0_ref, dq_ref, dk_ref, dv_ref):
        kj, qi = pl.program_id(1), pl.program_id(2)

        @pl.when((kj == 0) & (qi == 0))
        def _():
            dq_ref[...] = jnp.zeros_like(dq_ref)

        @pl.when(qi == 0)
        def _():
            dk_ref[...] = dk0_ref[...]
            dv_ref[...] = dv0_ref[...]

        def step(masked):
            q, k, doa = q_ref[...], k_ref[...], do_ref[...]
            p = jnp.exp(_dot(q, k, 1, 1) - lse_ref[0])
            if masked:
                p = jnp.where(_fox_mask(qi, kj, t), p, 0.0)
            pb = p.astype(BF16)
            ds = (p * _dot(doa, v_ref[...], 1, 1)).astype(BF16)
            dv_ref[...] += _dot(pb, doa, 0, 0)
            dk_ref[...] += _dot(ds, q, 0, 0)
            rows = pl.ds(pl.multiple_of(qi * t, t), t)
            dq_ref[rows, :] += _dot(ds, k, 1, 0)

        @pl.when(qi > kj)
        def _():
            step(False)

        @pl.when(qi == kj)
        def _():
            step(True)

    qb = pl.BlockSpec((t, AUG), lambda h, kj, qi: (jnp.maximum(qi, kj), h))
    kb = pl.BlockSpec((t, AUG), lambda h, kj, qi: (kj, h))
    return pl.pallas_call(
        body, name=name, grid=(H, nb, nb),
        in_specs=[qb, kb, kb, qb, pl.BlockSpec((1, t, 1), lambda h, kj, qi: (h, jnp.maximum(qi, kj), 0)), kb, kb],
        out_specs=[pl.BlockSpec((L, AUG), lambda h, kj, qi: (0, h)), kb, kb],
        out_shape=[jax.ShapeDtypeStruct((L, H * AUG), F32)] * 3,
        compiler_params=_cp("parallel", "arbitrary", "arbitrary"),
    )(QA, KA, VA, DOA, lse, *init)


def _fox_post_q(DQA, H, *, name):
    L = DQA.shape[0]
    hd = FOX_HD
    tr = _tile(L, 320, 16)

    def body(x_ref, dq_ref, dc_ref):
        lane = lax.broadcasted_iota(jnp.int32, (tr, LANE), 1)
        dc = jnp.zeros((tr, LANE), F32)
        for h in range(H):
            dq_ref[:, h * hd:(h + 1) * hd] = (x_ref[:, h * AUG:h * AUG + hd] * (hd ** -0.5)).astype(BF16)
            dc = jnp.where(lane == h, _lane_col(x_ref[:, h * AUG + hd:(h + 1) * AUG], 0), dc)
        dc_ref[...] = dc

    return pl.pallas_call(
        body, name=name, grid=(L // tr,), in_specs=[pl.BlockSpec((tr, H * AUG), lambda i: (i, 0))],
        out_specs=[pl.BlockSpec((tr, H * hd), lambda i: (i, 0)), pl.BlockSpec((tr, LANE), lambda i: (i, 0))],
        out_shape=[jax.ShapeDtypeStruct((L, H * hd), BF16), jax.ShapeDtypeStruct((L, LANE), F32)],
        compiler_params=_cp("parallel"),
    )(DQA)


def _fox_post_kv(DKA, DVA, H, *, name):
    L = DKA.shape[0]
    hd = FOX_HD
    tr = _tile(L, 320, 16)

    def body(k_ref, v_ref, o_ref, dc_ref):
        lane = lax.broadcasted_iota(jnp.int32, (tr, LANE), 1)
        dc = jnp.zeros((tr, LANE), F32)
        for h in range(H):
            o_ref[:, h * hd:(h + 1) * hd] = k_ref[:, h * AUG:h * AUG + hd].astype(BF16)
            o_ref[:, (H + h) * hd:(H + h + 1) * hd] = v_ref[:, h * AUG:h * AUG + hd].astype(BF16)
            dc = jnp.where(lane == h, -_lane_col(k_ref[:, h * AUG + hd:(h + 1) * AUG], 3), dc)
        dc_ref[...] = dc

    wide = pl.BlockSpec((tr, H * AUG), lambda i: (i, 0))
    return pl.pallas_call(
        body, name=name, grid=(L // tr,), in_specs=[wide, wide],
        out_specs=[pl.BlockSpec((tr, 2 * H * hd), lambda i: (i, 0)), pl.BlockSpec((tr, LANE), lambda i: (i, 0))],
        out_shape=[jax.ShapeDtypeStruct((L, 2 * H * hd), BF16), jax.ShapeDtypeStruct((L, LANE), F32)],
        compiler_params=_cp("parallel"),
    )(DKA, DVA)


def _fox_gate_fwd(o, QO, *, name):
    L, D = o.shape
    tr = _tile(L, 320, 16)

    def body(o_ref, g_ref, out_ref):
        out_ref[...] = (o_ref[...] * _sigmoid(g_ref[...])).astype(BF16)

    row = pl.BlockSpec((tr, D), lambda i: (i, 0))
    return pl.pallas_call(
        body, name=name, grid=(L // tr,), in_specs=[row, pl.BlockSpec((tr, D), lambda i: (i, 1))], out_specs=row,
        out_shape=jax.ShapeDtypeStruct((L, D), BF16), compiler_params=_cp("parallel"),
    )(o, QO)


def _fox_gate_bwd(d, o, QO, H, *, name):
    L, D = o.shape
    hd = FOX_HD
    tr = _tile(L, 320, 16)

    def body(d_ref, o_ref, g_ref, do_ref, dg_ref):
        for h in range(H):
            sl = slice(h * hd, (h + 1) * hd)
            s = _sigmoid(g_ref[:, sl])
            d = d_ref[:, sl]
            o = o_ref[:, sl]
            do = d * s
            dg_ref[:, sl] = (d * o * s * (1.0 - s)).astype(BF16)
            hi, mid, lo = _split3(-jnp.sum(do * o, axis=-1, keepdims=True))
            do_ref[:, h * AUG:h * AUG + hd] = do.astype(BF16)
            do_ref[:, h * AUG + hd:(h + 1) * AUG] = _aug_lanes(tr, [hi, mid, lo]).astype(BF16)

    row = pl.BlockSpec((tr, D), lambda i: (i, 0))
    return pl.pallas_call(
        body, name=name, grid=(L // tr,), in_specs=[row, row, pl.BlockSpec((tr, D), lambda i: (i, 1))],
        out_specs=[pl.BlockSpec((tr, H * AUG), lambda i: (i, 0)), row],
        out_shape=[jax.ShapeDtypeStruct((L, H * AUG), BF16), jax.ShapeDtypeStruct((L, D), BF16)],
        compiler_params=_cp("parallel"),
    )(d, o, QO)


def _row_tile(R, C, n_arrays):
    budget = VMEM_LIMIT // (3 * n_arrays * 4 * max(C, LANE))
    return _tile(R, max(16, budget // 16 * 16), 16)


def _sum_parts(parts, *, name, out_dtype=F32):
    R, C = parts[0].shape
    tr = _row_tile(R, C, len(parts) + 1)

    def body(*refs):
        acc = refs[0][...].astype(F32)
        for r in refs[1:-1]:
            acc = acc + r[...].astype(F32)
        refs[-1][...] = acc.astype(out_dtype)

    blk = pl.BlockSpec((tr, C), lambda i: (i, 0))
    return pl.pallas_call(
        body, name=name, grid=(R // tr,), in_specs=[blk] * len(parts), out_specs=blk,
        out_shape=jax.ShapeDtypeStruct((R, C), out_dtype), compiler_params=_cp("parallel"),
    )(*parts)


def _sum_slots(x, *, name):
    S, R, C = x.shape
    tr = _row_tile(R, C, S + 1)

    def body(x_ref, o_ref):
        acc = x_ref[0].astype(F32)
        for s in range(1, S):
            acc = acc + x_ref[s].astype(F32)
        o_ref[...] = acc

    return pl.pallas_call(
        body, name=name, grid=(R // tr,), in_specs=[pl.BlockSpec((S, tr, C), lambda i: (0, i, 0))],
        out_specs=pl.BlockSpec((tr, C), lambda i: (i, 0)),
        out_shape=jax.ShapeDtypeStruct((R, C), F32), compiler_params=_cp("parallel"),
    )(x)


def _adamw(w, m, v, gparts, *, name):
    R, C = w.shape
    tr = _row_tile(R, C, 7 + len(gparts))
    ng = len(gparts)

    def body(*refs):
        w_ref, m_ref, v_ref = refs[:3]
        g = refs[3][...]
        for r in refs[4:3 + ng]:
            g = g + r[...]
        g_ref, d_ref, nm_ref, nv_ref = refs[3 + ng:]
        nm = ADAM_B1 * m_ref[...] + (1.0 - ADAM_B1) * g
        nv = ADAM_B2 * v_ref[...] + (1.0 - ADAM_B2) * (g * g)
        m_hat = nm / (1.0 - ADAM_B1 ** ADAM_STEP)
        v_hat = nv / (1.0 - ADAM_B2 ** ADAM_STEP)
        g_ref[...] = g
        d_ref[...] = -ADAM_LR * (m_hat / (jnp.sqrt(v_hat) + ADAM_EPS) + ADAM_WD * w_ref[...])
        nm_ref[...] = nm
        nv_ref[...] = nv

    blk = pl.BlockSpec((tr, C), lambda i: (i, 0))
    return pl.pallas_call(
        body, name=name, grid=(R // tr,), in_specs=[blk] * (3 + ng), out_specs=[blk] * 4,
        out_shape=[jax.ShapeDtypeStruct((R, C), F32)] * 4, compiler_params=_cp("parallel"),
    )(w, m, v, *gparts)


def _chip_peers():
    x, y, c = lax.axis_index("x"), lax.axis_index("y"), lax.axis_index("c")
    return (x, y, c), [(1 - x, y), (x, 1 - y), (1 - x, 1 - y)]


def _gather_chips(arrs, *, name):
    n = len(arrs)
    hs = [a.shape[0] // 2 for a in arrs]

    def body(*refs):
        ins, outs = refs[:n], refs[n:2 * n]
        send, recv = refs[2 * n:]
        (x, y, c), chips = _chip_peers()
        me = 2 * x + y

        def landing(a, chip_idx):
            return outs[a].at[chip_idx, pl.ds(c * hs[a], hs[a])]

        ici, passed = [], []
        for a in range(n):
            for j, (px, py) in enumerate(chips):
                cp = pltpu.make_async_remote_copy(
                    src_ref=ins[a].at[pl.ds(c * hs[a], hs[a])], dst_ref=landing(a, me), send_sem=send.at[a, j],
                    recv_sem=recv.at[a, j], device_id=(px, py, c), device_id_type=MESH)
                cp.start()
                ici.append(cp)
        for a in range(n):
            for j, (px, py) in enumerate(chips):
                ici[3 * a + j].wait_recv()
                src = landing(a, 2 * px + py)
                cp = pltpu.make_async_remote_copy(
                    src_ref=src, dst_ref=src, send_sem=send.at[a, 3 + j], recv_sem=recv.at[a, 3 + j],
                    device_id=(x, y, 1 - c), device_id_type=MESH)
                cp.start()
                passed.append(cp)
        for cp in ici:
            cp.wait_send()
        for cp in passed:
            cp.wait()

    chip = 2 * lax.axis_index("x") + lax.axis_index("y")
    outs = pl.pallas_call(
        body, name=name, in_specs=[ANY] * n, out_specs=[ANY] * n,
        out_shape=[jax.ShapeDtypeStruct((4,) + a.shape, a.dtype) for a in arrs],
        scratch_shapes=[pltpu.SemaphoreType.DMA((n, 6)), pltpu.SemaphoreType.DMA((n, 6))],
    )(*arrs)
    return [lax.dynamic_update_index_in_dim(o, a, chip, 0) for o, a in zip(outs, arrs)]


def _scatter_chips(arrs, *, name):
    n = len(arrs)

    def body(*refs):
        ins, outs = refs[:n], refs[n:2 * n]
        send, recv = refs[2 * n:]
        (x, y, c), chips = _chip_peers()
        copies = []
        for a in range(n):
            for j, (px, py) in enumerate(chips):
                cp = pltpu.make_async_remote_copy(
                    src_ref=ins[a].at[2 * px + py], dst_ref=outs[a].at[j], send_sem=send.at[a, j],
                    recv_sem=recv.at[a, j], device_id=(px, py, c), device_id_type=MESH)
                cp.start()
                copies.append(cp)
        for cp in copies:
            cp.wait()

    return pl.pallas_call(
        body, name=name, in_specs=[ANY] * n, out_specs=[ANY] * n,
        out_shape=[jax.ShapeDtypeStruct((3,) + a.shape[1:], a.dtype) for a in arrs],
        scratch_shapes=[pltpu.SemaphoreType.DMA((n, 3)), pltpu.SemaphoreType.DMA((n, 3))],
    )(*arrs)


def _send_other_halves(arrs, *, name):
    n = len(arrs)
    hs = [a.shape[1] // 2 for a in arrs]

    def body(*refs):
        ins, outs = refs[:n], refs[n:2 * n]
        send, recv = refs[2 * n:]
        x, y, c = lax.axis_index("x"), lax.axis_index("y"), lax.axis_index("c")
        copies = []
        for a in range(n):
            cp = pltpu.make_async_remote_copy(
                src_ref=ins[a].at[pl.ds(0, 4), pl.ds((1 - c) * hs[a], hs[a])], dst_ref=outs[a], send_sem=send.at[a],
                recv_sem=recv.at[a], device_id=(x, y, 1 - c), device_id_type=MESH)
            cp.start()
            copies.append(cp)
        for cp in copies:
            cp.wait()

    return pl.pallas_call(
        body, name=name, in_specs=[ANY] * n, out_specs=[ANY] * n,
        out_shape=[jax.ShapeDtypeStruct((4, h) + a.shape[2:], a.dtype) for a, h in zip(arrs, hs)],
        scratch_shapes=[pltpu.SemaphoreType.DMA((n,)), pltpu.SemaphoreType.DMA((n,))],
    )(*arrs)


def _join_halves(arrs, *, name):
    n = len(arrs)

    def body(*refs):
        ins, outs = refs[:n], refs[n:2 * n]
        send, recv = refs[2 * n:]
        x, y, c = lax.axis_index("x"), lax.axis_index("y"), lax.axis_index("c")
        copies = []
        for a in range(n):
            h = ins[a].shape[0]
            cp = pltpu.make_async_remote_copy(
                src_ref=ins[a], dst_ref=outs[a].at[pl.ds(c * h, h)], send_sem=send.at[a], recv_sem=recv.at[a],
                device_id=(x, y, 1 - c), device_id_type=MESH)
            cp.start()
            copies.append(cp)
        for cp in copies:
            cp.wait()

    outs = pl.pallas_call(
        body, name=name, in_specs=[ANY] * n, out_specs=[ANY] * n,
        out_shape=[jax.ShapeDtypeStruct((2 * a.shape[0],) + a.shape[1:], a.dtype) for a in arrs],
        scratch_shapes=[pltpu.SemaphoreType.DMA((n,)), pltpu.SemaphoreType.DMA((n,))],
    )(*arrs)
    core = lax.axis_index("c")
    return [lax.dynamic_update_slice_in_dim(o, a, core * a.shape[0], 0) for o, a in zip(outs, arrs)]


def _gather_all(a, *, name):
    def body(in_ref, out_ref, send, recv):
        x, y, c = lax.axis_index("x"), lax.axis_index("y"), lax.axis_index("c")
        me = 4 * x + 2 * y + c
        copies = []
        for j in range(1, 8):
            fx, fy, fc = (j >> 2) & 1, (j >> 1) & 1, j & 1
            peer = (x ^ fx, y ^ fy, c ^ fc)
            cp = pltpu.make_async_remote_copy(
                src_ref=in_ref, dst_ref=out_ref.at[me], send_sem=send.at[j - 1], recv_sem=recv.at[j - 1],
                device_id=peer, device_id_type=MESH)
            cp.start()
            copies.append(cp)
        for cp in copies:
            cp.wait()

    out = pl.pallas_call(
        body, name=name, in_specs=[ANY], out_specs=ANY,
        out_shape=jax.ShapeDtypeStruct((8,) + a.shape, a.dtype),
        scratch_shapes=[pltpu.SemaphoreType.DMA((7,)), pltpu.SemaphoreType.DMA((7,))],
    )(a)
    me = 4 * lax.axis_index("x") + 2 * lax.axis_index("y") + lax.axis_index("c")
    return lax.dynamic_update_index_in_dim(out, a, me, 0)


def _pack(arrs):
    flat = jnp.concatenate([a.astype(F32).reshape(-1) for a in arrs])
    n = flat.shape[0]
    pad = (-n) % (16 * LANE)
    return jnp.pad(flat, (0, pad)).reshape(-1, LANE)


def _unpack(buf, shapes):
    flat = buf.reshape(-1)
    out, off = [], 0
    for s in shapes:
        n = int(np.prod(s))
        out.append(flat[off:off + n].reshape(s))
        off += n
    return out


def _to_shards(g, axis):
    parts = jnp.split(g, 4, axis=axis)
    return jnp.stack([p.reshape(-1, p.shape[-1]) for p in parts])


def kernel(x, meta, ln_g, ln_b, gla_w_in, gla_w_g2, gla_b_g2, gla_norm_g, gla_w_out, kv_w, kv_bf, fox_w_in, fox_w_out, ffn_w_up, ffn_conv_w, ffn_conv_b, ffn_w_down, loss_target, m_meta, m_ln_g, m_ln_b, m_gla_w_in, m_gla_w_g2, m_gla_b_g2, m_gla_norm_g, m_gla_w_out, m_kv_w, m_kv_bf, m_fox_w_in, m_fox_w_out, m_ffn_w_up, m_ffn_conv_w, m_ffn_conv_b, m_ffn_w_down, v_meta, v_ln_g, v_ln_b, v_gla_w_in, v_gla_w_g2, v_gla_b_g2, v_gla_norm_g, v_gla_w_out, v_kv_w, v_kv_bf, v_fox_w_in, v_fox_w_out, v_ffn_w_up, v_ffn_conv_w, v_ffn_conv_b, v_ffn_w_down):
    D = x.shape[-1]
    L = x.shape[1] + FRONT
    HG = GLA_HEADS
    DK, DV = D // 2, D
    HK, HV = DK // HG, DV // HG
    HF = D // FOX_HD
    DFF = ffn_w_down.shape[1] * 4
    chip = 2 * lax.axis_index("x") + lax.axis_index("y")

    big_names = ["gla_w_in", "gla_w_out", "kv_w", "fox_w_in", "fox_w_out", "ffn_w_up", "ffn_w_down"]
    big = dict(gla_w_in=gla_w_in, gla_w_out=gla_w_out, kv_w=kv_w, fox_w_in=fox_w_in, fox_w_out=fox_w_out,
               ffn_w_up=ffn_w_up, ffn_w_down=ffn_w_down)
    big_axis = dict(gla_w_in=2, gla_w_out=1, kv_w=1, fox_w_in=2, fox_w_out=1, ffn_w_up=2, ffn_w_down=1)
    small_names = ["meta", "ln_g", "ln_b", "gla_w_g2", "gla_b_g2", "gla_norm_g", "ffn_conv_w"]
    small = dict(meta=meta, ln_g=ln_g, ln_b=ln_b, gla_w_g2=gla_w_g2, gla_b_g2=gla_b_g2, gla_norm_g=gla_norm_g,
                 ffn_conv_w=ffn_conv_w)
    small_shapes = [small[k].shape for k in small_names]
    gathered = _gather_chips([big[k].astype(BF16) for k in big_names] + [_pack([small[k] for k in small_names])],
                             name="gather_weights")
    full = {k: jnp.concatenate([gathered[i][s] for s in range(4)], axis=big_axis[k]) for i, k in enumerate(big_names)}
    sm_sh = [_unpack(gathered[-1][s], small_shapes) for s in range(4)]
    fs = {k: jnp.concatenate([sm_sh[s][i] for s in range(4)], axis=-1) for i, k in enumerate(small_names)}

    pad_cols = lambda w: jnp.pad(w, ((0, 0), (0, LANE - w.shape[1])))
    W_P, W_Pgl = [], []
    for l in range(N_A_LAYERS):
        w = full["gla_w_in"][l]
        W_P.append(jnp.concatenate([w[:, :2 * DK + DV], w[:, 2 * DK + DV + GLA_RANK:],
                                    pad_cols(w[:, 2 * DK + DV:2 * DK + DV + GLA_RANK])], axis=1))
    w2p = [jnp.pad(fs["gla_w_g2"][l], ((0, LANE - GLA_RANK), (0, 0))).astype(BF16) for l in range(N_A_LAYERS)]
    W_kv = full["kv_w"][:, :2 * D]
    W_f = pad_cols(full["kv_w"][:, 2 * D:])
    W_kvf = jnp.concatenate([W_kv, W_f], axis=1)
    bf_pad = jnp.pad(kv_bf, (0, LANE - HF)).reshape(1, LANE)
    W_u = [full["ffn_w_up"][l][:, :DFF] for l in range(DEPTH)]
    W_g = [full["ffn_w_up"][l][:, DFF:] for l in range(DEPTH)]
    cw_u = [fs["ffn_conv_w"][l][:, :DFF] for l in range(DEPTH)]
    cw_g = [fs["ffn_conv_w"][l][:, DFF:] for l in range(DEPTH)]
    cb_u = [ffn_conv_b[l][None, :DFF] for l in range(DEPTH)]
    cb_g = [ffn_conv_b[l][None, DFF:] for l in range(DEPTH)]
    gl_blk = (2 * DK + 2 * DV) // LANE
    r_blk = (2 * DK + DV) // DV

    h = jnp.concatenate([jnp.concatenate([jnp.zeros((N_PAD, D), F32), fs["meta"]], axis=0), x[0]], axis=0)
    hb = h.astype(BF16)
    saved = []
    kvs = None
    for l in range(DEPTH):
        s = dict(h=h, hb=hb)
        if l < N_A_LAYERS:
            s["P"] = _mm(hb, W_P[l], name=f"gla_in_{l}")
            s["la"] = _gla_gate_fwd(s["P"], w2p[l], fs["gla_b_g2"][l][None], gl_blk, name=f"gla_gate_{l}")
            s["o"], s["S"] = _gla_chunk_fwd(s["P"], s["la"], HG, HK, HV, name=f"gla_chunk_{l}")
            s["gated"] = _gla_post_fwd(s["o"], s["P"], fs["gla_norm_g"][l][None], HG, HV, r_blk, name=f"gla_post_{l}")
            s["mix"] = _mm(s["gated"], full["gla_w_out"][l], name=f"gla_out_{l}")
        else:
            j = l - N_A_LAYERS
            if kvs is None:
                KV = _mm(hb, W_kv, out_dtype=BF16, name="kv_proj")
                f = _mm(hb, W_f, name="kv_gate_proj")
                c = _fox_c_fwd(f, bf_pad, name="fox_c")
                KA, VA = _fox_prep_kv(KV, c, HF, name="fox_prep_kv")
                kvs = dict(KA=KA, KT=KA.T, VA=VA, f=f, c=c, hb=hb)
            s["QO"] = _mm(hb, full["fox_w_in"][j], name=f"fox_in_{j}")
            s["QA"] = _fox_prep_q(s["QO"], kvs["c"], HF, name=f"fox_prep_q_{j}")
            s["o"], s["lse"] = _fox_attn_fwd(s["QA"], kvs["KT"], kvs["VA"], HF, name=f"fox_attn_{j}")
            s["gated"] = _fox_gate_fwd(s["o"], s["QO"], name=f"fox_gate_{j}")
            s["mix"] = _mm(s["gated"], full["fox_w_out"][j], name=f"fox_out_{j}")
        s["h1"], s["h1b"] = _ln_fwd(h, s["mix"], fs["ln_g"][l, 0][None], fs["ln_b"][l, 0][None], name=f"ln_a_{l}")
        s["Uu"] = _mm(s["h1b"], W_u[l], name=f"ffn_up_u_{l}")
        s["Ug"] = _mm(s["h1b"], W_g[l], name=f"ffn_up_g_{l}")
        s["a"] = _conv_act_fwd(s["Uu"], s["Ug"], cw_u[l], cw_g[l], cb_u[l], cb_g[l], name=f"ffn_conv_{l}")
        s["ffn"] = _mm(s["a"], full["ffn_w_down"][l], name=f"ffn_down_{l}")
        h, hb = _ln_fwd(s["h1"], s["ffn"], fs["ln_g"][l, 1][None], fs["ln_b"][l, 1][None], name=f"ln_b_{l}")
        saved.append(s)

    loss_acc, dh = _loss(h, loss_target[0], name="loss")

    gW = {}
    d_ln_g = [[None, None] for _ in range(DEPTH)]
    d_ln_b = [[None, None] for _ in range(DEPTH)]
    d_cw, d_cb = [None] * DEPTH, [None] * DEPTH
    d_wg2, d_bg2, d_ng = [None] * N_A_LAYERS, [None] * N_A_LAYERS, [None] * N_A_LAYERS
    dkv = (jnp.zeros((L, HF * AUG), F32), jnp.zeros((L, HF * AUG), F32))
    dcqs = []
    for l in reversed(range(DEPTH)):
        s = saved[l]
        dz, dzb, d_ln_g[l][1], d_ln_b[l][1] = _ln_bwd(dh, s["h1"], s["ffn"], fs["ln_g"][l, 1][None], name=f"ln_b_bwd_{l}")
        dA = _mm(dzb, full["ffn_w_down"][l], tb=True, name=f"ffn_down_dx_{l}")
        gW[("ffn_w_down", l)] = _mm(s["a"], dzb, ta=True, out_dtype=BF16, name=f"ffn_down_dw_{l}")
        dcu, dcg, dwu, dwg, dbu, dbg = _conv_act_bwd(s["Uu"], s["Ug"], cw_u[l], cw_g[l], cb_u[l], cb_g[l], dA,
                                                     name=f"ffn_conv_bwd_{l}")
        d_cw[l] = jnp.concatenate([dwu, dwg], axis=1)
        d_cb[l] = jnp.concatenate([dbu, dbg], axis=1)[0]
        dUu = _conv_in_bwd(dcu, cw_u[l], name=f"ffn_conv_dx_u_{l}")
        dUg = _conv_in_bwd(dcg, cw_g[l], name=f"ffn_conv_dx_g_{l}")
        gW[("ffn_w_up", l)] = jnp.concatenate(
            [_mm(s["h1b"], dUu, ta=True, out_dtype=BF16, name=f"ffn_up_dw_u_{l}"),
             _mm(s["h1b"], dUg, ta=True, out_dtype=BF16, name=f"ffn_up_dw_g_{l}")], axis=1)
        dh1 = _mm(dUu, W_u[l], tb=True, add=dz, add_scale=ALPHA, name=f"ffn_up_dx_u_{l}")
        dh1 = _mm(dUg, W_g[l], tb=True, add=dh1, name=f"ffn_up_dx_g_{l}")
        dz, dzb, d_ln_g[l][0], d_ln_b[l][0] = _ln_bwd(dh1, s["h"], s["mix"], fs["ln_g"][l, 0][None], name=f"ln_a_bwd_{l}")
        if l < N_A_LAYERS:
            dgated = _mm(dzb, full["gla_w_out"][l], tb=True, name=f"gla_out_dx_{l}")
            gW[("gla_w_out", l)] = _mm(s["gated"], dzb, ta=True, out_dtype=BF16, name=f"gla_out_dw_{l}")
            do, drb, d_ng[l] = _gla_post_bwd(dgated, s["o"], s["P"], fs["gla_norm_g"][l][None], HG, HV, r_blk,
                                             name=f"gla_post_bwd_{l}")
            dq, dk, dvb, dla = _gla_chunk_bwd(s["P"], s["la"], s["S"], do, HG, HK, HV, name=f"gla_chunk_bwd_{l}")
            dglb, dw2, d_bg2[l] = _gla_gate_bwd(dla, s["P"], w2p[l], fs["gla_b_g2"][l][None], gl_blk,
                                               name=f"gla_gate_bwd_{l}")
            d_wg2[l] = dw2[:GLA_RANK]
            dP = jnp.concatenate([dq, dk, dvb, drb, dglb], axis=1)
            gP = _mm(s["hb"], dP, ta=True, out_dtype=BF16, name=f"gla_in_dw_{l}")
            gW[("gla_w_in", l)] = jnp.concatenate(
                [gP[:, :2 * DK + DV], gP[:, 2 * DK + 2 * DV:2 * DK + 2 * DV + GLA_RANK], gP[:, 2 * DK + DV:2 * DK + 2 * DV]],
                axis=1)
            dh = _mm(dP, W_P[l], tb=True, add=dz, add_scale=ALPHA, name=f"gla_in_dx_{l}")
        else:
            j = l - N_A_LAYERS
            dgo = _mm(dzb, full["fox_w_out"][j], tb=True, name=f"fox_out_dx_{j}")
            gW[("fox_w_out", j)] = _mm(s["gated"], dzb, ta=True, out_dtype=BF16, name=f"fox_out_dw_{j}")
            DOA, dogb = _fox_gate_bwd(dgo, s["o"], s["QO"], HF, name=f"fox_gate_bwd_{j}")
            DQA, DKA, DVA = _fox_attn_bwd(s["QA"], kvs["KA"], kvs["VA"], DOA, s["lse"], dkv, HF, name=f"fox_attn_bwd_{j}")
            dkv = (DKA, DVA)
            dqb, dcq = _fox_post_q(DQA, HF, name=f"fox_post_q_{j}")
            dcqs.append(dcq)
            dQO = jnp.concatenate([dqb, dogb], axis=1)
            gW[("fox_w_in", j)] = _mm(s["hb"], dQO, ta=True, out_dtype=BF16, name=f"fox_in_dw_{j}")
            dh = _mm(dQO, full["fox_w_in"][j], tb=True, add=dz, add_scale=ALPHA, name=f"fox_in_dx_{j}")
            if j == 0:
                dkvb, dck = _fox_post_kv(DKA, DVA, HF, name="fox_post_kv")
                dfb, d_bf = _fox_c_bwd(dcqs + [dck], kvs["f"], bf_pad, name="fox_c_bwd")
                dKVF = jnp.concatenate([dkvb, dfb], axis=1)
                gkv = _mm(kvs["hb"], dKVF, ta=True, out_dtype=BF16, name="kv_dw")
                gW[("kv_w", 0)] = gkv[:, :2 * D + HF]
                dh = _mm(dKVF, W_kvf, tb=True, add=dh, name="kv_dx")

    stack = lambda k, n: jnp.stack([gW[(k, i)] for i in range(n)])
    gfull = dict(gla_w_in=stack("gla_w_in", N_A_LAYERS), gla_w_out=stack("gla_w_out", N_A_LAYERS), kv_w=gW[("kv_w", 0)],
                 fox_w_in=stack("fox_w_in", DEPTH - N_A_LAYERS), fox_w_out=stack("fox_w_out", DEPTH - N_A_LAYERS),
                 ffn_w_up=stack("ffn_w_up", DEPTH), ffn_w_down=stack("ffn_w_down", DEPTH))
    core = lax.axis_index("c")
    g4 = [_to_shards(gfull[k], big_axis[k]) for k in big_names]
    sib = _send_other_halves(g4, name="pair_exchange")
    pair = []
    for i, k in enumerate(big_names):
        _, R, C = g4[i].shape
        mine = lax.dynamic_slice_in_dim(g4[i], core * (R // 2), R // 2, axis=1)
        pair.append(_sum_parts([mine.reshape(2 * R, C), sib[i].reshape(2 * R, C)], out_dtype=BF16,
                               name=f"sum_pair_{k}").reshape(4, R // 2, C))
    recv = _scatter_chips(pair, name="scatter_grads")
    halves = []
    for i, k in enumerate(big_names):
        own = lax.dynamic_index_in_dim(pair[i], chip, axis=0, keepdims=False)
        halves.append(_sum_parts([own, recv[i][0], recv[i][1], recv[i][2]], name=f"sum_chips_{k}"))
    gsum = _join_halves(halves, name="join_halves")
    moments = dict(gla_w_in=(m_gla_w_in, v_gla_w_in), gla_w_out=(m_gla_w_out, v_gla_w_out), kv_w=(m_kv_w, v_kv_w),
                   fox_w_in=(m_fox_w_in, v_fox_w_in), fox_w_out=(m_fox_w_out, v_fox_w_out),
                   ffn_w_up=(m_ffn_w_up, v_ffn_w_up), ffn_w_down=(m_ffn_w_down, v_ffn_w_down))
    res = {}
    for i, k in enumerate(big_names):
        w = big[k]
        sh = w.shape
        flat = lambda a: a.reshape(-1, sh[-1])
        outs = _adamw(flat(w), flat(moments[k][0]), flat(moments[k][1]), [gsum[i]], name=f"adamw_{k}")
        res[k] = [o.reshape(sh) for o in outs]

    dmeta = dh[N_PAD:FRONT]
    sg = dict(meta=dmeta,
              ln_g=jnp.stack([jnp.concatenate(d_ln_g[l], axis=0) for l in range(DEPTH)]),
              ln_b=jnp.stack([jnp.concatenate(d_ln_b[l], axis=0) for l in range(DEPTH)]),
              gla_w_g2=jnp.stack(d_wg2), gla_b_g2=jnp.stack([d[0] for d in d_bg2]),
              gla_norm_g=jnp.stack([d[0] for d in d_ng]), ffn_conv_w=jnp.stack(d_cw),
              kv_bf=d_bf[0, :HF], ffn_conv_b=jnp.stack(d_cb), loss=loss_acc[0, :1])
    sg_names = small_names + ["kv_bf", "ffn_conv_b", "loss"]
    sg_shapes = [sg[k].shape for k in sg_names]
    red = _sum_slots(_gather_all(_pack([sg[k] for k in sg_names]), name="gather_small_grads"), name="sum_small_grads")
    red = dict(zip(sg_names, _unpack(red, sg_shapes)))
    loss = red["loss"][0]
    loc = {}
    for k in small_names:
        wdt = small[k].shape[-1]
        loc[k] = lax.dynamic_slice_in_dim(red[k], chip * wdt, wdt, axis=red[k].ndim - 1)
    loc["kv_bf"] = red["kv_bf"]
    loc["ffn_conv_b"] = red["ffn_conv_b"]
    sm_all = small_names + ["kv_bf", "ffn_conv_b"]
    sw = dict(small, kv_bf=kv_bf, ffn_conv_b=ffn_conv_b)
    sm_m = dict(meta=m_meta, ln_g=m_ln_g, ln_b=m_ln_b, gla_w_g2=m_gla_w_g2, gla_b_g2=m_gla_b_g2, gla_norm_g=m_gla_norm_g,
                ffn_conv_w=m_ffn_conv_w, kv_bf=m_kv_bf, ffn_conv_b=m_ffn_conv_b)
    sm_v = dict(meta=v_meta, ln_g=v_ln_g, ln_b=v_ln_b, gla_w_g2=v_gla_w_g2, gla_b_g2=v_gla_b_g2, gla_norm_g=v_gla_norm_g,
                ffn_conv_w=v_ffn_conv_w, kv_bf=v_kv_bf, ffn_conv_b=v_ffn_conv_b)
    shapes_loc = [sw[k].shape for k in sm_all]
    outs = _adamw(_pack([sw[k] for k in sm_all]), _pack([sm_m[k] for k in sm_all]), _pack([sm_v[k] for k in sm_all]),
                  [_pack([loc[k] for k in sm_all])], name="adamw_small")
    outs = [_unpack(o, shapes_loc) for o in outs]
    for i, k in enumerate(sm_all):
        res[k] = [outs[q][i] for q in range(4)]

    order = ["meta", "ln_g", "ln_b", "gla_w_in", "gla_w_g2", "gla_b_g2", "gla_norm_g", "gla_w_out", "kv_w", "kv_bf",
             "fox_w_in", "fox_w_out", "ffn_w_up", "ffn_conv_w", "ffn_conv_b", "ffn_w_down"]
    grad_x = dh[FRONT:][None]
    return (loss, grad_x, *[res[k][0] for k in order], *[res[k][1] for k in order], *[res[k][2] for k in order],
            *[res[k][3] for k in order])
```

```python
import functools

import numpy as np
import jax
import jax.numpy as jnp
from jax import lax
from jax.experimental import pallas as pl
from jax.experimental.pallas import tpu as pltpu

F32 = jnp.float32
BF16 = jnp.bfloat16
HIGHEST = lax.Precision.HIGHEST

DEPTH = 4
N_A_LAYERS = DEPTH // 2
N_META = 16
FRONT = 128
N_PAD = FRONT - N_META
ALPHA = (2.0 * DEPTH) ** 0.25
LN_EPS = 1e-5
GLA_HEADS = 4
GLA_RANK = 16
GLA_TAU = 16.0
GLA_CHUNK = 64
GLA_GROUP = 2
FOX_HD = 128
LANE = 128
ADAM_LR = 0.001
ADAM_B1 = 0.9
ADAM_B2 = 0.999
ADAM_EPS = 1e-08
ADAM_WD = 0.01
ADAM_STEP = 10
NEG = -(2.0 ** 100)
VMEM_LIMIT = 50 * 1024 * 1024
MESH = pl.DeviceIdType.MESH
ANY = pl.BlockSpec(memory_space=pl.ANY)


def _tile(n, pref, align):
    best = None
    t = align
    while t <= min(n, pref):
        if n % t == 0:
            best = t
        t += align
    return best if best is not None else n


def _cp(*sem):
    return pltpu.CompilerParams(dimension_semantics=sem, vmem_limit_bytes=VMEM_LIMIT)


def _dot(a, b, ca, cb, precision=None):
    return lax.dot_general(a, b, (((ca,), (cb,)), ((), ())), precision=precision,
                           preferred_element_type=F32)


def _sigmoid(x):
    return 1.0 / (1.0 + jnp.exp(-x))


def _log_sigmoid(z):
    return jnp.minimum(z, 0.0) - jnp.log(1.0 + jnp.exp(-jnp.abs(z)))


def _rows(i, tr, n=None):
    n = tr if n is None else n
    return i * tr + lax.broadcasted_iota(jnp.int32, (n, 1), 0)


def _mm(a, b, *, ta=False, tb=False, out_dtype=F32, add=None, add_scale=1.0, name):
    if ta:
        K, M = a.shape
    else:
        M, K = a.shape
    if tb:
        N, K2 = b.shape
    else:
        K2, N = b.shape
    assert K == K2, (a.shape, b.shape, ta, tb)
    tm = _tile(M, 1024, LANE) if ta else _tile(M, 1040, 16)
    tn = _tile(N, 1024, LANE)
    tk = _tile(K, 2048, LANE if ((not ta) or tb) else 16)
    nk = K // tk
    ca = 0 if ta else 1
    cb = 1 if tb else 0

    def body(*refs):
        if add is None:
            a_ref, b_ref, o_ref = refs[:3]
            add_ref = None
        else:
            a_ref, b_ref, add_ref, o_ref = refs[:4]
        part = _dot(a_ref[...].astype(BF16), b_ref[...].astype(BF16), ca, cb)

        def finish(acc):
            if add_ref is not None:
                acc = acc + add_scale * add_ref[...]
            o_ref[...] = acc.astype(out_dtype)

        if nk == 1:
            finish(part)
        else:
            acc_ref = refs[-1]
            k = pl.program_id(2)

            @pl.when(k == 0)
            def _():
                acc_ref[...] = part

            @pl.when(k > 0)
            def _():
                acc_ref[...] += part

            @pl.when(k == nk - 1)
            def _():
                finish(acc_ref[...])

    a_spec = pl.BlockSpec((tk, tm), lambda i, j, k: (k, i)) if ta else pl.BlockSpec((tm, tk), lambda i, j, k: (i, k))
    b_spec = pl.BlockSpec((tn, tk), lambda i, j, k: (j, k)) if tb else pl.BlockSpec((tk, tn), lambda i, j, k: (k, j))
    o_spec = pl.BlockSpec((tm, tn), lambda i, j, k: (i, j))
    in_specs = [a_spec, b_spec] + ([o_spec] if add is not None else [])
    args = (a, b) + ((add,) if add is not None else ())
    return pl.pallas_call(
        body, name=name, grid=(M // tm, N // tn, nk), in_specs=in_specs, out_specs=o_spec,
        out_shape=jax.ShapeDtypeStruct((M, N), out_dtype),
        scratch_shapes=[pltpu.VMEM((tm, tn), F32)] if nk > 1 else [],
        compiler_params=_cp("parallel", "parallel", "arbitrary"),
    )(*args)


def _ln_stats(h, mix):
    z = ALPHA * h + mix
    mu = jnp.mean(z, axis=-1, keepdims=True)
    zc = z - mu
    var = jnp.mean(zc * zc, axis=-1, keepdims=True)
    rstd = lax.rsqrt(var + LN_EPS)
    return zc * rstd, rstd


def _ln_fwd(h, mix, g, b, *, name):
    L, D = h.shape
    tr = _tile(L, 160, 16)

    def body(h_ref, m_ref, g_ref, b_ref, o_ref, ob_ref):
        xhat, _ = _ln_stats(h_ref[...], m_ref[...])
        y = xhat * g_ref[...] + b_ref[...]
        o_ref[...] = y
        ob_ref[...] = y.astype(BF16)

    row = pl.BlockSpec((tr, D), lambda i: (i, 0))
    vec = pl.BlockSpec((1, D), lambda i: (0, 0))
    return pl.pallas_call(
        body, name=name, grid=(L // tr,), in_specs=[row, row, vec, vec], out_specs=[row, row],
        out_shape=[jax.ShapeDtypeStruct((L, D), F32), jax.ShapeDtypeStruct((L, D), BF16)],
        compiler_params=_cp("parallel"),
    )(h, mix, g, b)


def _ln_bwd(dy, h, mix, g, *, name):
    L, D = h.shape
    tr = _tile(L, 160, 16)

    def body(dy_ref, h_ref, m_ref, g_ref, dz_ref, dzb_ref, dg_ref, db_ref):
        i = pl.program_id(0)
        xhat, rstd = _ln_stats(h_ref[...], m_ref[...])
        dy = dy_ref[...]
        dxh = dy * g_ref[...]
        m1 = jnp.mean(dxh, axis=-1, keepdims=True)
        m2 = jnp.mean(dxh * xhat, axis=-1, keepdims=True)
        dz = rstd * (dxh - m1 - xhat * m2)
        dz_ref[...] = dz
        dzb_ref[...] = dz.astype(BF16)
        pg = jnp.sum(dy * xhat, axis=0, keepdims=True)
        pb = jnp.sum(dy, axis=0, keepdims=True)

        @pl.when(i == 0)
        def _():
            dg_ref[...] = pg
            db_ref[...] = pb

        @pl.when(i > 0)
        def _():
            dg_ref[...] += pg
            db_ref[...] += pb

    row = pl.BlockSpec((tr, D), lambda i: (i, 0))
    vec = pl.BlockSpec((1, D), lambda i: (0, 0))
    return pl.pallas_call(
        body, name=name, grid=(L // tr,), in_specs=[row, row, row, vec], out_specs=[row, row, vec, vec],
        out_shape=[jax.ShapeDtypeStruct((L, D), F32), jax.ShapeDtypeStruct((L, D), BF16),
                   jax.ShapeDtypeStruct((1, D), F32), jax.ShapeDtypeStruct((1, D), F32)],
        compiler_params=_cp("arbitrary"),
    )(dy, h, mix, g)


def _loss(h, target, *, name):
    L, D = h.shape
    tr = FRONT

    def body(h_ref, t_ref, acc_ref, dy_ref):
        i = pl.program_id(0)
        e = jnp.where(i >= 1, h_ref[...] - t_ref[...], 0.0)
        dy_ref[...] = e * (1.0 / D)
        part = 0.5 * jnp.sum(jnp.sum(e * e, axis=-1, keepdims=True) * (1.0 / D), axis=0, keepdims=True)

        @pl.when(i == 0)
        def _():
            acc_ref[...] = jnp.zeros_like(acc_ref)

        acc_ref[...] += jnp.broadcast_to(part, acc_ref.shape)

    return pl.pallas_call(
        body, name=name, grid=(L // tr,),
        in_specs=[pl.BlockSpec((tr, D), lambda i: (i, 0)),
                  pl.BlockSpec((tr, D), lambda i: (jnp.maximum(i - 1, 0), 0))],
        out_specs=[pl.BlockSpec((8, LANE), lambda i: (0, 0)), pl.BlockSpec((tr, D), lambda i: (i, 0))],
        out_shape=[jax.ShapeDtypeStruct((8, LANE), F32), jax.ShapeDtypeStruct((L, D), F32)],
        compiler_params=_cp("arbitrary"),
    )(h, target)


def _gla_gate_fwd(P, w2p, b2, gl_blk, *, name):
    L = P.shape[0]
    DK = w2p.shape[1]
    tr = _tile(L, 640, 16)

    def body(gl_ref, w_ref, b_ref, o_ref):
        i = pl.program_id(0)
        z = _dot(gl_ref[...].astype(BF16), w_ref[...], 1, 0) + b_ref[...]
        la = _log_sigmoid(z) * (1.0 / GLA_TAU)
        o_ref[...] = jnp.where(_rows(i, tr) >= N_PAD, la, 0.0)

    return pl.pallas_call(
        body, name=name, grid=(L // tr,),
        in_specs=[pl.BlockSpec((tr, LANE), lambda i: (i, gl_blk)),
                  pl.BlockSpec((LANE, DK), lambda i: (0, 0)), pl.BlockSpec((1, DK), lambda i: (0, 0))],
        out_specs=pl.BlockSpec((tr, DK), lambda i: (i, 0)),
        out_shape=jax.ShapeDtypeStruct((L, DK), F32), compiler_params=_cp("parallel"),
    )(P, w2p, b2)


def _gla_gate_bwd(dla, P, w2p, b2, gl_blk, *, name):
    L = P.shape[0]
    DK = w2p.shape[1]
    tr = _tile(L, 640, 16)

    def body(dla_ref, gl_ref, w_ref, b_ref, dgl_ref, dw_ref, db_ref):
        i = pl.program_id(0)
        glb = gl_ref[...].astype(BF16)
        z = _dot(glb, w_ref[...], 1, 0) + b_ref[...]
        dz = jnp.where(_rows(i, tr) >= N_PAD, dla_ref[...], 0.0) * (1.0 / GLA_TAU) * _sigmoid(-z)
        dzb = dz.astype(BF16)
        dgl_ref[...] = _dot(dzb, w_ref[...], 1, 1).astype(BF16)
        pw = _dot(glb, dzb, 0, 0)
        pb = jnp.sum(dz, axis=0, keepdims=True)

        @pl.when(i == 0)
        def _():
            dw_ref[...] = pw
            db_ref[...] = pb

        @pl.when(i > 0)
        def _():
            dw_ref[...] += pw
            db_ref[...] += pb

    return pl.pallas_call(
        body, name=name, grid=(L // tr,),
        in_specs=[pl.BlockSpec((tr, DK), lambda i: (i, 0)), pl.BlockSpec((tr, LANE), lambda i: (i, gl_blk)),
                  pl.BlockSpec((LANE, DK), lambda i: (0, 0)), pl.BlockSpec((1, DK), lambda i: (0, 0))],
        out_specs=[pl.BlockSpec((tr, LANE), lambda i: (i, 0)), pl.BlockSpec((LANE, DK), lambda i: (0, 0)),
                   pl.BlockSpec((1, DK), lambda i: (0, 0))],
        out_shape=[jax.ShapeDtypeStruct((L, LANE), BF16), jax.ShapeDtypeStruct((LANE, DK), F32),
                   jax.ShapeDtypeStruct((1, DK), F32)],
        compiler_params=_cp("arbitrary"),
    )(dla, P, w2p, b2)


def _chunk_terms(q, k, g, n, scale, HV):
    C = q.shape[0]
    ri = lax.broadcasted_iota(jnp.int32, (C, C), 0)
    ci = lax.broadcasted_iota(jnp.int32, (C, C), 1)
    tri = ri >= ci
    valid = _rows(n, C) >= N_PAD
    km = jnp.where(valid, k, 0.0)
    b = _dot(tri.astype(F32), g, 1, 0, precision=HIGHEST)
    bl_row = jnp.sum(g, axis=0, keepdims=True)
    bl_col = _dot(g, jnp.ones((C, HV), F32), 0, 0, precision=HIGHEST)
    eb = jnp.exp(b)
    enb = jnp.exp(-b)
    qe = q * scale * eb
    ke = km * enb
    ebl_row = jnp.exp(bl_row)
    kl = ke * ebl_row
    return dict(tri=tri, valid=valid, eb=eb, enb=enb, qe=qe, ke=ke, kl=kl, ebl_row=ebl_row,
                ebl_col=jnp.exp(bl_col), ri=ri, ci=ci)


def _gla_chunk_fwd(P, la, H, HK, HV, *, name):
    L = P.shape[0]
    C = GLA_CHUNK
    N = L // C
    scale = HK ** -0.5

    G = GLA_GROUP if H % GLA_GROUP == 0 else 1
    HG = H // G

    def body(q_ref, k_ref, v_ref, g_ref, o_ref, s_ref, S):
        n = pl.program_id(1)

        @pl.when(n == 0)
        def _():
            S[...] = jnp.zeros_like(S)

        for g in range(G):
            ks, vs = slice(g * HK, (g + 1) * HK), slice(g * HV, (g + 1) * HV)
            S0 = S[g]
            s_ref[g, 0] = S0
            t = _chunk_terms(q_ref[:, ks], k_ref[:, ks], g_ref[:, ks], n, scale, HV)
            vb = v_ref[:, vs].astype(BF16)
            qeb = t["qe"].astype(BF16)
            inter = _dot(qeb, S0.astype(BF16), 1, 0)
            att = jnp.where(t["tri"], _dot(qeb, t["ke"].astype(BF16), 1, 1), 0.0)
            o_ref[:, vs] = inter + _dot(att.astype(BF16), vb, 1, 0)
            S[g] = t["ebl_col"] * S0 + _dot(t["kl"].astype(BF16), vb, 0, 0)

    return pl.pallas_call(
        body, name=name, grid=(HG, N),
        in_specs=[pl.BlockSpec((C, G * HK), lambda h, n: (n, h)), pl.BlockSpec((C, G * HK), lambda h, n: (n, HG + h)),
                  pl.BlockSpec((C, G * HV), lambda h, n: (n, HG + h)), pl.BlockSpec((C, G * HK), lambda h, n: (n, h))],
        out_specs=[pl.BlockSpec((C, G * HV), lambda h, n: (n, h)),
                   pl.BlockSpec((G, 1, HK, HV), lambda h, n: (h, n, 0, 0))],
        out_shape=[jax.ShapeDtypeStruct((L, H * HV), F32), jax.ShapeDtypeStruct((H, N, HK, HV), F32)],
        scratch_shapes=[pltpu.VMEM((G, HK, HV), F32)],
        compiler_params=_cp("parallel", "arbitrary"),
    )(P, P, P, la)


def _gla_chunk_bwd(P, la, S_all, do, H, HK, HV, *, name):
    L = P.shape[0]
    C = GLA_CHUNK
    N = L // C
    scale = HK ** -0.5

    G = 1
    HG = H // G

    def body(q_ref, k_ref, v_ref, g_ref, s_ref, do_ref, dq_ref, dk_ref, dv_ref, dg_ref, dS):
        step = pl.program_id(1)
        n = N - 1 - step

        @pl.when(step == 0)
        def _():
            dS[...] = jnp.zeros_like(dS)

        for g in range(G):
            ks, vs = slice(g * HK, (g + 1) * HK), slice(g * HV, (g + 1) * HV)
            dS1 = dS[g]
            S0 = s_ref[g, 0]
            t = _chunk_terms(q_ref[:, ks], k_ref[:, ks], g_ref[:, ks], n, scale, HV)
            tri, qe, ke, kl = t["tri"], t["qe"], t["ke"], t["kl"]
            vb = v_ref[:, vs].astype(BF16)
            dob = do_ref[:, vs].astype(BF16)
            qeb, keb, dSb = qe.astype(BF16), ke.astype(BF16), dS1.astype(BF16)
            dA = jnp.where(tri, _dot(dob, vb, 1, 1), 0.0).astype(BF16)
            A = jnp.where(tri, _dot(qeb, keb, 1, 1), 0.0).astype(BF16)
            dqe = _dot(dob, S0.astype(BF16), 1, 1) + _dot(dA, keb, 1, 0)
            dkl = _dot(vb, dSb, 1, 1)
            dke = _dot(dA, qeb, 0, 0) + dkl * t["ebl_row"]
            dv_ref[:, vs] = (_dot(A, dob, 0, 0) + _dot(kl.astype(BF16), dSb, 1, 0)).astype(BF16)
            debl = (jnp.sum(_dot(jnp.ones((8, HV), F32), dS1 * S0, 1, 1, precision=HIGHEST), axis=0, keepdims=True) * 0.125
                    + jnp.sum(dkl * ke, axis=0, keepdims=True))
            dbl = debl * t["ebl_row"]
            db = dqe * qe - dke * ke + jnp.where(lax.broadcasted_iota(jnp.int32, (C, 1), 0) == C - 1, dbl, 0.0)
            triu = (t["ci"] >= t["ri"]).astype(F32)
            dq_ref[:, ks] = (dqe * t["eb"] * scale).astype(BF16)
            dk_ref[:, ks] = jnp.where(t["valid"], dke * t["enb"], 0.0).astype(BF16)
            dg_ref[:, ks] = _dot(triu, db, 1, 0, precision=HIGHEST)
            dS[g] = t["ebl_col"] * dS1 + _dot(qeb, dob, 0, 0)

    rev = lambda h, s: (N - 1 - s, h)
    return pl.pallas_call(
        body, name=name, grid=(HG, N),
        in_specs=[pl.BlockSpec((C, G * HK), rev), pl.BlockSpec((C, G * HK), lambda h, s: (N - 1 - s, HG + h)),
                  pl.BlockSpec((C, G * HV), lambda h, s: (N - 1 - s, HG + h)), pl.BlockSpec((C, G * HK), rev),
                  pl.BlockSpec((G, 1, HK, HV), lambda h, s: (h, N - 1 - s, 0, 0)), pl.BlockSpec((C, G * HV), rev)],
        out_specs=[pl.BlockSpec((C, G * HK), rev), pl.BlockSpec((C, G * HK), rev),
                   pl.BlockSpec((C, G * HV), rev), pl.BlockSpec((C, G * HK), rev)],
        out_shape=[jax.ShapeDtypeStruct((L, H * HK), BF16), jax.ShapeDtypeStruct((L, H * HK), BF16),
                   jax.ShapeDtypeStruct((L, H * HV), BF16), jax.ShapeDtypeStruct((L, H * HK), F32)],
        scratch_shapes=[pltpu.VMEM((G, HK, HV), F32)],
        compiler_params=_cp("parallel", "arbitrary"),
    )(P, P, P, la, S_all, do)


def _silu_parts(x):
    s = _sigmoid(x)
    return x * s, s * (1.0 + x * (1.0 - s))


def _gla_post_fwd(o, P, ng, H, HV, r_blk, *, name):
    L, DV = o.shape
    tr = _tile(L, 320, 16)

    def body(o_ref, r_ref, g_ref, out_ref):
        for hd in range(H):
            sl = slice(hd * HV, (hd + 1) * HV)
            oh = o_ref[:, sl]
            rr = lax.rsqrt(jnp.mean(oh * oh, axis=-1, keepdims=True) + LN_EPS)
            silu, _ = _silu_parts(r_ref[:, sl])
            out_ref[:, sl] = (oh * rr * g_ref[...] * silu).astype(BF16)

    return pl.pallas_call(
        body, name=name, grid=(L // tr,),
        in_specs=[pl.BlockSpec((tr, DV), lambda i: (i, 0)), pl.BlockSpec((tr, DV), lambda i: (i, r_blk)),
                  pl.BlockSpec((1, HV), lambda i: (0, 0))],
        out_specs=pl.BlockSpec((tr, DV), lambda i: (i, 0)),
        out_shape=jax.ShapeDtypeStruct((L, DV), BF16), compiler_params=_cp("parallel"),
    )(o, P, ng)


def _gla_post_bwd(dgated, o, P, ng, H, HV, r_blk, *, name):
    L, DV = o.shape
    tr = _tile(L, 320, 16)

    def body(d_ref, o_ref, r_ref, g_ref, do_ref, dr_ref, dng_ref):
        i = pl.program_id(0)
        png = jnp.zeros((1, HV), F32)
        for hd in range(H):
            sl = slice(hd * HV, (hd + 1) * HV)
            oh = o_ref[:, sl]
            d = d_ref[:, sl]
            rr = lax.rsqrt(jnp.mean(oh * oh, axis=-1, keepdims=True) + LN_EPS)
            yh = oh * rr
            silu, dsilu = _silu_parts(r_ref[:, sl])
            dn = d * silu
            dr_ref[:, sl] = (d * yh * g_ref[...] * dsilu).astype(BF16)
            png = png + jnp.sum(dn * yh, axis=0, keepdims=True)
            dyh = dn * g_ref[...]
            do_ref[:, sl] = rr * (dyh - yh * jnp.mean(dyh * yh, axis=-1, keepdims=True))

        @pl.when(i == 0)
        def _():
            dng_ref[...] = png

        @pl.when(i > 0)
        def _():
            dng_ref[...] += png

    row = pl.BlockSpec((tr, DV), lambda i: (i, 0))
    return pl.pallas_call(
        body, name=name, grid=(L // tr,),
        in_specs=[row, row, pl.BlockSpec((tr, DV), lambda i: (i, r_blk)), pl.BlockSpec((1, HV), lambda i: (0, 0))],
        out_specs=[row, row, pl.BlockSpec((1, HV), lambda i: (0, 0))],
        out_shape=[jax.ShapeDtypeStruct((L, DV), F32), jax.ShapeDtypeStruct((L, DV), BF16),
                   jax.ShapeDtypeStruct((1, HV), F32)],
        compiler_params=_cp("arbitrary"),
    )(dgated, o, P, ng)


def _shift_down(x, halo, s):
    if s == 0:
        return x
    tr = x.shape[0]
    xx = jnp.concatenate([halo, x], axis=0)
    return pltpu.roll(xx, s, axis=0)[8:8 + tr]


def _shift_up(x, halo, s):
    if s == 0:
        return x
    tr = x.shape[0]
    xx = jnp.concatenate([x, halo], axis=0)
    return pltpu.roll(xx, tr + 8 - s, axis=0)[0:tr]


def _conv_taps(x_ref, halo_ref, i, tr):
    x = jnp.where(_rows(i, tr) >= N_PAD, x_ref[...], 0.0)
    halo = jnp.where(i * tr - 8 + lax.broadcasted_iota(jnp.int32, (8, 1), 0) >= N_PAD, halo_ref[...], 0.0)
    return [_shift_down(x, halo, s) for s in range(3)]


def _conv_apply(taps, w_ref, b_ref):
    return taps[2] * w_ref[0:1, :] + taps[1] * w_ref[1:2, :] + taps[0] * w_ref[2:3, :] + b_ref[...]


def _conv_specs(tr, tc):
    blk = pl.BlockSpec((tr, tc), lambda j, i: (i, j))
    halo = pl.BlockSpec((8, tc), lambda j, i: (jnp.maximum(i * (tr // 8) - 1, 0), j))
    w = pl.BlockSpec((3, tc), lambda j, i: (0, j))
    b = pl.BlockSpec((1, tc), lambda j, i: (0, j))
    return blk, halo, w, b


def _conv_act_fwd(Uu, Ug, wu, wg, bu, bg, *, name):
    L, DFF = Uu.shape
    tr = _tile(L, 320, 16)
    tc = _tile(DFF, 512, LANE)

    def body(xu_ref, hu_ref, xg_ref, hg_ref, wu_ref, wg_ref, bu_ref, bg_ref, o_ref):
        i = pl.program_id(1)
        u = _conv_apply(_conv_taps(xu_ref, hu_ref, i, tr), wu_ref, bu_ref)
        g = _conv_apply(_conv_taps(xg_ref, hg_ref, i, tr), wg_ref, bg_ref)
        o_ref[...] = (_silu_parts(g)[0] * u).astype(BF16)

    blk, halo, w, b = _conv_specs(tr, tc)
    return pl.pallas_call(
        body, name=name, grid=(DFF // tc, L // tr),
        in_specs=[blk, halo, blk, halo, w, w, b, b], out_specs=blk,
        out_shape=jax.ShapeDtypeStruct((L, DFF), BF16), compiler_params=_cp("parallel", "parallel"),
    )(Uu, Uu, Ug, Ug, wu, wg, bu, bg)


def _conv_act_bwd(Uu, Ug, wu, wg, bu, bg, dA, *, name):
    L, DFF = Uu.shape
    tr = _tile(L, 320, 16)
    tc = _tile(DFF, 512, LANE)

    def body(xu_ref, hu_ref, xg_ref, hg_ref, wu_ref, wg_ref, bu_ref, bg_ref, da_ref,
             du_ref, dg_ref, dwu_ref, dwg_ref, dbu_ref, dbg_ref):
        i = pl.program_id(1)
        tu = _conv_taps(xu_ref, hu_ref, i, tr)
        tg = _conv_taps(xg_ref, hg_ref, i, tr)
        u = _conv_apply(tu, wu_ref, bu_ref)
        g = _conv_apply(tg, wg_ref, bg_ref)
        silu, dsilu = _silu_parts(g)
        da = da_ref[...]
        du = da * silu
        dg = da * u * dsilu
        du_ref[...] = du
        dg_ref[...] = dg

        @pl.when(i == 0)
        def _():
            dwu_ref[...] = jnp.zeros_like(dwu_ref)
            dwg_ref[...] = jnp.zeros_like(dwg_ref)
            dbu_ref[...] = jnp.zeros_like(dbu_ref)
            dbg_ref[...] = jnp.zeros_like(dbg_ref)

        for j in range(3):
            dwu_ref[j:j + 1, :] += jnp.sum(du * tu[2 - j], axis=0, keepdims=True)
            dwg_ref[j:j + 1, :] += jnp.sum(dg * tg[2 - j], axis=0, keepdims=True)
        dbu_ref[...] += jnp.sum(du, axis=0, keepdims=True)
        dbg_ref[...] += jnp.sum(dg, axis=0, keepdims=True)

    blk, halo, w, b = _conv_specs(tr, tc)
    return pl.pallas_call(
        body, name=name, grid=(DFF // tc, L // tr),
        in_specs=[blk, halo, blk, halo, w, w, b, b, blk], out_specs=[blk, blk, w, w, b, b],
        out_shape=[jax.ShapeDtypeStruct((L, DFF), F32), jax.ShapeDtypeStruct((L, DFF), F32),
                   jax.ShapeDtypeStruct((3, DFF), F32), jax.ShapeDtypeStruct((3, DFF), F32),
                   jax.ShapeDtypeStruct((1, DFF), F32), jax.ShapeDtypeStruct((1, DFF), F32)],
        compiler_params=_cp("parallel", "arbitrary"),
    )(Uu, Uu, Ug, Ug, wu, wg, bu, bg, dA)


def _conv_in_bwd(dh, w, *, name):
    L, DFF = dh.shape
    tr = _tile(L, 320, 16)
    tc = _tile(DFF, 512, LANE)
    nb8 = L // 8

    def body(x_ref, halo_ref, w_ref, o_ref):
        i = pl.program_id(1)
        x = x_ref[...]
        halo = jnp.where((i + 1) * tr + lax.broadcasted_iota(jnp.int32, (8, 1), 0) < L, halo_ref[...], 0.0)
        d = (x * w_ref[2:3, :] + _shift_up(x, halo, 1) * w_ref[1:2, :] + _shift_up(x, halo, 2) * w_ref[0:1, :])
        o_ref[...] = jnp.where(_rows(i, tr) >= N_PAD, d, 0.0).astype(BF16)

    blk = pl.BlockSpec((tr, tc), lambda j, i: (i, j))
    halo = pl.BlockSpec((8, tc), lambda j, i: (jnp.minimum((i + 1) * (tr // 8), nb8 - 1), j))
    return pl.pallas_call(
        body, name=name, grid=(DFF // tc, L // tr),
        in_specs=[blk, halo, pl.BlockSpec((3, tc), lambda j, i: (0, j))], out_specs=blk,
        out_shape=jax.ShapeDtypeStruct((L, DFF), BF16), compiler_params=_cp("parallel", "parallel"),
    )(dh, dh, w)


def _fox_c_fwd(f, bf, *, name):
    L = f.shape[0]
    tr = _tile(L, 320, 16)

    def body(f_ref, b_ref, c_ref, carry):
        i = pl.program_id(0)

        @pl.when(i == 0)
        def _():
            carry[...] = jnp.zeros_like(carry)

        lf = jnp.where(_rows(i, tr) >= N_PAD, _log_sigmoid(f_ref[...] + b_ref[...]), 0.0)
        tri = (lax.broadcasted_iota(jnp.int32, (tr, tr), 0) >= lax.broadcasted_iota(jnp.int32, (tr, tr), 1)).astype(F32)
        c_ref[...] = _dot(tri, lf, 1, 0, precision=HIGHEST) + carry[...]
        carry[...] += jnp.sum(lf, axis=0, keepdims=True)

    return pl.pallas_call(
        body, name=name, grid=(L // tr,),
        in_specs=[pl.BlockSpec((tr, LANE), lambda i: (i, 0)), pl.BlockSpec((1, LANE), lambda i: (0, 0))],
        out_specs=pl.BlockSpec((tr, LANE), lambda i: (i, 0)),
        out_shape=jax.ShapeDtypeStruct((L, LANE), F32), scratch_shapes=[pltpu.VMEM((1, LANE), F32)],
        compiler_params=_cp("arbitrary"),
    )(f, bf)


def _fox_c_bwd(dcs, f, bf, *, name):
    L = f.shape[0]
    tr = _tile(L, 320, 16)
    nb = L // tr
    nd = len(dcs)

    def body(*refs):
        f_ref, b_ref, df_ref, db_ref, carry = refs[nd:]
        s = pl.program_id(0)
        i = nb - 1 - s

        @pl.when(s == 0)
        def _():
            carry[...] = jnp.zeros_like(carry)
            db_ref[...] = jnp.zeros_like(db_ref)

        dc = refs[0][...]
        for r in refs[1:nd]:
            dc = dc + r[...]
        triu = (lax.broadcasted_iota(jnp.int32, (tr, tr), 1) >= lax.broadcasted_iota(jnp.int32, (tr, tr), 0)).astype(F32)
        dlf = _dot(triu, dc, 1, 0, precision=HIGHEST) + carry[...]
        carry[...] += jnp.sum(dc, axis=0, keepdims=True)
        df = jnp.where(_rows(i, tr) >= N_PAD, dlf, 0.0) * _sigmoid(-(f_ref[...] + b_ref[...]))
        df_ref[...] = df.astype(BF16)
        db_ref[...] += jnp.sum(df, axis=0, keepdims=True)

    rev = pl.BlockSpec((tr, LANE), lambda s: (nb - 1 - s, 0))
    vec = pl.BlockSpec((1, LANE), lambda s: (0, 0))
    return pl.pallas_call(
        body, name=name, grid=(nb,), in_specs=[rev] * (nd + 1) + [vec], out_specs=[rev, vec],
        out_shape=[jax.ShapeDtypeStruct((L, LANE), BF16), jax.ShapeDtypeStruct((1, LANE), F32)],
        scratch_shapes=[pltpu.VMEM((1, LANE), F32)], compiler_params=_cp("arbitrary"),
    )(*dcs, f, bf)


AUG = 2 * FOX_HD
FOX_GROUP = 4
FOX_ROW_SPLIT = 5


def _split3(x):
    hi = x.astype(BF16).astype(F32)
    r = x - hi
    mid = r.astype(BF16).astype(F32)
    return hi, mid, (r - mid).astype(BF16).astype(F32)


def _aug_lanes(n, vals):
    lane = lax.broadcasted_iota(jnp.int32, (n, FOX_HD), 1)
    out = jnp.zeros((n, FOX_HD), F32)
    for j, v in enumerate(vals):
        out = jnp.where(lane == j, v, out)
    return out


def _lane_col(x, j):
    lane = lax.broadcasted_iota(jnp.int32, x.shape, 1)
    return jnp.sum(jnp.where(lane == j, x, 0.0), axis=-1, keepdims=True)


def _fox_prep_q(QO, c, H, *, name):
    L = QO.shape[0]
    hd = FOX_HD
    tr = _tile(L, 320, 16)

    def body(q_ref, c_ref, o_ref):
        c = c_ref[...]
        for h in range(H):
            hi, mid, lo = _split3(_lane_col(c, h))
            o_ref[:, h * AUG:h * AUG + hd] = (q_ref[:, h * hd:(h + 1) * hd] * (hd ** -0.5)).astype(BF16)
            o_ref[:, h * AUG + hd:(h + 1) * AUG] = _aug_lanes(tr, [hi, mid, lo, 1.0, 1.0, 1.0]).astype(BF16)

    return pl.pallas_call(
        body, name=name, grid=(L // tr,),
        in_specs=[pl.BlockSpec((tr, H * hd), lambda i: (i, 0)), pl.BlockSpec((tr, LANE), lambda i: (i, 0))],
        out_specs=pl.BlockSpec((tr, H * AUG), lambda i: (i, 0)),
        out_shape=jax.ShapeDtypeStruct((L, H * AUG), BF16), compiler_params=_cp("parallel"),
    )(QO, c)


def _fox_prep_kv(KV, c, H, *, name):
    L = KV.shape[0]
    hd = FOX_HD
    tr = _tile(L, 320, 16)

    def body(k_ref, v_ref, c_ref, ko_ref, vo_ref):
        i = pl.program_id(0)
        c = c_ref[...]
        pad = _rows(i, tr) < N_PAD
        for h in range(H):
            hi, mid, lo = _split3(_lane_col(c, h))
            aug = _aug_lanes(tr, [1.0, 1.0, 1.0, jnp.where(pad, NEG, -hi), jnp.where(pad, 0.0, -mid),
                                  jnp.where(pad, 0.0, -lo)])
            ko_ref[:, h * AUG:h * AUG + hd] = k_ref[:, h * hd:(h + 1) * hd]
            ko_ref[:, h * AUG + hd:(h + 1) * AUG] = aug.astype(BF16)
            vo_ref[:, h * AUG:h * AUG + hd] = v_ref[:, h * hd:(h + 1) * hd]
            vo_ref[:, h * AUG + hd:(h + 1) * AUG] = jnp.ones((tr, hd), BF16)

    wide = pl.BlockSpec((tr, H * AUG), lambda i: (i, 0))
    return pl.pallas_call(
        body, name=name, grid=(L // tr,),
        in_specs=[pl.BlockSpec((tr, H * hd), lambda i: (i, 0)), pl.BlockSpec((tr, H * hd), lambda i: (i, 1)),
                  pl.BlockSpec((tr, LANE), lambda i: (i, 0))],
        out_specs=[wide, wide],
        out_shape=[jax.ShapeDtypeStruct((L, H * AUG), BF16)] * 2, compiler_params=_cp("parallel"),
    )(KV, KV, c)


def _fox_mask(qi, kj, t):
    ti = qi * t + lax.broadcasted_iota(jnp.int32, (t, t), 0)
    si = kj * t + lax.broadcasted_iota(jnp.int32, (t, t), 1)
    return (si <= ti) & ((si >= N_PAD) | (si == ti))


def _fox_attn_fwd(QA, KT, VA, H, *, name):
    L = QA.shape[0]
    hd = FOX_HD
    t = _tile(L, 640, LANE)
    nb = L // t
    G = FOX_GROUP if H % FOX_GROUP == 0 else 1
    nr = FOX_ROW_SPLIT if t % (8 * FOX_ROW_SPLIT) == 0 else 1
    tr = t // nr

    def body(q_ref, k_ref, v_ref, o_ref, lse_ref, m_s, acc):
        qi, kj = pl.program_id(1), pl.program_id(2)

        @pl.when(kj == 0)
        def _():
            m_s[...] = jnp.full_like(m_s, NEG)
            acc[...] = jnp.zeros_like(acc)

        def step(masked):
            for g in range(G):
                cs = slice(g * AUG, (g + 1) * AUG)
                for r in range(nr):
                    rows = slice(r * tr, (r + 1) * tr)
                    s = _dot(q_ref[rows, cs], k_ref[cs, :], 1, 0)
                    if masked:
                        ti = qi * t + r * tr + lax.broadcasted_iota(jnp.int32, (tr, t), 0)
                        si = kj * t + lax.broadcasted_iota(jnp.int32, (tr, t), 1)
                        mask = (si <= ti) & ((si >= N_PAD) | (si == ti))
                        s = jnp.where(mask, s, NEG)
                    m_old = m_s[g, rows]
                    m_new = jnp.maximum(m_old, jnp.max(s, axis=-1, keepdims=True))
                    p = jnp.exp(s - m_new)
                    if masked:
                        p = jnp.where(mask, p, 0.0)
                    acc[g, rows] = jnp.exp(m_old - m_new) * acc[g, rows] + _dot(p.astype(BF16), v_ref[:, cs], 1, 0)
                    m_s[g, rows] = m_new

        @pl.when(kj < qi)
        def _():
            step(False)

        @pl.when(kj == qi)
        def _():
            step(True)

        @pl.when(kj == nb - 1)
        def _():
            for g in range(G):
                a = acc[g]
                l = a[:, hd:]
                o_ref[:, g * hd:(g + 1) * hd] = a[:, :hd] / l
                lse_ref[g] = m_s[g] + jnp.log(jnp.max(l, axis=-1, keepdims=True))

    return pl.pallas_call(
        body, name=name, grid=(H // G, nb, nb),
        in_specs=[pl.BlockSpec((t, G * AUG), lambda h, qi, kj: (qi, h)),
                  pl.BlockSpec((G * AUG, t), lambda h, qi, kj: (h, jnp.minimum(kj, qi))),
                  pl.BlockSpec((t, G * AUG), lambda h, qi, kj: (jnp.minimum(kj, qi), h))],
        out_specs=[pl.BlockSpec((t, G * hd), lambda h, qi, kj: (qi, h)),
                   pl.BlockSpec((G, t, 1), lambda h, qi, kj: (h, qi, 0))],
        out_shape=[jax.ShapeDtypeStruct((L, H * hd), F32), jax.ShapeDtypeStruct((H, L, 1), F32)],
        scratch_shapes=[pltpu.VMEM((G, t, 1), F32), pltpu.VMEM((G, t, AUG), F32)],
        compiler_params=_cp("parallel", "parallel", "arbitrary"),
    )(QA, KT, VA)


def _fox_attn_bwd(QA, KA, VA, DOA, lse, init, H, *, name):
    L = QA.shape[0]
    t = _tile(L, 640, LANE)
    nb = L // t

    def body(q_ref, k_ref, v_ref, do_ref, lse_ref, dk0_ref, dv0_ref, dq_ref, dk_ref, dv_ref):
        kj, qi = pl.program_id(1), pl.program_id(2)

        @pl.when((kj == 0) & (qi == 0))
        def _():
            dq_ref[...] = jnp.zeros_like(dq_ref)

        @pl.when(qi == 0)
        def _():
            dk_ref[...] = dk0_ref[...]
            dv_ref[...] = dv0_ref[...]

        def step(masked):
            q, k, doa = q_ref[...], k_ref[...], do_ref[...]
            p = jnp.exp(_dot(q, k, 1, 1) - lse_ref[0])
            if masked:
                p = jnp.where(_fox_mask(qi, kj, t), p, 0.0)
            pb = p.astype(BF16)
            ds = (p * _dot(doa, v_ref[...], 1, 1)).astype(BF16)
            dv_ref[...] += _dot(pb, doa, 0, 0)
            dk_ref[...] += _dot(ds, q, 0, 0)
            rows = pl.ds(pl.multiple_of(qi * t, t), t)
            dq_ref[rows, :] += _dot(ds, k, 1, 0)

        @pl.when(qi > kj)
        def _():
            step(False)

        @pl.when(qi == kj)
        def _():
            step(True)

    qb = pl.BlockSpec((t, AUG), lambda h, kj, qi: (jnp.maximum(qi, kj), h))
    kb = pl.BlockSpec((t, AUG), lambda h, kj, qi: (kj, h))
    return pl.pallas_call(
        body, name=name, grid=(H, nb, nb),
        in_specs=[qb, kb, kb, qb, pl.BlockSpec((1, t, 1), lambda h, kj, qi: (h, jnp.maximum(qi, kj), 0)), kb, kb],
        out_specs=[pl.BlockSpec((L, AUG), lambda h, kj, qi: (0, h)), kb, kb],
        out_shape=[jax.ShapeDtypeStruct((L, H * AUG), F32)] * 3,
        compiler_params=_cp("parallel", "arbitrary", "arbitrary"),
    )(QA, KA, VA, DOA, lse, *init)


def _fox_post_q(DQA, H, *, name):
    L = DQA.shape[0]
    hd = FOX_HD
    tr = _tile(L, 320, 16)

    def body(x_ref, dq_ref, dc_ref):
        lane = lax.broadcasted_iota(jnp.int32, (tr, LANE), 1)
        dc = jnp.zeros((tr, LANE), F32)
        for h in range(H):
            dq_ref[:, h * hd:(h + 1) * hd] = (x_ref[:, h * AUG:h * AUG + hd] * (hd ** -0.5)).astype(BF16)
            dc = jnp.where(lane == h, _lane_col(x_ref[:, h * AUG + hd:(h + 1) * AUG], 0), dc)
        dc_ref[...] = dc

    return pl.pallas_call(
        body, name=name, grid=(L // tr,), in_specs=[pl.BlockSpec((tr, H * AUG), lambda i: (i, 0))],
        out_specs=[pl.BlockSpec((tr, H * hd), lambda i: (i, 0)), pl.BlockSpec((tr, LANE), lambda i: (i, 0))],
        out_shape=[jax.ShapeDtypeStruct((L, H * hd), BF16), jax.ShapeDtypeStruct((L, LANE), F32)],
        compiler_params=_cp("parallel"),
    )(DQA)


def _fox_post_kv(DKA, DVA, H, *, name):
    L = DKA.shape[0]
    hd = FOX_HD
    tr = _tile(L, 320, 16)

    def body(k_ref, v_ref, o_ref, dc_ref):
        lane = lax.broadcasted_iota(jnp.int32, (tr, LANE), 1)
        dc = jnp.zeros((tr, LANE), F32)
        for h in range(H):
            o_ref[:, h * hd:(h + 1) * hd] = k_ref[:, h * AUG:h * AUG + hd].astype(BF16)
            o_ref[:, (H + h) * hd:(H + h + 1) * hd] = v_ref[:, h * AUG:h * AUG + hd].astype(BF16)
            dc = jnp.where(lane == h, -_lane_col(k_ref[:, h * AUG + hd:(h + 1) * AUG], 3), dc)
        dc_ref[...] = dc

    wide = pl.BlockSpec((tr, H * AUG), lambda i: (i, 0))
    return pl.pallas_call(
        body, name=name, grid=(L // tr,), in_specs=[wide, wide],
        out_specs=[pl.BlockSpec((tr, 2 * H * hd), lambda i: (i, 0)), pl.BlockSpec((tr, LANE), lambda i: (i, 0))],
        out_shape=[jax.ShapeDtypeStruct((L, 2 * H * hd), BF16), jax.ShapeDtypeStruct((L, LANE), F32)],
        compiler_params=_cp("parallel"),
    )(DKA, DVA)


def _fox_gate_fwd(o, QO, *, name):
    L, D = o.shape
    tr = _tile(L, 320, 16)

    def body(o_ref, g_ref, out_ref):
        out_ref[...] = (o_ref[...] * _sigmoid(g_ref[...])).astype(BF16)

    row = pl.BlockSpec((tr, D), lambda i: (i, 0))
    return pl.pallas_call(
        body, name=name, grid=(L // tr,), in_specs=[row, pl.BlockSpec((tr, D), lambda i: (i, 1))], out_specs=row,
        out_shape=jax.ShapeDtypeStruct((L, D), BF16), compiler_params=_cp("parallel"),
    )(o, QO)


def _fox_gate_bwd(d, o, QO, H, *, name):
    L, D = o.shape
    hd = FOX_HD
    tr = _tile(L, 320, 16)

    def body(d_ref, o_ref, g_ref, do_ref, dg_ref):
        for h in range(H):
            sl = slice(h * hd, (h + 1) * hd)
            s = _sigmoid(g_ref[:, sl])
            d = d_ref[:, sl]
            o = o_ref[:, sl]
            do = d * s
            dg_ref[:, sl] = (d * o * s * (1.0 - s)).astype(BF16)
            hi, mid, lo = _split3(-jnp.sum(do * o, axis=-1, keepdims=True))
            do_ref[:, h * AUG:h * AUG + hd] = do.astype(BF16)
            do_ref[:, h * AUG + hd:(h + 1) * AUG] = _aug_lanes(tr, [hi, mid, lo]).astype(BF16)

    row = pl.BlockSpec((tr, D), lambda i: (i, 0))
    return pl.pallas_call(
        body, name=name, grid=(L // tr,), in_specs=[row, row, pl.BlockSpec((tr, D), lambda i: (i, 1))],
        out_specs=[pl.BlockSpec((tr, H * AUG), lambda i: (i, 0)), row],
        out_shape=[jax.ShapeDtypeStruct((L, H * AUG), BF16), jax.ShapeDtypeStruct((L, D), BF16)],
        compiler_params=_cp("parallel"),
    )(d, o, QO)


def _row_tile(R, C, n_arrays):
    budget = VMEM_LIMIT // (3 * n_arrays * 4 * max(C, LANE))
    return _tile(R, max(16, budget // 16 * 16), 16)


def _sum_parts(parts, *, name, out_dtype=F32):
    R, C = parts[0].shape
    tr = _row_tile(R, C, len(parts) + 1)

    def body(*refs):
        acc = refs[0][...].astype(F32)
        for r in refs[1:-1]:
            acc = acc + r[...].astype(F32)
        refs[-1][...] = acc.astype(out_dtype)

    blk = pl.BlockSpec((tr, C), lambda i: (i, 0))
    return pl.pallas_call(
        body, name=name, grid=(R // tr,), in_specs=[blk] * len(parts), out_specs=blk,
        out_shape=jax.ShapeDtypeStruct((R, C), out_dtype), compiler_params=_cp("parallel"),
    )(*parts)


def _sum_slots(x, *, name):
    S, R, C = x.shape
    tr = _row_tile(R, C, S + 1)

    def body(x_ref, o_ref):
        acc = x_ref[0].astype(F32)
        for s in range(1, S):
            acc = acc + x_ref[s].astype(F32)
        o_ref[...] = acc

    return pl.pallas_call(
        body, name=name, grid=(R // tr,), in_specs=[pl.BlockSpec((S, tr, C), lambda i: (0, i, 0))],
        out_specs=pl.BlockSpec((tr, C), lambda i: (i, 0)),
        out_shape=jax.ShapeDtypeStruct((R, C), F32), compiler_params=_cp("parallel"),
    )(x)


def _adamw(w, m, v, gparts, *, name):
    R, C = w.shape
    tr = _row_tile(R, C, 7 + len(gparts))
    ng = len(gparts)

    def body(*refs):
        w_ref, m_ref, v_ref = refs[:3]
        g = refs[3][...]
        for r in refs[4:3 + ng]:
            g = g + r[...]
        g_ref, d_ref, nm_ref, nv_ref = refs[3 + ng:]
        nm = ADAM_B1 * m_ref[...] + (1.0 - ADAM_B1) * g
        nv = ADAM_B2 * v_ref[...] + (1.0 - ADAM_B2) * (g * g)
        m_hat = nm / (1.0 - ADAM_B1 ** ADAM_STEP)
        v_hat = nv / (1.0 - ADAM_B2 ** ADAM_STEP)
        g_ref[...] = g
        d_ref[...] = -ADAM_LR * (m_hat / (jnp.sqrt(v_hat) + ADAM_EPS) + ADAM_WD * w_ref[...])
        nm_ref[...] = nm
        nv_ref[...] = nv

    blk = pl.BlockSpec((tr, C), lambda i: (i, 0))
    return pl.pallas_call(
        body, name=name, grid=(R // tr,), in_specs=[blk] * (3 + ng), out_specs=[blk] * 4,
        out_shape=[jax.ShapeDtypeStruct((R, C), F32)] * 4, compiler_params=_cp("parallel"),
    )(w, m, v, *gparts)


def _chip_peers():
    x, y, c = lax.axis_index("x"), lax.axis_index("y"), lax.axis_index("c")
    return (x, y, c), [(1 - x, y), (x, 1 - y), (1 - x, 1 - y)]


def _gather_chips(arrs, *, name):
    n = len(arrs)
    hs = [a.shape[0] // 2 for a in arrs]

    def body(*refs):
        ins, outs = refs[:n], refs[n:2 * n]
        send, recv = refs[2 * n:]
        (x, y, c), chips = _chip_peers()
        me = 2 * x + y

        def landing(a, chip_idx):
            return outs[a].at[chip_idx, pl.ds(c * hs[a], hs[a])]

        ici, passed = [], []
        for a in range(n):
            for j, (px, py) in enumerate(chips):
                cp = pltpu.make_async_remote_copy(
                    src_ref=ins[a].at[pl.ds(c * hs[a], hs[a])], dst_ref=landing(a, me), send_sem=send.at[a, j],
                    recv_sem=recv.at[a, j], device_id=(px, py, c), device_id_type=MESH)
                cp.start()
                ici.append(cp)
        for a in range(n):
            for j, (px, py) in enumerate(chips):
                ici[3 * a + j].wait_recv()
                src = landing(a, 2 * px + py)
                cp = pltpu.make_async_remote_copy(
                    src_ref=src, dst_ref=src, send_sem=send.at[a, 3 + j], recv_sem=recv.at[a, 3 + j],
                    device_id=(x, y, 1 - c), device_id_type=MESH)
                cp.start()
                passed.append(cp)
        for cp in ici:
            cp.wait_send()
        for cp in passed:
            cp.wait()

    chip = 2 * lax.axis_index("x") + lax.axis_index("y")
    outs = pl.pallas_call(
        body, name=name, in_specs=[ANY] * n, out_specs=[ANY] * n,
        out_shape=[jax.ShapeDtypeStruct((4,) + a.shape, a.dtype) for a in arrs],
        scratch_shapes=[pltpu.SemaphoreType.DMA((n, 6)), pltpu.SemaphoreType.DMA((n, 6))],
    )(*arrs)
    return [lax.dynamic_update_index_in_dim(o, a, chip, 0) for o, a in zip(outs, arrs)]


HBM_SPEC = pl.BlockSpec(memory_space=pltpu.HBM)
SEM_SPEC = pl.BlockSpec(memory_space=pltpu.SEMAPHORE)
DATAFLOW = pltpu.SideEffectType.DATAFLOW_SIDE_EFFECTING


def _late_copies(srcs, lands, send, recv):
    (x, y, c), chips = _chip_peers()
    return [pltpu.make_async_remote_copy(
        src_ref=srcs[a], dst_ref=lands[a].at[2 * x + y], send_sem=send.at[3 * a + j], recv_sem=recv.at[3 * a + j],
        device_id=(px, py, c), device_id_type=MESH) for a in range(len(srcs)) for j, (px, py) in enumerate(chips)]


def _gather_chips_start(arrs, after, *, name):
    n = len(arrs)

    def body(*refs):
        srcs, lands = refs[:n], refs[n:2 * n]
        send, recv = refs[2 * n + 1], refs[2 * n + 2]
        for cp in _late_copies(srcs, lands, send, recv):
            cp.start()
        refs[-1][...] = jnp.zeros_like(refs[-1])

    hbm = lambda a: pltpu.with_memory_space_constraint(a, pltpu.HBM)
    outs = pl.pallas_call(
        body, name=name,
        out_shape=(pltpu.SemaphoreType.DMA((3 * n,)), pltpu.SemaphoreType.DMA((3 * n,)),
                   *[pltpu.HBM(a.shape, a.dtype) for a in arrs], *[pltpu.HBM((4,) + a.shape, a.dtype) for a in arrs],
                   jax.ShapeDtypeStruct((8, LANE), F32)),
        in_specs=[HBM_SPEC] * (2 * n) + [ANY],
        out_specs=(SEM_SPEC, SEM_SPEC, *[HBM_SPEC] * (2 * n), pl.BlockSpec(memory_space=pltpu.VMEM)),
        input_output_aliases={i: 2 + i for i in range(2 * n)},
        compiler_params=pltpu.CompilerParams(has_side_effects=DATAFLOW),
    )(*[hbm(a) for a in arrs], *[hbm(lax.empty((4,) + a.shape, a.dtype)) for a in arrs], after)
    return outs[0], outs[1], list(outs[2:2 + n]), list(outs[2 + n:2 + 2 * n]), outs[-1]


def _gather_chips_wait(started, after, *, name):
    send, recv, srcs, lands, _ = started
    n = len(srcs)

    def body(*refs):
        for cp in _late_copies(refs[:n], refs[n:2 * n], refs[2 * n], refs[2 * n + 1]):
            cp.wait_send()
            cp.wait_recv()

    outs = pl.pallas_call(
        body, name=name,
        out_shape=tuple(pltpu.HBM(a.shape, a.dtype) for a in srcs + lands),
        in_specs=[HBM_SPEC] * (2 * n) + [SEM_SPEC, SEM_SPEC, ANY], out_specs=tuple([HBM_SPEC] * (2 * n)),
        input_output_aliases={i: i for i in range(2 * n)},
        compiler_params=pltpu.CompilerParams(has_side_effects=DATAFLOW),
    )(*srcs, *lands, send, recv, after)
    return list(outs[n:])


def _scatter_chips(arrs, *, name):
    n = len(arrs)

    def body(*refs):
        ins, outs = refs[:n], refs[n:2 * n]
        send, recv = refs[2 * n:]
        (x, y, c), chips = _chip_peers()
        copies = []
        for a in range(n):
            for j, (px, py) in enumerate(chips):
                cp = pltpu.make_async_remote_copy(
                    src_ref=ins[a].at[2 * px + py], dst_ref=outs[a].at[j], send_sem=send.at[a, j],
                    recv_sem=recv.at[a, j], device_id=(px, py, c), device_id_type=MESH)
                cp.start()
                copies.append(cp)
        for cp in copies:
            cp.wait()

    return pl.pallas_call(
        body, name=name, in_specs=[ANY] * n, out_specs=[ANY] * n,
        out_shape=[jax.ShapeDtypeStruct((3,) + a.shape[1:], a.dtype) for a in arrs],
        scratch_shapes=[pltpu.SemaphoreType.DMA((n, 3)), pltpu.SemaphoreType.DMA((n, 3))],
    )(*arrs)


def _send_other_halves(arrs, *, name):
    n = len(arrs)
    hs = [a.shape[1] // 2 for a in arrs]

    def body(*refs):
        ins, outs = refs[:n], refs[n:2 * n]
        send, recv = refs[2 * n:]
        x, y, c = lax.axis_index("x"), lax.axis_index("y"), lax.axis_index("c")
        copies = []
        for a in range(n):
            cp = pltpu.make_async_remote_copy(
                src_ref=ins[a].at[pl.ds(0, 4), pl.ds((1 - c) * hs[a], hs[a])], dst_ref=outs[a], send_sem=send.at[a],
                recv_sem=recv.at[a], device_id=(x, y, 1 - c), device_id_type=MESH)
            cp.start()
            copies.append(cp)
        for cp in copies:
            cp.wait()

    return pl.pallas_call(
        body, name=name, in_specs=[ANY] * n, out_specs=[ANY] * n,
        out_shape=[jax.ShapeDtypeStruct((4, h) + a.shape[2:], a.dtype) for a, h in zip(arrs, hs)],
        scratch_shapes=[pltpu.SemaphoreType.DMA((n,)), pltpu.SemaphoreType.DMA((n,))],
    )(*arrs)


def _join_halves(arrs, *, name):
    n = len(arrs)

    def body(*refs):
        ins, outs = refs[:n], refs[n:2 * n]
        send, recv = refs[2 * n:]
        x, y, c = lax.axis_index("x"), lax.axis_index("y"), lax.axis_index("c")
        copies = []
        for a in range(n):
            h = ins[a].shape[0]
            cp = pltpu.make_async_remote_copy(
                src_ref=ins[a], dst_ref=outs[a].at[pl.ds(c * h, h)], send_sem=send.at[a], recv_sem=recv.at[a],
                device_id=(x, y, 1 - c), device_id_type=MESH)
            cp.start()
            copies.append(cp)
        for cp in copies:
            cp.wait()

    outs = pl.pallas_call(
        body, name=name, in_specs=[ANY] * n, out_specs=[ANY] * n,
        out_shape=[jax.ShapeDtypeStruct((2 * a.shape[0],) + a.shape[1:], a.dtype) for a in arrs],
        scratch_shapes=[pltpu.SemaphoreType.DMA((n,)), pltpu.SemaphoreType.DMA((n,))],
    )(*arrs)
    core = lax.axis_index("c")
    return [lax.dynamic_update_slice_in_dim(o, a, core * a.shape[0], 0) for o, a in zip(outs, arrs)]


def _gather_all(a, *, name):
    def body(in_ref, out_ref, send, recv):
        x, y, c = lax.axis_index("x"), lax.axis_index("y"), lax.axis_index("c")
        me = 4 * x + 2 * y + c
        copies = []
        for j in range(1, 8):
            fx, fy, fc = (j >> 2) & 1, (j >> 1) & 1, j & 1
            peer = (x ^ fx, y ^ fy, c ^ fc)
            cp = pltpu.make_async_remote_copy(
                src_ref=in_ref, dst_ref=out_ref.at[me], send_sem=send.at[j - 1], recv_sem=recv.at[j - 1],
                device_id=peer, device_id_type=MESH)
            cp.start()
            copies.append(cp)
        for cp in copies:
            cp.wait()

    out = pl.pallas_call(
        body, name=name, in_specs=[ANY], out_specs=ANY,
        out_shape=jax.ShapeDtypeStruct((8,) + a.shape, a.dtype),
        scratch_shapes=[pltpu.SemaphoreType.DMA((7,)), pltpu.SemaphoreType.DMA((7,))],
    )(a)
    me = 4 * lax.axis_index("x") + 2 * lax.axis_index("y") + lax.axis_index("c")
    return lax.dynamic_update_index_in_dim(out, a, me, 0)


def _pack(arrs):
    flat = jnp.concatenate([a.astype(F32).reshape(-1) for a in arrs])
    n = flat.shape[0]
    pad = (-n) % (16 * LANE)
    return jnp.pad(flat, (0, pad)).reshape(-1, LANE)


def _unpack(buf, shapes):
    flat = buf.reshape(-1)
    out, off = [], 0
    for s in shapes:
        n = int(np.prod(s))
        out.append(flat[off:off + n].reshape(s))
        off += n
    return out


def _to_shards(g, axis):
    parts = jnp.split(g, 4, axis=axis)
    return jnp.stack([p.reshape(-1, p.shape[-1]) for p in parts])


def kernel(x, meta, ln_g, ln_b, gla_w_in, gla_w_g2, gla_b_g2, gla_norm_g, gla_w_out, kv_w, kv_bf, fox_w_in, fox_w_out, ffn_w_up, ffn_conv_w, ffn_conv_b, ffn_w_down, loss_target, m_meta, m_ln_g, m_ln_b, m_gla_w_in, m_gla_w_g2, m_gla_b_g2, m_gla_norm_g, m_gla_w_out, m_kv_w, m_kv_bf, m_fox_w_in, m_fox_w_out, m_ffn_w_up, m_ffn_conv_w, m_ffn_conv_b, m_ffn_w_down, v_meta, v_ln_g, v_ln_b, v_gla_w_in, v_gla_w_g2, v_gla_b_g2, v_gla_norm_g, v_gla_w_out, v_kv_w, v_kv_bf, v_fox_w_in, v_fox_w_out, v_ffn_w_up, v_ffn_conv_w, v_ffn_conv_b, v_ffn_w_down):
    D = x.shape[-1]
    L = x.shape[1] + FRONT
    HG = GLA_HEADS
    DK, DV = D // 2, D
    HK, HV = DK // HG, DV // HG
    HF = D // FOX_HD
    DFF = ffn_w_down.shape[1] * 4
    chip = 2 * lax.axis_index("x") + lax.axis_index("y")

    big_names = ["gla_w_in", "gla_w_out", "kv_w", "fox_w_in", "fox_w_out", "ffn_w_up", "ffn_w_down"]
    big = dict(gla_w_in=gla_w_in, gla_w_out=gla_w_out, kv_w=kv_w, fox_w_in=fox_w_in, fox_w_out=fox_w_out,
               ffn_w_up=ffn_w_up, ffn_w_down=ffn_w_down)
    big_axis = dict(gla_w_in=2, gla_w_out=1, kv_w=1, fox_w_in=2, fox_w_out=1, ffn_w_up=2, ffn_w_down=1)
    small_names = ["meta", "ln_g", "ln_b", "gla_w_g2", "gla_b_g2", "gla_norm_g", "ffn_conv_w"]
    small = dict(meta=meta, ln_g=ln_g, ln_b=ln_b, gla_w_g2=gla_w_g2, gla_b_g2=gla_b_g2, gla_norm_g=gla_norm_g,
                 ffn_conv_w=ffn_conv_w)
    small_shapes = [small[k].shape for k in small_names]
    bf = lambda a: a.astype(BF16)
    g0 = _gather_chips([bf(gla_w_in[0]), bf(gla_w_out[0]), bf(ffn_w_up[0]), bf(ffn_w_down[0]),
                        _pack([small[k] for k in small_names])], name="gather_weights")
    late_src = [bf(gla_w_in[1]), bf(gla_w_out[1]), bf(kv_w), bf(fox_w_in), bf(fox_w_out), bf(ffn_w_up[1:]),
                bf(ffn_w_down[1:])]
    late = _gather_chips_start(late_src, g0[-1], name="gather_late_start")
    sm_sh = [_unpack(g0[-1][s], small_shapes) for s in range(4)]
    fs = {k: jnp.concatenate([sm_sh[s][i] for s in range(4)], axis=-1) for i, k in enumerate(small_names)}

    pad_cols = lambda w: jnp.pad(w, ((0, 0), (0, LANE - w.shape[1])))
    cat = lambda parts, axis: jnp.concatenate(parts, axis=axis)

    def gla_in_matrix(w):
        return cat([w[:, :2 * DK + DV], w[:, 2 * DK + DV + GLA_RANK:], pad_cols(w[:, 2 * DK + DV:2 * DK + DV + GLA_RANK])], 1)

    def ffn_matrices(up, down):
        return dict(u=cat([up(0), up(1)], 1), g=cat([up(2), up(3)], 1), d=cat([down(s) for s in range(4)], 0))

    WL = [None] * DEPTH
    WL[0] = dict(P=gla_in_matrix(cat([g0[0][s] for s in range(4)], 1)), go=cat([g0[1][s] for s in range(4)], 0),
                 **ffn_matrices(lambda s: g0[2][s], lambda s: g0[3][s]))
    w2p = [jnp.pad(fs["gla_w_g2"][l], ((0, LANE - GLA_RANK), (0, 0))).astype(BF16) for l in range(N_A_LAYERS)]
    bf_pad = jnp.pad(kv_bf, (0, LANE - HF)).reshape(1, LANE)
    cw_u = [fs["ffn_conv_w"][l][:, :DFF] for l in range(DEPTH)]
    cw_g = [fs["ffn_conv_w"][l][:, DFF:] for l in range(DEPTH)]
    cb_u = [ffn_conv_b[l][None, :DFF] for l in range(DEPTH)]
    cb_g = [ffn_conv_b[l][None, DFF:] for l in range(DEPTH)]
    gl_blk = (2 * DK + 2 * DV) // LANE
    r_blk = (2 * DK + DV) // DV

    h = jnp.concatenate([jnp.concatenate([jnp.zeros((N_PAD, D), F32), fs["meta"]], axis=0), x[0]], axis=0)
    hb = (h + late[-1][0, 0]).astype(BF16)
    saved = []
    kvs = None
    for l in range(DEPTH):
        if l == 1:
            lands = _gather_chips_wait(late, h, name="gather_late_wait")
            lands = [lax.dynamic_update_index_in_dim(o, a, chip, 0) for o, a in zip(lands, late_src)]
            Lgi, Lgo, Lkv, Lfi, Lfo, Lup, Ldn = lands
            WL[1] = dict(P=gla_in_matrix(cat([Lgi[s] for s in range(4)], 1)), go=cat([Lgo[s] for s in range(4)], 0),
                         **ffn_matrices(lambda s: Lup[s][0], lambda s: Ldn[s][0]))
            for ll in range(N_A_LAYERS, DEPTH):
                jj = ll - N_A_LAYERS
                WL[ll] = dict(fi=cat([Lfi[s][jj] for s in range(4)], 1), fo=cat([Lfo[s][jj] for s in range(4)], 0),
                              **ffn_matrices(lambda s, ll=ll: Lup[s][ll - 1], lambda s, ll=ll: Ldn[s][ll - 1]))
            kv_full = cat([Lkv[s] for s in range(4)], 1)
            W_kv = kv_full[:, :2 * D]
            W_f = pad_cols(kv_full[:, 2 * D:])
            W_kvf = cat([W_kv, W_f], 1)
        s = dict(h=h, hb=hb)
        if l < N_A_LAYERS:
            s["P"] = _mm(hb, WL[l]["P"], name=f"gla_in_{l}")
            s["la"] = _gla_gate_fwd(s["P"], w2p[l], fs["gla_b_g2"][l][None], gl_blk, name=f"gla_gate_{l}")
            s["o"], s["S"] = _gla_chunk_fwd(s["P"], s["la"], HG, HK, HV, name=f"gla_chunk_{l}")
            s["gated"] = _gla_post_fwd(s["o"], s["P"], fs["gla_norm_g"][l][None], HG, HV, r_blk, name=f"gla_post_{l}")
            s["mix"] = _mm(s["gated"], WL[l]["go"], name=f"gla_out_{l}")
        else:
            j = l - N_A_LAYERS
            if kvs is None:
                KV = _mm(hb, W_kv, out_dtype=BF16, name="kv_proj")
                f = _mm(hb, W_f, name="kv_gate_proj")
                c = _fox_c_fwd(f, bf_pad, name="fox_c")
                KA, VA = _fox_prep_kv(KV, c, HF, name="fox_prep_kv")
                kvs = dict(KA=KA, KT=KA.T, VA=VA, f=f, c=c, hb=hb)
            s["QO"] = _mm(hb, WL[l]["fi"], name=f"fox_in_{j}")
            s["QA"] = _fox_prep_q(s["QO"], kvs["c"], HF, name=f"fox_prep_q_{j}")
            s["o"], s["lse"] = _fox_attn_fwd(s["QA"], kvs["KT"], kvs["VA"], HF, name=f"fox_attn_{j}")
            s["gated"] = _fox_gate_fwd(s["o"], s["QO"], name=f"fox_gate_{j}")
            s["mix"] = _mm(s["gated"], WL[l]["fo"], name=f"fox_out_{j}")
        s["h1"], s["h1b"] = _ln_fwd(h, s["mix"], fs["ln_g"][l, 0][None], fs["ln_b"][l, 0][None], name=f"ln_a_{l}")
        s["Uu"] = _mm(s["h1b"], WL[l]["u"], name=f"ffn_up_u_{l}")
        s["Ug"] = _mm(s["h1b"], WL[l]["g"], name=f"ffn_up_g_{l}")
        s["a"] = _conv_act_fwd(s["Uu"], s["Ug"], cw_u[l], cw_g[l], cb_u[l], cb_g[l], name=f"ffn_conv_{l}")
        s["ffn"] = _mm(s["a"], WL[l]["d"], name=f"ffn_down_{l}")
        h, hb = _ln_fwd(s["h1"], s["ffn"], fs["ln_g"][l, 1][None], fs["ln_b"][l, 1][None], name=f"ln_b_{l}")
        saved.append(s)

    loss_acc, dh = _loss(h, loss_target[0], name="loss")

    gW = {}
    d_ln_g = [[None, None] for _ in range(DEPTH)]
    d_ln_b = [[None, None] for _ in range(DEPTH)]
    d_cw, d_cb = [None] * DEPTH, [None] * DEPTH
    d_wg2, d_bg2, d_ng = [None] * N_A_LAYERS, [None] * N_A_LAYERS, [None] * N_A_LAYERS
    dkv = (jnp.zeros((L, HF * AUG), F32), jnp.zeros((L, HF * AUG), F32))
    dcqs = []
    for l in reversed(range(DEPTH)):
        s = saved[l]
        dz, dzb, d_ln_g[l][1], d_ln_b[l][1] = _ln_bwd(dh, s["h1"], s["ffn"], fs["ln_g"][l, 1][None], name=f"ln_b_bwd_{l}")
        dA = _mm(dzb, WL[l]["d"], tb=True, name=f"ffn_down_dx_{l}")
        gW[("ffn_w_down", l)] = _mm(s["a"], dzb, ta=True, out_dtype=BF16, name=f"ffn_down_dw_{l}")
        dcu, dcg, dwu, dwg, dbu, dbg = _conv_act_bwd(s["Uu"], s["Ug"], cw_u[l], cw_g[l], cb_u[l], cb_g[l], dA,
                                                     name=f"ffn_conv_bwd_{l}")
        d_cw[l] = jnp.concatenate([dwu, dwg], axis=1)
        d_cb[l] = jnp.concatenate([dbu, dbg], axis=1)[0]
        dUu = _conv_in_bwd(dcu, cw_u[l], name=f"ffn_conv_dx_u_{l}")
        dUg = _conv_in_bwd(dcg, cw_g[l], name=f"ffn_conv_dx_g_{l}")
        gW[("ffn_w_up", l)] = jnp.concatenate(
            [_mm(s["h1b"], dUu, ta=True, out_dtype=BF16, name=f"ffn_up_dw_u_{l}"),
             _mm(s["h1b"], dUg, ta=True, out_dtype=BF16, name=f"ffn_up_dw_g_{l}")], axis=1)
        dh1 = _mm(dUu, WL[l]["u"], tb=True, add=dz, add_scale=ALPHA, name=f"ffn_up_dx_u_{l}")
        dh1 = _mm(dUg, WL[l]["g"], tb=True, add=dh1, name=f"ffn_up_dx_g_{l}")
        dz, dzb, d_ln_g[l][0], d_ln_b[l][0] = _ln_bwd(dh1, s["h"], s["mix"], fs["ln_g"][l, 0][None], name=f"ln_a_bwd_{l}")
        if l < N_A_LAYERS:
            dgated = _mm(dzb, WL[l]["go"], tb=True, name=f"gla_out_dx_{l}")
            gW[("gla_w_out", l)] = _mm(s["gated"], dzb, ta=True, out_dtype=BF16, name=f"gla_out_dw_{l}")
            do, drb, d_ng[l] = _gla_post_bwd(dgated, s["o"], s["P"], fs["gla_norm_g"][l][None], HG, HV, r_blk,
                                             name=f"gla_post_bwd_{l}")
            dq, dk, dvb, dla = _gla_chunk_bwd(s["P"], s["la"], s["S"], do, HG, HK, HV, name=f"gla_chunk_bwd_{l}")
            dglb, dw2, d_bg2[l] = _gla_gate_bwd(dla, s["P"], w2p[l], fs["gla_b_g2"][l][None], gl_blk,
                                               name=f"gla_gate_bwd_{l}")
            d_wg2[l] = dw2[:GLA_RANK]
            dP = jnp.concatenate([dq, dk, dvb, drb, dglb], axis=1)
            gP = _mm(s["hb"], dP, ta=True, out_dtype=BF16, name=f"gla_in_dw_{l}")
            gW[("gla_w_in", l)] = jnp.concatenate(
                [gP[:, :2 * DK + DV], gP[:, 2 * DK + 2 * DV:2 * DK + 2 * DV + GLA_RANK], gP[:, 2 * DK + DV:2 * DK + 2 * DV]],
                axis=1)
            dh = _mm(dP, WL[l]["P"], tb=True, add=dz, add_scale=ALPHA, name=f"gla_in_dx_{l}")
        else:
            j = l - N_A_LAYERS
            dgo = _mm(dzb, WL[l]["fo"], tb=True, name=f"fox_out_dx_{j}")
            gW[("fox_w_out", j)] = _mm(s["gated"], dzb, ta=True, out_dtype=BF16, name=f"fox_out_dw_{j}")
            DOA, dogb = _fox_gate_bwd(dgo, s["o"], s["QO"], HF, name=f"fox_gate_bwd_{j}")
            DQA, DKA, DVA = _fox_attn_bwd(s["QA"], kvs["KA"], kvs["VA"], DOA, s["lse"], dkv, HF, name=f"fox_attn_bwd_{j}")
            dkv = (DKA, DVA)
            dqb, dcq = _fox_post_q(DQA, HF, name=f"fox_post_q_{j}")
            dcqs.append(dcq)
            dQO = jnp.concatenate([dqb, dogb], axis=1)
            gW[("fox_w_in", j)] = _mm(s["hb"], dQO, ta=True, out_dtype=BF16, name=f"fox_in_dw_{j}")
            dh = _mm(dQO, WL[l]["fi"], tb=True, add=dz, add_scale=ALPHA, name=f"fox_in_dx_{j}")
            if j == 0:
                dkvb, dck = _fox_post_kv(DKA, DVA, HF, name="fox_post_kv")
                dfb, d_bf = _fox_c_bwd(dcqs + [dck], kvs["f"], bf_pad, name="fox_c_bwd")
                dKVF = jnp.concatenate([dkvb, dfb], axis=1)
                gkv = _mm(kvs["hb"], dKVF, ta=True, out_dtype=BF16, name="kv_dw")
                gW[("kv_w", 0)] = gkv[:, :2 * D + HF]
                dh = _mm(dKVF, W_kvf, tb=True, add=dh, name="kv_dx")

    stack = lambda k, n: jnp.stack([gW[(k, i)] for i in range(n)])
    gfull = dict(gla_w_in=stack("gla_w_in", N_A_LAYERS), gla_w_out=stack("gla_w_out", N_A_LAYERS), kv_w=gW[("kv_w", 0)],
                 fox_w_in=stack("fox_w_in", DEPTH - N_A_LAYERS), fox_w_out=stack("fox_w_out", DEPTH - N_A_LAYERS),
                 ffn_w_up=stack("ffn_w_up", DEPTH), ffn_w_down=stack("ffn_w_down", DEPTH))
    core = lax.axis_index("c")
    g4 = [_to_shards(gfull[k], big_axis[k]) for k in big_names]
    sib = _send_other_halves(g4, name="pair_exchange")
    pair = []
    for i, k in enumerate(big_names):
        _, R, C = g4[i].shape
        mine = lax.dynamic_slice_in_dim(g4[i], core * (R // 2), R // 2, axis=1)
        pair.append(_sum_parts([mine.reshape(2 * R, C), sib[i].reshape(2 * R, C)], out_dtype=BF16,
                               name=f"sum_pair_{k}").reshape(4, R // 2, C))
    recv = _scatter_chips(pair, name="scatter_grads")
    halves = []
    for i, k in enumerate(big_names):
        own = lax.dynamic_index_in_dim(pair[i], chip, axis=0, keepdims=False)
        halves.append(_sum_parts([own, recv[i][0], recv[i][1], recv[i][2]], name=f"sum_chips_{k}"))
    gsum = _join_halves(halves, name="join_halves")
    moments = dict(gla_w_in=(m_gla_w_in, v_gla_w_in), gla_w_out=(m_gla_w_out, v_gla_w_out), kv_w=(m_kv_w, v_kv_w),
                   fox_w_in=(m_fox_w_in, v_fox_w_in), fox_w_out=(m_fox_w_out, v_fox_w_out),
                   ffn_w_up=(m_ffn_w_up, v_ffn_w_up), ffn_w_down=(m_ffn_w_down, v_ffn_w_down))
    res = {}
    for i, k in enumerate(big_names):
        w = big[k]
        sh = w.shape
        flat = lambda a: a.reshape(-1, sh[-1])
        outs = _adamw(flat(w), flat(moments[k][0]), flat(moments[k][1]), [gsum[i]], name=f"adamw_{k}")
        res[k] = [o.reshape(sh) for o in outs]

    dmeta = dh[N_PAD:FRONT]
    sg = dict(meta=dmeta,
              ln_g=jnp.stack([jnp.concatenate(d_ln_g[l], axis=0) for l in range(DEPTH)]),
              ln_b=jnp.stack([jnp.concatenate(d_ln_b[l], axis=0) for l in range(DEPTH)]),
              gla_w_g2=jnp.stack(d_wg2), gla_b_g2=jnp.stack([d[0] for d in d_bg2]),
              gla_norm_g=jnp.stack([d[0] for d in d_ng]), ffn_conv_w=jnp.stack(d_cw),
              kv_bf=d_bf[0, :HF], ffn_conv_b=jnp.stack(d_cb), loss=loss_acc[0, :1])
    sg_names = small_names + ["kv_bf", "ffn_conv_b", "loss"]
    sg_shapes = [sg[k].shape for k in sg_names]
    red = _sum_slots(_gather_all(_pack([sg[k] for k in sg_names]), name="gather_small_grads"), name="sum_small_grads")
    red = dict(zip(sg_names, _unpack(red, sg_shapes)))
    loss = red["loss"][0]
    loc = {}
    for k in small_names:
        wdt = small[k].shape[-1]
        loc[k] = lax.dynamic_slice_in_dim(red[k], chip * wdt, wdt, axis=red[k].ndim - 1)
    loc["kv_bf"] = red["kv_bf"]
    loc["ffn_conv_b"] = red["ffn_conv_b"]
    sm_all = small_names + ["kv_bf", "ffn_conv_b"]
    sw = dict(small, kv_bf=kv_bf, ffn_conv_b=ffn_conv_b)
    sm_m = dict(meta=m_meta, ln_g=m_ln_g, ln_b=m_ln_b, gla_w_g2=m_gla_w_g2, gla_b_g2=m_gla_b_g2, gla_norm_g=m_gla_norm_g,
                ffn_conv_w=m_ffn_conv_w, kv_bf=m_kv_bf, ffn_conv_b=m_ffn_conv_b)
    sm_v = dict(meta=v_meta, ln_g=v_ln_g, ln_b=v_ln_b, gla_w_g2=v_gla_w_g2, gla_b_g2=v_gla_b_g2, gla_norm_g=v_gla_norm_g,
                ffn_conv_w=v_ffn_conv_w, kv_bf=v_kv_bf, ffn_conv_b=v_ffn_conv_b)
    shapes_loc = [sw[k].shape for k in sm_all]
    outs = _adamw(_pack([sw[k] for k in sm_all]), _pack([sm_m[k] for k in sm_all]), _pack([sm_v[k] for k in sm_all]),
                  [_pack([loc[k] for k in sm_all])], name="adamw_small")
    outs = [_unpack(o, shapes_loc) for o in outs]
    for i, k in enumerate(sm_all):
        res[k] = [outs[q][i] for q in range(4)]

    order = ["meta", "ln_g", "ln_b", "gla_w_in", "gla_w_g2", "gla_b_g2", "gla_norm_g", "gla_w_out", "kv_w", "kv_bf",
             "fox_w_in", "fox_w_out", "ffn_w_up", "ffn_conv_w", "ffn_conv_b", "ffn_w_down"]
    grad_x = dh[FRONT:][None]
    return (loss, grad_x, *[res[k][0] for k in order], *[res[k][1] for k in order], *[res[k][2] for k in order],
            *[res[k][3] for k in order])
```

```python
import functools

import numpy as np
import jax
import jax.numpy as jnp
from jax import lax
from jax.experimental import pallas as pl
from jax.experimental.pallas import tpu as pltpu

F32 = jnp.float32
BF16 = jnp.bfloat16
HIGHEST = lax.Precision.HIGHEST

DEPTH = 4
N_A_LAYERS = DEPTH // 2
N_META = 16
FRONT = 128
N_PAD = FRONT - N_META
ALPHA = (2.0 * DEPTH) ** 0.25
LN_EPS = 1e-5
GLA_HEADS = 4
GLA_RANK = 16
GLA_TAU = 16.0
GLA_CHUNK = 64
GLA_GROUP = 2
FOX_HD = 128
LANE = 128
ADAM_LR = 0.001
ADAM_B1 = 0.9
ADAM_B2 = 0.999
ADAM_EPS = 1e-08
ADAM_WD = 0.01
ADAM_STEP = 10
NEG = -(2.0 ** 100)
VMEM_LIMIT = 50 * 1024 * 1024
MESH = pl.DeviceIdType.MESH
ANY = pl.BlockSpec(memory_space=pl.ANY)


def _tile(n, pref, align):
    best = None
    t = align
    while t <= min(n, pref):
        if n % t == 0:
            best = t
        t += align
    return best if best is not None else n


def _cp(*sem):
    return pltpu.CompilerParams(dimension_semantics=sem, vmem_limit_bytes=VMEM_LIMIT)


def _dot(a, b, ca, cb, precision=None):
    return lax.dot_general(a, b, (((ca,), (cb,)), ((), ())), precision=precision,
                           preferred_element_type=F32)


def _sigmoid(x):
    return 1.0 / (1.0 + jnp.exp(-x))


def _log_sigmoid(z):
    return jnp.minimum(z, 0.0) - jnp.log(1.0 + jnp.exp(-jnp.abs(z)))


def _rows(i, tr, n=None):
    n = tr if n is None else n
    return i * tr + lax.broadcasted_iota(jnp.int32, (n, 1), 0)


def _mm(a, b, *, ta=False, tb=False, out_dtype=F32, add=None, add_scale=1.0, name):
    if ta:
        K, M = a.shape
    else:
        M, K = a.shape
    if tb:
        N, K2 = b.shape
    else:
        K2, N = b.shape
    assert K == K2, (a.shape, b.shape, ta, tb)
    tm = _tile(M, 1024, LANE) if ta else _tile(M, 1040, 16)
    tn = _tile(N, 1024, LANE)
    tk = _tile(K, 2048, LANE if ((not ta) or tb) else 16)
    nk = K // tk
    ca = 0 if ta else 1
    cb = 1 if tb else 0

    def body(*refs):
        if add is None:
            a_ref, b_ref, o_ref = refs[:3]
            add_ref = None
        else:
            a_ref, b_ref, add_ref, o_ref = refs[:4]
        part = _dot(a_ref[...].astype(BF16), b_ref[...].astype(BF16), ca, cb)

        def finish(acc):
            if add_ref is not None:
                acc = acc + add_scale * add_ref[...]
            o_ref[...] = acc.astype(out_dtype)

        if nk == 1:
            finish(part)
        else:
            acc_ref = refs[-1]
            k = pl.program_id(2)

            @pl.when(k == 0)
            def _():
                acc_ref[...] = part

            @pl.when(k > 0)
            def _():
                acc_ref[...] += part

            @pl.when(k == nk - 1)
            def _():
                finish(acc_ref[...])

    a_spec = pl.BlockSpec((tk, tm), lambda i, j, k: (k, i)) if ta else pl.BlockSpec((tm, tk), lambda i, j, k: (i, k))
    b_spec = pl.BlockSpec((tn, tk), lambda i, j, k: (j, k)) if tb else pl.BlockSpec((tk, tn), lambda i, j, k: (k, j))
    o_spec = pl.BlockSpec((tm, tn), lambda i, j, k: (i, j))
    in_specs = [a_spec, b_spec] + ([o_spec] if add is not None else [])
    args = (a, b) + ((add,) if add is not None else ())
    return pl.pallas_call(
        body, name=name, grid=(M // tm, N // tn, nk), in_specs=in_specs, out_specs=o_spec,
        out_shape=jax.ShapeDtypeStruct((M, N), out_dtype),
        scratch_shapes=[pltpu.VMEM((tm, tn), F32)] if nk > 1 else [],
        compiler_params=_cp("parallel", "parallel", "arbitrary"),
    )(*args)


def _ln_stats(h, mix):
    z = ALPHA * h + mix
    mu = jnp.mean(z, axis=-1, keepdims=True)
    zc = z - mu
    var = jnp.mean(zc * zc, axis=-1, keepdims=True)
    rstd = lax.rsqrt(var + LN_EPS)
    return zc * rstd, rstd


def _ln_fwd(h, mix, g, b, *, name):
    L, D = h.shape
    tr = _tile(L, 160, 16)

    def body(h_ref, m_ref, g_ref, b_ref, o_ref, ob_ref):
        xhat, _ = _ln_stats(h_ref[...], m_ref[...])
        y = xhat * g_ref[...] + b_ref[...]
        o_ref[...] = y
        ob_ref[...] = y.astype(BF16)

    row = pl.BlockSpec((tr, D), lambda i: (i, 0))
    vec = pl.BlockSpec((1, D), lambda i: (0, 0))
    return pl.pallas_call(
        body, name=name, grid=(L // tr,), in_specs=[row, row, vec, vec], out_specs=[row, row],
        out_shape=[jax.ShapeDtypeStruct((L, D), F32), jax.ShapeDtypeStruct((L, D), BF16)],
        compiler_params=_cp("parallel"),
    )(h, mix, g, b)


def _ln_bwd(dy, h, mix, g, *, name):
    L, D = h.shape
    tr = _tile(L, 160, 16)

    def body(dy_ref, h_ref, m_ref, g_ref, dz_ref, dzb_ref, dg_ref, db_ref):
        i = pl.program_id(0)
        xhat, rstd = _ln_stats(h_ref[...], m_ref[...])
        dy = dy_ref[...]
        dxh = dy * g_ref[...]
        m1 = jnp.mean(dxh, axis=-1, keepdims=True)
        m2 = jnp.mean(dxh * xhat, axis=-1, keepdims=True)
        dz = rstd * (dxh - m1 - xhat * m2)
        dz_ref[...] = dz
        dzb_ref[...] = dz.astype(BF16)
        pg = jnp.sum(dy * xhat, axis=0, keepdims=True)
        pb = jnp.sum(dy, axis=0, keepdims=True)

        @pl.when(i == 0)
        def _():
            dg_ref[...] = pg
            db_ref[...] = pb

        @pl.when(i > 0)
        def _():
            dg_ref[...] += pg
            db_ref[...] += pb

    row = pl.BlockSpec((tr, D), lambda i: (i, 0))
    vec = pl.BlockSpec((1, D), lambda i: (0, 0))
    return pl.pallas_call(
        body, name=name, grid=(L // tr,), in_specs=[row, row, row, vec], out_specs=[row, row, vec, vec],
        out_shape=[jax.ShapeDtypeStruct((L, D), F32), jax.ShapeDtypeStruct((L, D), BF16),
                   jax.ShapeDtypeStruct((1, D), F32), jax.ShapeDtypeStruct((1, D), F32)],
        compiler_params=_cp("arbitrary"),
    )(dy, h, mix, g)


def _loss(h, target, *, name):
    L, D = h.shape
    tr = FRONT

    def body(h_ref, t_ref, acc_ref, dy_ref):
        i = pl.program_id(0)
        e = jnp.where(i >= 1, h_ref[...] - t_ref[...], 0.0)
        dy_ref[...] = e * (1.0 / D)
        part = 0.5 * jnp.sum(jnp.sum(e * e, axis=-1, keepdims=True) * (1.0 / D), axis=0, keepdims=True)

        @pl.when(i == 0)
        def _():
            acc_ref[...] = jnp.zeros_like(acc_ref)

        acc_ref[...] += jnp.broadcast_to(part, acc_ref.shape)

    return pl.pallas_call(
        body, name=name, grid=(L // tr,),
        in_specs=[pl.BlockSpec((tr, D), lambda i: (i, 0)),
                  pl.BlockSpec((tr, D), lambda i: (jnp.maximum(i - 1, 0), 0))],
        out_specs=[pl.BlockSpec((8, LANE), lambda i: (0, 0)), pl.BlockSpec((tr, D), lambda i: (i, 0))],
        out_shape=[jax.ShapeDtypeStruct((8, LANE), F32), jax.ShapeDtypeStruct((L, D), F32)],
        compiler_params=_cp("arbitrary"),
    )(h, target)


def _gla_gate_fwd(P, w2p, b2, gl_blk, *, name):
    L = P.shape[0]
    DK = w2p.shape[1]
    tr = _tile(L, 640, 16)

    def body(gl_ref, w_ref, b_ref, o_ref):
        i = pl.program_id(0)
        z = _dot(gl_ref[...].astype(BF16), w_ref[...], 1, 0) + b_ref[...]
        la = _log_sigmoid(z) * (1.0 / GLA_TAU)
        o_ref[...] = jnp.where(_rows(i, tr) >= N_PAD, la, 0.0)

    return pl.pallas_call(
        body, name=name, grid=(L // tr,),
        in_specs=[pl.BlockSpec((tr, LANE), lambda i: (i, gl_blk)),
                  pl.BlockSpec((LANE, DK), lambda i: (0, 0)), pl.BlockSpec((1, DK), lambda i: (0, 0))],
        out_specs=pl.BlockSpec((tr, DK), lambda i: (i, 0)),
        out_shape=jax.ShapeDtypeStruct((L, DK), F32), compiler_params=_cp("parallel"),
    )(P, w2p, b2)


def _gla_gate_bwd(dla, P, w2p, b2, gl_blk, *, name):
    L = P.shape[0]
    DK = w2p.shape[1]
    tr = _tile(L, 640, 16)

    def body(dla_ref, gl_ref, w_ref, b_ref, dgl_ref, dw_ref, db_ref):
        i = pl.program_id(0)
        glb = gl_ref[...].astype(BF16)
        z = _dot(glb, w_ref[...], 1, 0) + b_ref[...]
        dz = jnp.where(_rows(i, tr) >= N_PAD, dla_ref[...], 0.0) * (1.0 / GLA_TAU) * _sigmoid(-z)
        dzb = dz.astype(BF16)
        dgl_ref[...] = _dot(dzb, w_ref[...], 1, 1).astype(BF16)
        pw = _dot(glb, dzb, 0, 0)
        pb = jnp.sum(dz, axis=0, keepdims=True)

        @pl.when(i == 0)
        def _():
            dw_ref[...] = pw
            db_ref[...] = pb

        @pl.when(i > 0)
        def _():
            dw_ref[...] += pw
            db_ref[...] += pb

    return pl.pallas_call(
        body, name=name, grid=(L // tr,),
        in_specs=[pl.BlockSpec((tr, DK), lambda i: (i, 0)), pl.BlockSpec((tr, LANE), lambda i: (i, gl_blk)),
                  pl.BlockSpec((LANE, DK), lambda i: (0, 0)), pl.BlockSpec((1, DK), lambda i: (0, 0))],
        out_specs=[pl.BlockSpec((tr, LANE), lambda i: (i, 0)), pl.BlockSpec((LANE, DK), lambda i: (0, 0)),
                   pl.BlockSpec((1, DK), lambda i: (0, 0))],
        out_shape=[jax.ShapeDtypeStruct((L, LANE), BF16), jax.ShapeDtypeStruct((LANE, DK), F32),
                   jax.ShapeDtypeStruct((1, DK), F32)],
        compiler_params=_cp("arbitrary"),
    )(dla, P, w2p, b2)


def _chunk_terms(q, k, g, n, scale, HV):
    C = q.shape[0]
    ri = lax.broadcasted_iota(jnp.int32, (C, C), 0)
    ci = lax.broadcasted_iota(jnp.int32, (C, C), 1)
    tri = ri >= ci
    valid = _rows(n, C) >= N_PAD
    km = jnp.where(valid, k, 0.0)
    b = _dot(tri.astype(F32), g, 1, 0, precision=HIGHEST)
    bl_row = jnp.sum(g, axis=0, keepdims=True)
    bl_col = _dot(g, jnp.ones((C, HV), F32), 0, 0, precision=HIGHEST)
    eb = jnp.exp(b)
    enb = jnp.exp(-b)
    qe = q * scale * eb
    ke = km * enb
    ebl_row = jnp.exp(bl_row)
    kl = ke * ebl_row
    return dict(tri=tri, valid=valid, eb=eb, enb=enb, qe=qe, ke=ke, kl=kl, ebl_row=ebl_row,
                ebl_col=jnp.exp(bl_col), ri=ri, ci=ci)


def _gla_chunk_fwd(P, la, H, HK, HV, *, name):
    L = P.shape[0]
    C = GLA_CHUNK
    N = L // C
    scale = HK ** -0.5

    G = GLA_GROUP if H % GLA_GROUP == 0 else 1
    HG = H // G

    def body(q_ref, k_ref, v_ref, g_ref, o_ref, s_ref, S):
        n = pl.program_id(1)

        @pl.when(n == 0)
        def _():
            S[...] = jnp.zeros_like(S)

        for g in range(G):
            ks, vs = slice(g * HK, (g + 1) * HK), slice(g * HV, (g + 1) * HV)
            S0 = S[g]
            s_ref[g, 0] = S0
            t = _chunk_terms(q_ref[:, ks], k_ref[:, ks], g_ref[:, ks], n, scale, HV)
            vb = v_ref[:, vs].astype(BF16)
            qeb = t["qe"].astype(BF16)
            inter = _dot(qeb, S0.astype(BF16), 1, 0)
            att = jnp.where(t["tri"], _dot(qeb, t["ke"].astype(BF16), 1, 1), 0.0)
            o_ref[:, vs] = inter + _dot(att.astype(BF16), vb, 1, 0)
            S[g] = t["ebl_col"] * S0 + _dot(t["kl"].astype(BF16), vb, 0, 0)

    return pl.pallas_call(
        body, name=name, grid=(HG, N),
        in_specs=[pl.BlockSpec((C, G * HK), lambda h, n: (n, h)), pl.BlockSpec((C, G * HK), lambda h, n: (n, HG + h)),
                  pl.BlockSpec((C, G * HV), lambda h, n: (n, HG + h)), pl.BlockSpec((C, G * HK), lambda h, n: (n, h))],
        out_specs=[pl.BlockSpec((C, G * HV), lambda h, n: (n, h)),
                   pl.BlockSpec((G, 1, HK, HV), lambda h, n: (h, n, 0, 0))],
        out_shape=[jax.ShapeDtypeStruct((L, H * HV), F32), jax.ShapeDtypeStruct((H, N, HK, HV), F32)],
        scratch_shapes=[pltpu.VMEM((G, HK, HV), F32)],
        compiler_params=_cp("parallel", "arbitrary"),
    )(P, P, P, la)


def _gla_chunk_bwd(P, la, S_all, do, H, HK, HV, *, name):
    L = P.shape[0]
    C = GLA_CHUNK
    N = L // C
    scale = HK ** -0.5

    G = 1
    HG = H // G

    def body(q_ref, k_ref, v_ref, g_ref, s_ref, do_ref, dq_ref, dk_ref, dv_ref, dg_ref, dS):
        step = pl.program_id(1)
        n = N - 1 - step

        @pl.when(step == 0)
        def _():
            dS[...] = jnp.zeros_like(dS)

        for g in range(G):
            ks, vs = slice(g * HK, (g + 1) * HK), slice(g * HV, (g + 1) * HV)
            dS1 = dS[g]
            S0 = s_ref[g, 0]
            t = _chunk_terms(q_ref[:, ks], k_ref[:, ks], g_ref[:, ks], n, scale, HV)
            tri, qe, ke, kl = t["tri"], t["qe"], t["ke"], t["kl"]
            vb = v_ref[:, vs].astype(BF16)
            dob = do_ref[:, vs].astype(BF16)
            qeb, keb, dSb = qe.astype(BF16), ke.astype(BF16), dS1.astype(BF16)
            dA = jnp.where(tri, _dot(dob, vb, 1, 1), 0.0).astype(BF16)
            A = jnp.where(tri, _dot(qeb, keb, 1, 1), 0.0).astype(BF16)
            dqe = _dot(dob, S0.astype(BF16), 1, 1) + _dot(dA, keb, 1, 0)
            dkl = _dot(vb, dSb, 1, 1)
            dke = _dot(dA, qeb, 0, 0) + dkl * t["ebl_row"]
            dv_ref[:, vs] = (_dot(A, dob, 0, 0) + _dot(kl.astype(BF16), dSb, 1, 0)).astype(BF16)
            debl = (jnp.sum(_dot(jnp.ones((8, HV), F32), dS1 * S0, 1, 1, precision=HIGHEST), axis=0, keepdims=True) * 0.125
                    + jnp.sum(dkl * ke, axis=0, keepdims=True))
            dbl = debl * t["ebl_row"]
            db = dqe * qe - dke * ke + jnp.where(lax.broadcasted_iota(jnp.int32, (C, 1), 0) == C - 1, dbl, 0.0)
            triu = (t["ci"] >= t["ri"]).astype(F32)
            dq_ref[:, ks] = (dqe * t["eb"] * scale).astype(BF16)
            dk_ref[:, ks] = jnp.where(t["valid"], dke * t["enb"], 0.0).astype(BF16)
            dg_ref[:, ks] = _dot(triu, db, 1, 0, precision=HIGHEST)
            dS[g] = t["ebl_col"] * dS1 + _dot(qeb, dob, 0, 0)

    rev = lambda h, s: (N - 1 - s, h)
    return pl.pallas_call(
        body, name=name, grid=(HG, N),
        in_specs=[pl.BlockSpec((C, G * HK), rev), pl.BlockSpec((C, G * HK), lambda h, s: (N - 1 - s, HG + h)),
                  pl.BlockSpec((C, G * HV), lambda h, s: (N - 1 - s, HG + h)), pl.BlockSpec((C, G * HK), rev),
                  pl.BlockSpec((G, 1, HK, HV), lambda h, s: (h, N - 1 - s, 0, 0)), pl.BlockSpec((C, G * HV), rev)],
        out_specs=[pl.BlockSpec((C, G * HK), rev), pl.BlockSpec((C, G * HK), rev),
                   pl.BlockSpec((C, G * HV), rev), pl.BlockSpec((C, G * HK), rev)],
        out_shape=[jax.ShapeDtypeStruct((L, H * HK), BF16), jax.ShapeDtypeStruct((L, H * HK), BF16),
                   jax.ShapeDtypeStruct((L, H * HV), BF16), jax.ShapeDtypeStruct((L, H * HK), F32)],
        scratch_shapes=[pltpu.VMEM((G, HK, HV), F32)],
        compiler_params=_cp("parallel", "arbitrary"),
    )(P, P, P, la, S_all, do)


def _silu_parts(x):
    s = _sigmoid(x)
    return x * s, s * (1.0 + x * (1.0 - s))


def _gla_post_fwd(o, P, ng, H, HV, r_blk, *, name):
    L, DV = o.shape
    tr = _tile(L, 320, 16)

    def body(o_ref, r_ref, g_ref, out_ref):
        for hd in range(H):
            sl = slice(hd * HV, (hd + 1) * HV)
            oh = o_ref[:, sl]
            rr = lax.rsqrt(jnp.mean(oh * oh, axis=-1, keepdims=True) + LN_EPS)
            silu, _ = _silu_parts(r_ref[:, sl])
            out_ref[:, sl] = (oh * rr * g_ref[...] * silu).astype(BF16)

    return pl.pallas_call(
        body, name=name, grid=(L // tr,),
        in_specs=[pl.BlockSpec((tr, DV), lambda i: (i, 0)), pl.BlockSpec((tr, DV), lambda i: (i, r_blk)),
                  pl.BlockSpec((1, HV), lambda i: (0, 0))],
        out_specs=pl.BlockSpec((tr, DV), lambda i: (i, 0)),
        out_shape=jax.ShapeDtypeStruct((L, DV), BF16), compiler_params=_cp("parallel"),
    )(o, P, ng)


def _gla_post_bwd(dgated, o, P, ng, H, HV, r_blk, *, name):
    L, DV = o.shape
    tr = _tile(L, 320, 16)

    def body(d_ref, o_ref, r_ref, g_ref, do_ref, dr_ref, dng_ref):
        i = pl.program_id(0)
        png = jnp.zeros((1, HV), F32)
        for hd in range(H):
            sl = slice(hd * HV, (hd + 1) * HV)
            oh = o_ref[:, sl]
            d = d_ref[:, sl]
            rr = lax.rsqrt(jnp.mean(oh * oh, axis=-1, keepdims=True) + LN_EPS)
            yh = oh * rr
            silu, dsilu = _silu_parts(r_ref[:, sl])
            dn = d * silu
            dr_ref[:, sl] = (d * yh * g_ref[...] * dsilu).astype(BF16)
            png = png + jnp.sum(dn * yh, axis=0, keepdims=True)
            dyh = dn * g_ref[...]
            do_ref[:, sl] = rr * (dyh - yh * jnp.mean(dyh * yh, axis=-1, keepdims=True))

        @pl.when(i == 0)
        def _():
            dng_ref[...] = png

        @pl.when(i > 0)
        def _():
            dng_ref[...] += png

    row = pl.BlockSpec((tr, DV), lambda i: (i, 0))
    return pl.pallas_call(
        body, name=name, grid=(L // tr,),
        in_specs=[row, row, pl.BlockSpec((tr, DV), lambda i: (i, r_blk)), pl.BlockSpec((1, HV), lambda i: (0, 0))],
        out_specs=[row, row, pl.BlockSpec((1, HV), lambda i: (0, 0))],
        out_shape=[jax.ShapeDtypeStruct((L, DV), F32), jax.ShapeDtypeStruct((L, DV), BF16),
                   jax.ShapeDtypeStruct((1, HV), F32)],
        compiler_params=_cp("arbitrary"),
    )(dgated, o, P, ng)


def _shift_down(x, halo, s):
    if s == 0:
        return x
    tr = x.shape[0]
    xx = jnp.concatenate([halo, x], axis=0)
    return pltpu.roll(xx, s, axis=0)[8:8 + tr]


def _shift_up(x, halo, s):
    if s == 0:
        return x
    tr = x.shape[0]
    xx = jnp.concatenate([x, halo], axis=0)
    return pltpu.roll(xx, tr + 8 - s, axis=0)[0:tr]


def _conv_taps(x_ref, halo_ref, i, tr):
    x = jnp.where(_rows(i, tr) >= N_PAD, x_ref[...], 0.0)
    halo = jnp.where(i * tr - 8 + lax.broadcasted_iota(jnp.int32, (8, 1), 0) >= N_PAD, halo_ref[...], 0.0)
    return [_shift_down(x, halo, s) for s in range(3)]


def _conv_apply(taps, w_ref, b_ref):
    return taps[2] * w_ref[0:1, :] + taps[1] * w_ref[1:2, :] + taps[0] * w_ref[2:3, :] + b_ref[...]


def _conv_specs(tr, tc):
    blk = pl.BlockSpec((tr, tc), lambda j, i: (i, j))
    halo = pl.BlockSpec((8, tc), lambda j, i: (jnp.maximum(i * (tr // 8) - 1, 0), j))
    w = pl.BlockSpec((3, tc), lambda j, i: (0, j))
    b = pl.BlockSpec((1, tc), lambda j, i: (0, j))
    return blk, halo, w, b


def _conv_act_fwd(Uu, Ug, wu, wg, bu, bg, *, name):
    L, DFF = Uu.shape
    tr = _tile(L, 320, 16)
    tc = _tile(DFF, 512, LANE)

    def body(xu_ref, hu_ref, xg_ref, hg_ref, wu_ref, wg_ref, bu_ref, bg_ref, o_ref):
        i = pl.program_id(1)
        u = _conv_apply(_conv_taps(xu_ref, hu_ref, i, tr), wu_ref, bu_ref)
        g = _conv_apply(_conv_taps(xg_ref, hg_ref, i, tr), wg_ref, bg_ref)
        o_ref[...] = (_silu_parts(g)[0] * u).astype(BF16)

    blk, halo, w, b = _conv_specs(tr, tc)
    return pl.pallas_call(
        body, name=name, grid=(DFF // tc, L // tr),
        in_specs=[blk, halo, blk, halo, w, w, b, b], out_specs=blk,
        out_shape=jax.ShapeDtypeStruct((L, DFF), BF16), compiler_params=_cp("parallel", "parallel"),
    )(Uu, Uu, Ug, Ug, wu, wg, bu, bg)


def _conv_act_bwd(Uu, Ug, wu, wg, bu, bg, dA, *, name):
    L, DFF = Uu.shape
    tr = _tile(L, 320, 16)
    tc = _tile(DFF, 512, LANE)

    def body(xu_ref, hu_ref, xg_ref, hg_ref, wu_ref, wg_ref, bu_ref, bg_ref, da_ref,
             du_ref, dg_ref, dwu_ref, dwg_ref, dbu_ref, dbg_ref):
        i = pl.program_id(1)
        tu = _conv_taps(xu_ref, hu_ref, i, tr)
        tg = _conv_taps(xg_ref, hg_ref, i, tr)
        u = _conv_apply(tu, wu_ref, bu_ref)
        g = _conv_apply(tg, wg_ref, bg_ref)
        silu, dsilu = _silu_parts(g)
        da = da_ref[...]
        du = da * silu
        dg = da * u * dsilu
        du_ref[...] = du
        dg_ref[...] = dg

        @pl.when(i == 0)
        def _():
            dwu_ref[...] = jnp.zeros_like(dwu_ref)
            dwg_ref[...] = jnp.zeros_like(dwg_ref)
            dbu_ref[...] = jnp.zeros_like(dbu_ref)
            dbg_ref[...] = jnp.zeros_like(dbg_ref)

        for j in range(3):
            dwu_ref[j:j + 1, :] += jnp.sum(du * tu[2 - j], axis=0, keepdims=True)
            dwg_ref[j:j + 1, :] += jnp.sum(dg * tg[2 - j], axis=0, keepdims=True)
        dbu_ref[...] += jnp.sum(du, axis=0, keepdims=True)
        dbg_ref[...] += jnp.sum(dg, axis=0, keepdims=True)

    blk, halo, w, b = _conv_specs(tr, tc)
    return pl.pallas_call(
        body, name=name, grid=(DFF // tc, L // tr),
        in_specs=[blk, halo, blk, halo, w, w, b, b, blk], out_specs=[blk, blk, w, w, b, b],
        out_shape=[jax.ShapeDtypeStruct((L, DFF), F32), jax.ShapeDtypeStruct((L, DFF), F32),
                   jax.ShapeDtypeStruct((3, DFF), F32), jax.ShapeDtypeStruct((3, DFF), F32),
                   jax.ShapeDtypeStruct((1, DFF), F32), jax.ShapeDtypeStruct((1, DFF), F32)],
        compiler_params=_cp("parallel", "arbitrary"),
    )(Uu, Uu, Ug, Ug, wu, wg, bu, bg, dA)


def _conv_in_bwd(dh, w, *, name):
    L, DFF = dh.shape
    tr = _tile(L, 320, 16)
    tc = _tile(DFF, 512, LANE)
    nb8 = L // 8

    def body(x_ref, halo_ref, w_ref, o_ref):
        i = pl.program_id(1)
        x = x_ref[...]
        halo = jnp.where((i + 1) * tr + lax.broadcasted_iota(jnp.int32, (8, 1), 0) < L, halo_ref[...], 0.0)
        d = (x * w_ref[2:3, :] + _shift_up(x, halo, 1) * w_ref[1:2, :] + _shift_up(x, halo, 2) * w_ref[0:1, :])
        o_ref[...] = jnp.where(_rows(i, tr) >= N_PAD, d, 0.0).astype(BF16)

    blk = pl.BlockSpec((tr, tc), lambda j, i: (i, j))
    halo = pl.BlockSpec((8, tc), lambda j, i: (jnp.minimum((i + 1) * (tr // 8), nb8 - 1), j))
    return pl.pallas_call(
        body, name=name, grid=(DFF // tc, L // tr),
        in_specs=[blk, halo, pl.BlockSpec((3, tc), lambda j, i: (0, j))], out_specs=blk,
        out_shape=jax.ShapeDtypeStruct((L, DFF), BF16), compiler_params=_cp("parallel", "parallel"),
    )(dh, dh, w)


def _fox_c_fwd(f, bf, *, name):
    L = f.shape[0]
    tr = _tile(L, 320, 16)

    def body(f_ref, b_ref, c_ref, carry):
        i = pl.program_id(0)

        @pl.when(i == 0)
        def _():
            carry[...] = jnp.zeros_like(carry)

        lf = jnp.where(_rows(i, tr) >= N_PAD, _log_sigmoid(f_ref[...] + b_ref[...]), 0.0)
        tri = (lax.broadcasted_iota(jnp.int32, (tr, tr), 0) >= lax.broadcasted_iota(jnp.int32, (tr, tr), 1)).astype(F32)
        c_ref[...] = _dot(tri, lf, 1, 0, precision=HIGHEST) + carry[...]
        carry[...] += jnp.sum(lf, axis=0, keepdims=True)

    return pl.pallas_call(
        body, name=name, grid=(L // tr,),
        in_specs=[pl.BlockSpec((tr, LANE), lambda i: (i, 0)), pl.BlockSpec((1, LANE), lambda i: (0, 0))],
        out_specs=pl.BlockSpec((tr, LANE), lambda i: (i, 0)),
        out_shape=jax.ShapeDtypeStruct((L, LANE), F32), scratch_shapes=[pltpu.VMEM((1, LANE), F32)],
        compiler_params=_cp("arbitrary"),
    )(f, bf)


def _fox_c_bwd(dcs, f, bf, *, name):
    L = f.shape[0]
    tr = _tile(L, 320, 16)
    nb = L // tr
    nd = len(dcs)

    def body(*refs):
        f_ref, b_ref, df_ref, db_ref, carry = refs[nd:]
        s = pl.program_id(0)
        i = nb - 1 - s

        @pl.when(s == 0)
        def _():
            carry[...] = jnp.zeros_like(carry)
            db_ref[...] = jnp.zeros_like(db_ref)

        dc = refs[0][...]
        for r in refs[1:nd]:
            dc = dc + r[...]
        triu = (lax.broadcasted_iota(jnp.int32, (tr, tr), 1) >= lax.broadcasted_iota(jnp.int32, (tr, tr), 0)).astype(F32)
        dlf = _dot(triu, dc, 1, 0, precision=HIGHEST) + carry[...]
        carry[...] += jnp.sum(dc, axis=0, keepdims=True)
        df = jnp.where(_rows(i, tr) >= N_PAD, dlf, 0.0) * _sigmoid(-(f_ref[...] + b_ref[...]))
        df_ref[...] = df.astype(BF16)
        db_ref[...] += jnp.sum(df, axis=0, keepdims=True)

    rev = pl.BlockSpec((tr, LANE), lambda s: (nb - 1 - s, 0))
    vec = pl.BlockSpec((1, LANE), lambda s: (0, 0))
    return pl.pallas_call(
        body, name=name, grid=(nb,), in_specs=[rev] * (nd + 1) + [vec], out_specs=[rev, vec],
        out_shape=[jax.ShapeDtypeStruct((L, LANE), BF16), jax.ShapeDtypeStruct((1, LANE), F32)],
        scratch_shapes=[pltpu.VMEM((1, LANE), F32)], compiler_params=_cp("arbitrary"),
    )(*dcs, f, bf)


AUG = 2 * FOX_HD
FOX_GROUP = 4
FOX_ROW_SPLIT = 5


def _split3(x):
    hi = x.astype(BF16).astype(F32)
    r = x - hi
    mid = r.astype(BF16).astype(F32)
    return hi, mid, (r - mid).astype(BF16).astype(F32)


def _aug_lanes(n, vals):
    lane = lax.broadcasted_iota(jnp.int32, (n, FOX_HD), 1)
    out = jnp.zeros((n, FOX_HD), F32)
    for j, v in enumerate(vals):
        out = jnp.where(lane == j, v, out)
    return out


def _lane_col(x, j):
    lane = lax.broadcasted_iota(jnp.int32, x.shape, 1)
    return jnp.sum(jnp.where(lane == j, x, 0.0), axis=-1, keepdims=True)


def _fox_prep_q(QO, c, H, *, name):
    L = QO.shape[0]
    hd = FOX_HD
    tr = _tile(L, 320, 16)

    def body(q_ref, c_ref, o_ref):
        c = c_ref[...]
        for h in range(H):
            hi, mid, lo = _split3(_lane_col(c, h))
            o_ref[:, h * AUG:h * AUG + hd] = (q_ref[:, h * hd:(h + 1) * hd] * (hd ** -0.5)).astype(BF16)
            o_ref[:, h * AUG + hd:(h + 1) * AUG] = _aug_lanes(tr, [hi, mid, lo, 1.0, 1.0, 1.0]).astype(BF16)

    return pl.pallas_call(
        body, name=name, grid=(L // tr,),
        in_specs=[pl.BlockSpec((tr, H * hd), lambda i: (i, 0)), pl.BlockSpec((tr, LANE), lambda i: (i, 0))],
        out_specs=pl.BlockSpec((tr, H * AUG), lambda i: (i, 0)),
        out_shape=jax.ShapeDtypeStruct((L, H * AUG), BF16), compiler_params=_cp("parallel"),
    )(QO, c)


def _fox_prep_kv(KV, c, H, *, name):
    L = KV.shape[0]
    hd = FOX_HD
    tr = _tile(L, 320, 16)

    def body(k_ref, v_ref, c_ref, ko_ref, vo_ref):
        i = pl.program_id(0)
        c = c_ref[...]
        pad = _rows(i, tr) < N_PAD
        for h in range(H):
            hi, mid, lo = _split3(_lane_col(c, h))
            aug = _aug_lanes(tr, [1.0, 1.0, 1.0, jnp.where(pad, NEG, -hi), jnp.where(pad, 0.0, -mid),
                                  jnp.where(pad, 0.0, -lo)])
            ko_ref[:, h * AUG:h * AUG + hd] = k_ref[:, h * hd:(h + 1) * hd]
            ko_ref[:, h * AUG + hd:(h + 1) * AUG] = aug.astype(BF16)
            vo_ref[:, h * AUG:h * AUG + hd] = v_ref[:, h * hd:(h + 1) * hd]
            vo_ref[:, h * AUG + hd:(h + 1) * AUG] = jnp.ones((tr, hd), BF16)

    wide = pl.BlockSpec((tr, H * AUG), lambda i: (i, 0))
    return pl.pallas_call(
        body, name=name, grid=(L // tr,),
        in_specs=[pl.BlockSpec((tr, H * hd), lambda i: (i, 0)), pl.BlockSpec((tr, H * hd), lambda i: (i, 1)),
                  pl.BlockSpec((tr, LANE), lambda i: (i, 0))],
        out_specs=[wide, wide],
        out_shape=[jax.ShapeDtypeStruct((L, H * AUG), BF16)] * 2, compiler_params=_cp("parallel"),
    )(KV, KV, c)


def _fox_mask(qi, kj, t):
    ti = qi * t + lax.broadcasted_iota(jnp.int32, (t, t), 0)
    si = kj * t + lax.broadcasted_iota(jnp.int32, (t, t), 1)
    return (si <= ti) & ((si >= N_PAD) | (si == ti))


def _fox_attn_fwd(QA, KT, VA, H, *, name):
    L = QA.shape[0]
    hd = FOX_HD
    t = _tile(L, 640, LANE)
    nb = L // t
    G = FOX_GROUP if H % FOX_GROUP == 0 else 1
    nr = FOX_ROW_SPLIT if t % (8 * FOX_ROW_SPLIT) == 0 else 1
    tr = t // nr

    def body(q_ref, k_ref, v_ref, o_ref, lse_ref, m_s, acc):
        qi, kj = pl.program_id(1), pl.program_id(2)

        @pl.when(kj == 0)
        def _():
            m_s[...] = jnp.full_like(m_s, NEG)
            acc[...] = jnp.zeros_like(acc)

        def step(masked):
            for g in range(G):
                cs = slice(g * AUG, (g + 1) * AUG)
                for r in range(nr):
                    rows = slice(r * tr, (r + 1) * tr)
                    s = _dot(q_ref[rows, cs], k_ref[cs, :], 1, 0)
                    if masked:
                        ti = qi * t + r * tr + lax.broadcasted_iota(jnp.int32, (tr, t), 0)
                        si = kj * t + lax.broadcasted_iota(jnp.int32, (tr, t), 1)
                        mask = (si <= ti) & ((si >= N_PAD) | (si == ti))
                        s = jnp.where(mask, s, NEG)
                    m_old = m_s[g, rows]
                    m_new = jnp.maximum(m_old, jnp.max(s, axis=-1, keepdims=True))
                    p = jnp.exp(s - m_new)
                    if masked:
                        p = jnp.where(mask, p, 0.0)
                    acc[g, rows] = jnp.exp(m_old - m_new) * acc[g, rows] + _dot(p.astype(BF16), v_ref[:, cs], 1, 0)
                    m_s[g, rows] = m_new

        @pl.when(kj < qi)
        def _():
            step(False)

        @pl.when(kj == qi)
        def _():
            step(True)

        @pl.when(kj == nb - 1)
        def _():
            for g in range(G):
                a = acc[g]
                l = a[:, hd:]
                o_ref[:, g * hd:(g + 1) * hd] = a[:, :hd] / l
                lse_ref[g] = m_s[g] + jnp.log(jnp.max(l, axis=-1, keepdims=True))

    return pl.pallas_call(
        body, name=name, grid=(H // G, nb, nb),
        in_specs=[pl.BlockSpec((t, G * AUG), lambda h, qi, kj: (qi, h)),
                  pl.BlockSpec((G * AUG, t), lambda h, qi, kj: (h, jnp.minimum(kj, qi))),
                  pl.BlockSpec((t, G * AUG), lambda h, qi, kj: (jnp.minimum(kj, qi), h))],
        out_specs=[pl.BlockSpec((t, G * hd), lambda h, qi, kj: (qi, h)),
                   pl.BlockSpec((G, t, 1), lambda h, qi, kj: (h, qi, 0))],
        out_shape=[jax.ShapeDtypeStruct((L, H * hd), F32), jax.ShapeDtypeStruct((H, L, 1), F32)],
        scratch_shapes=[pltpu.VMEM((G, t, 1), F32), pltpu.VMEM((G, t, AUG), F32)],
        compiler_params=_cp("parallel", "parallel", "arbitrary"),
    )(QA, KT, VA)


def _fox_attn_bwd(QA, KA, VA, DOA, lse, init, H, *, name):
    L = QA.shape[0]
    t = _tile(L, 640, LANE)
    nb = L // t

    def body(q_ref, k_ref, v_ref, do_ref, lse_ref, dk0_ref, dv0_ref, dq_ref, dk_ref, dv_ref):
        kj, qi = pl.program_id(1), pl.program_id(2)

        @pl.when((kj == 0) & (qi == 0))
        def _():
            dq_ref[...] = jnp.zeros_like(dq_ref)

        @pl.when(qi == 0)
        def _():
            dk_ref[...] = dk0_ref[...]
            dv_ref[...] = dv0_ref[...]

        def step(masked):
            q, k, doa = q_ref[...], k_ref[...], do_ref[...]
            p = jnp.exp(_dot(q, k, 1, 1) - lse_ref[0])
            if masked:
                p = jnp.where(_fox_mask(qi, kj, t), p, 0.0)
            pb = p.astype(BF16)
            ds = (p * _dot(doa, v_ref[...], 1, 1)).astype(BF16)
            dv_ref[...] += _dot(pb, doa, 0, 0)
            dk_ref[...] += _dot(ds, q, 0, 0)
            rows = pl.ds(pl.multiple_of(qi * t, t), t)
            dq_ref[rows, :] += _dot(ds, k, 1, 0)

        @pl.when(qi > kj)
        def _():
            step(False)

        @pl.when(qi == kj)
        def _():
            step(True)

    qb = pl.BlockSpec((t, AUG), lambda h, kj, qi: (jnp.maximum(qi, kj), h))
    kb = pl.BlockSpec((t, AUG), lambda h, kj, qi: (kj, h))
    return pl.pallas_call(
        body, name=name, grid=(H, nb, nb),
        in_specs=[qb, kb, kb, qb, pl.BlockSpec((1, t, 1), lambda h, kj, qi: (h, jnp.maximum(qi, kj), 0)), kb, kb],
        out_specs=[pl.BlockSpec((L, AUG), lambda h, kj, qi: (0, h)), kb, kb],
        out_shape=[jax.ShapeDtypeStruct((L, H * AUG), F32)] * 3,
        compiler_params=_cp("parallel", "arbitrary", "arbitrary"),
    )(QA, KA, VA, DOA, lse, *init)


def _fox_post_q(DQA, H, *, name):
    L = DQA.shape[0]
    hd = FOX_HD
    tr = _tile(L, 320, 16)

    def body(x_ref, dq_ref, dc_ref):
        lane = lax.broadcasted_iota(jnp.int32, (tr, LANE), 1)
        dc = jnp.zeros((tr, LANE), F32)
        for h in range(H):
            dq_ref[:, h * hd:(h + 1) * hd] = (x_ref[:, h * AUG:h * AUG + hd] * (hd ** -0.5)).astype(BF16)
            dc = jnp.where(lane == h, _lane_col(x_ref[:, h * AUG + hd:(h + 1) * AUG], 0), dc)
        dc_ref[...] = dc

    return pl.pallas_call(
        body, name=name, grid=(L // tr,), in_specs=[pl.BlockSpec((tr, H * AUG), lambda i: (i, 0))],
        out_specs=[pl.BlockSpec((tr, H * hd), lambda i: (i, 0)), pl.BlockSpec((tr, LANE), lambda i: (i, 0))],
        out_shape=[jax.ShapeDtypeStruct((L, H * hd), BF16), jax.ShapeDtypeStruct((L, LANE), F32)],
        compiler_params=_cp("parallel"),
    )(DQA)


def _fox_post_kv(DKA, DVA, H, *, name):
    L = DKA.shape[0]
    hd = FOX_HD
    tr = _tile(L, 320, 16)

    def body(k_ref, v_ref, o_ref, dc_ref):
        lane = lax.broadcasted_iota(jnp.int32, (tr, LANE), 1)
        dc = jnp.zeros((tr, LANE), F32)
        for h in range(H):
            o_ref[:, h * hd:(h + 1) * hd] = k_ref[:, h * AUG:h * AUG + hd].astype(BF16)
            o_ref[:, (H + h) * hd:(H + h + 1) * hd] = v_ref[:, h * AUG:h * AUG + hd].astype(BF16)
            dc = jnp.where(lane == h, -_lane_col(k_ref[:, h * AUG + hd:(h + 1) * AUG], 3), dc)
        dc_ref[...] = dc

    wide = pl.BlockSpec((tr, H * AUG), lambda i: (i, 0))
    return pl.pallas_call(
        body, name=name, grid=(L // tr,), in_specs=[wide, wide],
        out_specs=[pl.BlockSpec((tr, 2 * H * hd), lambda i: (i, 0)), pl.BlockSpec((tr, LANE), lambda i: (i, 0))],
        out_shape=[jax.ShapeDtypeStruct((L, 2 * H * hd), BF16), jax.ShapeDtypeStruct((L, LANE), F32)],
        compiler_params=_cp("parallel"),
    )(DKA, DVA)


def _fox_gate_fwd(o, QO, *, name):
    L, D = o.shape
    tr = _tile(L, 320, 16)

    def body(o_ref, g_ref, out_ref):
        out_ref[...] = (o_ref[...] * _sigmoid(g_ref[...])).astype(BF16)

    row = pl.BlockSpec((tr, D), lambda i: (i, 0))
    return pl.pallas_call(
        body, name=name, grid=(L // tr,), in_specs=[row, pl.BlockSpec((tr, D), lambda i: (i, 1))], out_specs=row,
        out_shape=jax.ShapeDtypeStruct((L, D), BF16), compiler_params=_cp("parallel"),
    )(o, QO)


def _fox_gate_bwd(d, o, QO, H, *, name):
    L, D = o.shape
    hd = FOX_HD
    tr = _tile(L, 320, 16)

    def body(d_ref, o_ref, g_ref, do_ref, dg_ref):
        for h in range(H):
            sl = slice(h * hd, (h + 1) * hd)
            s = _sigmoid(g_ref[:, sl])
            d = d_ref[:, sl]
            o = o_ref[:, sl]
            do = d * s
            dg_ref[:, sl] = (d * o * s * (1.0 - s)).astype(BF16)
            hi, mid, lo = _split3(-jnp.sum(do * o, axis=-1, keepdims=True))
            do_ref[:, h * AUG:h * AUG + hd] = do.astype(BF16)
            do_ref[:, h * AUG + hd:(h + 1) * AUG] = _aug_lanes(tr, [hi, mid, lo]).astype(BF16)

    row = pl.BlockSpec((tr, D), lambda i: (i, 0))
    return pl.pallas_call(
        body, name=name, grid=(L // tr,), in_specs=[row, row, pl.BlockSpec((tr, D), lambda i: (i, 1))],
        out_specs=[pl.BlockSpec((tr, H * AUG), lambda i: (i, 0)), row],
        out_shape=[jax.ShapeDtypeStruct((L, H * AUG), BF16), jax.ShapeDtypeStruct((L, D), BF16)],
        compiler_params=_cp("parallel"),
    )(d, o, QO)


def _row_tile(R, C, n_arrays):
    budget = VMEM_LIMIT // (3 * n_arrays * 4 * max(C, LANE))
    return _tile(R, max(16, budget // 16 * 16), 16)


def _sum_parts(parts, *, name, out_dtype=F32):
    R, C = parts[0].shape
    tr = _row_tile(R, C, len(parts) + 1)

    def body(*refs):
        acc = refs[0][...].astype(F32)
        for r in refs[1:-1]:
            acc = acc + r[...].astype(F32)
        refs[-1][...] = acc.astype(out_dtype)

    blk = pl.BlockSpec((tr, C), lambda i: (i, 0))
    return pl.pallas_call(
        body, name=name, grid=(R // tr,), in_specs=[blk] * len(parts), out_specs=blk,
        out_shape=jax.ShapeDtypeStruct((R, C), out_dtype), compiler_params=_cp("parallel"),
    )(*parts)


def _sum_slots(x, *, name):
    S, R, C = x.shape
    tr = _row_tile(R, C, S + 1)

    def body(x_ref, o_ref):
        acc = x_ref[0].astype(F32)
        for s in range(1, S):
            acc = acc + x_ref[s].astype(F32)
        o_ref[...] = acc

    return pl.pallas_call(
        body, name=name, grid=(R // tr,), in_specs=[pl.BlockSpec((S, tr, C), lambda i: (0, i, 0))],
        out_specs=pl.BlockSpec((tr, C), lambda i: (i, 0)),
        out_shape=jax.ShapeDtypeStruct((R, C), F32), compiler_params=_cp("parallel"),
    )(x)


def _adamw(w, m, v, gparts, *, name, stacked=False):
    R, C = w.shape
    ng = len(gparts)
    tr = _row_tile(R // 2 if stacked else R, C, 7 + ng)
    nh = (R // 2) // tr

    def body(*refs):
        w_ref, m_ref, v_ref = refs[:3]
        if stacked:
            g = jnp.where(pl.program_id(0) < nh, refs[3][...], refs[4][...])
        else:
            g = refs[3][...]
            for r in refs[4:3 + ng]:
                g = g + r[...]
        g_ref, d_ref, nm_ref, nv_ref = refs[3 + ng:]
        nm = ADAM_B1 * m_ref[...] + (1.0 - ADAM_B1) * g
        nv = ADAM_B2 * v_ref[...] + (1.0 - ADAM_B2) * (g * g)
        m_hat = nm / (1.0 - ADAM_B1 ** ADAM_STEP)
        v_hat = nv / (1.0 - ADAM_B2 ** ADAM_STEP)
        g_ref[...] = g
        d_ref[...] = -ADAM_LR * (m_hat / (jnp.sqrt(v_hat) + ADAM_EPS) + ADAM_WD * w_ref[...])
        nm_ref[...] = nm
        nv_ref[...] = nv

    blk = pl.BlockSpec((tr, C), lambda i: (i, 0))
    if stacked:
        g_specs = [pl.BlockSpec((tr, C), lambda i: (jnp.minimum(i, nh - 1), 0)),
                   pl.BlockSpec((tr, C), lambda i: (jnp.maximum(i - nh, 0), 0))]
    else:
        g_specs = [blk] * ng
    return pl.pallas_call(
        body, name=name, grid=(R // tr,), in_specs=[blk] * 3 + g_specs, out_specs=[blk] * 4,
        out_shape=[jax.ShapeDtypeStruct((R, C), F32)] * 4, compiler_params=_cp("parallel"),
    )(w, m, v, *gparts)


def _chip_peers():
    x, y, c = lax.axis_index("x"), lax.axis_index("y"), lax.axis_index("c")
    return (x, y, c), [(1 - x, y), (x, 1 - y), (1 - x, 1 - y)]


def _gather_chips(arrs, *, name):
    n = len(arrs)
    hs = [a.shape[0] // 2 for a in arrs]

    def body(*refs):
        ins, outs = refs[:n], refs[n:2 * n]
        send, recv = refs[2 * n:]
        (x, y, c), chips = _chip_peers()
        me = 2 * x + y

        def landing(a, chip_idx):
            return outs[a].at[chip_idx, pl.ds(c * hs[a], hs[a])]

        ici, passed = [], []
        for a in range(n):
            for j, (px, py) in enumerate(chips):
                cp = pltpu.make_async_remote_copy(
                    src_ref=ins[a].at[pl.ds(c * hs[a], hs[a])], dst_ref=landing(a, me), send_sem=send.at[a, j],
                    recv_sem=recv.at[a, j], device_id=(px, py, c), device_id_type=MESH)
                cp.start()
                ici.append(cp)
        for a in range(n):
            for j, (px, py) in enumerate(chips):
                ici[3 * a + j].wait_recv()
                src = landing(a, 2 * px + py)
                cp = pltpu.make_async_remote_copy(
                    src_ref=src, dst_ref=src, send_sem=send.at[a, 3 + j], recv_sem=recv.at[a, 3 + j],
                    device_id=(x, y, 1 - c), device_id_type=MESH)
                cp.start()
                passed.append(cp)
        for cp in ici:
            cp.wait_send()
        for cp in passed:
            cp.wait()

    chip = 2 * lax.axis_index("x") + lax.axis_index("y")
    outs = pl.pallas_call(
        body, name=name, in_specs=[ANY] * n, out_specs=[ANY] * n,
        out_shape=[jax.ShapeDtypeStruct((4,) + a.shape, a.dtype) for a in arrs],
        scratch_shapes=[pltpu.SemaphoreType.DMA((n, 6)), pltpu.SemaphoreType.DMA((n, 6))],
    )(*arrs)
    return [lax.dynamic_update_index_in_dim(o, a, chip, 0) for o, a in zip(outs, arrs)]


HBM_SPEC = pl.BlockSpec(memory_space=pltpu.HBM)
SEM_SPEC = pl.BlockSpec(memory_space=pltpu.SEMAPHORE)
DATAFLOW = pltpu.SideEffectType.DATAFLOW_SIDE_EFFECTING


def _late_copies(srcs, lands, send, recv, scatter):
    (x, y, c), chips = _chip_peers()
    out = []
    for a in range(len(srcs)):
        for j, (px, py) in enumerate(chips):
            src, dst = (srcs[a].at[2 * px + py], lands[a].at[j]) if scatter else (srcs[a], lands[a].at[2 * x + y])
            out.append(pltpu.make_async_remote_copy(
                src_ref=src, dst_ref=dst, send_sem=send.at[3 * a + j], recv_sem=recv.at[3 * a + j],
                device_id=(px, py, c), device_id_type=MESH))
    return out


def _gather_chips_start(arrs, after, *, name, scatter=False):
    n = len(arrs)
    land_shapes = [((3,) + a.shape[1:]) if scatter else ((4,) + a.shape) for a in arrs]

    def body(*refs):
        srcs, lands = refs[:n], refs[n:2 * n]
        send, recv = refs[2 * n + 1], refs[2 * n + 2]
        for cp in _late_copies(srcs, lands, send, recv, scatter):
            cp.start()
        refs[-1][...] = jnp.zeros_like(refs[-1])

    hbm = lambda a: pltpu.with_memory_space_constraint(a, pltpu.HBM)
    outs = pl.pallas_call(
        body, name=name,
        out_shape=(pltpu.SemaphoreType.DMA((3 * n,)), pltpu.SemaphoreType.DMA((3 * n,)),
                   *[pltpu.HBM(a.shape, a.dtype) for a in arrs],
                   *[pltpu.HBM(s, a.dtype) for s, a in zip(land_shapes, arrs)], jax.ShapeDtypeStruct((8, LANE), F32)),
        in_specs=[HBM_SPEC] * (2 * n) + [ANY],
        out_specs=(SEM_SPEC, SEM_SPEC, *[HBM_SPEC] * (2 * n), pl.BlockSpec(memory_space=pltpu.VMEM)),
        input_output_aliases={i: 2 + i for i in range(2 * n)},
        compiler_params=pltpu.CompilerParams(has_side_effects=DATAFLOW),
    )(*[hbm(a) for a in arrs], *[hbm(lax.empty(s, a.dtype)) for s, a in zip(land_shapes, arrs)], after)
    return outs[0], outs[1], list(outs[2:2 + n]), list(outs[2 + n:2 + 2 * n]), outs[-1], scatter


def _gather_chips_wait(started, after, *, name):
    send, recv, srcs, lands, _, scatter = started
    n = len(srcs)

    def body(*refs):
        for cp in _late_copies(refs[:n], refs[n:2 * n], refs[2 * n], refs[2 * n + 1], scatter):
            cp.wait_send()
            cp.wait_recv()

    outs = pl.pallas_call(
        body, name=name,
        out_shape=tuple(pltpu.HBM(a.shape, a.dtype) for a in srcs + lands),
        in_specs=[HBM_SPEC] * (2 * n) + [SEM_SPEC, SEM_SPEC, ANY], out_specs=tuple([HBM_SPEC] * (2 * n)),
        input_output_aliases={i: i for i in range(2 * n)},
        compiler_params=pltpu.CompilerParams(has_side_effects=DATAFLOW),
    )(*srcs, *lands, send, recv, after)
    return list(outs[n:])


def _scatter_chips(arrs, *, name):
    n = len(arrs)

    def body(*refs):
        ins, outs = refs[:n], refs[n:2 * n]
        send, recv = refs[2 * n:]
        (x, y, c), chips = _chip_peers()
        copies = []
        for a in range(n):
            for j, (px, py) in enumerate(chips):
                cp = pltpu.make_async_remote_copy(
                    src_ref=ins[a].at[2 * px + py], dst_ref=outs[a].at[j], send_sem=send.at[a, j],
                    recv_sem=recv.at[a, j], device_id=(px, py, c), device_id_type=MESH)
                cp.start()
                copies.append(cp)
        for cp in copies:
            cp.wait()

    return pl.pallas_call(
        body, name=name, in_specs=[ANY] * n, out_specs=[ANY] * n,
        out_shape=[jax.ShapeDtypeStruct((3,) + a.shape[1:], a.dtype) for a in arrs],
        scratch_shapes=[pltpu.SemaphoreType.DMA((n, 3)), pltpu.SemaphoreType.DMA((n, 3))],
    )(*arrs)


def _send_other_halves(arrs, *, name):
    n = len(arrs)
    hs = [a.shape[1] // 2 for a in arrs]

    def body(*refs):
        ins, outs = refs[:n], refs[n:2 * n]
        send, recv = refs[2 * n:]
        x, y, c = lax.axis_index("x"), lax.axis_index("y"), lax.axis_index("c")
        copies = []
        for a in range(n):
            cp = pltpu.make_async_remote_copy(
                src_ref=ins[a].at[pl.ds(0, 4), pl.ds((1 - c) * hs[a], hs[a])], dst_ref=outs[a], send_sem=send.at[a],
                recv_sem=recv.at[a], device_id=(x, y, 1 - c), device_id_type=MESH)
            cp.start()
            copies.append(cp)
        for cp in copies:
            cp.wait()

    return pl.pallas_call(
        body, name=name, in_specs=[ANY] * n, out_specs=[ANY] * n,
        out_shape=[jax.ShapeDtypeStruct((4, h) + a.shape[2:], a.dtype) for a, h in zip(arrs, hs)],
        scratch_shapes=[pltpu.SemaphoreType.DMA((n,)), pltpu.SemaphoreType.DMA((n,))],
    )(*arrs)


def _join_halves(arrs, *, name):
    n = len(arrs)

    def body(*refs):
        ins, outs = refs[:n], refs[n:2 * n]
        send, recv = refs[2 * n:]
        x, y, c = lax.axis_index("x"), lax.axis_index("y"), lax.axis_index("c")
        copies = []
        for a in range(n):
            h = ins[a].shape[0]
            cp = pltpu.make_async_remote_copy(
                src_ref=ins[a], dst_ref=outs[a].at[pl.ds(c * h, h)], send_sem=send.at[a], recv_sem=recv.at[a],
                device_id=(x, y, 1 - c), device_id_type=MESH)
            cp.start()
            copies.append(cp)
        for cp in copies:
            cp.wait()

    outs = pl.pallas_call(
        body, name=name, in_specs=[ANY] * n, out_specs=[ANY] * n,
        out_shape=[jax.ShapeDtypeStruct((2 * a.shape[0],) + a.shape[1:], a.dtype) for a in arrs],
        scratch_shapes=[pltpu.SemaphoreType.DMA((n,)), pltpu.SemaphoreType.DMA((n,))],
    )(*arrs)
    core = lax.axis_index("c")
    return [lax.dynamic_update_slice_in_dim(o, a, core * a.shape[0], 0) for o, a in zip(outs, arrs)]


def _gather_all(a, *, name):
    def body(in_ref, out_ref, send, recv):
        x, y, c = lax.axis_index("x"), lax.axis_index("y"), lax.axis_index("c")
        me = 4 * x + 2 * y + c
        copies = []
        for j in range(1, 8):
            fx, fy, fc = (j >> 2) & 1, (j >> 1) & 1, j & 1
            peer = (x ^ fx, y ^ fy, c ^ fc)
            cp = pltpu.make_async_remote_copy(
                src_ref=in_ref, dst_ref=out_ref.at[me], send_sem=send.at[j - 1], recv_sem=recv.at[j - 1],
                device_id=peer, device_id_type=MESH)
            cp.start()
            copies.append(cp)
        for cp in copies:
            cp.wait()

    out = pl.pallas_call(
        body, name=name, in_specs=[ANY], out_specs=ANY,
        out_shape=jax.ShapeDtypeStruct((8,) + a.shape, a.dtype),
        scratch_shapes=[pltpu.SemaphoreType.DMA((7,)), pltpu.SemaphoreType.DMA((7,))],
    )(a)
    me = 4 * lax.axis_index("x") + 2 * lax.axis_index("y") + lax.axis_index("c")
    return lax.dynamic_update_index_in_dim(out, a, me, 0)


def _pack(arrs):
    flat = jnp.concatenate([a.astype(F32).reshape(-1) for a in arrs])
    n = flat.shape[0]
    pad = (-n) % (16 * LANE)
    return jnp.pad(flat, (0, pad)).reshape(-1, LANE)


def _unpack(buf, shapes):
    flat = buf.reshape(-1)
    out, off = [], 0
    for s in shapes:
        n = int(np.prod(s))
        out.append(flat[off:off + n].reshape(s))
        off += n
    return out


def _to_shards(g, axis):
    parts = jnp.split(g, 4, axis=axis)
    return jnp.stack([p.reshape(-1, p.shape[-1]) for p in parts])


def kernel(x, meta, ln_g, ln_b, gla_w_in, gla_w_g2, gla_b_g2, gla_norm_g, gla_w_out, kv_w, kv_bf, fox_w_in, fox_w_out, ffn_w_up, ffn_conv_w, ffn_conv_b, ffn_w_down, loss_target, m_meta, m_ln_g, m_ln_b, m_gla_w_in, m_gla_w_g2, m_gla_b_g2, m_gla_norm_g, m_gla_w_out, m_kv_w, m_kv_bf, m_fox_w_in, m_fox_w_out, m_ffn_w_up, m_ffn_conv_w, m_ffn_conv_b, m_ffn_w_down, v_meta, v_ln_g, v_ln_b, v_gla_w_in, v_gla_w_g2, v_gla_b_g2, v_gla_norm_g, v_gla_w_out, v_kv_w, v_kv_bf, v_fox_w_in, v_fox_w_out, v_ffn_w_up, v_ffn_conv_w, v_ffn_conv_b, v_ffn_w_down):
    D = x.shape[-1]
    L = x.shape[1] + FRONT
    HG = GLA_HEADS
    DK, DV = D // 2, D
    HK, HV = DK // HG, DV // HG
    HF = D // FOX_HD
    DFF = ffn_w_down.shape[1] * 4
    chip = 2 * lax.axis_index("x") + lax.axis_index("y")

    big_names = ["gla_w_in", "gla_w_out", "kv_w", "fox_w_in", "fox_w_out", "ffn_w_up", "ffn_w_down"]
    big = dict(gla_w_in=gla_w_in, gla_w_out=gla_w_out, kv_w=kv_w, fox_w_in=fox_w_in, fox_w_out=fox_w_out,
               ffn_w_up=ffn_w_up, ffn_w_down=ffn_w_down)
    big_axis = dict(gla_w_in=2, gla_w_out=1, kv_w=1, fox_w_in=2, fox_w_out=1, ffn_w_up=2, ffn_w_down=1)
    small_names = ["meta", "ln_g", "ln_b", "gla_w_g2", "gla_b_g2", "gla_norm_g", "ffn_conv_w"]
    small = dict(meta=meta, ln_g=ln_g, ln_b=ln_b, gla_w_g2=gla_w_g2, gla_b_g2=gla_b_g2, gla_norm_g=gla_norm_g,
                 ffn_conv_w=ffn_conv_w)
    small_shapes = [small[k].shape for k in small_names]
    bf = lambda a: a.astype(BF16)
    g0 = _gather_chips([bf(gla_w_in[0]), bf(gla_w_out[0]), bf(ffn_w_up[0]), bf(ffn_w_down[0]),
                        _pack([small[k] for k in small_names])], name="gather_weights")
    src1 = [bf(gla_w_in[1]), bf(gla_w_out[1]), bf(ffn_w_up[1]), bf(ffn_w_down[1])]
    src2 = [bf(kv_w), bf(fox_w_in), bf(fox_w_out), bf(ffn_w_up[2:]), bf(ffn_w_down[2:])]
    late1 = _gather_chips_start(src1, g0[-1], name="gather_layer1_start")
    late2 = _gather_chips_start(src2, late1[4], name="gather_layer23_start")
    sm_sh = [_unpack(g0[-1][s], small_shapes) for s in range(4)]
    fs = {k: jnp.concatenate([sm_sh[s][i] for s in range(4)], axis=-1) for i, k in enumerate(small_names)}

    pad_cols = lambda w: jnp.pad(w, ((0, 0), (0, LANE - w.shape[1])))
    cat = lambda parts, axis: jnp.concatenate(parts, axis=axis)

    def gla_in_matrix(w):
        return cat([w[:, :2 * DK + DV], w[:, 2 * DK + DV + GLA_RANK:], pad_cols(w[:, 2 * DK + DV:2 * DK + DV + GLA_RANK])], 1)

    def ffn_matrices(up, down):
        return dict(u=cat([up(0), up(1)], 1), g=cat([up(2), up(3)], 1), d=cat([down(s) for s in range(4)], 0))

    WL = [None] * DEPTH
    WL[0] = dict(P=gla_in_matrix(cat([g0[0][s] for s in range(4)], 1)), go=cat([g0[1][s] for s in range(4)], 0),
                 **ffn_matrices(lambda s: g0[2][s], lambda s: g0[3][s]))
    w2p = [jnp.pad(fs["gla_w_g2"][l], ((0, LANE - GLA_RANK), (0, 0))).astype(BF16) for l in range(N_A_LAYERS)]
    bf_pad = jnp.pad(kv_bf, (0, LANE - HF)).reshape(1, LANE)
    cw_u = [fs["ffn_conv_w"][l][:, :DFF] for l in range(DEPTH)]
    cw_g = [fs["ffn_conv_w"][l][:, DFF:] for l in range(DEPTH)]
    cb_u = [ffn_conv_b[l][None, :DFF] for l in range(DEPTH)]
    cb_g = [ffn_conv_b[l][None, DFF:] for l in range(DEPTH)]
    gl_blk = (2 * DK + 2 * DV) // LANE
    r_blk = (2 * DK + DV) // DV

    h = jnp.concatenate([jnp.concatenate([jnp.zeros((N_PAD, D), F32), fs["meta"]], axis=0), x[0]], axis=0)
    hb = (h + late2[4][0, 0]).astype(BF16)
    saved = []
    kvs = None
    for l in range(DEPTH):
        if l == 1:
            lands = _gather_chips_wait(late1, h, name="gather_layer1_wait")
            Lgi, Lgo, Lup, Ldn = [lax.dynamic_update_index_in_dim(o, a, chip, 0) for o, a in zip(lands, src1)]
            WL[1] = dict(P=gla_in_matrix(cat([Lgi[s] for s in range(4)], 1)), go=cat([Lgo[s] for s in range(4)], 0),
                         **ffn_matrices(lambda s: Lup[s], lambda s: Ldn[s]))
        if l == N_A_LAYERS:
            lands = _gather_chips_wait(late2, h, name="gather_layer23_wait")
            Lkv, Lfi, Lfo, Lup2, Ldn2 = [lax.dynamic_update_index_in_dim(o, a, chip, 0) for o, a in zip(lands, src2)]
            for ll in range(N_A_LAYERS, DEPTH):
                jj = ll - N_A_LAYERS
                WL[ll] = dict(fi=cat([Lfi[s][jj] for s in range(4)], 1), fo=cat([Lfo[s][jj] for s in range(4)], 0),
                              **ffn_matrices(lambda s, jj=jj: Lup2[s][jj], lambda s, jj=jj: Ldn2[s][jj]))
            kv_full = cat([Lkv[s] for s in range(4)], 1)
            W_kv = kv_full[:, :2 * D]
            W_f = pad_cols(kv_full[:, 2 * D:])
            W_kvf = cat([W_kv, W_f], 1)
        s = dict(h=h, hb=hb)
        if l < N_A_LAYERS:
            s["P"] = _mm(hb, WL[l]["P"], name=f"gla_in_{l}")
            s["la"] = _gla_gate_fwd(s["P"], w2p[l], fs["gla_b_g2"][l][None], gl_blk, name=f"gla_gate_{l}")
            s["o"], s["S"] = _gla_chunk_fwd(s["P"], s["la"], HG, HK, HV, name=f"gla_chunk_{l}")
            s["gated"] = _gla_post_fwd(s["o"], s["P"], fs["gla_norm_g"][l][None], HG, HV, r_blk, name=f"gla_post_{l}")
            s["mix"] = _mm(s["gated"], WL[l]["go"], name=f"gla_out_{l}")
        else:
            j = l - N_A_LAYERS
            if kvs is None:
                KV = _mm(hb, W_kv, out_dtype=BF16, name="kv_proj")
                f = _mm(hb, W_f, name="kv_gate_proj")
                c = _fox_c_fwd(f, bf_pad, name="fox_c")
                KA, VA = _fox_prep_kv(KV, c, HF, name="fox_prep_kv")
                kvs = dict(KA=KA, KT=KA.T, VA=VA, f=f, c=c, hb=hb)
            s["QO"] = _mm(hb, WL[l]["fi"], name=f"fox_in_{j}")
            s["QA"] = _fox_prep_q(s["QO"], kvs["c"], HF, name=f"fox_prep_q_{j}")
            s["o"], s["lse"] = _fox_attn_fwd(s["QA"], kvs["KT"], kvs["VA"], HF, name=f"fox_attn_{j}")
            s["gated"] = _fox_gate_fwd(s["o"], s["QO"], name=f"fox_gate_{j}")
            s["mix"] = _mm(s["gated"], WL[l]["fo"], name=f"fox_out_{j}")
        s["h1"], s["h1b"] = _ln_fwd(h, s["mix"], fs["ln_g"][l, 0][None], fs["ln_b"][l, 0][None], name=f"ln_a_{l}")
        s["Uu"] = _mm(s["h1b"], WL[l]["u"], name=f"ffn_up_u_{l}")
        s["Ug"] = _mm(s["h1b"], WL[l]["g"], name=f"ffn_up_g_{l}")
        s["a"] = _conv_act_fwd(s["Uu"], s["Ug"], cw_u[l], cw_g[l], cb_u[l], cb_g[l], name=f"ffn_conv_{l}")
        s["ffn"] = _mm(s["a"], WL[l]["d"], name=f"ffn_down_{l}")
        h, hb = _ln_fwd(s["h1"], s["ffn"], fs["ln_g"][l, 1][None], fs["ln_b"][l, 1][None], name=f"ln_b_{l}")
        saved.append(s)

    loss_acc, dh = _loss(h, loss_target[0], name="loss")

    gW = {}
    d_ln_g = [[None, None] for _ in range(DEPTH)]
    d_ln_b = [[None, None] for _ in range(DEPTH)]
    d_cw, d_cb = [None] * DEPTH, [None] * DEPTH
    d_wg2, d_bg2, d_ng = [None] * N_A_LAYERS, [None] * N_A_LAYERS, [None] * N_A_LAYERS
    dkv = (jnp.zeros((L, HF * AUG), F32), jnp.zeros((L, HF * AUG), F32))
    dcqs = []
    core = lax.axis_index("c")
    stack = lambda k, lo, hi: jnp.stack([gW[(k, i)] for i in range(lo, hi)])

    def pair_sums(entries, tag):
        g4 = [_to_shards(g, ax) for _, g, ax in entries]
        sib = _send_other_halves(g4, name=f"pair_exchange_{tag}")
        out = []
        for (key, _, _), a, b in zip(entries, g4, sib):
            _, R, C = a.shape
            mine = lax.dynamic_slice_in_dim(a, core * (R // 2), R // 2, axis=1)
            out.append(_sum_parts([mine.reshape(2 * R, C), b.reshape(2 * R, C)], out_dtype=BF16,
                                  name=f"sum_pair_{key}").reshape(4, R // 2, C))
        return out

    def chip_sums(entries, pair, recv):
        return [_sum_parts([lax.dynamic_index_in_dim(p, chip, axis=0, keepdims=False), r[0], r[1], r[2]],
                           name=f"sum_chips_{key}") for (key, _, _), p, r in zip(entries, pair, recv)]

    ln_tok = 0.0
    for l in reversed(range(DEPTH)):
        s = saved[l]
        dz, dzb, d_ln_g[l][1], d_ln_b[l][1] = _ln_bwd(dh, s["h1"], s["ffn"], fs["ln_g"][l, 1][None] + ln_tok,
                                                      name=f"ln_b_bwd_{l}")
        dA = _mm(dzb, WL[l]["d"], tb=True, name=f"ffn_down_dx_{l}")
        gW[("ffn_w_down", l)] = _mm(s["a"], dzb, ta=True, out_dtype=BF16, name=f"ffn_down_dw_{l}")
        dcu, dcg, dwu, dwg, dbu, dbg = _conv_act_bwd(s["Uu"], s["Ug"], cw_u[l], cw_g[l], cb_u[l], cb_g[l], dA,
                                                     name=f"ffn_conv_bwd_{l}")
        d_cw[l] = jnp.concatenate([dwu, dwg], axis=1)
        d_cb[l] = jnp.concatenate([dbu, dbg], axis=1)[0]
        dUu = _conv_in_bwd(dcu, cw_u[l], name=f"ffn_conv_dx_u_{l}")
        dUg = _conv_in_bwd(dcg, cw_g[l], name=f"ffn_conv_dx_g_{l}")
        gW[("ffn_w_up", l)] = jnp.concatenate(
            [_mm(s["h1b"], dUu, ta=True, out_dtype=BF16, name=f"ffn_up_dw_u_{l}"),
             _mm(s["h1b"], dUg, ta=True, out_dtype=BF16, name=f"ffn_up_dw_g_{l}")], axis=1)
        dh1 = _mm(dUu, WL[l]["u"], tb=True, add=dz, add_scale=ALPHA, name=f"ffn_up_dx_u_{l}")
        dh1 = _mm(dUg, WL[l]["g"], tb=True, add=dh1, name=f"ffn_up_dx_g_{l}")
        dz, dzb, d_ln_g[l][0], d_ln_b[l][0] = _ln_bwd(dh1, s["h"], s["mix"], fs["ln_g"][l, 0][None], name=f"ln_a_bwd_{l}")
        if l < N_A_LAYERS:
            dgated = _mm(dzb, WL[l]["go"], tb=True, name=f"gla_out_dx_{l}")
            gW[("gla_w_out", l)] = _mm(s["gated"], dzb, ta=True, out_dtype=BF16, name=f"gla_out_dw_{l}")
            do, drb, d_ng[l] = _gla_post_bwd(dgated, s["o"], s["P"], fs["gla_norm_g"][l][None], HG, HV, r_blk,
                                             name=f"gla_post_bwd_{l}")
            dq, dk, dvb, dla = _gla_chunk_bwd(s["P"], s["la"], s["S"], do, HG, HK, HV, name=f"gla_chunk_bwd_{l}")
            dglb, dw2, d_bg2[l] = _gla_gate_bwd(dla, s["P"], w2p[l], fs["gla_b_g2"][l][None], gl_blk,
                                               name=f"gla_gate_bwd_{l}")
            d_wg2[l] = dw2[:GLA_RANK]
            dP = jnp.concatenate([dq, dk, dvb, drb, dglb], axis=1)
            gP = _mm(s["hb"], dP, ta=True, out_dtype=BF16, name=f"gla_in_dw_{l}")
            gW[("gla_w_in", l)] = jnp.concatenate(
                [gP[:, :2 * DK + DV], gP[:, 2 * DK + 2 * DV:2 * DK + 2 * DV + GLA_RANK], gP[:, 2 * DK + DV:2 * DK + 2 * DV]],
                axis=1)
            dh = _mm(dP, WL[l]["P"], tb=True, add=dz, add_scale=ALPHA, name=f"gla_in_dx_{l}")
        else:
            j = l - N_A_LAYERS
            dgo = _mm(dzb, WL[l]["fo"], tb=True, name=f"fox_out_dx_{j}")
            gW[("fox_w_out", j)] = _mm(s["gated"], dzb, ta=True, out_dtype=BF16, name=f"fox_out_dw_{j}")
            DOA, dogb = _fox_gate_bwd(dgo, s["o"], s["QO"], HF, name=f"fox_gate_bwd_{j}")
            DQA, DKA, DVA = _fox_attn_bwd(s["QA"], kvs["KA"], kvs["VA"], DOA, s["lse"], dkv, HF, name=f"fox_attn_bwd_{j}")
            dkv = (DKA, DVA)
            dqb, dcq = _fox_post_q(DQA, HF, name=f"fox_post_q_{j}")
            dcqs.append(dcq)
            dQO = jnp.concatenate([dqb, dogb], axis=1)
            gW[("fox_w_in", j)] = _mm(s["hb"], dQO, ta=True, out_dtype=BF16, name=f"fox_in_dw_{j}")
            dh = _mm(dQO, WL[l]["fi"], tb=True, add=dz, add_scale=ALPHA, name=f"fox_in_dx_{j}")
            if j == 0:
                dkvb, dck = _fox_post_kv(DKA, DVA, HF, name="fox_post_kv")
                dfb, d_bf = _fox_c_bwd(dcqs + [dck], kvs["f"], bf_pad, name="fox_c_bwd")
                dKVF = jnp.concatenate([dkvb, dfb], axis=1)
                gkv = _mm(kvs["hb"], dKVF, ta=True, out_dtype=BF16, name="kv_dw")
                gW[("kv_w", 0)] = gkv[:, :2 * D + HF]
                dh = _mm(dKVF, W_kvf, tb=True, add=dh, name="kv_dx")
                late_entries = [("fox_w_in", stack("fox_w_in", 0, 2), 2), ("fox_w_out", stack("fox_w_out", 0, 2), 1),
                                ("kv_w", gW[("kv_w", 0)], 1), ("ffn_w_up_hi", stack("ffn_w_up", N_A_LAYERS, DEPTH), 2),
                                ("ffn_w_down_hi", stack("ffn_w_down", N_A_LAYERS, DEPTH), 1)]
                pair_late = pair_sums(late_entries, "late")
                sc_late = _gather_chips_start(pair_late, dh, name="scatter_late_start", scatter=True)
                ln_tok = sc_late[4][0, 0]

    early_entries = [("gla_w_in", stack("gla_w_in", 0, N_A_LAYERS), 2), ("gla_w_out", stack("gla_w_out", 0, N_A_LAYERS), 1),
                     ("ffn_w_up_lo", stack("ffn_w_up", 0, N_A_LAYERS), 2), ("ffn_w_down_lo", stack("ffn_w_down", 0, N_A_LAYERS), 1)]
    pair_early = pair_sums(early_entries, "early")
    recv_early = _scatter_chips(pair_early, name="scatter_grads")
    recv_late = _gather_chips_wait(sc_late, recv_early[0], name="scatter_late_wait")
    halves = chip_sums(early_entries, pair_early, recv_early) + chip_sums(late_entries, pair_late, recv_late)
    joined = _join_halves(halves, name="join_halves")
    gsum = {key: g for (key, _, _), g in zip(early_entries + late_entries, joined)}
    moments = dict(gla_w_in=(m_gla_w_in, v_gla_w_in), gla_w_out=(m_gla_w_out, v_gla_w_out), kv_w=(m_kv_w, v_kv_w),
                   fox_w_in=(m_fox_w_in, v_fox_w_in), fox_w_out=(m_fox_w_out, v_fox_w_out),
                   ffn_w_up=(m_ffn_w_up, v_ffn_w_up), ffn_w_down=(m_ffn_w_down, v_ffn_w_down))
    res = {}
    for k in big_names:
        w = big[k]
        sh = w.shape
        flat = lambda a: a.reshape(-1, sh[-1])
        split = k + "_lo" in gsum
        outs = _adamw(flat(w), flat(moments[k][0]), flat(moments[k][1]),
                      [gsum[k + "_lo"], gsum[k + "_hi"]] if split else [gsum[k]], stacked=split, name=f"adamw_{k}")
        res[k] = [o.reshape(sh) for o in outs]

    dmeta = dh[N_PAD:FRONT]
    sg = dict(meta=dmeta,
              ln_g=jnp.stack([jnp.concatenate(d_ln_g[l], axis=0) for l in range(DEPTH)]),
              ln_b=jnp.stack([jnp.concatenate(d_ln_b[l], axis=0) for l in range(DEPTH)]),
              gla_w_g2=jnp.stack(d_wg2), gla_b_g2=jnp.stack([d[0] for d in d_bg2]),
              gla_norm_g=jnp.stack([d[0] for d in d_ng]), ffn_conv_w=jnp.stack(d_cw),
              kv_bf=d_bf[0, :HF], ffn_conv_b=jnp.stack(d_cb), loss=loss_acc[0, :1])
    sg_names = small_names + ["kv_bf", "ffn_conv_b", "loss"]
    sg_shapes = [sg[k].shape for k in sg_names]
    red = _sum_slots(_gather_all(_pack([sg[k] for k in sg_names]), name="gather_small_grads"), name="sum_small_grads")
    red = dict(zip(sg_names, _unpack(red, sg_shapes)))
    loss = red["loss"][0]
    loc = {}
    for k in small_names:
        wdt = small[k].shape[-1]
        loc[k] = lax.dynamic_slice_in_dim(red[k], chip * wdt, wdt, axis=red[k].ndim - 1)
    loc["kv_bf"] = red["kv_bf"]
    loc["ffn_conv_b"] = red["ffn_conv_b"]
    sm_all = small_names + ["kv_bf", "ffn_conv_b"]
    sw = dict(small, kv_bf=kv_bf, ffn_conv_b=ffn_conv_b)
    sm_m = dict(meta=m_meta, ln_g=m_ln_g, ln_b=m_ln_b, gla_w_g2=m_gla_w_g2, gla_b_g2=m_gla_b_g2, gla_norm_g=m_gla_norm_g,
                ffn_conv_w=m_ffn_conv_w, kv_bf=m_kv_bf, ffn_conv_b=m_ffn_conv_b)
    sm_v = dict(meta=v_meta, ln_g=v_ln_g, ln_b=v_ln_b, gla_w_g2=v_gla_w_g2, gla_b_g2=v_gla_b_g2, gla_norm_g=v_gla_norm_g,
                ffn_conv_w=v_ffn_conv_w, kv_bf=v_kv_bf, ffn_conv_b=v_ffn_conv_b)
    shapes_loc = [sw[k].shape for k in sm_all]
    outs = _adamw(_pack([sw[k] for k in sm_all]), _pack([sm_m[k] for k in sm_all]), _pack([sm_v[k] for k in sm_all]),
                  [_pack([loc[k] for k in sm_all])], name="adamw_small")
    outs = [_unpack(o, shapes_loc) for o in outs]
    for i, k in enumerate(sm_all):
        res[k] = [outs[q][i] for q in range(4)]

    order = ["meta", "ln_g", "ln_b", "gla_w_in", "gla_w_g2", "gla_b_g2", "gla_norm_g", "gla_w_out", "kv_w", "kv_bf",
             "fox_w_in", "fox_w_out", "ffn_w_up", "ffn_conv_w", "ffn_conv_b", "ffn_w_down"]
    grad_x = dh[FRONT:][None]
    return (loss, grad_x, *[res[k][0] for k in order], *[res[k][1] for k in order], *[res[k][2] for k in order],
            *[res[k][3] for k in order])
```

```python
import functools

import numpy as np
import jax
import jax.numpy as jnp
from jax import lax
from jax.experimental import pallas as pl
from jax.experimental.pallas import tpu as pltpu

F32 = jnp.float32
BF16 = jnp.bfloat16
HIGHEST = lax.Precision.HIGHEST

DEPTH = 4
N_A_LAYERS = DEPTH // 2
N_META = 16
FRONT = 128
N_PAD = FRONT - N_META
ALPHA = (2.0 * DEPTH) ** 0.25
LN_EPS = 1e-5
GLA_HEADS = 4
GLA_RANK = 16
GLA_TAU = 16.0
GLA_CHUNK = 64
GLA_GROUP = 2
FOX_HD = 128
LANE = 128
ADAM_LR = 0.001
ADAM_B1 = 0.9
ADAM_B2 = 0.999
ADAM_EPS = 1e-08
ADAM_WD = 0.01
ADAM_STEP = 10
NEG = -(2.0 ** 100)
VMEM_LIMIT = 50 * 1024 * 1024
MESH = pl.DeviceIdType.MESH
ANY = pl.BlockSpec(memory_space=pl.ANY)


def _tile(n, pref, align):
    best = None
    t = align
    while t <= min(n, pref):
        if n % t == 0:
            best = t
        t += align
    return best if best is not None else n


def _cp(*sem):
    return pltpu.CompilerParams(dimension_semantics=sem, vmem_limit_bytes=VMEM_LIMIT)


def _dot(a, b, ca, cb, precision=None):
    return lax.dot_general(a, b, (((ca,), (cb,)), ((), ())), precision=precision,
                           preferred_element_type=F32)


def _sigmoid(x):
    return 1.0 / (1.0 + jnp.exp(-x))


def _log_sigmoid(z):
    return jnp.minimum(z, 0.0) - jnp.log(1.0 + jnp.exp(-jnp.abs(z)))


def _rows(i, tr, n=None):
    n = tr if n is None else n
    return i * tr + lax.broadcasted_iota(jnp.int32, (n, 1), 0)


def _mm(a, b, *, ta=False, tb=False, out_dtype=F32, add=None, add_scale=1.0, name):
    if ta:
        K, M = a.shape
    else:
        M, K = a.shape
    if tb:
        N, K2 = b.shape
    else:
        K2, N = b.shape
    assert K == K2, (a.shape, b.shape, ta, tb)
    tm = _tile(M, 1024, LANE) if ta else _tile(M, 1040, 16)
    tn = _tile(N, 1024, LANE)
    tk = _tile(K, 2048, LANE if ((not ta) or tb) else 16)
    nk = K // tk
    ca = 0 if ta else 1
    cb = 1 if tb else 0

    def body(*refs):
        if add is None:
            a_ref, b_ref, o_ref = refs[:3]
            add_ref = None
        else:
            a_ref, b_ref, add_ref, o_ref = refs[:4]
        part = _dot(a_ref[...].astype(BF16), b_ref[...].astype(BF16), ca, cb)

        def finish(acc):
            if add_ref is not None:
                acc = acc + add_scale * add_ref[...]
            o_ref[...] = acc.astype(out_dtype)

        if nk == 1:
            finish(part)
        else:
            acc_ref = refs[-1]
            k = pl.program_id(2)

            @pl.when(k == 0)
            def _():
                acc_ref[...] = part

            @pl.when(k > 0)
            def _():
                acc_ref[...] += part

            @pl.when(k == nk - 1)
            def _():
                finish(acc_ref[...])

    a_spec = pl.BlockSpec((tk, tm), lambda i, j, k: (k, i)) if ta else pl.BlockSpec((tm, tk), lambda i, j, k: (i, k))
    b_spec = pl.BlockSpec((tn, tk), lambda i, j, k: (j, k)) if tb else pl.BlockSpec((tk, tn), lambda i, j, k: (k, j))
    o_spec = pl.BlockSpec((tm, tn), lambda i, j, k: (i, j))
    in_specs = [a_spec, b_spec] + ([o_spec] if add is not None else [])
    args = (a, b) + ((add,) if add is not None else ())
    return pl.pallas_call(
        body, name=name, grid=(M // tm, N // tn, nk), in_specs=in_specs, out_specs=o_spec,
        out_shape=jax.ShapeDtypeStruct((M, N), out_dtype),
        scratch_shapes=[pltpu.VMEM((tm, tn), F32)] if nk > 1 else [],
        compiler_params=_cp("parallel", "parallel", "arbitrary"),
    )(*args)


def _ln_stats(h, mix):
    z = ALPHA * h + mix
    mu = jnp.mean(z, axis=-1, keepdims=True)
    zc = z - mu
    var = jnp.mean(zc * zc, axis=-1, keepdims=True)
    rstd = lax.rsqrt(var + LN_EPS)
    return zc * rstd, rstd


def _ln_fwd(h, mix, g, b, *, name):
    L, D = h.shape
    tr = _tile(L, 160, 16)

    def body(h_ref, m_ref, g_ref, b_ref, o_ref, ob_ref):
        xhat, _ = _ln_stats(h_ref[...], m_ref[...])
        y = xhat * g_ref[...] + b_ref[...]
        o_ref[...] = y
        ob_ref[...] = y.astype(BF16)

    row = pl.BlockSpec((tr, D), lambda i: (i, 0))
    vec = pl.BlockSpec((1, D), lambda i: (0, 0))
    return pl.pallas_call(
        body, name=name, grid=(L // tr,), in_specs=[row, row, vec, vec], out_specs=[row, row],
        out_shape=[jax.ShapeDtypeStruct((L, D), F32), jax.ShapeDtypeStruct((L, D), BF16)],
        compiler_params=_cp("parallel"),
    )(h, mix, g, b)


def _ln_bwd(dy, h, mix, g, *, name):
    L, D = h.shape
    tr = _tile(L, 160, 16)

    def body(dy_ref, h_ref, m_ref, g_ref, dz_ref, dzb_ref, dg_ref, db_ref):
        i = pl.program_id(0)
        xhat, rstd = _ln_stats(h_ref[...], m_ref[...])
        dy = dy_ref[...]
        dxh = dy * g_ref[...]
        m1 = jnp.mean(dxh, axis=-1, keepdims=True)
        m2 = jnp.mean(dxh * xhat, axis=-1, keepdims=True)
        dz = rstd * (dxh - m1 - xhat * m2)
        dz_ref[...] = dz
        dzb_ref[...] = dz.astype(BF16)
        pg = jnp.sum(dy * xhat, axis=0, keepdims=True)
        pb = jnp.sum(dy, axis=0, keepdims=True)

        @pl.when(i == 0)
        def _():
            dg_ref[...] = pg
            db_ref[...] = pb

        @pl.when(i > 0)
        def _():
            dg_ref[...] += pg
            db_ref[...] += pb

    row = pl.BlockSpec((tr, D), lambda i: (i, 0))
    vec = pl.BlockSpec((1, D), lambda i: (0, 0))
    return pl.pallas_call(
        body, name=name, grid=(L // tr,), in_specs=[row, row, row, vec], out_specs=[row, row, vec, vec],
        out_shape=[jax.ShapeDtypeStruct((L, D), F32), jax.ShapeDtypeStruct((L, D), BF16),
                   jax.ShapeDtypeStruct((1, D), F32), jax.ShapeDtypeStruct((1, D), F32)],
        compiler_params=_cp("arbitrary"),
    )(dy, h, mix, g)


def _loss(h, target, *, name):
    L, D = h.shape
    tr = FRONT

    def body(h_ref, t_ref, acc_ref, dy_ref):
        i = pl.program_id(0)
        e = jnp.where(i >= 1, h_ref[...] - t_ref[...], 0.0)
        dy_ref[...] = e * (1.0 / D)
        part = 0.5 * jnp.sum(jnp.sum(e * e, axis=-1, keepdims=True) * (1.0 / D), axis=0, keepdims=True)

        @pl.when(i == 0)
        def _():
            acc_ref[...] = jnp.zeros_like(acc_ref)

        acc_ref[...] += jnp.broadcast_to(part, acc_ref.shape)

    return pl.pallas_call(
        body, name=name, grid=(L // tr,),
        in_specs=[pl.BlockSpec((tr, D), lambda i: (i, 0)),
                  pl.BlockSpec((tr, D), lambda i: (jnp.maximum(i - 1, 0), 0))],
        out_specs=[pl.BlockSpec((8, LANE), lambda i: (0, 0)), pl.BlockSpec((tr, D), lambda i: (i, 0))],
        out_shape=[jax.ShapeDtypeStruct((8, LANE), F32), jax.ShapeDtypeStruct((L, D), F32)],
        compiler_params=_cp("arbitrary"),
    )(h, target)


def _gla_gate_fwd(P, w2p, b2, gl_blk, *, name):
    L = P.shape[0]
    DK = w2p.shape[1]
    tr = _tile(L, 640, 16)

    def body(gl_ref, w_ref, b_ref, o_ref):
        i = pl.program_id(0)
        z = _dot(gl_ref[...].astype(BF16), w_ref[...], 1, 0) + b_ref[...]
        la = _log_sigmoid(z) * (1.0 / GLA_TAU)
        o_ref[...] = jnp.where(_rows(i, tr) >= N_PAD, la, 0.0)

    return pl.pallas_call(
        body, name=name, grid=(L // tr,),
        in_specs=[pl.BlockSpec((tr, LANE), lambda i: (i, gl_blk)),
                  pl.BlockSpec((LANE, DK), lambda i: (0, 0)), pl.BlockSpec((1, DK), lambda i: (0, 0))],
        out_specs=pl.BlockSpec((tr, DK), lambda i: (i, 0)),
        out_shape=jax.ShapeDtypeStruct((L, DK), F32), compiler_params=_cp("parallel"),
    )(P, w2p, b2)


def _gla_gate_bwd(dla, P, w2p, b2, gl_blk, *, name):
    L = P.shape[0]
    DK = w2p.shape[1]
    tr = _tile(L, 640, 16)

    def body(dla_ref, gl_ref, w_ref, b_ref, dgl_ref, dw_ref, db_ref):
        i = pl.program_id(0)
        glb = gl_ref[...].astype(BF16)
        z = _dot(glb, w_ref[...], 1, 0) + b_ref[...]
        dz = jnp.where(_rows(i, tr) >= N_PAD, dla_ref[...], 0.0) * (1.0 / GLA_TAU) * _sigmoid(-z)
        dzb = dz.astype(BF16)
        dgl_ref[...] = _dot(dzb, w_ref[...], 1, 1).astype(BF16)
        pw = _dot(glb, dzb, 0, 0)
        pb = jnp.sum(dz, axis=0, keepdims=True)

        @pl.when(i == 0)
        def _():
            dw_ref[...] = pw
            db_ref[...] = pb

        @pl.when(i > 0)
        def _():
            dw_ref[...] += pw
            db_ref[...] += pb

    return pl.pallas_call(
        body, name=name, grid=(L // tr,),
        in_specs=[pl.BlockSpec((tr, DK), lambda i: (i, 0)), pl.BlockSpec((tr, LANE), lambda i: (i, gl_blk)),
                  pl.BlockSpec((LANE, DK), lambda i: (0, 0)), pl.BlockSpec((1, DK), lambda i: (0, 0))],
        out_specs=[pl.BlockSpec((tr, LANE), lambda i: (i, 0)), pl.BlockSpec((LANE, DK), lambda i: (0, 0)),
                   pl.BlockSpec((1, DK), lambda i: (0, 0))],
        out_shape=[jax.ShapeDtypeStruct((L, LANE), BF16), jax.ShapeDtypeStruct((LANE, DK), F32),
                   jax.ShapeDtypeStruct((1, DK), F32)],
        compiler_params=_cp("arbitrary"),
    )(dla, P, w2p, b2)


def _chunk_terms(q, k, g, n, scale, HV):
    C = q.shape[0]
    ri = lax.broadcasted_iota(jnp.int32, (C, C), 0)
    ci = lax.broadcasted_iota(jnp.int32, (C, C), 1)
    tri = ri >= ci
    valid = _rows(n, C) >= N_PAD
    km = jnp.where(valid, k, 0.0)
    b = _dot(tri.astype(F32), g, 1, 0, precision=HIGHEST)
    bl_row = jnp.sum(g, axis=0, keepdims=True)
    bl_col = _dot(g, jnp.ones((C, HV), F32), 0, 0, precision=HIGHEST)
    eb = jnp.exp(b)
    enb = jnp.exp(-b)
    qe = q * scale * eb
    ke = km * enb
    ebl_row = jnp.exp(bl_row)
    kl = ke * ebl_row
    return dict(tri=tri, valid=valid, eb=eb, enb=enb, qe=qe, ke=ke, kl=kl, ebl_row=ebl_row,
                ebl_col=jnp.exp(bl_col), ri=ri, ci=ci)


def _gla_chunk_fwd(P, la, H, HK, HV, *, name):
    L = P.shape[0]
    C = GLA_CHUNK
    N = L // C
    scale = HK ** -0.5

    G = GLA_GROUP if H % GLA_GROUP == 0 else 1
    HG = H // G

    def body(q_ref, k_ref, v_ref, g_ref, o_ref, s_ref, S):
        n = pl.program_id(1)

        @pl.when(n == 0)
        def _():
            S[...] = jnp.zeros_like(S)

        for g in range(G):
            ks, vs = slice(g * HK, (g + 1) * HK), slice(g * HV, (g + 1) * HV)
            S0 = S[g]
            s_ref[g, 0] = S0
            t = _chunk_terms(q_ref[:, ks], k_ref[:, ks], g_ref[:, ks], n, scale, HV)
            vb = v_ref[:, vs].astype(BF16)
            qeb = t["qe"].astype(BF16)
            inter = _dot(qeb, S0.astype(BF16), 1, 0)
            att = jnp.where(t["tri"], _dot(qeb, t["ke"].astype(BF16), 1, 1), 0.0)
            o_ref[:, vs] = inter + _dot(att.astype(BF16), vb, 1, 0)
            S[g] = t["ebl_col"] * S0 + _dot(t["kl"].astype(BF16), vb, 0, 0)

    return pl.pallas_call(
        body, name=name, grid=(HG, N),
        in_specs=[pl.BlockSpec((C, G * HK), lambda h, n: (n, h)), pl.BlockSpec((C, G * HK), lambda h, n: (n, HG + h)),
                  pl.BlockSpec((C, G * HV), lambda h, n: (n, HG + h)), pl.BlockSpec((C, G * HK), lambda h, n: (n, h))],
        out_specs=[pl.BlockSpec((C, G * HV), lambda h, n: (n, h)),
                   pl.BlockSpec((G, 1, HK, HV), lambda h, n: (h, n, 0, 0))],
        out_shape=[jax.ShapeDtypeStruct((L, H * HV), F32), jax.ShapeDtypeStruct((H, N, HK, HV), F32)],
        scratch_shapes=[pltpu.VMEM((G, HK, HV), F32)],
        compiler_params=_cp("parallel", "arbitrary"),
    )(P, P, P, la)


def _gla_chunk_bwd(P, la, S_all, do, H, HK, HV, *, name):
    L = P.shape[0]
    C = GLA_CHUNK
    N = L // C
    scale = HK ** -0.5

    G = 1
    HG = H // G

    def body(q_ref, k_ref, v_ref, g_ref, s_ref, do_ref, dq_ref, dk_ref, dv_ref, dg_ref, dS):
        step = pl.program_id(1)
        n = N - 1 - step

        @pl.when(step == 0)
        def _():
            dS[...] = jnp.zeros_like(dS)

        for g in range(G):
            ks, vs = slice(g * HK, (g + 1) * HK), slice(g * HV, (g + 1) * HV)
            dS1 = dS[g]
            S0 = s_ref[g, 0]
            t = _chunk_terms(q_ref[:, ks], k_ref[:, ks], g_ref[:, ks], n, scale, HV)
            tri, qe, ke, kl = t["tri"], t["qe"], t["ke"], t["kl"]
            vb = v_ref[:, vs].astype(BF16)
            dob = do_ref[:, vs].astype(BF16)
            qeb, keb, dSb = qe.astype(BF16), ke.astype(BF16), dS1.astype(BF16)
            dA = jnp.where(tri, _dot(dob, vb, 1, 1), 0.0).astype(BF16)
            A = jnp.where(tri, _dot(qeb, keb, 1, 1), 0.0).astype(BF16)
            dqe = _dot(dob, S0.astype(BF16), 1, 1) + _dot(dA, keb, 1, 0)
            dkl = _dot(vb, dSb, 1, 1)
            dke = _dot(dA, qeb, 0, 0) + dkl * t["ebl_row"]
            dv_ref[:, vs] = (_dot(A, dob, 0, 0) + _dot(kl.astype(BF16), dSb, 1, 0)).astype(BF16)
            debl = (jnp.sum(_dot(jnp.ones((8, HV), F32), dS1 * S0, 1, 1, precision=HIGHEST), axis=0, keepdims=True) * 0.125
                    + jnp.sum(dkl * ke, axis=0, keepdims=True))
            dbl = debl * t["ebl_row"]
            db = dqe * qe - dke * ke + jnp.where(lax.broadcasted_iota(jnp.int32, (C, 1), 0) == C - 1, dbl, 0.0)
            triu = (t["ci"] >= t["ri"]).astype(F32)
            dq_ref[:, ks] = (dqe * t["eb"] * scale).astype(BF16)
            dk_ref[:, ks] = jnp.where(t["valid"], dke * t["enb"], 0.0).astype(BF16)
            dg_ref[:, ks] = _dot(triu, db, 1, 0, precision=HIGHEST)
            dS[g] = t["ebl_col"] * dS1 + _dot(qeb, dob, 0, 0)

    rev = lambda h, s: (N - 1 - s, h)
    return pl.pallas_call(
        body, name=name, grid=(HG, N),
        in_specs=[pl.BlockSpec((C, G * HK), rev), pl.BlockSpec((C, G * HK), lambda h, s: (N - 1 - s, HG + h)),
                  pl.BlockSpec((C, G * HV), lambda h, s: (N - 1 - s, HG + h)), pl.BlockSpec((C, G * HK), rev),
                  pl.BlockSpec((G, 1, HK, HV), lambda h, s: (h, N - 1 - s, 0, 0)), pl.BlockSpec((C, G * HV), rev)],
        out_specs=[pl.BlockSpec((C, G * HK), rev), pl.BlockSpec((C, G * HK), rev),
                   pl.BlockSpec((C, G * HV), rev), pl.BlockSpec((C, G * HK), rev)],
        out_shape=[jax.ShapeDtypeStruct((L, H * HK), BF16), jax.ShapeDtypeStruct((L, H * HK), BF16),
                   jax.ShapeDtypeStruct((L, H * HV), BF16), jax.ShapeDtypeStruct((L, H * HK), F32)],
        scratch_shapes=[pltpu.VMEM((G, HK, HV), F32)],
        compiler_params=_cp("parallel", "arbitrary"),
    )(P, P, P, la, S_all, do)


def _silu_parts(x):
    s = _sigmoid(x)
    return x * s, s * (1.0 + x * (1.0 - s))


def _gla_post_fwd(o, P, ng, H, HV, r_blk, *, name):
    L, DV = o.shape
    tr = _tile(L, 320, 16)

    def body(o_ref, r_ref, g_ref, out_ref):
        for hd in range(H):
            sl = slice(hd * HV, (hd + 1) * HV)
            oh = o_ref[:, sl]
            rr = lax.rsqrt(jnp.mean(oh * oh, axis=-1, keepdims=True) + LN_EPS)
            silu, _ = _silu_parts(r_ref[:, sl])
            out_ref[:, sl] = (oh * rr * g_ref[...] * silu).astype(BF16)

    return pl.pallas_call(
        body, name=name, grid=(L // tr,),
        in_specs=[pl.BlockSpec((tr, DV), lambda i: (i, 0)), pl.BlockSpec((tr, DV), lambda i: (i, r_blk)),
                  pl.BlockSpec((1, HV), lambda i: (0, 0))],
        out_specs=pl.BlockSpec((tr, DV), lambda i: (i, 0)),
        out_shape=jax.ShapeDtypeStruct((L, DV), BF16), compiler_params=_cp("parallel"),
    )(o, P, ng)


def _gla_post_bwd(dgated, o, P, ng, H, HV, r_blk, *, name):
    L, DV = o.shape
    tr = _tile(L, 320, 16)

    def body(d_ref, o_ref, r_ref, g_ref, do_ref, dr_ref, dng_ref):
        i = pl.program_id(0)
        png = jnp.zeros((1, HV), F32)
        for hd in range(H):
            sl = slice(hd * HV, (hd + 1) * HV)
            oh = o_ref[:, sl]
            d = d_ref[:, sl]
            rr = lax.rsqrt(jnp.mean(oh * oh, axis=-1, keepdims=True) + LN_EPS)
            yh = oh * rr
            silu, dsilu = _silu_parts(r_ref[:, sl])
            dn = d * silu
            dr_ref[:, sl] = (d * yh * g_ref[...] * dsilu).astype(BF16)
            png = png + jnp.sum(dn * yh, axis=0, keepdims=True)
            dyh = dn * g_ref[...]
            do_ref[:, sl] = rr * (dyh - yh * jnp.mean(dyh * yh, axis=-1, keepdims=True))

        @pl.when(i == 0)
        def _():
            dng_ref[...] = png

        @pl.when(i > 0)
        def _():
            dng_ref[...] += png

    row = pl.BlockSpec((tr, DV), lambda i: (i, 0))
    return pl.pallas_call(
        body, name=name, grid=(L // tr,),
        in_specs=[row, row, pl.BlockSpec((tr, DV), lambda i: (i, r_blk)), pl.BlockSpec((1, HV), lambda i: (0, 0))],
        out_specs=[row, row, pl.BlockSpec((1, HV), lambda i: (0, 0))],
        out_shape=[jax.ShapeDtypeStruct((L, DV), F32), jax.ShapeDtypeStruct((L, DV), BF16),
                   jax.ShapeDtypeStruct((1, HV), F32)],
        compiler_params=_cp("arbitrary"),
    )(dgated, o, P, ng)


def _shift_down(x, halo, s):
    if s == 0:
        return x
    tr = x.shape[0]
    xx = jnp.concatenate([halo, x], axis=0)
    return pltpu.roll(xx, s, axis=0)[8:8 + tr]


def _shift_up(x, halo, s):
    if s == 0:
        return x
    tr = x.shape[0]
    xx = jnp.concatenate([x, halo], axis=0)
    return pltpu.roll(xx, tr + 8 - s, axis=0)[0:tr]


def _conv_taps(x_ref, halo_ref, i, tr):
    x = jnp.where(_rows(i, tr) >= N_PAD, x_ref[...], 0.0)
    halo = jnp.where(i * tr - 8 + lax.broadcasted_iota(jnp.int32, (8, 1), 0) >= N_PAD, halo_ref[...], 0.0)
    return [_shift_down(x, halo, s) for s in range(3)]


def _conv_apply(taps, w_ref, b_ref):
    return taps[2] * w_ref[0:1, :] + taps[1] * w_ref[1:2, :] + taps[0] * w_ref[2:3, :] + b_ref[...]


def _conv_specs(tr, tc):
    blk = pl.BlockSpec((tr, tc), lambda j, i: (i, j))
    halo = pl.BlockSpec((8, tc), lambda j, i: (jnp.maximum(i * (tr // 8) - 1, 0), j))
    w = pl.BlockSpec((3, tc), lambda j, i: (0, j))
    b = pl.BlockSpec((1, tc), lambda j, i: (0, j))
    return blk, halo, w, b


def _conv_act_fwd(Uu, Ug, wu, wg, bu, bg, *, name):
    L, DFF = Uu.shape
    tr = _tile(L, 320, 16)
    tc = _tile(DFF, 512, LANE)

    def body(xu_ref, hu_ref, xg_ref, hg_ref, wu_ref, wg_ref, bu_ref, bg_ref, o_ref):
        i = pl.program_id(1)
        u = _conv_apply(_conv_taps(xu_ref, hu_ref, i, tr), wu_ref, bu_ref)
        g = _conv_apply(_conv_taps(xg_ref, hg_ref, i, tr), wg_ref, bg_ref)
        o_ref[...] = (_silu_parts(g)[0] * u).astype(BF16)

    blk, halo, w, b = _conv_specs(tr, tc)
    return pl.pallas_call(
        body, name=name, grid=(DFF // tc, L // tr),
        in_specs=[blk, halo, blk, halo, w, w, b, b], out_specs=blk,
        out_shape=jax.ShapeDtypeStruct((L, DFF), BF16), compiler_params=_cp("parallel", "parallel"),
    )(Uu, Uu, Ug, Ug, wu, wg, bu, bg)


def _conv_act_bwd(Uu, Ug, wu, wg, bu, bg, dA, *, name):
    L, DFF = Uu.shape
    tr = _tile(L, 320, 16)
    tc = _tile(DFF, 512, LANE)

    def body(xu_ref, hu_ref, xg_ref, hg_ref, wu_ref, wg_ref, bu_ref, bg_ref, da_ref,
             du_ref, dg_ref, dwu_ref, dwg_ref, dbu_ref, dbg_ref):
        i = pl.program_id(1)
        tu = _conv_taps(xu_ref, hu_ref, i, tr)
        tg = _conv_taps(xg_ref, hg_ref, i, tr)
        u = _conv_apply(tu, wu_ref, bu_ref)
        g = _conv_apply(tg, wg_ref, bg_ref)
        silu, dsilu = _silu_parts(g)
        da = da_ref[...]
        du = da * silu
        dg = da * u * dsilu
        du_ref[...] = du
        dg_ref[...] = dg

        @pl.when(i == 0)
        def _():
            dwu_ref[...] = jnp.zeros_like(dwu_ref)
            dwg_ref[...] = jnp.zeros_like(dwg_ref)
            dbu_ref[...] = jnp.zeros_like(dbu_ref)
            dbg_ref[...] = jnp.zeros_like(dbg_ref)

        for j in range(3):
            dwu_ref[j:j + 1, :] += jnp.sum(du * tu[2 - j], axis=0, keepdims=True)
            dwg_ref[j:j + 1, :] += jnp.sum(dg * tg[2 - j], axis=0, keepdims=True)
        dbu_ref[...] += jnp.sum(du, axis=0, keepdims=True)
        dbg_ref[...] += jnp.sum(dg, axis=0, keepdims=True)

    blk, halo, w, b = _conv_specs(tr, tc)
    return pl.pallas_call(
        body, name=name, grid=(DFF // tc, L // tr),
        in_specs=[blk, halo, blk, halo, w, w, b, b, blk], out_specs=[blk, blk, w, w, b, b],
        out_shape=[jax.ShapeDtypeStruct((L, DFF), F32), jax.ShapeDtypeStruct((L, DFF), F32),
                   jax.ShapeDtypeStruct((3, DFF), F32), jax.ShapeDtypeStruct((3, DFF), F32),
                   jax.ShapeDtypeStruct((1, DFF), F32), jax.ShapeDtypeStruct((1, DFF), F32)],
        compiler_params=_cp("parallel", "arbitrary"),
    )(Uu, Uu, Ug, Ug, wu, wg, bu, bg, dA)


def _conv_in_bwd(dh, w, *, name):
    L, DFF = dh.shape
    tr = _tile(L, 320, 16)
    tc = _tile(DFF, 512, LANE)
    nb8 = L // 8

    def body(x_ref, halo_ref, w_ref, o_ref):
        i = pl.program_id(1)
        x = x_ref[...]
        halo = jnp.where((i + 1) * tr + lax.broadcasted_iota(jnp.int32, (8, 1), 0) < L, halo_ref[...], 0.0)
        d = (x * w_ref[2:3, :] + _shift_up(x, halo, 1) * w_ref[1:2, :] + _shift_up(x, halo, 2) * w_ref[0:1, :])
        o_ref[...] = jnp.where(_rows(i, tr) >= N_PAD, d, 0.0).astype(BF16)

    blk = pl.BlockSpec((tr, tc), lambda j, i: (i, j))
    halo = pl.BlockSpec((8, tc), lambda j, i: (jnp.minimum((i + 1) * (tr // 8), nb8 - 1), j))
    return pl.pallas_call(
        body, name=name, grid=(DFF // tc, L // tr),
        in_specs=[blk, halo, pl.BlockSpec((3, tc), lambda j, i: (0, j))], out_specs=blk,
        out_shape=jax.ShapeDtypeStruct((L, DFF), BF16), compiler_params=_cp("parallel", "parallel"),
    )(dh, dh, w)


def _fox_c_fwd(f, bf, *, name):
    L = f.shape[0]
    tr = _tile(L, 320, 16)

    def body(f_ref, b_ref, c_ref, carry):
        i = pl.program_id(0)

        @pl.when(i == 0)
        def _():
            carry[...] = jnp.zeros_like(carry)

        lf = jnp.where(_rows(i, tr) >= N_PAD, _log_sigmoid(f_ref[...] + b_ref[...]), 0.0)
        tri = (lax.broadcasted_iota(jnp.int32, (tr, tr), 0) >= lax.broadcasted_iota(jnp.int32, (tr, tr), 1)).astype(F32)
        c_ref[...] = _dot(tri, lf, 1, 0, precision=HIGHEST) + carry[...]
        carry[...] += jnp.sum(lf, axis=0, keepdims=True)

    return pl.pallas_call(
        body, name=name, grid=(L // tr,),
        in_specs=[pl.BlockSpec((tr, LANE), lambda i: (i, 0)), pl.BlockSpec((1, LANE), lambda i: (0, 0))],
        out_specs=pl.BlockSpec((tr, LANE), lambda i: (i, 0)),
        out_shape=jax.ShapeDtypeStruct((L, LANE), F32), scratch_shapes=[pltpu.VMEM((1, LANE), F32)],
        compiler_params=_cp("arbitrary"),
    )(f, bf)


def _fox_c_bwd(dcs, f, bf, *, name):
    L = f.shape[0]
    tr = _tile(L, 320, 16)
    nb = L // tr
    nd = len(dcs)

    def body(*refs):
        f_ref, b_ref, df_ref, db_ref, carry = refs[nd:]
        s = pl.program_id(0)
        i = nb - 1 - s

        @pl.when(s == 0)
        def _():
            carry[...] = jnp.zeros_like(carry)
            db_ref[...] = jnp.zeros_like(db_ref)

        dc = refs[0][...]
        for r in refs[1:nd]:
            dc = dc + r[...]
        triu = (lax.broadcasted_iota(jnp.int32, (tr, tr), 1) >= lax.broadcasted_iota(jnp.int32, (tr, tr), 0)).astype(F32)
        dlf = _dot(triu, dc, 1, 0, precision=HIGHEST) + carry[...]
        carry[...] += jnp.sum(dc, axis=0, keepdims=True)
        df = jnp.where(_rows(i, tr) >= N_PAD, dlf, 0.0) * _sigmoid(-(f_ref[...] + b_ref[...]))
        df_ref[...] = df.astype(BF16)
        db_ref[...] += jnp.sum(df, axis=0, keepdims=True)

    rev = pl.BlockSpec((tr, LANE), lambda s: (nb - 1 - s, 0))
    vec = pl.BlockSpec((1, LANE), lambda s: (0, 0))
    return pl.pallas_call(
        body, name=name, grid=(nb,), in_specs=[rev] * (nd + 1) + [vec], out_specs=[rev, vec],
        out_shape=[jax.ShapeDtypeStruct((L, LANE), BF16), jax.ShapeDtypeStruct((1, LANE), F32)],
        scratch_shapes=[pltpu.VMEM((1, LANE), F32)], compiler_params=_cp("arbitrary"),
    )(*dcs, f, bf)


AUG = 2 * FOX_HD
FOX_GROUP = 4
FOX_ROW_SPLIT = 5


def _split3(x):
    hi = x.astype(BF16).astype(F32)
    r = x - hi
    mid = r.astype(BF16).astype(F32)
    return hi, mid, (r - mid).astype(BF16).astype(F32)


def _aug_lanes(n, vals):
    lane = lax.broadcasted_iota(jnp.int32, (n, FOX_HD), 1)
    out = jnp.zeros((n, FOX_HD), F32)
    for j, v in enumerate(vals):
        out = jnp.where(lane == j, v, out)
    return out


def _lane_col(x, j):
    lane = lax.broadcasted_iota(jnp.int32, x.shape, 1)
    return jnp.sum(jnp.where(lane == j, x, 0.0), axis=-1, keepdims=True)


def _fox_prep_q(QO, c, H, *, name):
    L = QO.shape[0]
    hd = FOX_HD
    tr = _tile(L, 320, 16)

    def body(q_ref, c_ref, o_ref):
        c = c_ref[...]
        for h in range(H):
            hi, mid, lo = _split3(_lane_col(c, h))
            o_ref[:, h * AUG:h * AUG + hd] = (q_ref[:, h * hd:(h + 1) * hd] * (hd ** -0.5)).astype(BF16)
            o_ref[:, h * AUG + hd:(h + 1) * AUG] = _aug_lanes(tr, [hi, mid, lo, 1.0, 1.0, 1.0]).astype(BF16)

    return pl.pallas_call(
        body, name=name, grid=(L // tr,),
        in_specs=[pl.BlockSpec((tr, H * hd), lambda i: (i, 0)), pl.BlockSpec((tr, LANE), lambda i: (i, 0))],
        out_specs=pl.BlockSpec((tr, H * AUG), lambda i: (i, 0)),
        out_shape=jax.ShapeDtypeStruct((L, H * AUG), BF16), compiler_params=_cp("parallel"),
    )(QO, c)


def _fox_prep_kv(KV, c, H, *, name):
    L = KV.shape[0]
    hd = FOX_HD
    tr = _tile(L, 320, 16)

    def body(k_ref, v_ref, c_ref, ko_ref, vo_ref):
        i = pl.program_id(0)
        c = c_ref[...]
        pad = _rows(i, tr) < N_PAD
        for h in range(H):
            hi, mid, lo = _split3(_lane_col(c, h))
            aug = _aug_lanes(tr, [1.0, 1.0, 1.0, jnp.where(pad, NEG, -hi), jnp.where(pad, 0.0, -mid),
                                  jnp.where(pad, 0.0, -lo)])
            ko_ref[:, h * AUG:h * AUG + hd] = k_ref[:, h * hd:(h + 1) * hd]
            ko_ref[:, h * AUG + hd:(h + 1) * AUG] = aug.astype(BF16)
            vo_ref[:, h * AUG:h * AUG + hd] = v_ref[:, h * hd:(h + 1) * hd]
            vo_ref[:, h * AUG + hd:(h + 1) * AUG] = jnp.ones((tr, hd), BF16)

    wide = pl.BlockSpec((tr, H * AUG), lambda i: (i, 0))
    return pl.pallas_call(
        body, name=name, grid=(L // tr,),
        in_specs=[pl.BlockSpec((tr, H * hd), lambda i: (i, 0)), pl.BlockSpec((tr, H * hd), lambda i: (i, 1)),
                  pl.BlockSpec((tr, LANE), lambda i: (i, 0))],
        out_specs=[wide, wide],
        out_shape=[jax.ShapeDtypeStruct((L, H * AUG), BF16)] * 2, compiler_params=_cp("parallel"),
    )(KV, KV, c)


def _fox_mask(qi, kj, t):
    ti = qi * t + lax.broadcasted_iota(jnp.int32, (t, t), 0)
    si = kj * t + lax.broadcasted_iota(jnp.int32, (t, t), 1)
    return (si <= ti) & ((si >= N_PAD) | (si == ti))


def _fox_attn_fwd(QA, KT, VA, H, *, name):
    L = QA.shape[0]
    hd = FOX_HD
    t = _tile(L, 640, LANE)
    nb = L // t
    G = FOX_GROUP if H % FOX_GROUP == 0 else 1
    nr = FOX_ROW_SPLIT if t % (8 * FOX_ROW_SPLIT) == 0 else 1
    tr = t // nr

    def body(q_ref, k_ref, v_ref, o_ref, lse_ref, m_s, acc):
        qi, kj = pl.program_id(1), pl.program_id(2)

        @pl.when(kj == 0)
        def _():
            m_s[...] = jnp.full_like(m_s, NEG)
            acc[...] = jnp.zeros_like(acc)

        def step(masked):
            for g in range(G):
                cs = slice(g * AUG, (g + 1) * AUG)
                for r in range(nr):
                    rows = slice(r * tr, (r + 1) * tr)
                    s = _dot(q_ref[rows, cs], k_ref[cs, :], 1, 0)
                    if masked:
                        ti = qi * t + r * tr + lax.broadcasted_iota(jnp.int32, (tr, t), 0)
                        si = kj * t + lax.broadcasted_iota(jnp.int32, (tr, t), 1)
                        mask = (si <= ti) & ((si >= N_PAD) | (si == ti))
                        s = jnp.where(mask, s, NEG)
                    m_old = m_s[g, rows]
                    m_new = jnp.maximum(m_old, jnp.max(s, axis=-1, keepdims=True))
                    p = jnp.exp(s - m_new)
                    if masked:
                        p = jnp.where(mask, p, 0.0)
                    acc[g, rows] = jnp.exp(m_old - m_new) * acc[g, rows] + _dot(p.astype(BF16), v_ref[:, cs], 1, 0)
                    m_s[g, rows] = m_new

        @pl.when(kj < qi)
        def _():
            step(False)

        @pl.when(kj == qi)
        def _():
            step(True)

        @pl.when(kj == nb - 1)
        def _():
            for g in range(G):
                a = acc[g]
                l = a[:, hd:]
                o_ref[:, g * hd:(g + 1) * hd] = a[:, :hd] / l
                lse_ref[g] = m_s[g] + jnp.log(jnp.max(l, axis=-1, keepdims=True))

    return pl.pallas_call(
        body, name=name, grid=(H // G, nb, nb),
        in_specs=[pl.BlockSpec((t, G * AUG), lambda h, qi, kj: (qi, h)),
                  pl.BlockSpec((G * AUG, t), lambda h, qi, kj: (h, jnp.minimum(kj, qi))),
                  pl.BlockSpec((t, G * AUG), lambda h, qi, kj: (jnp.minimum(kj, qi), h))],
        out_specs=[pl.BlockSpec((t, G * hd), lambda h, qi, kj: (qi, h)),
                   pl.BlockSpec((G, t, 1), lambda h, qi, kj: (h, qi, 0))],
        out_shape=[jax.ShapeDtypeStruct((L, H * hd), F32), jax.ShapeDtypeStruct((H, L, 1), F32)],
        scratch_shapes=[pltpu.VMEM((G, t, 1), F32), pltpu.VMEM((G, t, AUG), F32)],
        compiler_params=_cp("parallel", "parallel", "arbitrary"),
    )(QA, KT, VA)


def _fox_attn_bwd(QA, KA, VA, DOA, lse, init, H, *, name):
    L = QA.shape[0]
    t = _tile(L, 640, LANE)
    nb = L // t

    def body(q_ref, k_ref, v_ref, do_ref, lse_ref, dk0_ref, dv0_ref, dq_ref, dk_ref, dv_ref):
        kj, qi = pl.program_id(1), pl.program_id(2)

        @pl.when((kj == 0) & (qi == 0))
        def _():
            dq_ref[...] = jnp.zeros_like(dq_ref)

        @pl.when(qi == 0)
        def _():
            dk_ref[...] = dk0_ref[...]
            dv_ref[...] = dv0_ref[...]

        def step(masked):
            q, k, doa = q_ref[...], k_ref[...], do_ref[...]
            p = jnp.exp(_dot(q, k, 1, 1) - lse_ref[0])
            if masked:
                p = jnp.where(_fox_mask(qi, kj, t), p, 0.0)
            pb = p.astype(BF16)
            ds = (p * _dot(doa, v_ref[...], 1, 1)).astype(BF16)
            dv_ref[...] += _dot(pb, doa, 0, 0)
            dk_ref[...] += _dot(ds, q, 0, 0)
            rows = pl.ds(pl.multiple_of(qi * t, t), t)
            dq_ref[rows, :] += _dot(ds, k, 1, 0)

        @pl.when(qi > kj)
        def _():
            step(False)

        @pl.when(qi == kj)
        def _():
            step(True)

    qb = pl.BlockSpec((t, AUG), lambda h, kj, qi: (jnp.maximum(qi, kj), h))
    kb = pl.BlockSpec((t, AUG), lambda h, kj, qi: (kj, h))
    return pl.pallas_call(
        body, name=name, grid=(H, nb, nb),
        in_specs=[qb, kb, kb, qb, pl.BlockSpec((1, t, 1), lambda h, kj, qi: (h, jnp.maximum(qi, kj), 0)), kb, kb],
        out_specs=[pl.BlockSpec((L, AUG), lambda h, kj, qi: (0, h)), kb, kb],
        out_shape=[jax.ShapeDtypeStruct((L, H * AUG), F32)] * 3,
        compiler_params=_cp("parallel", "arbitrary", "arbitrary"),
    )(QA, KA, VA, DOA, lse, *init)


def _fox_post_q(DQA, H, *, name):
    L = DQA.shape[0]
    hd = FOX_HD
    tr = _tile(L, 320, 16)

    def body(x_ref, dq_ref, dc_ref):
        lane = lax.broadcasted_iota(jnp.int32, (tr, LANE), 1)
        dc = jnp.zeros((tr, LANE), F32)
        for h in range(H):
            dq_ref[:, h * hd:(h + 1) * hd] = (x_ref[:, h * AUG:h * AUG + hd] * (hd ** -0.5)).astype(BF16)
            dc = jnp.where(lane == h, _lane_col(x_ref[:, h * AUG + hd:(h + 1) * AUG], 0), dc)
        dc_ref[...] = dc

    return pl.pallas_call(
        body, name=name, grid=(L // tr,), in_specs=[pl.BlockSpec((tr, H * AUG), lambda i: (i, 0))],
        out_specs=[pl.BlockSpec((tr, H * hd), lambda i: (i, 0)), pl.BlockSpec((tr, LANE), lambda i: (i, 0))],
        out_shape=[jax.ShapeDtypeStruct((L, H * hd), BF16), jax.ShapeDtypeStruct((L, LANE), F32)],
        compiler_params=_cp("parallel"),
    )(DQA)


def _fox_post_kv(DKA, DVA, H, *, name):
    L = DKA.shape[0]
    hd = FOX_HD
    tr = _tile(L, 320, 16)

    def body(k_ref, v_ref, o_ref, dc_ref):
        lane = lax.broadcasted_iota(jnp.int32, (tr, LANE), 1)
        dc = jnp.zeros((tr, LANE), F32)
        for h in range(H):
            o_ref[:, h * hd:(h + 1) * hd] = k_ref[:, h * AUG:h * AUG + hd].astype(BF16)
            o_ref[:, (H + h) * hd:(H + h + 1) * hd] = v_ref[:, h * AUG:h * AUG + hd].astype(BF16)
            dc = jnp.where(lane == h, -_lane_col(k_ref[:, h * AUG + hd:(h + 1) * AUG], 3), dc)
        dc_ref[...] = dc

    wide = pl.BlockSpec((tr, H * AUG), lambda i: (i, 0))
    return pl.pallas_call(
        body, name=name, grid=(L // tr,), in_specs=[wide, wide],
        out_specs=[pl.BlockSpec((tr, 2 * H * hd), lambda i: (i, 0)), pl.BlockSpec((tr, LANE), lambda i: (i, 0))],
        out_shape=[jax.ShapeDtypeStruct((L, 2 * H * hd), BF16), jax.ShapeDtypeStruct((L, LANE), F32)],
        compiler_params=_cp("parallel"),
    )(DKA, DVA)


def _fox_gate_fwd(o, QO, *, name):
    L, D = o.shape
    tr = _tile(L, 320, 16)

    def body(o_ref, g_ref, out_ref):
        out_ref[...] = (o_ref[...] * _sigmoid(g_ref[...])).astype(BF16)

    row = pl.BlockSpec((tr, D), lambda i: (i, 0))
    return pl.pallas_call(
        body, name=name, grid=(L // tr,), in_specs=[row, pl.BlockSpec((tr, D), lambda i: (i, 1))], out_specs=row,
        out_shape=jax.ShapeDtypeStruct((L, D), BF16), compiler_params=_cp("parallel"),
    )(o, QO)


def _fox_gate_bwd(d, o, QO, H, *, name):
    L, D = o.shape
    hd = FOX_HD
    tr = _tile(L, 320, 16)

    def body(d_ref, o_ref, g_ref, do_ref, dg_ref):
        for h in range(H):
            sl = slice(h * hd, (h + 1) * hd)
            s = _sigmoid(g_ref[:, sl])
            d = d_ref[:, sl]
            o = o_ref[:, sl]
            do = d * s
            dg_ref[:, sl] = (d * o * s * (1.0 - s)).astype(BF16)
            hi, mid, lo = _split3(-jnp.sum(do * o, axis=-1, keepdims=True))
            do_ref[:, h * AUG:h * AUG + hd] = do.astype(BF16)
            do_ref[:, h * AUG + hd:(h + 1) * AUG] = _aug_lanes(tr, [hi, mid, lo]).astype(BF16)

    row = pl.BlockSpec((tr, D), lambda i: (i, 0))
    return pl.pallas_call(
        body, name=name, grid=(L // tr,), in_specs=[row, row, pl.BlockSpec((tr, D), lambda i: (i, 1))],
        out_specs=[pl.BlockSpec((tr, H * AUG), lambda i: (i, 0)), row],
        out_shape=[jax.ShapeDtypeStruct((L, H * AUG), BF16), jax.ShapeDtypeStruct((L, D), BF16)],
        compiler_params=_cp("parallel"),
    )(d, o, QO)


def _row_tile(R, C, n_arrays):
    budget = VMEM_LIMIT // (3 * n_arrays * 4 * max(C, LANE))
    return _tile(R, max(16, budget // 16 * 16), 16)


def _sum_parts(parts, *, name, out_dtype=F32):
    R, C = parts[0].shape
    tr = _row_tile(R, C, len(parts) + 1)

    def body(*refs):
        acc = refs[0][...].astype(F32)
        for r in refs[1:-1]:
            acc = acc + r[...].astype(F32)
        refs[-1][...] = acc.astype(out_dtype)

    blk = pl.BlockSpec((tr, C), lambda i: (i, 0))
    return pl.pallas_call(
        body, name=name, grid=(R // tr,), in_specs=[blk] * len(parts), out_specs=blk,
        out_shape=jax.ShapeDtypeStruct((R, C), out_dtype), compiler_params=_cp("parallel"),
    )(*parts)


def _sum_slots(x, *, name):
    S, R, C = x.shape
    tr = _row_tile(R, C, S + 1)

    def body(x_ref, o_ref):
        acc = x_ref[0].astype(F32)
        for s in range(1, S):
            acc = acc + x_ref[s].astype(F32)
        o_ref[...] = acc

    return pl.pallas_call(
        body, name=name, grid=(R // tr,), in_specs=[pl.BlockSpec((S, tr, C), lambda i: (0, i, 0))],
        out_specs=pl.BlockSpec((tr, C), lambda i: (i, 0)),
        out_shape=jax.ShapeDtypeStruct((R, C), F32), compiler_params=_cp("parallel"),
    )(x)


def _adamw(w, m, v, gparts, *, name, stacked=False):
    R, C = w.shape
    ng = len(gparts)
    tr = _row_tile(R // 2 if stacked else R, C, 7 + ng)
    nh = (R // 2) // tr

    def body(*refs):
        w_ref, m_ref, v_ref = refs[:3]
        if stacked:
            g = jnp.where(pl.program_id(0) < nh, refs[3][...], refs[4][...])
        else:
            g = refs[3][...]
            for r in refs[4:3 + ng]:
                g = g + r[...]
        g_ref, d_ref, nm_ref, nv_ref = refs[3 + ng:]
        nm = ADAM_B1 * m_ref[...] + (1.0 - ADAM_B1) * g
        nv = ADAM_B2 * v_ref[...] + (1.0 - ADAM_B2) * (g * g)
        m_hat = nm / (1.0 - ADAM_B1 ** ADAM_STEP)
        v_hat = nv / (1.0 - ADAM_B2 ** ADAM_STEP)
        g_ref[...] = g
        d_ref[...] = -ADAM_LR * (m_hat / (jnp.sqrt(v_hat) + ADAM_EPS) + ADAM_WD * w_ref[...])
        nm_ref[...] = nm
        nv_ref[...] = nv

    blk = pl.BlockSpec((tr, C), lambda i: (i, 0))
    if stacked:
        g_specs = [pl.BlockSpec((tr, C), lambda i: (jnp.minimum(i, nh - 1), 0)),
                   pl.BlockSpec((tr, C), lambda i: (jnp.maximum(i - nh, 0), 0))]
    else:
        g_specs = [blk] * ng
    return pl.pallas_call(
        body, name=name, grid=(R // tr,), in_specs=[blk] * 3 + g_specs, out_specs=[blk] * 4,
        out_shape=[jax.ShapeDtypeStruct((R, C), F32)] * 4, compiler_params=_cp("parallel"),
    )(w, m, v, *gparts)


def _chip_peers():
    x, y, c = lax.axis_index("x"), lax.axis_index("y"), lax.axis_index("c")
    return (x, y, c), [(1 - x, y), (x, 1 - y), (1 - x, 1 - y)]


def _gather_chips(arrs, *, name):
    n = len(arrs)
    hs = [a.shape[0] // 2 for a in arrs]

    def body(*refs):
        ins, outs = refs[:n], refs[n:2 * n]
        send, recv = refs[2 * n:]
        (x, y, c), chips = _chip_peers()
        me = 2 * x + y

        def landing(a, chip_idx):
            return outs[a].at[chip_idx, pl.ds(c * hs[a], hs[a])]

        ici, passed = [], []
        for a in range(n):
            for j, (px, py) in enumerate(chips):
                cp = pltpu.make_async_remote_copy(
                    src_ref=ins[a].at[pl.ds(c * hs[a], hs[a])], dst_ref=landing(a, me), send_sem=send.at[a, j],
                    recv_sem=recv.at[a, j], device_id=(px, py, c), device_id_type=MESH)
                cp.start()
                ici.append(cp)
        for a in range(n):
            for j, (px, py) in enumerate(chips):
                ici[3 * a + j].wait_recv()
                src = landing(a, 2 * px + py)
                cp = pltpu.make_async_remote_copy(
                    src_ref=src, dst_ref=src, send_sem=send.at[a, 3 + j], recv_sem=recv.at[a, 3 + j],
                    device_id=(x, y, 1 - c), device_id_type=MESH)
                cp.start()
                passed.append(cp)
        for cp in ici:
            cp.wait_send()
        for cp in passed:
            cp.wait()

    chip = 2 * lax.axis_index("x") + lax.axis_index("y")
    outs = pl.pallas_call(
        body, name=name, in_specs=[ANY] * n, out_specs=[ANY] * n,
        out_shape=[jax.ShapeDtypeStruct((4,) + a.shape, a.dtype) for a in arrs],
        scratch_shapes=[pltpu.SemaphoreType.DMA((n, 6)), pltpu.SemaphoreType.DMA((n, 6))],
    )(*arrs)
    return [lax.dynamic_update_index_in_dim(o, a, chip, 0) for o, a in zip(outs, arrs)]


HBM_SPEC = pl.BlockSpec(memory_space=pltpu.HBM)
SEM_SPEC = pl.BlockSpec(memory_space=pltpu.SEMAPHORE)
DATAFLOW = pltpu.SideEffectType.DATAFLOW_SIDE_EFFECTING


def _late_copies(srcs, lands, send, recv, scatter):
    (x, y, c), chips = _chip_peers()
    out = []
    for a in range(len(srcs)):
        for j, (px, py) in enumerate(chips):
            src, dst = (srcs[a].at[2 * px + py], lands[a].at[j]) if scatter else (srcs[a], lands[a].at[2 * x + y])
            out.append(pltpu.make_async_remote_copy(
                src_ref=src, dst_ref=dst, send_sem=send.at[3 * a + j], recv_sem=recv.at[3 * a + j],
                device_id=(px, py, c), device_id_type=MESH))
    return out


def _gather_chips_start(arrs, after, *, name, scatter=False):
    n = len(arrs)
    land_shapes = [((3,) + a.shape[1:]) if scatter else ((4,) + a.shape) for a in arrs]

    def body(*refs):
        srcs, lands = refs[:n], refs[n:2 * n]
        send, recv = refs[2 * n + 1], refs[2 * n + 2]
        for cp in _late_copies(srcs, lands, send, recv, scatter):
            cp.start()
        refs[-1][...] = jnp.zeros_like(refs[-1])

    hbm = lambda a: pltpu.with_memory_space_constraint(a, pltpu.HBM)
    outs = pl.pallas_call(
        body, name=name,
        out_shape=(pltpu.SemaphoreType.DMA((3 * n,)), pltpu.SemaphoreType.DMA((3 * n,)),
                   *[pltpu.HBM(a.shape, a.dtype) for a in arrs],
                   *[pltpu.HBM(s, a.dtype) for s, a in zip(land_shapes, arrs)], jax.ShapeDtypeStruct((8, LANE), F32)),
        in_specs=[HBM_SPEC] * (2 * n) + [ANY],
        out_specs=(SEM_SPEC, SEM_SPEC, *[HBM_SPEC] * (2 * n), pl.BlockSpec(memory_space=pltpu.VMEM)),
        input_output_aliases={i: 2 + i for i in range(2 * n)},
        compiler_params=pltpu.CompilerParams(has_side_effects=DATAFLOW),
    )(*[hbm(a) for a in arrs], *[hbm(lax.empty(s, a.dtype)) for s, a in zip(land_shapes, arrs)], after)
    return outs[0], outs[1], list(outs[2:2 + n]), list(outs[2 + n:2 + 2 * n]), outs[-1], scatter


def _gather_chips_wait(started, after, *, name):
    send, recv, srcs, lands, _, scatter = started
    n = len(srcs)

    def body(*refs):
        for cp in _late_copies(refs[:n], refs[n:2 * n], refs[2 * n], refs[2 * n + 1], scatter):
            cp.wait_send()
            cp.wait_recv()

    outs = pl.pallas_call(
        body, name=name,
        out_shape=tuple(pltpu.HBM(a.shape, a.dtype) for a in srcs + lands),
        in_specs=[HBM_SPEC] * (2 * n) + [SEM_SPEC, SEM_SPEC, ANY], out_specs=tuple([HBM_SPEC] * (2 * n)),
        input_output_aliases={i: i for i in range(2 * n)},
        compiler_params=pltpu.CompilerParams(has_side_effects=DATAFLOW),
    )(*srcs, *lands, send, recv, after)
    return list(outs[n:])


def _scatter_chips(arrs, *, name):
    n = len(arrs)

    def body(*refs):
        ins, outs = refs[:n], refs[n:2 * n]
        send, recv = refs[2 * n:]
        (x, y, c), chips = _chip_peers()
        copies = []
        for a in range(n):
            for j, (px, py) in enumerate(chips):
                cp = pltpu.make_async_remote_copy(
                    src_ref=ins[a].at[2 * px + py], dst_ref=outs[a].at[j], send_sem=send.at[a, j],
                    recv_sem=recv.at[a, j], device_id=(px, py, c), device_id_type=MESH)
                cp.start()
                copies.append(cp)
        for cp in copies:
            cp.wait()

    return pl.pallas_call(
        body, name=name, in_specs=[ANY] * n, out_specs=[ANY] * n,
        out_shape=[jax.ShapeDtypeStruct((3,) + a.shape[1:], a.dtype) for a in arrs],
        scratch_shapes=[pltpu.SemaphoreType.DMA((n, 3)), pltpu.SemaphoreType.DMA((n, 3))],
    )(*arrs)


def _send_other_halves(arrs, *, name):
    n = len(arrs)
    hs = [a.shape[1] // 2 for a in arrs]

    def body(*refs):
        ins, outs = refs[:n], refs[n:2 * n]
        send, recv = refs[2 * n:]
        x, y, c = lax.axis_index("x"), lax.axis_index("y"), lax.axis_index("c")
        copies = []
        for a in range(n):
            cp = pltpu.make_async_remote_copy(
                src_ref=ins[a].at[pl.ds(0, 4), pl.ds((1 - c) * hs[a], hs[a])], dst_ref=outs[a], send_sem=send.at[a],
                recv_sem=recv.at[a], device_id=(x, y, 1 - c), device_id_type=MESH)
            cp.start()
            copies.append(cp)
        for cp in copies:
            cp.wait()

    return pl.pallas_call(
        body, name=name, in_specs=[ANY] * n, out_specs=[ANY] * n,
        out_shape=[jax.ShapeDtypeStruct((4, h) + a.shape[2:], a.dtype) for a, h in zip(arrs, hs)],
        scratch_shapes=[pltpu.SemaphoreType.DMA((n,)), pltpu.SemaphoreType.DMA((n,))],
    )(*arrs)


def _join_halves(arrs, *, name):
    n = len(arrs)

    def body(*refs):
        ins, outs = refs[:n], refs[n:2 * n]
        send, recv = refs[2 * n:]
        x, y, c = lax.axis_index("x"), lax.axis_index("y"), lax.axis_index("c")
        copies = []
        for a in range(n):
            h = ins[a].shape[0]
            cp = pltpu.make_async_remote_copy(
                src_ref=ins[a], dst_ref=outs[a].at[pl.ds(c * h, h)], send_sem=send.at[a], recv_sem=recv.at[a],
                device_id=(x, y, 1 - c), device_id_type=MESH)
            cp.start()
            copies.append(cp)
        for cp in copies:
            cp.wait()

    outs = pl.pallas_call(
        body, name=name, in_specs=[ANY] * n, out_specs=[ANY] * n,
        out_shape=[jax.ShapeDtypeStruct((2 * a.shape[0],) + a.shape[1:], a.dtype) for a in arrs],
        scratch_shapes=[pltpu.SemaphoreType.DMA((n,)), pltpu.SemaphoreType.DMA((n,))],
    )(*arrs)
    core = lax.axis_index("c")
    return [lax.dynamic_update_slice_in_dim(o, a, core * a.shape[0], 0) for o, a in zip(outs, arrs)]


def _gather_all(a, *, name):
    def body(in_ref, out_ref, send, recv):
        x, y, c = lax.axis_index("x"), lax.axis_index("y"), lax.axis_index("c")
        me = 4 * x + 2 * y + c
        copies = []
        for j in range(1, 8):
            fx, fy, fc = (j >> 2) & 1, (j >> 1) & 1, j & 1
            peer = (x ^ fx, y ^ fy, c ^ fc)
            cp = pltpu.make_async_remote_copy(
                src_ref=in_ref, dst_ref=out_ref.at[me], send_sem=send.at[j - 1], recv_sem=recv.at[j - 1],
                device_id=peer, device_id_type=MESH)
            cp.start()
            copies.append(cp)
        for cp in copies:
            cp.wait()

    out = pl.pallas_call(
        body, name=name, in_specs=[ANY], out_specs=ANY,
        out_shape=jax.ShapeDtypeStruct((8,) + a.shape, a.dtype),
        scratch_shapes=[pltpu.SemaphoreType.DMA((7,)), pltpu.SemaphoreType.DMA((7,))],
    )(a)
    me = 4 * lax.axis_index("x") + 2 * lax.axis_index("y") + lax.axis_index("c")
    return lax.dynamic_update_index_in_dim(out, a, me, 0)


def _pack(arrs):
    flat = jnp.concatenate([a.astype(F32).reshape(-1) for a in arrs])
    n = flat.shape[0]
    pad = (-n) % (16 * LANE)
    return jnp.pad(flat, (0, pad)).reshape(-1, LANE)


def _unpack(buf, shapes):
    flat = buf.reshape(-1)
    out, off = [], 0
    for s in shapes:
        n = int(np.prod(s))
        out.append(flat[off:off + n].reshape(s))
        off += n
    return out


def _to_shards(g, axis):
    parts = jnp.split(g, 4, axis=axis)
    return jnp.stack([p.reshape(-1, p.shape[-1]) for p in parts])


def kernel(x, meta, ln_g, ln_b, gla_w_in, gla_w_g2, gla_b_g2, gla_norm_g, gla_w_out, kv_w, kv_bf, fox_w_in, fox_w_out, ffn_w_up, ffn_conv_w, ffn_conv_b, ffn_w_down, loss_target, m_meta, m_ln_g, m_ln_b, m_gla_w_in, m_gla_w_g2, m_gla_b_g2, m_gla_norm_g, m_gla_w_out, m_kv_w, m_kv_bf, m_fox_w_in, m_fox_w_out, m_ffn_w_up, m_ffn_conv_w, m_ffn_conv_b, m_ffn_w_down, v_meta, v_ln_g, v_ln_b, v_gla_w_in, v_gla_w_g2, v_gla_b_g2, v_gla_norm_g, v_gla_w_out, v_kv_w, v_kv_bf, v_fox_w_in, v_fox_w_out, v_ffn_w_up, v_ffn_conv_w, v_ffn_conv_b, v_ffn_w_down):
    D = x.shape[-1]
    L = x.shape[1] + FRONT
    HG = GLA_HEADS
    DK, DV = D // 2, D
    HK, HV = DK // HG, DV // HG
    HF = D // FOX_HD
    DFF = ffn_w_down.shape[1] * 4
    chip = 2 * lax.axis_index("x") + lax.axis_index("y")

    big_names = ["gla_w_in", "gla_w_out", "kv_w", "fox_w_in", "fox_w_out", "ffn_w_up", "ffn_w_down"]
    big = dict(gla_w_in=gla_w_in, gla_w_out=gla_w_out, kv_w=kv_w, fox_w_in=fox_w_in, fox_w_out=fox_w_out,
               ffn_w_up=ffn_w_up, ffn_w_down=ffn_w_down)
    big_axis = dict(gla_w_in=2, gla_w_out=1, kv_w=1, fox_w_in=2, fox_w_out=1, ffn_w_up=2, ffn_w_down=1)
    small_names = ["meta", "ln_g", "ln_b", "gla_w_g2", "gla_b_g2", "gla_norm_g", "ffn_conv_w"]
    small = dict(meta=meta, ln_g=ln_g, ln_b=ln_b, gla_w_g2=gla_w_g2, gla_b_g2=gla_b_g2, gla_norm_g=gla_norm_g,
                 ffn_conv_w=ffn_conv_w)
    small_shapes = [small[k].shape for k in small_names]
    bf = lambda a: a.astype(BF16)
    g0 = _gather_chips([bf(gla_w_in[0]), bf(gla_w_out[0]), _pack([small[k] for k in small_names])], name="gather_weights")
    src0 = [bf(ffn_w_up[0]), bf(ffn_w_down[0])]
    src1 = [bf(gla_w_in[1]), bf(gla_w_out[1]), bf(ffn_w_up[1]), bf(ffn_w_down[1])]
    src2 = [bf(kv_w), bf(fox_w_in), bf(fox_w_out), bf(ffn_w_up[2:]), bf(ffn_w_down[2:])]
    late0 = _gather_chips_start(src0, g0[-1], name="gather_ffn0_start")
    late1 = _gather_chips_start(src1, late0[4], name="gather_layer1_start")
    late2 = _gather_chips_start(src2, late1[4], name="gather_layer23_start")
    sm_sh = [_unpack(g0[-1][s], small_shapes) for s in range(4)]
    fs = {k: jnp.concatenate([sm_sh[s][i] for s in range(4)], axis=-1) for i, k in enumerate(small_names)}

    pad_cols = lambda w: jnp.pad(w, ((0, 0), (0, LANE - w.shape[1])))
    cat = lambda parts, axis: jnp.concatenate(parts, axis=axis)

    def gla_in_matrix(w):
        return cat([w[:, :2 * DK + DV], w[:, 2 * DK + DV + GLA_RANK:], pad_cols(w[:, 2 * DK + DV:2 * DK + DV + GLA_RANK])], 1)

    def ffn_matrices(up, down):
        return dict(u=cat([up(0), up(1)], 1), g=cat([up(2), up(3)], 1), d=cat([down(s) for s in range(4)], 0))

    WL = [None] * DEPTH
    WL[0] = dict(P=gla_in_matrix(cat([g0[0][s] for s in range(4)], 1)), go=cat([g0[1][s] for s in range(4)], 0))
    w2p = [jnp.pad(fs["gla_w_g2"][l], ((0, LANE - GLA_RANK), (0, 0))).astype(BF16) for l in range(N_A_LAYERS)]
    bf_pad = jnp.pad(kv_bf, (0, LANE - HF)).reshape(1, LANE)
    cw_u = [fs["ffn_conv_w"][l][:, :DFF] for l in range(DEPTH)]
    cw_g = [fs["ffn_conv_w"][l][:, DFF:] for l in range(DEPTH)]
    cb_u = [ffn_conv_b[l][None, :DFF] for l in range(DEPTH)]
    cb_g = [ffn_conv_b[l][None, DFF:] for l in range(DEPTH)]
    gl_blk = (2 * DK + 2 * DV) // LANE
    r_blk = (2 * DK + DV) // DV

    h = jnp.concatenate([jnp.concatenate([jnp.zeros((N_PAD, D), F32), fs["meta"]], axis=0), x[0]], axis=0)
    hb = (h + late2[4][0, 0]).astype(BF16)
    saved = []
    kvs = None
    for l in range(DEPTH):
        if l == 1:
            lands = _gather_chips_wait(late1, h, name="gather_layer1_wait")
            Lgi, Lgo, Lup, Ldn = [lax.dynamic_update_index_in_dim(o, a, chip, 0) for o, a in zip(lands, src1)]
            WL[1] = dict(P=gla_in_matrix(cat([Lgi[s] for s in range(4)], 1)), go=cat([Lgo[s] for s in range(4)], 0),
                         **ffn_matrices(lambda s: Lup[s], lambda s: Ldn[s]))
        if l == N_A_LAYERS:
            lands = _gather_chips_wait(late2, h, name="gather_layer23_wait")
            Lkv, Lfi, Lfo, Lup2, Ldn2 = [lax.dynamic_update_index_in_dim(o, a, chip, 0) for o, a in zip(lands, src2)]
            for ll in range(N_A_LAYERS, DEPTH):
                jj = ll - N_A_LAYERS
                WL[ll] = dict(fi=cat([Lfi[s][jj] for s in range(4)], 1), fo=cat([Lfo[s][jj] for s in range(4)], 0),
                              **ffn_matrices(lambda s, jj=jj: Lup2[s][jj], lambda s, jj=jj: Ldn2[s][jj]))
            kv_full = cat([Lkv[s] for s in range(4)], 1)
            W_kv = kv_full[:, :2 * D]
            W_f = pad_cols(kv_full[:, 2 * D:])
            W_kvf = cat([W_kv, W_f], 1)
        s = dict(h=h, hb=hb)
        if l < N_A_LAYERS:
            s["P"] = _mm(hb, WL[l]["P"], name=f"gla_in_{l}")
            s["la"] = _gla_gate_fwd(s["P"], w2p[l], fs["gla_b_g2"][l][None], gl_blk, name=f"gla_gate_{l}")
            s["o"], s["S"] = _gla_chunk_fwd(s["P"], s["la"], HG, HK, HV, name=f"gla_chunk_{l}")
            s["gated"] = _gla_post_fwd(s["o"], s["P"], fs["gla_norm_g"][l][None], HG, HV, r_blk, name=f"gla_post_{l}")
            s["mix"] = _mm(s["gated"], WL[l]["go"], name=f"gla_out_{l}")
        else:
            j = l - N_A_LAYERS
            if kvs is None:
                KV = _mm(hb, W_kv, out_dtype=BF16, name="kv_proj")
                f = _mm(hb, W_f, name="kv_gate_proj")
                c = _fox_c_fwd(f, bf_pad, name="fox_c")
                KA, VA = _fox_prep_kv(KV, c, HF, name="fox_prep_kv")
                kvs = dict(KA=KA, KT=KA.T, VA=VA, f=f, c=c, hb=hb)
            s["QO"] = _mm(hb, WL[l]["fi"], name=f"fox_in_{j}")
            s["QA"] = _fox_prep_q(s["QO"], kvs["c"], HF, name=f"fox_prep_q_{j}")
            s["o"], s["lse"] = _fox_attn_fwd(s["QA"], kvs["KT"], kvs["VA"], HF, name=f"fox_attn_{j}")
            s["gated"] = _fox_gate_fwd(s["o"], s["QO"], name=f"fox_gate_{j}")
            s["mix"] = _mm(s["gated"], WL[l]["fo"], name=f"fox_out_{j}")
        if l == 0:
            lands = _gather_chips_wait(late0, s["mix"], name="gather_ffn0_wait")
            Lup0, Ldn0 = [lax.dynamic_update_index_in_dim(o, a, chip, 0) for o, a in zip(lands, src0)]
            WL[0].update(ffn_matrices(lambda s_: Lup0[s_], lambda s_: Ldn0[s_]))
        s["h1"], s["h1b"] = _ln_fwd(h, s["mix"], fs["ln_g"][l, 0][None], fs["ln_b"][l, 0][None], name=f"ln_a_{l}")
        s["Uu"] = _mm(s["h1b"], WL[l]["u"], name=f"ffn_up_u_{l}")
        s["Ug"] = _mm(s["h1b"], WL[l]["g"], name=f"ffn_up_g_{l}")
        s["a"] = _conv_act_fwd(s["Uu"], s["Ug"], cw_u[l], cw_g[l], cb_u[l], cb_g[l], name=f"ffn_conv_{l}")
        s["ffn"] = _mm(s["a"], WL[l]["d"], name=f"ffn_down_{l}")
        h, hb = _ln_fwd(s["h1"], s["ffn"], fs["ln_g"][l, 1][None], fs["ln_b"][l, 1][None], name=f"ln_b_{l}")
        saved.append(s)

    loss_acc, dh = _loss(h, loss_target[0], name="loss")

    gW = {}
    d_ln_g = [[None, None] for _ in range(DEPTH)]
    d_ln_b = [[None, None] for _ in range(DEPTH)]
    d_cw, d_cb = [None] * DEPTH, [None] * DEPTH
    d_wg2, d_bg2, d_ng = [None] * N_A_LAYERS, [None] * N_A_LAYERS, [None] * N_A_LAYERS
    dkv = (jnp.zeros((L, HF * AUG), F32), jnp.zeros((L, HF * AUG), F32))
    dcqs = []
    core = lax.axis_index("c")
    stack = lambda k, lo, hi: jnp.stack([gW[(k, i)] for i in range(lo, hi)])

    def pair_sums(entries, tag):
        g4 = [_to_shards(g, ax) for _, g, ax in entries]
        sib = _send_other_halves(g4, name=f"pair_exchange_{tag}")
        out = []
        for (key, _, _), a, b in zip(entries, g4, sib):
            _, R, C = a.shape
            mine = lax.dynamic_slice_in_dim(a, core * (R // 2), R // 2, axis=1)
            out.append(_sum_parts([mine.reshape(2 * R, C), b.reshape(2 * R, C)], out_dtype=BF16,
                                  name=f"sum_pair_{key}").reshape(4, R // 2, C))
        return out

    def chip_sums(entries, pair, recv):
        return [_sum_parts([lax.dynamic_index_in_dim(p, chip, axis=0, keepdims=False), r[0], r[1], r[2]],
                           name=f"sum_chips_{key}") for (key, _, _), p, r in zip(entries, pair, recv)]

    ln_tok = 0.0
    for l in reversed(range(DEPTH)):
        s = saved[l]
        dz, dzb, d_ln_g[l][1], d_ln_b[l][1] = _ln_bwd(dh, s["h1"], s["ffn"], fs["ln_g"][l, 1][None] + ln_tok,
                                                      name=f"ln_b_bwd_{l}")
        dA = _mm(dzb, WL[l]["d"], tb=True, name=f"ffn_down_dx_{l}")
        gW[("ffn_w_down", l)] = _mm(s["a"], dzb, ta=True, out_dtype=BF16, name=f"ffn_down_dw_{l}")
        dcu, dcg, dwu, dwg, dbu, dbg = _conv_act_bwd(s["Uu"], s["Ug"], cw_u[l], cw_g[l], cb_u[l], cb_g[l], dA,
                                                     name=f"ffn_conv_bwd_{l}")
        d_cw[l] = jnp.concatenate([dwu, dwg], axis=1)
        d_cb[l] = jnp.concatenate([dbu, dbg], axis=1)[0]
        dUu = _conv_in_bwd(dcu, cw_u[l], name=f"ffn_conv_dx_u_{l}")
        dUg = _conv_in_bwd(dcg, cw_g[l], name=f"ffn_conv_dx_g_{l}")
        gW[("ffn_w_up", l)] = jnp.concatenate(
            [_mm(s["h1b"], dUu, ta=True, out_dtype=BF16, name=f"ffn_up_dw_u_{l}"),
             _mm(s["h1b"], dUg, ta=True, out_dtype=BF16, name=f"ffn_up_dw_g_{l}")], axis=1)
        dh1 = _mm(dUu, WL[l]["u"], tb=True, add=dz, add_scale=ALPHA, name=f"ffn_up_dx_u_{l}")
        dh1 = _mm(dUg, WL[l]["g"], tb=True, add=dh1, name=f"ffn_up_dx_g_{l}")
        dz, dzb, d_ln_g[l][0], d_ln_b[l][0] = _ln_bwd(dh1, s["h"], s["mix"], fs["ln_g"][l, 0][None], name=f"ln_a_bwd_{l}")
        if l < N_A_LAYERS:
            dgated = _mm(dzb, WL[l]["go"], tb=True, name=f"gla_out_dx_{l}")
            gW[("gla_w_out", l)] = _mm(s["gated"], dzb, ta=True, out_dtype=BF16, name=f"gla_out_dw_{l}")
            do, drb, d_ng[l] = _gla_post_bwd(dgated, s["o"], s["P"], fs["gla_norm_g"][l][None], HG, HV, r_blk,
                                             name=f"gla_post_bwd_{l}")
            dq, dk, dvb, dla = _gla_chunk_bwd(s["P"], s["la"], s["S"], do, HG, HK, HV, name=f"gla_chunk_bwd_{l}")
            dglb, dw2, d_bg2[l] = _gla_gate_bwd(dla, s["P"], w2p[l], fs["gla_b_g2"][l][None], gl_blk,
                                               name=f"gla_gate_bwd_{l}")
            d_wg2[l] = dw2[:GLA_RANK]
            dP = jnp.concatenate([dq, dk, dvb, drb, dglb], axis=1)
            gP = _mm(s["hb"], dP, ta=True, out_dtype=BF16, name=f"gla_in_dw_{l}")
            gW[("gla_w_in", l)] = jnp.concatenate(
                [gP[:, :2 * DK + DV], gP[:, 2 * DK + 2 * DV:2 * DK + 2 * DV + GLA_RANK], gP[:, 2 * DK + DV:2 * DK + 2 * DV]],
                axis=1)
            dh = _mm(dP, WL[l]["P"], tb=True, add=dz, add_scale=ALPHA, name=f"gla_in_dx_{l}")
        else:
            j = l - N_A_LAYERS
            dgo = _mm(dzb, WL[l]["fo"], tb=True, name=f"fox_out_dx_{j}")
            gW[("fox_w_out", j)] = _mm(s["gated"], dzb, ta=True, out_dtype=BF16, name=f"fox_out_dw_{j}")
            DOA, dogb = _fox_gate_bwd(dgo, s["o"], s["QO"], HF, name=f"fox_gate_bwd_{j}")
            DQA, DKA, DVA = _fox_attn_bwd(s["QA"], kvs["KA"], kvs["VA"], DOA, s["lse"], dkv, HF, name=f"fox_attn_bwd_{j}")
            dkv = (DKA, DVA)
            dqb, dcq = _fox_post_q(DQA, HF, name=f"fox_post_q_{j}")
            dcqs.append(dcq)
            dQO = jnp.concatenate([dqb, dogb], axis=1)
            gW[("fox_w_in", j)] = _mm(s["hb"], dQO, ta=True, out_dtype=BF16, name=f"fox_in_dw_{j}")
            dh = _mm(dQO, WL[l]["fi"], tb=True, add=dz, add_scale=ALPHA, name=f"fox_in_dx_{j}")
            if j == 0:
                dkvb, dck = _fox_post_kv(DKA, DVA, HF, name="fox_post_kv")
                dfb, d_bf = _fox_c_bwd(dcqs + [dck], kvs["f"], bf_pad, name="fox_c_bwd")
                dKVF = jnp.concatenate([dkvb, dfb], axis=1)
                gkv = _mm(kvs["hb"], dKVF, ta=True, out_dtype=BF16, name="kv_dw")
                gW[("kv_w", 0)] = gkv[:, :2 * D + HF]
                dh = _mm(dKVF, W_kvf, tb=True, add=dh, name="kv_dx")
                late_entries = [("fox_w_in", stack("fox_w_in", 0, 2), 2), ("fox_w_out", stack("fox_w_out", 0, 2), 1),
                                ("kv_w", gW[("kv_w", 0)], 1), ("ffn_w_up_hi", stack("ffn_w_up", N_A_LAYERS, DEPTH), 2),
                                ("ffn_w_down_hi", stack("ffn_w_down", N_A_LAYERS, DEPTH), 1)]
                pair_late = pair_sums(late_entries, "late")
                sc_late = _gather_chips_start(pair_late, dh, name="scatter_late_start", scatter=True)
                ln_tok = sc_late[4][0, 0]

    early_entries = [("gla_w_in", stack("gla_w_in", 0, N_A_LAYERS), 2), ("gla_w_out", stack("gla_w_out", 0, N_A_LAYERS), 1),
                     ("ffn_w_up_lo", stack("ffn_w_up", 0, N_A_LAYERS), 2), ("ffn_w_down_lo", stack("ffn_w_down", 0, N_A_LAYERS), 1)]
    pair_early = pair_sums(early_entries, "early")
    sc_early = _gather_chips_start(pair_early, dh, name="scatter_early_start", scatter=True)
    recv_late = _gather_chips_wait(sc_late, sc_early[4], name="scatter_late_wait")
    joined_late = _join_halves(chip_sums(late_entries, pair_late, recv_late), name="join_halves_late")
    gsum = {key: g for (key, _, _), g in zip(late_entries, joined_late)}
    moments = dict(gla_w_in=(m_gla_w_in, v_gla_w_in), gla_w_out=(m_gla_w_out, v_gla_w_out), kv_w=(m_kv_w, v_kv_w),
                   fox_w_in=(m_fox_w_in, v_fox_w_in), fox_w_out=(m_fox_w_out, v_fox_w_out),
                   ffn_w_up=(m_ffn_w_up, v_ffn_w_up), ffn_w_down=(m_ffn_w_down, v_ffn_w_down))
    res = {}

    def apply_adamw(k):
        w = big[k]
        sh = w.shape
        flat = lambda a: a.reshape(-1, sh[-1])
        split = k + "_lo" in gsum
        outs = _adamw(flat(w), flat(moments[k][0]), flat(moments[k][1]),
                      [gsum[k + "_lo"], gsum[k + "_hi"]] if split else [gsum[k]], stacked=split, name=f"adamw_{k}")
        res[k] = [o.reshape(sh) for o in outs]

    for k in ("fox_w_in", "fox_w_out", "kv_w"):
        apply_adamw(k)

    dmeta = dh[N_PAD:FRONT]
    sg = dict(meta=dmeta,
              ln_g=jnp.stack([jnp.concatenate(d_ln_g[l], axis=0) for l in range(DEPTH)]),
              ln_b=jnp.stack([jnp.concatenate(d_ln_b[l], axis=0) for l in range(DEPTH)]),
              gla_w_g2=jnp.stack(d_wg2), gla_b_g2=jnp.stack([d[0] for d in d_bg2]),
              gla_norm_g=jnp.stack([d[0] for d in d_ng]), ffn_conv_w=jnp.stack(d_cw),
              kv_bf=d_bf[0, :HF], ffn_conv_b=jnp.stack(d_cb), loss=loss_acc[0, :1])
    sg_names = small_names + ["kv_bf", "ffn_conv_b", "loss"]
    sg_shapes = [sg[k].shape for k in sg_names]
    red = _sum_slots(_gather_all(_pack([sg[k] for k in sg_names]), name="gather_small_grads"), name="sum_small_grads")
    red = dict(zip(sg_names, _unpack(red, sg_shapes)))
    loss = red["loss"][0]
    loc = {}
    for k in small_names:
        wdt = small[k].shape[-1]
        loc[k] = lax.dynamic_slice_in_dim(red[k], chip * wdt, wdt, axis=red[k].ndim - 1)
    loc["kv_bf"] = red["kv_bf"]
    loc["ffn_conv_b"] = red["ffn_conv_b"]
    sm_all = small_names + ["kv_bf", "ffn_conv_b"]
    sw = dict(small, kv_bf=kv_bf, ffn_conv_b=ffn_conv_b)
    sm_m = dict(meta=m_meta, ln_g=m_ln_g, ln_b=m_ln_b, gla_w_g2=m_gla_w_g2, gla_b_g2=m_gla_b_g2, gla_norm_g=m_gla_norm_g,
                ffn_conv_w=m_ffn_conv_w, kv_bf=m_kv_bf, ffn_conv_b=m_ffn_conv_b)
    sm_v = dict(meta=v_meta, ln_g=v_ln_g, ln_b=v_ln_b, gla_w_g2=v_gla_w_g2, gla_b_g2=v_gla_b_g2, gla_norm_g=v_gla_norm_g,
                ffn_conv_w=v_ffn_conv_w, kv_bf=v_kv_bf, ffn_conv_b=v_ffn_conv_b)
    shapes_loc = [sw[k].shape for k in sm_all]
    outs = _adamw(_pack([sw[k] for k in sm_all]), _pack([sm_m[k] for k in sm_all]), _pack([sm_v[k] for k in sm_all]),
                  [_pack([loc[k] for k in sm_all])], name="adamw_small")
    outs = [_unpack(o, shapes_loc) for o in outs]
    for i, k in enumerate(sm_all):
        res[k] = [outs[q][i] for q in range(4)]

    behind = jnp.zeros((8, LANE), F32) + res["kv_w"][0][:1, :1] + res["kv_bf"][0][:1]
    recv_early = _gather_chips_wait(sc_early, behind, name="scatter_early_wait")
    joined_early = _join_halves(chip_sums(early_entries, pair_early, recv_early), name="join_halves_early")
    gsum.update({key: g for (key, _, _), g in zip(early_entries, joined_early)})
    for k in ("gla_w_in", "gla_w_out", "ffn_w_up", "ffn_w_down"):
        apply_adamw(k)

    order = ["meta", "ln_g", "ln_b", "gla_w_in", "gla_w_g2", "gla_b_g2", "gla_norm_g", "gla_w_out", "kv_w", "kv_bf",
             "fox_w_in", "fox_w_out", "ffn_w_up", "ffn_conv_w", "ffn_conv_b", "ffn_w_down"]
    grad_x = dh[FRONT:][None]
    return (loss, grad_x, *[res[k][0] for k in order], *[res[k][1] for k in order], *[res[k][2] for k in order],
            *[res[k][3] for k in order])
```

```python
import functools
import math

import numpy as np
import jax
import jax.numpy as jnp
from jax import lax
from jax.experimental import pallas as pl
from jax.experimental.pallas import tpu as pltpu

F32 = jnp.float32
BF16 = jnp.bfloat16
HIGHEST = lax.Precision.HIGHEST

DEPTH = 4
N_A_LAYERS = DEPTH // 2
N_META = 16
FRONT = 128
N_PAD = FRONT - N_META
ALPHA = (2.0 * DEPTH) ** 0.25
LN_EPS = 1e-5
GLA_HEADS = 4
GLA_RANK = 16
GLA_TAU = 16.0
GLA_CHUNK = 64
GLA_GROUP = 2
FOX_HD = 128
LANE = 128
ADAM_LR = 0.001
ADAM_B1 = 0.9
ADAM_B2 = 0.999
ADAM_EPS = 1e-08
ADAM_WD = 0.01
ADAM_STEP = 10
NEG = -(2.0 ** 100)
VMEM_LIMIT = 50 * 1024 * 1024
MESH = pl.DeviceIdType.MESH
ANY = pl.BlockSpec(memory_space=pl.ANY)


def _tile(n, pref, align):
    best = None
    t = align
    while t <= min(n, pref):
        if n % t == 0:
            best = t
        t += align
    return best if best is not None else n


def _cp(*sem):
    return pltpu.CompilerParams(dimension_semantics=sem, vmem_limit_bytes=VMEM_LIMIT)


def _dot(a, b, ca, cb, precision=None):
    return lax.dot_general(a, b, (((ca,), (cb,)), ((), ())), precision=precision,
                           preferred_element_type=F32)


def _sigmoid(x):
    return 1.0 / (1.0 + jnp.exp(-x))


def _log_sigmoid(z):
    return jnp.minimum(z, 0.0) - jnp.log(1.0 + jnp.exp(-jnp.abs(z)))


def _rows(i, tr, n=None):
    n = tr if n is None else n
    return i * tr + lax.broadcasted_iota(jnp.int32, (n, 1), 0)


def _mm(a, b, *, ta=False, tb=False, out_dtype=F32, add=None, add_scale=1.0, name):
    if ta:
        K, M = a.shape
    else:
        M, K = a.shape
    if tb:
        N, K2 = b.shape
    else:
        K2, N = b.shape
    assert K == K2, (a.shape, b.shape, ta, tb)
    tm = _tile(M, 1024, LANE) if ta else _tile(M, 1040, 16)
    tn = _tile(N, 1024, LANE)
    tk = _tile(K, 2048, LANE if ((not ta) or tb) else 16)
    nk = K // tk
    ca = 0 if ta else 1
    cb = 1 if tb else 0

    def body(*refs):
        if add is None:
            a_ref, b_ref, o_ref = refs[:3]
            add_ref = None
        else:
            a_ref, b_ref, add_ref, o_ref = refs[:4]
        part = _dot(a_ref[...].astype(BF16), b_ref[...].astype(BF16), ca, cb)

        def finish(acc):
            if add_ref is not None:
                acc = acc + add_scale * add_ref[...]
            o_ref[...] = acc.astype(out_dtype)

        if nk == 1:
            finish(part)
        else:
            acc_ref = refs[-1]
            k = pl.program_id(2)

            @pl.when(k == 0)
            def _():
                acc_ref[...] = part

            @pl.when(k > 0)
            def _():
                acc_ref[...] += part

            @pl.when(k == nk - 1)
            def _():
                finish(acc_ref[...])

    a_spec = pl.BlockSpec((tk, tm), lambda i, j, k: (k, i)) if ta else pl.BlockSpec((tm, tk), lambda i, j, k: (i, k))
    b_spec = pl.BlockSpec((tn, tk), lambda i, j, k: (j, k)) if tb else pl.BlockSpec((tk, tn), lambda i, j, k: (k, j))
    o_spec = pl.BlockSpec((tm, tn), lambda i, j, k: (i, j))
    in_specs = [a_spec, b_spec] + ([o_spec] if add is not None else [])
    args = (a, b) + ((add,) if add is not None else ())
    return pl.pallas_call(
        body, name=name, grid=(M // tm, N // tn, nk), in_specs=in_specs, out_specs=o_spec,
        out_shape=jax.ShapeDtypeStruct((M, N), out_dtype),
        scratch_shapes=[pltpu.VMEM((tm, tn), F32)] if nk > 1 else [],
        compiler_params=_cp("parallel", "parallel", "arbitrary"),
    )(*args)


def _ln_stats(h, mix):
    z = ALPHA * h + mix
    mu = jnp.mean(z, axis=-1, keepdims=True)
    zc = z - mu
    var = jnp.mean(zc * zc, axis=-1, keepdims=True)
    rstd = lax.rsqrt(var + LN_EPS)
    return zc * rstd, rstd


def _ln_fwd(h, mix, g, b, *, name):
    L, D = h.shape
    tr = _tile(L, 160, 16)

    def body(h_ref, m_ref, g_ref, b_ref, o_ref, ob_ref):
        xhat, _ = _ln_stats(h_ref[...], m_ref[...])
        y = xhat * g_ref[...] + b_ref[...]
        o_ref[...] = y
        ob_ref[...] = y.astype(BF16)

    row = pl.BlockSpec((tr, D), lambda i: (i, 0))
    vec = pl.BlockSpec((1, D), lambda i: (0, 0))
    return pl.pallas_call(
        body, name=name, grid=(L // tr,), in_specs=[row, row, vec, vec], out_specs=[row, row],
        out_shape=[jax.ShapeDtypeStruct((L, D), F32), jax.ShapeDtypeStruct((L, D), BF16)],
        compiler_params=_cp("parallel"),
    )(h, mix, g, b)


def _ln_bwd(dy, h, mix, g, *, name):
    L, D = h.shape
    tr = _tile(L, 160, 16)

    def body(dy_ref, h_ref, m_ref, g_ref, dz_ref, dzb_ref, dg_ref, db_ref):
        i = pl.program_id(0)
        xhat, rstd = _ln_stats(h_ref[...], m_ref[...])
        dy = dy_ref[...]
        dxh = dy * g_ref[...]
        m1 = jnp.mean(dxh, axis=-1, keepdims=True)
        m2 = jnp.mean(dxh * xhat, axis=-1, keepdims=True)
        dz = rstd * (dxh - m1 - xhat * m2)
        dz_ref[...] = dz
        dzb_ref[...] = dz.astype(BF16)
        pg = jnp.sum(dy * xhat, axis=0, keepdims=True)
        pb = jnp.sum(dy, axis=0, keepdims=True)

        @pl.when(i == 0)
        def _():
            dg_ref[...] = pg
            db_ref[...] = pb

        @pl.when(i > 0)
        def _():
            dg_ref[...] += pg
            db_ref[...] += pb

    row = pl.BlockSpec((tr, D), lambda i: (i, 0))
    vec = pl.BlockSpec((1, D), lambda i: (0, 0))
    return pl.pallas_call(
        body, name=name, grid=(L // tr,), in_specs=[row, row, row, vec], out_specs=[row, row, vec, vec],
        out_shape=[jax.ShapeDtypeStruct((L, D), F32), jax.ShapeDtypeStruct((L, D), BF16),
                   jax.ShapeDtypeStruct((1, D), F32), jax.ShapeDtypeStruct((1, D), F32)],
        compiler_params=_cp("arbitrary"),
    )(dy, h, mix, g)


def _loss(h, target, *, name):
    L, D = h.shape
    tr = FRONT

    def body(h_ref, t_ref, acc_ref, dy_ref):
        i = pl.program_id(0)
        e = jnp.where(i >= 1, h_ref[...] - t_ref[...], 0.0)
        dy_ref[...] = e * (1.0 / D)
        part = 0.5 * jnp.sum(jnp.sum(e * e, axis=-1, keepdims=True) * (1.0 / D), axis=0, keepdims=True)

        @pl.when(i == 0)
        def _():
            acc_ref[...] = jnp.zeros_like(acc_ref)

        acc_ref[...] += jnp.broadcast_to(part, acc_ref.shape)

    return pl.pallas_call(
        body, name=name, grid=(L // tr,),
        in_specs=[pl.BlockSpec((tr, D), lambda i: (i, 0)),
                  pl.BlockSpec((tr, D), lambda i: (jnp.maximum(i - 1, 0), 0))],
        out_specs=[pl.BlockSpec((8, LANE), lambda i: (0, 0)), pl.BlockSpec((tr, D), lambda i: (i, 0))],
        out_shape=[jax.ShapeDtypeStruct((8, LANE), F32), jax.ShapeDtypeStruct((L, D), F32)],
        compiler_params=_cp("arbitrary"),
    )(h, target)


def _gla_gate_fwd(P, w2p, b2, gl_blk, *, name):
    L = P.shape[0]
    DK = w2p.shape[1]
    tr = _tile(L, 640, 16)

    def body(gl_ref, w_ref, b_ref, o_ref):
        i = pl.program_id(0)
        z = _dot(gl_ref[...].astype(BF16), w_ref[...], 1, 0) + b_ref[...]
        la = _log_sigmoid(z) * (1.0 / GLA_TAU)
        o_ref[...] = jnp.where(_rows(i, tr) >= N_PAD, la, 0.0)

    return pl.pallas_call(
        body, name=name, grid=(L // tr,),
        in_specs=[pl.BlockSpec((tr, LANE), lambda i: (i, gl_blk)),
                  pl.BlockSpec((LANE, DK), lambda i: (0, 0)), pl.BlockSpec((1, DK), lambda i: (0, 0))],
        out_specs=pl.BlockSpec((tr, DK), lambda i: (i, 0)),
        out_shape=jax.ShapeDtypeStruct((L, DK), F32), compiler_params=_cp("parallel"),
    )(P, w2p, b2)


def _gla_gate_bwd(dla, P, w2p, b2, gl_blk, *, name):
    L = P.shape[0]
    DK = w2p.shape[1]
    tr = _tile(L, 640, 16)

    def body(dla_ref, gl_ref, w_ref, b_ref, dgl_ref, dw_ref, db_ref):
        i = pl.program_id(0)
        glb = gl_ref[...].astype(BF16)
        z = _dot(glb, w_ref[...], 1, 0) + b_ref[...]
        dz = jnp.where(_rows(i, tr) >= N_PAD, dla_ref[...], 0.0) * (1.0 / GLA_TAU) * _sigmoid(-z)
        dzb = dz.astype(BF16)
        dgl_ref[...] = _dot(dzb, w_ref[...], 1, 1).astype(BF16)
        pw = _dot(glb, dzb, 0, 0)
        pb = jnp.sum(dz, axis=0, keepdims=True)

        @pl.when(i == 0)
        def _():
            dw_ref[...] = pw
            db_ref[...] = pb

        @pl.when(i > 0)
        def _():
            dw_ref[...] += pw
            db_ref[...] += pb

    return pl.pallas_call(
        body, name=name, grid=(L // tr,),
        in_specs=[pl.BlockSpec((tr, DK), lambda i: (i, 0)), pl.BlockSpec((tr, LANE), lambda i: (i, gl_blk)),
                  pl.BlockSpec((LANE, DK), lambda i: (0, 0)), pl.BlockSpec((1, DK), lambda i: (0, 0))],
        out_specs=[pl.BlockSpec((tr, LANE), lambda i: (i, 0)), pl.BlockSpec((LANE, DK), lambda i: (0, 0)),
                   pl.BlockSpec((1, DK), lambda i: (0, 0))],
        out_shape=[jax.ShapeDtypeStruct((L, LANE), BF16), jax.ShapeDtypeStruct((LANE, DK), F32),
                   jax.ShapeDtypeStruct((1, DK), F32)],
        compiler_params=_cp("arbitrary"),
    )(dla, P, w2p, b2)


def _chunk_terms(q, k, g, n, scale, HV):
    C = q.shape[0]
    ri = lax.broadcasted_iota(jnp.int32, (C, C), 0)
    ci = lax.broadcasted_iota(jnp.int32, (C, C), 1)
    tri = ri >= ci
    valid = _rows(n, C) >= N_PAD
    km = jnp.where(valid, k, 0.0)
    b = _dot(tri.astype(F32), g, 1, 0, precision=HIGHEST)
    bl_row = jnp.sum(g, axis=0, keepdims=True)
    bl_col = _dot(g, jnp.ones((C, HV), F32), 0, 0, precision=HIGHEST)
    eb = jnp.exp(b)
    enb = jnp.exp(-b)
    qe = q * scale * eb
    ke = km * enb
    ebl_row = jnp.exp(bl_row)
    kl = ke * ebl_row
    return dict(tri=tri, valid=valid, eb=eb, enb=enb, qe=qe, ke=ke, kl=kl, ebl_row=ebl_row,
                ebl_col=jnp.exp(bl_col), ri=ri, ci=ci)


def _gla_chunk_fwd(P, la, H, HK, HV, *, name):
    L = P.shape[0]
    C = GLA_CHUNK
    N = L // C
    scale = HK ** -0.5

    G = GLA_GROUP if H % GLA_GROUP == 0 else 1
    HG = H // G

    def body(q_ref, k_ref, v_ref, g_ref, o_ref, s_ref, S):
        n = pl.program_id(1)

        @pl.when(n == 0)
        def _():
            S[...] = jnp.zeros_like(S)

        for g in range(G):
            ks, vs = slice(g * HK, (g + 1) * HK), slice(g * HV, (g + 1) * HV)
            S0 = S[g]
            s_ref[g, 0] = S0
            t = _chunk_terms(q_ref[:, ks], k_ref[:, ks], g_ref[:, ks], n, scale, HV)
            vb = v_ref[:, vs].astype(BF16)
            qeb = t["qe"].astype(BF16)
            inter = _dot(qeb, S0.astype(BF16), 1, 0)
            att = jnp.where(t["tri"], _dot(qeb, t["ke"].astype(BF16), 1, 1), 0.0)
            o_ref[:, vs] = inter + _dot(att.astype(BF16), vb, 1, 0)
            S[g] = t["ebl_col"] * S0 + _dot(t["kl"].astype(BF16), vb, 0, 0)

    return pl.pallas_call(
        body, name=name, grid=(HG, N),
        in_specs=[pl.BlockSpec((C, G * HK), lambda h, n: (n, h)), pl.BlockSpec((C, G * HK), lambda h, n: (n, HG + h)),
                  pl.BlockSpec((C, G * HV), lambda h, n: (n, HG + h)), pl.BlockSpec((C, G * HK), lambda h, n: (n, h))],
        out_specs=[pl.BlockSpec((C, G * HV), lambda h, n: (n, h)),
                   pl.BlockSpec((G, 1, HK, HV), lambda h, n: (h, n, 0, 0))],
        out_shape=[jax.ShapeDtypeStruct((L, H * HV), F32), jax.ShapeDtypeStruct((H, N, HK, HV), F32)],
        scratch_shapes=[pltpu.VMEM((G, HK, HV), F32)],
        compiler_params=_cp("parallel", "arbitrary"),
    )(P, P, P, la)


def _gla_chunk_bwd(P, la, S_all, do, H, HK, HV, *, name):
    L = P.shape[0]
    C = GLA_CHUNK
    N = L // C
    scale = HK ** -0.5

    G = 1
    HG = H // G

    def body(q_ref, k_ref, v_ref, g_ref, s_ref, do_ref, dq_ref, dk_ref, dv_ref, dg_ref, dS):
        step = pl.program_id(1)
        n = N - 1 - step

        @pl.when(step == 0)
        def _():
            dS[...] = jnp.zeros_like(dS)

        for g in range(G):
            ks, vs = slice(g * HK, (g + 1) * HK), slice(g * HV, (g + 1) * HV)
            dS1 = dS[g]
            S0 = s_ref[g, 0]
            t = _chunk_terms(q_ref[:, ks], k_ref[:, ks], g_ref[:, ks], n, scale, HV)
            tri, qe, ke, kl = t["tri"], t["qe"], t["ke"], t["kl"]
            vb = v_ref[:, vs].astype(BF16)
            dob = do_ref[:, vs].astype(BF16)
            qeb, keb, dSb = qe.astype(BF16), ke.astype(BF16), dS1.astype(BF16)
            dA = jnp.where(tri, _dot(dob, vb, 1, 1), 0.0).astype(BF16)
            A = jnp.where(tri, _dot(qeb, keb, 1, 1), 0.0).astype(BF16)
            dqe = _dot(dob, S0.astype(BF16), 1, 1) + _dot(dA, keb, 1, 0)
            dkl = _dot(vb, dSb, 1, 1)
            dke = _dot(dA, qeb, 0, 0) + dkl * t["ebl_row"]
            dv_ref[:, vs] = (_dot(A, dob, 0, 0) + _dot(kl.astype(BF16), dSb, 1, 0)).astype(BF16)
            debl = (jnp.sum(_dot(jnp.ones((8, HV), F32), dS1 * S0, 1, 1, precision=HIGHEST), axis=0, keepdims=True) * 0.125
                    + jnp.sum(dkl * ke, axis=0, keepdims=True))
            dbl = debl * t["ebl_row"]
            db = dqe * qe - dke * ke + jnp.where(lax.broadcasted_iota(jnp.int32, (C, 1), 0) == C - 1, dbl, 0.0)
            triu = (t["ci"] >= t["ri"]).astype(F32)
            dq_ref[:, ks] = (dqe * t["eb"] * scale).astype(BF16)
            dk_ref[:, ks] = jnp.where(t["valid"], dke * t["enb"], 0.0).astype(BF16)
            dg_ref[:, ks] = _dot(triu, db, 1, 0, precision=HIGHEST)
            dS[g] = t["ebl_col"] * dS1 + _dot(qeb, dob, 0, 0)

    rev = lambda h, s: (N - 1 - s, h)
    return pl.pallas_call(
        body, name=name, grid=(HG, N),
        in_specs=[pl.BlockSpec((C, G * HK), rev), pl.BlockSpec((C, G * HK), lambda h, s: (N - 1 - s, HG + h)),
                  pl.BlockSpec((C, G * HV), lambda h, s: (N - 1 - s, HG + h)), pl.BlockSpec((C, G * HK), rev),
                  pl.BlockSpec((G, 1, HK, HV), lambda h, s: (h, N - 1 - s, 0, 0)), pl.BlockSpec((C, G * HV), rev)],
        out_specs=[pl.BlockSpec((C, G * HK), rev), pl.BlockSpec((C, G * HK), rev),
                   pl.BlockSpec((C, G * HV), rev), pl.BlockSpec((C, G * HK), rev)],
        out_shape=[jax.ShapeDtypeStruct((L, H * HK), BF16), jax.ShapeDtypeStruct((L, H * HK), BF16),
                   jax.ShapeDtypeStruct((L, H * HV), BF16), jax.ShapeDtypeStruct((L, H * HK), F32)],
        scratch_shapes=[pltpu.VMEM((G, HK, HV), F32)],
        compiler_params=_cp("parallel", "arbitrary"),
    )(P, P, P, la, S_all, do)


def _silu_parts(x):
    s = _sigmoid(x)
    return x * s, s * (1.0 + x * (1.0 - s))


def _gla_post_fwd(o, P, ng, H, HV, r_blk, *, name):
    L, DV = o.shape
    tr = _tile(L, 320, 16)

    def body(o_ref, r_ref, g_ref, out_ref):
        for hd in range(H):
            sl = slice(hd * HV, (hd + 1) * HV)
            oh = o_ref[:, sl]
            rr = lax.rsqrt(jnp.mean(oh * oh, axis=-1, keepdims=True) + LN_EPS)
            silu, _ = _silu_parts(r_ref[:, sl])
            out_ref[:, sl] = (oh * rr * g_ref[...] * silu).astype(BF16)

    return pl.pallas_call(
        body, name=name, grid=(L // tr,),
        in_specs=[pl.BlockSpec((tr, DV), lambda i: (i, 0)), pl.BlockSpec((tr, DV), lambda i: (i, r_blk)),
                  pl.BlockSpec((1, HV), lambda i: (0, 0))],
        out_specs=pl.BlockSpec((tr, DV), lambda i: (i, 0)),
        out_shape=jax.ShapeDtypeStruct((L, DV), BF16), compiler_params=_cp("parallel"),
    )(o, P, ng)


def _gla_post_bwd(dgated, o, P, ng, H, HV, r_blk, *, name):
    L, DV = o.shape
    tr = _tile(L, 320, 16)

    def body(d_ref, o_ref, r_ref, g_ref, do_ref, dr_ref, dng_ref):
        i = pl.program_id(0)
        png = jnp.zeros((1, HV), F32)
        for hd in range(H):
            sl = slice(hd * HV, (hd + 1) * HV)
            oh = o_ref[:, sl]
            d = d_ref[:, sl]
            rr = lax.rsqrt(jnp.mean(oh * oh, axis=-1, keepdims=True) + LN_EPS)
            yh = oh * rr
            silu, dsilu = _silu_parts(r_ref[:, sl])
            dn = d * silu
            dr_ref[:, sl] = (d * yh * g_ref[...] * dsilu).astype(BF16)
            png = png + jnp.sum(dn * yh, axis=0, keepdims=True)
            dyh = dn * g_ref[...]
            do_ref[:, sl] = rr * (dyh - yh * jnp.mean(dyh * yh, axis=-1, keepdims=True))

        @pl.when(i == 0)
        def _():
            dng_ref[...] = png

        @pl.when(i > 0)
        def _():
            dng_ref[...] += png

    row = pl.BlockSpec((tr, DV), lambda i: (i, 0))
    return pl.pallas_call(
        body, name=name, grid=(L // tr,),
        in_specs=[row, row, pl.BlockSpec((tr, DV), lambda i: (i, r_blk)), pl.BlockSpec((1, HV), lambda i: (0, 0))],
        out_specs=[row, row, pl.BlockSpec((1, HV), lambda i: (0, 0))],
        out_shape=[jax.ShapeDtypeStruct((L, DV), F32), jax.ShapeDtypeStruct((L, DV), BF16),
                   jax.ShapeDtypeStruct((1, HV), F32)],
        compiler_params=_cp("arbitrary"),
    )(dgated, o, P, ng)


def _shift_down(x, halo, s):
    if s == 0:
        return x
    tr = x.shape[0]
    xx = jnp.concatenate([halo, x], axis=0)
    return pltpu.roll(xx, s, axis=0)[8:8 + tr]


def _shift_up(x, halo, s):
    if s == 0:
        return x
    tr = x.shape[0]
    xx = jnp.concatenate([x, halo], axis=0)
    return pltpu.roll(xx, tr + 8 - s, axis=0)[0:tr]


def _conv_taps(x_ref, halo_ref, i, tr):
    x = jnp.where(_rows(i, tr) >= N_PAD, x_ref[...], 0.0)
    halo = jnp.where(i * tr - 8 + lax.broadcasted_iota(jnp.int32, (8, 1), 0) >= N_PAD, halo_ref[...], 0.0)
    return [_shift_down(x, halo, s) for s in range(3)]


def _conv_apply(taps, w_ref, b_ref):
    return taps[2] * w_ref[0:1, :] + taps[1] * w_ref[1:2, :] + taps[0] * w_ref[2:3, :] + b_ref[...]


def _conv_specs(tr, tc):
    blk = pl.BlockSpec((tr, tc), lambda j, i: (i, j))
    halo = pl.BlockSpec((8, tc), lambda j, i: (jnp.maximum(i * (tr // 8) - 1, 0), j))
    w = pl.BlockSpec((3, tc), lambda j, i: (0, j))
    b = pl.BlockSpec((1, tc), lambda j, i: (0, j))
    return blk, halo, w, b


def _conv_act_fwd(Uu, Ug, wu, wg, bu, bg, *, name):
    L, DFF = Uu.shape
    tr = _tile(L, 320, 16)
    tc = _tile(DFF, 512, LANE)

    def body(xu_ref, hu_ref, xg_ref, hg_ref, wu_ref, wg_ref, bu_ref, bg_ref, o_ref):
        i = pl.program_id(1)
        u = _conv_apply(_conv_taps(xu_ref, hu_ref, i, tr), wu_ref, bu_ref)
        g = _conv_apply(_conv_taps(xg_ref, hg_ref, i, tr), wg_ref, bg_ref)
        o_ref[...] = (_silu_parts(g)[0] * u).astype(BF16)

    blk, halo, w, b = _conv_specs(tr, tc)
    return pl.pallas_call(
        body, name=name, grid=(DFF // tc, L // tr),
        in_specs=[blk, halo, blk, halo, w, w, b, b], out_specs=blk,
        out_shape=jax.ShapeDtypeStruct((L, DFF), BF16), compiler_params=_cp("parallel", "parallel"),
    )(Uu, Uu, Ug, Ug, wu, wg, bu, bg)


def _conv_act_bwd(Uu, Ug, wu, wg, bu, bg, dA, *, name):
    L, DFF = Uu.shape
    tr = _tile(L, 320, 16)
    tc = _tile(DFF, 512, LANE)

    def body(xu_ref, hu_ref, xg_ref, hg_ref, wu_ref, wg_ref, bu_ref, bg_ref, da_ref,
             du_ref, dg_ref, dwu_ref, dwg_ref, dbu_ref, dbg_ref):
        i = pl.program_id(1)
        tu = _conv_taps(xu_ref, hu_ref, i, tr)
        tg = _conv_taps(xg_ref, hg_ref, i, tr)
        u = _conv_apply(tu, wu_ref, bu_ref)
        g = _conv_apply(tg, wg_ref, bg_ref)
        silu, dsilu = _silu_parts(g)
        da = da_ref[...]
        du = da * silu
        dg = da * u * dsilu
        du_ref[...] = du
        dg_ref[...] = dg

        @pl.when(i == 0)
        def _():
            dwu_ref[...] = jnp.zeros_like(dwu_ref)
            dwg_ref[...] = jnp.zeros_like(dwg_ref)
            dbu_ref[...] = jnp.zeros_like(dbu_ref)
            dbg_ref[...] = jnp.zeros_like(dbg_ref)

        for j in range(3):
            dwu_ref[j:j + 1, :] += jnp.sum(du * tu[2 - j], axis=0, keepdims=True)
            dwg_ref[j:j + 1, :] += jnp.sum(dg * tg[2 - j], axis=0, keepdims=True)
        dbu_ref[...] += jnp.sum(du, axis=0, keepdims=True)
        dbg_ref[...] += jnp.sum(dg, axis=0, keepdims=True)

    blk, halo, w, b = _conv_specs(tr, tc)
    return pl.pallas_call(
        body, name=name, grid=(DFF // tc, L // tr),
        in_specs=[blk, halo, blk, halo, w, w, b, b, blk], out_specs=[blk, blk, w, w, b, b],
        out_shape=[jax.ShapeDtypeStruct((L, DFF), F32), jax.ShapeDtypeStruct((L, DFF), F32),
                   jax.ShapeDtypeStruct((3, DFF), F32), jax.ShapeDtypeStruct((3, DFF), F32),
                   jax.ShapeDtypeStruct((1, DFF), F32), jax.ShapeDtypeStruct((1, DFF), F32)],
        compiler_params=_cp("parallel", "arbitrary"),
    )(Uu, Uu, Ug, Ug, wu, wg, bu, bg, dA)


def _conv_in_bwd(dh, w, *, name):
    L, DFF = dh.shape
    tr = _tile(L, 320, 16)
    tc = _tile(DFF, 512, LANE)
    nb8 = L // 8

    def body(x_ref, halo_ref, w_ref, o_ref):
        i = pl.program_id(1)
        x = x_ref[...]
        halo = jnp.where((i + 1) * tr + lax.broadcasted_iota(jnp.int32, (8, 1), 0) < L, halo_ref[...], 0.0)
        d = (x * w_ref[2:3, :] + _shift_up(x, halo, 1) * w_ref[1:2, :] + _shift_up(x, halo, 2) * w_ref[0:1, :])
        o_ref[...] = jnp.where(_rows(i, tr) >= N_PAD, d, 0.0).astype(BF16)

    blk = pl.BlockSpec((tr, tc), lambda j, i: (i, j))
    halo = pl.BlockSpec((8, tc), lambda j, i: (jnp.minimum((i + 1) * (tr // 8), nb8 - 1), j))
    return pl.pallas_call(
        body, name=name, grid=(DFF // tc, L // tr),
        in_specs=[blk, halo, pl.BlockSpec((3, tc), lambda j, i: (0, j))], out_specs=blk,
        out_shape=jax.ShapeDtypeStruct((L, DFF), BF16), compiler_params=_cp("parallel", "parallel"),
    )(dh, dh, w)


def _fox_c_fwd(f, bf, *, name):
    L = f.shape[0]
    tr = _tile(L, 320, 16)

    def body(f_ref, b_ref, c_ref, carry):
        i = pl.program_id(0)

        @pl.when(i == 0)
        def _():
            carry[...] = jnp.zeros_like(carry)

        lf = jnp.where(_rows(i, tr) >= N_PAD, _log_sigmoid(f_ref[...] + b_ref[...]), 0.0)
        tri = (lax.broadcasted_iota(jnp.int32, (tr, tr), 0) >= lax.broadcasted_iota(jnp.int32, (tr, tr), 1)).astype(F32)
        c_ref[...] = _dot(tri, lf, 1, 0, precision=HIGHEST) + carry[...]
        carry[...] += jnp.sum(lf, axis=0, keepdims=True)

    return pl.pallas_call(
        body, name=name, grid=(L // tr,),
        in_specs=[pl.BlockSpec((tr, LANE), lambda i: (i, 0)), pl.BlockSpec((1, LANE), lambda i: (0, 0))],
        out_specs=pl.BlockSpec((tr, LANE), lambda i: (i, 0)),
        out_shape=jax.ShapeDtypeStruct((L, LANE), F32), scratch_shapes=[pltpu.VMEM((1, LANE), F32)],
        compiler_params=_cp("arbitrary"),
    )(f, bf)


def _fox_c_bwd(dcs, f, bf, *, name):
    L = f.shape[0]
    tr = _tile(L, 320, 16)
    nb = L // tr
    nd = len(dcs)

    def body(*refs):
        f_ref, b_ref, df_ref, db_ref, carry = refs[nd:]
        s = pl.program_id(0)
        i = nb - 1 - s

        @pl.when(s == 0)
        def _():
            carry[...] = jnp.zeros_like(carry)
            db_ref[...] = jnp.zeros_like(db_ref)

        dc = refs[0][...]
        for r in refs[1:nd]:
            dc = dc + r[...]
        triu = (lax.broadcasted_iota(jnp.int32, (tr, tr), 1) >= lax.broadcasted_iota(jnp.int32, (tr, tr), 0)).astype(F32)
        dlf = _dot(triu, dc, 1, 0, precision=HIGHEST) + carry[...]
        carry[...] += jnp.sum(dc, axis=0, keepdims=True)
        df = jnp.where(_rows(i, tr) >= N_PAD, dlf, 0.0) * _sigmoid(-(f_ref[...] + b_ref[...]))
        df_ref[...] = df.astype(BF16)
        db_ref[...] += jnp.sum(df, axis=0, keepdims=True)

    rev = pl.BlockSpec((tr, LANE), lambda s: (nb - 1 - s, 0))
    vec = pl.BlockSpec((1, LANE), lambda s: (0, 0))
    return pl.pallas_call(
        body, name=name, grid=(nb,), in_specs=[rev] * (nd + 1) + [vec], out_specs=[rev, vec],
        out_shape=[jax.ShapeDtypeStruct((L, LANE), BF16), jax.ShapeDtypeStruct((1, LANE), F32)],
        scratch_shapes=[pltpu.VMEM((1, LANE), F32)], compiler_params=_cp("arbitrary"),
    )(*dcs, f, bf)


AUG = 2 * FOX_HD
FOX_GROUP = 4
FOX_ROW_SPLIT = 5


def _split3(x):
    hi = x.astype(BF16).astype(F32)
    r = x - hi
    mid = r.astype(BF16).astype(F32)
    return hi, mid, (r - mid).astype(BF16).astype(F32)


def _aug_lanes(n, vals):
    lane = lax.broadcasted_iota(jnp.int32, (n, FOX_HD), 1)
    out = jnp.zeros((n, FOX_HD), F32)
    for j, v in enumerate(vals):
        out = jnp.where(lane == j, v, out)
    return out


def _lane_col(x, j):
    lane = lax.broadcasted_iota(jnp.int32, x.shape, 1)
    return jnp.sum(jnp.where(lane == j, x, 0.0), axis=-1, keepdims=True)


def _fox_prep_q(QO, c, H, *, name):
    L = QO.shape[0]
    hd = FOX_HD
    tr = _tile(L, 320, 16)

    def body(q_ref, c_ref, o_ref):
        c = c_ref[...]
        for h in range(H):
            hi, mid, lo = _split3(_lane_col(c, h))
            o_ref[:, h * AUG:h * AUG + hd] = (q_ref[:, h * hd:(h + 1) * hd] * (hd ** -0.5)).astype(BF16)
            o_ref[:, h * AUG + hd:(h + 1) * AUG] = _aug_lanes(tr, [hi, mid, lo, 1.0, 1.0, 1.0]).astype(BF16)

    return pl.pallas_call(
        body, name=name, grid=(L // tr,),
        in_specs=[pl.BlockSpec((tr, H * hd), lambda i: (i, 0)), pl.BlockSpec((tr, LANE), lambda i: (i, 0))],
        out_specs=pl.BlockSpec((tr, H * AUG), lambda i: (i, 0)),
        out_shape=jax.ShapeDtypeStruct((L, H * AUG), BF16), compiler_params=_cp("parallel"),
    )(QO, c)


def _fox_prep_kv(KV, c, H, *, name):
    L = KV.shape[0]
    hd = FOX_HD
    tr = _tile(L, 320, 16)

    def body(k_ref, v_ref, c_ref, ko_ref, vo_ref):
        i = pl.program_id(0)
        c = c_ref[...]
        pad = _rows(i, tr) < N_PAD
        for h in range(H):
            hi, mid, lo = _split3(_lane_col(c, h))
            aug = _aug_lanes(tr, [1.0, 1.0, 1.0, jnp.where(pad, NEG, -hi), jnp.where(pad, 0.0, -mid),
                                  jnp.where(pad, 0.0, -lo)])
            ko_ref[:, h * AUG:h * AUG + hd] = k_ref[:, h * hd:(h + 1) * hd]
            ko_ref[:, h * AUG + hd:(h + 1) * AUG] = aug.astype(BF16)
            vo_ref[:, h * AUG:h * AUG + hd] = v_ref[:, h * hd:(h + 1) * hd]
            vo_ref[:, h * AUG + hd:(h + 1) * AUG] = jnp.ones((tr, hd), BF16)

    wide = pl.BlockSpec((tr, H * AUG), lambda i: (i, 0))
    return pl.pallas_call(
        body, name=name, grid=(L // tr,),
        in_specs=[pl.BlockSpec((tr, H * hd), lambda i: (i, 0)), pl.BlockSpec((tr, H * hd), lambda i: (i, 1)),
                  pl.BlockSpec((tr, LANE), lambda i: (i, 0))],
        out_specs=[wide, wide],
        out_shape=[jax.ShapeDtypeStruct((L, H * AUG), BF16)] * 2, compiler_params=_cp("parallel"),
    )(KV, KV, c)


def _fox_mask(qi, kj, t):
    ti = qi * t + lax.broadcasted_iota(jnp.int32, (t, t), 0)
    si = kj * t + lax.broadcasted_iota(jnp.int32, (t, t), 1)
    return (si <= ti) & ((si >= N_PAD) | (si == ti))


def _fox_attn_fwd(QA, KT, VA, H, *, name):
    L = QA.shape[0]
    hd = FOX_HD
    t = _tile(L, 640, LANE)
    nb = L // t
    G = FOX_GROUP if H % FOX_GROUP == 0 else 1
    nr = FOX_ROW_SPLIT if t % (8 * FOX_ROW_SPLIT) == 0 else 1
    tr = t // nr

    def body(q_ref, k_ref, v_ref, o_ref, lse_ref, m_s, acc):
        qi, kj = pl.program_id(1), pl.program_id(2)

        @pl.when(kj == 0)
        def _():
            m_s[...] = jnp.full_like(m_s, NEG)
            acc[...] = jnp.zeros_like(acc)

        def step(masked):
            for g in range(G):
                cs = slice(g * AUG, (g + 1) * AUG)
                for r in range(nr):
                    rows = slice(r * tr, (r + 1) * tr)
                    s = _dot(q_ref[rows, cs], k_ref[cs, :], 1, 0)
                    if masked:
                        ti = qi * t + r * tr + lax.broadcasted_iota(jnp.int32, (tr, t), 0)
                        si = kj * t + lax.broadcasted_iota(jnp.int32, (tr, t), 1)
                        mask = (si <= ti) & ((si >= N_PAD) | (si == ti))
                        s = jnp.where(mask, s, NEG)
                    m_old = m_s[g, rows]
                    m_new = jnp.maximum(m_old, jnp.max(s, axis=-1, keepdims=True))
                    p = jnp.exp(s - m_new)
                    if masked:
                        p = jnp.where(mask, p, 0.0)
                    acc[g, rows] = jnp.exp(m_old - m_new) * acc[g, rows] + _dot(p.astype(BF16), v_ref[:, cs], 1, 0)
                    m_s[g, rows] = m_new

        @pl.when(kj < qi)
        def _():
            step(False)

        @pl.when(kj == qi)
        def _():
            step(True)

        @pl.when(kj == nb - 1)
        def _():
            for g in range(G):
                a = acc[g]
                l = a[:, hd:]
                o_ref[:, g * hd:(g + 1) * hd] = a[:, :hd] / l
                lse_ref[g] = m_s[g] + jnp.log(jnp.max(l, axis=-1, keepdims=True))

    return pl.pallas_call(
        body, name=name, grid=(H // G, nb, nb),
        in_specs=[pl.BlockSpec((t, G * AUG), lambda h, qi, kj: (qi, h)),
                  pl.BlockSpec((G * AUG, t), lambda h, qi, kj: (h, jnp.minimum(kj, qi))),
                  pl.BlockSpec((t, G * AUG), lambda h, qi, kj: (jnp.minimum(kj, qi), h))],
        out_specs=[pl.BlockSpec((t, G * hd), lambda h, qi, kj: (qi, h)),
                   pl.BlockSpec((G, t, 1), lambda h, qi, kj: (h, qi, 0))],
        out_shape=[jax.ShapeDtypeStruct((L, H * hd), F32), jax.ShapeDtypeStruct((H, L, 1), F32)],
        scratch_shapes=[pltpu.VMEM((G, t, 1), F32), pltpu.VMEM((G, t, AUG), F32)],
        compiler_params=_cp("parallel", "parallel", "arbitrary"),
    )(QA, KT, VA)


def _fox_attn_bwd(QA, KA, VA, DOA, lse, init, H, *, name):
    L = QA.shape[0]
    t = _tile(L, 640, LANE)
    nb = L // t

    def body(q_ref, k_ref, v_ref, do_ref, lse_ref, dk0_ref, dv0_ref, dq_ref, dk_ref, dv_ref):
        kj, qi = pl.program_id(1), pl.program_id(2)

        @pl.when((kj == 0) & (qi == 0))
        def _():
            dq_ref[...] = jnp.zeros_like(dq_ref)

        @pl.when(qi == 0)
        def _():
            dk_ref[...] = dk0_ref[...]
            dv_ref[...] = dv0_ref[...]

        def step(masked):
            q, k, doa = q_ref[...], k_ref[...], do_ref[...]
            p = jnp.exp(_dot(q, k, 1, 1) - lse_ref[0])
            if masked:
                p = jnp.where(_fox_mask(qi, kj, t), p, 0.0)
            pb = p.astype(BF16)
            ds = (p * _dot(doa, v_ref[...], 1, 1)).astype(BF16)
            dv_ref[...] += _dot(pb, doa, 0, 0)
            dk_ref[...] += _dot(ds, q, 0, 0)
            rows = pl.ds(pl.multiple_of(qi * t, t), t)
            dq_ref[rows, :] += _dot(ds, k, 1, 0)

        @pl.when(qi > kj)
        def _():
            step(False)

        @pl.when(qi == kj)
        def _():
            step(True)

    qb = pl.BlockSpec((t, AUG), lambda h, kj, qi: (jnp.maximum(qi, kj), h))
    kb = pl.BlockSpec((t, AUG), lambda h, kj, qi: (kj, h))
    return pl.pallas_call(
        body, name=name, grid=(H, nb, nb),
        in_specs=[qb, kb, kb, qb, pl.BlockSpec((1, t, 1), lambda h, kj, qi: (h, jnp.maximum(qi, kj), 0)), kb, kb],
        out_specs=[pl.BlockSpec((L, AUG), lambda h, kj, qi: (0, h)), kb, kb],
        out_shape=[jax.ShapeDtypeStruct((L, H * AUG), F32)] * 3,
        compiler_params=_cp("parallel", "arbitrary", "arbitrary"),
    )(QA, KA, VA, DOA, lse, *init)


def _fox_post_q(DQA, H, *, name):
    L = DQA.shape[0]
    hd = FOX_HD
    tr = _tile(L, 320, 16)

    def body(x_ref, dq_ref, dc_ref):
        lane = lax.broadcasted_iota(jnp.int32, (tr, LANE), 1)
        dc = jnp.zeros((tr, LANE), F32)
        for h in range(H):
            dq_ref[:, h * hd:(h + 1) * hd] = (x_ref[:, h * AUG:h * AUG + hd] * (hd ** -0.5)).astype(BF16)
            dc = jnp.where(lane == h, _lane_col(x_ref[:, h * AUG + hd:(h + 1) * AUG], 0), dc)
        dc_ref[...] = dc

    return pl.pallas_call(
        body, name=name, grid=(L // tr,), in_specs=[pl.BlockSpec((tr, H * AUG), lambda i: (i, 0))],
        out_specs=[pl.BlockSpec((tr, H * hd), lambda i: (i, 0)), pl.BlockSpec((tr, LANE), lambda i: (i, 0))],
        out_shape=[jax.ShapeDtypeStruct((L, H * hd), BF16), jax.ShapeDtypeStruct((L, LANE), F32)],
        compiler_params=_cp("parallel"),
    )(DQA)


def _fox_post_kv(DKA, DVA, H, *, name):
    L = DKA.shape[0]
    hd = FOX_HD
    tr = _tile(L, 320, 16)

    def body(k_ref, v_ref, o_ref, dc_ref):
        lane = lax.broadcasted_iota(jnp.int32, (tr, LANE), 1)
        dc = jnp.zeros((tr, LANE), F32)
        for h in range(H):
            o_ref[:, h * hd:(h + 1) * hd] = k_ref[:, h * AUG:h * AUG + hd].astype(BF16)
            o_ref[:, (H + h) * hd:(H + h + 1) * hd] = v_ref[:, h * AUG:h * AUG + hd].astype(BF16)
            dc = jnp.where(lane == h, -_lane_col(k_ref[:, h * AUG + hd:(h + 1) * AUG], 3), dc)
        dc_ref[...] = dc

    wide = pl.BlockSpec((tr, H * AUG), lambda i: (i, 0))
    return pl.pallas_call(
        body, name=name, grid=(L // tr,), in_specs=[wide, wide],
        out_specs=[pl.BlockSpec((tr, 2 * H * hd), lambda i: (i, 0)), pl.BlockSpec((tr, LANE), lambda i: (i, 0))],
        out_shape=[jax.ShapeDtypeStruct((L, 2 * H * hd), BF16), jax.ShapeDtypeStruct((L, LANE), F32)],
        compiler_params=_cp("parallel"),
    )(DKA, DVA)


def _fox_gate_fwd(o, QO, *, name):
    L, D = o.shape
    tr = _tile(L, 320, 16)

    def body(o_ref, g_ref, out_ref):
        out_ref[...] = (o_ref[...] * _sigmoid(g_ref[...])).astype(BF16)

    row = pl.BlockSpec((tr, D), lambda i: (i, 0))
    return pl.pallas_call(
        body, name=name, grid=(L // tr,), in_specs=[row, pl.BlockSpec((tr, D), lambda i: (i, 1))], out_specs=row,
        out_shape=jax.ShapeDtypeStruct((L, D), BF16), compiler_params=_cp("parallel"),
    )(o, QO)


def _fox_gate_bwd(d, o, QO, H, *, name):
    L, D = o.shape
    hd = FOX_HD
    tr = _tile(L, 320, 16)

    def body(d_ref, o_ref, g_ref, do_ref, dg_ref):
        for h in range(H):
            sl = slice(h * hd, (h + 1) * hd)
            s = _sigmoid(g_ref[:, sl])
            d = d_ref[:, sl]
            o = o_ref[:, sl]
            do = d * s
            dg_ref[:, sl] = (d * o * s * (1.0 - s)).astype(BF16)
            hi, mid, lo = _split3(-jnp.sum(do * o, axis=-1, keepdims=True))
            do_ref[:, h * AUG:h * AUG + hd] = do.astype(BF16)
            do_ref[:, h * AUG + hd:(h + 1) * AUG] = _aug_lanes(tr, [hi, mid, lo]).astype(BF16)

    row = pl.BlockSpec((tr, D), lambda i: (i, 0))
    return pl.pallas_call(
        body, name=name, grid=(L // tr,), in_specs=[row, row, pl.BlockSpec((tr, D), lambda i: (i, 1))],
        out_specs=[pl.BlockSpec((tr, H * AUG), lambda i: (i, 0)), row],
        out_shape=[jax.ShapeDtypeStruct((L, H * AUG), BF16), jax.ShapeDtypeStruct((L, D), BF16)],
        compiler_params=_cp("parallel"),
    )(d, o, QO)


def _row_tile(R, C, n_arrays):
    budget = VMEM_LIMIT // (3 * n_arrays * 4 * max(C, LANE))
    return _tile(R, max(16, budget // 16 * 16), 16)


def _sum_parts(parts, *, name, out_dtype=F32):
    R, C = parts[0].shape
    tr = _row_tile(R, C, len(parts) + 1)

    def body(*refs):
        acc = refs[0][...].astype(F32)
        for r in refs[1:-1]:
            acc = acc + r[...].astype(F32)
        refs[-1][...] = acc.astype(out_dtype)

    blk = pl.BlockSpec((tr, C), lambda i: (i, 0))
    return pl.pallas_call(
        body, name=name, grid=(R // tr,), in_specs=[blk] * len(parts), out_specs=blk,
        out_shape=jax.ShapeDtypeStruct((R, C), out_dtype), compiler_params=_cp("parallel"),
    )(*parts)


def _sum_slots(x, *, name):
    S, R, C = x.shape
    tr = _row_tile(R, C, S + 1)

    def body(x_ref, o_ref):
        acc = x_ref[0].astype(F32)
        for s in range(1, S):
            acc = acc + x_ref[s].astype(F32)
        o_ref[...] = acc

    return pl.pallas_call(
        body, name=name, grid=(R // tr,), in_specs=[pl.BlockSpec((S, tr, C), lambda i: (0, i, 0))],
        out_specs=pl.BlockSpec((tr, C), lambda i: (i, 0)),
        out_shape=jax.ShapeDtypeStruct((R, C), F32), compiler_params=_cp("parallel"),
    )(x)


def _adamw(w, m, v, gparts, *, name, stacked=False):
    R, C = w.shape
    ng = len(gparts)
    if stacked:
        rows = [g.shape[0] for g in gparts]
        assert sum(rows) == R, (rows, R)
        tr = _row_tile(math.gcd(*rows), C, 7 + ng)
        counts = [r // tr for r in rows]
        starts = [sum(counts[:p]) for p in range(ng)]
    else:
        tr = _row_tile(R, C, 7 + ng)

    def body(*refs):
        w_ref, m_ref, v_ref = refs[:3]
        if stacked:
            g = refs[3 + ng - 1][...]
            for p in range(ng - 2, -1, -1):
                g = jnp.where(pl.program_id(0) < starts[p + 1], refs[3 + p][...], g)
        else:
            g = refs[3][...]
            for r in refs[4:3 + ng]:
                g = g + r[...]
        g_ref, d_ref, nm_ref, nv_ref = refs[3 + ng:]
        nm = ADAM_B1 * m_ref[...] + (1.0 - ADAM_B1) * g
        nv = ADAM_B2 * v_ref[...] + (1.0 - ADAM_B2) * (g * g)
        m_hat = nm / (1.0 - ADAM_B1 ** ADAM_STEP)
        v_hat = nv / (1.0 - ADAM_B2 ** ADAM_STEP)
        g_ref[...] = g
        d_ref[...] = -ADAM_LR * (m_hat / (jnp.sqrt(v_hat) + ADAM_EPS) + ADAM_WD * w_ref[...])
        nm_ref[...] = nm
        nv_ref[...] = nv

    blk = pl.BlockSpec((tr, C), lambda i: (i, 0))
    if stacked:
        g_specs = [pl.BlockSpec((tr, C), lambda i, s=starts[p], n=counts[p]: (jnp.clip(i - s, 0, n - 1), 0))
                   for p in range(ng)]
    else:
        g_specs = [blk] * ng
    return pl.pallas_call(
        body, name=name, grid=(R // tr,), in_specs=[blk] * 3 + g_specs, out_specs=[blk] * 4,
        out_shape=[jax.ShapeDtypeStruct((R, C), F32)] * 4, compiler_params=_cp("parallel"),
    )(w, m, v, *gparts)


def _chip_peers():
    x, y, c = lax.axis_index("x"), lax.axis_index("y"), lax.axis_index("c")
    return (x, y, c), [(1 - x, y), (x, 1 - y), (1 - x, 1 - y)]


def _gather_chips(arrs, *, name):
    n = len(arrs)
    hs = [a.shape[0] // 2 for a in arrs]

    def body(*refs):
        ins, outs = refs[:n], refs[n:2 * n]
        send, recv = refs[2 * n:]
        (x, y, c), chips = _chip_peers()
        me = 2 * x + y

        def landing(a, chip_idx):
            return outs[a].at[chip_idx, pl.ds(c * hs[a], hs[a])]

        ici, passed = [], []
        for a in range(n):
            for j, (px, py) in enumerate(chips):
                cp = pltpu.make_async_remote_copy(
                    src_ref=ins[a].at[pl.ds(c * hs[a], hs[a])], dst_ref=landing(a, me), send_sem=send.at[a, j],
                    recv_sem=recv.at[a, j], device_id=(px, py, c), device_id_type=MESH)
                cp.start()
                ici.append(cp)
        for a in range(n):
            for j, (px, py) in enumerate(chips):
                ici[3 * a + j].wait_recv()
                src = landing(a, 2 * px + py)
                cp = pltpu.make_async_remote_copy(
                    src_ref=src, dst_ref=src, send_sem=send.at[a, 3 + j], recv_sem=recv.at[a, 3 + j],
                    device_id=(x, y, 1 - c), device_id_type=MESH)
                cp.start()
                passed.append(cp)
        for cp in ici:
            cp.wait_send()
        for cp in passed:
            cp.wait()

    chip = 2 * lax.axis_index("x") + lax.axis_index("y")
    outs = pl.pallas_call(
        body, name=name, in_specs=[ANY] * n, out_specs=[ANY] * n,
        out_shape=[jax.ShapeDtypeStruct((4,) + a.shape, a.dtype) for a in arrs],
        scratch_shapes=[pltpu.SemaphoreType.DMA((n, 6)), pltpu.SemaphoreType.DMA((n, 6))],
    )(*arrs)
    return [lax.dynamic_update_index_in_dim(o, a, chip, 0) for o, a in zip(outs, arrs)]


HBM_SPEC = pl.BlockSpec(memory_space=pltpu.HBM)
SEM_SPEC = pl.BlockSpec(memory_space=pltpu.SEMAPHORE)
DATAFLOW = pltpu.SideEffectType.DATAFLOW_SIDE_EFFECTING


def _late_copies(srcs, lands, send, recv, scatter):
    (x, y, c), chips = _chip_peers()
    out = []
    for a in range(len(srcs)):
        for j, (px, py) in enumerate(chips):
            src, dst = (srcs[a].at[2 * px + py], lands[a].at[j]) if scatter else (srcs[a], lands[a].at[2 * x + y])
            out.append(pltpu.make_async_remote_copy(
                src_ref=src, dst_ref=dst, send_sem=send.at[3 * a + j], recv_sem=recv.at[3 * a + j],
                device_id=(px, py, c), device_id_type=MESH))
    return out


def _gather_chips_start(arrs, after, *, name, scatter=False):
    n = len(arrs)
    land_shapes = [((3,) + a.shape[1:]) if scatter else ((4,) + a.shape) for a in arrs]

    def body(*refs):
        srcs, lands = refs[:n], refs[n:2 * n]
        send, recv = refs[2 * n + 1], refs[2 * n + 2]
        for cp in _late_copies(srcs, lands, send, recv, scatter):
            cp.start()
        refs[-1][...] = jnp.zeros_like(refs[-1])

    hbm = lambda a: pltpu.with_memory_space_constraint(a, pltpu.HBM)
    outs = pl.pallas_call(
        body, name=name,
        out_shape=(pltpu.SemaphoreType.DMA((3 * n,)), pltpu.SemaphoreType.DMA((3 * n,)),
                   *[pltpu.HBM(a.shape, a.dtype) for a in arrs],
                   *[pltpu.HBM(s, a.dtype) for s, a in zip(land_shapes, arrs)], jax.ShapeDtypeStruct((8, LANE), F32)),
        in_specs=[HBM_SPEC] * (2 * n) + [ANY],
        out_specs=(SEM_SPEC, SEM_SPEC, *[HBM_SPEC] * (2 * n), pl.BlockSpec(memory_space=pltpu.VMEM)),
        input_output_aliases={i: 2 + i for i in range(2 * n)},
        compiler_params=pltpu.CompilerParams(has_side_effects=DATAFLOW),
    )(*[hbm(a) for a in arrs], *[hbm(lax.empty(s, a.dtype)) for s, a in zip(land_shapes, arrs)], after)
    return outs[0], outs[1], list(outs[2:2 + n]), list(outs[2 + n:2 + 2 * n]), outs[-1], scatter


def _gather_chips_wait(started, after, *, name):
    send, recv, srcs, lands, _, scatter = started
    n = len(srcs)

    def body(*refs):
        for cp in _late_copies(refs[:n], refs[n:2 * n], refs[2 * n], refs[2 * n + 1], scatter):
            cp.wait_send()
            cp.wait_recv()

    outs = pl.pallas_call(
        body, name=name,
        out_shape=tuple(pltpu.HBM(a.shape, a.dtype) for a in srcs + lands),
        in_specs=[HBM_SPEC] * (2 * n) + [SEM_SPEC, SEM_SPEC, ANY], out_specs=tuple([HBM_SPEC] * (2 * n)),
        input_output_aliases={i: i for i in range(2 * n)},
        compiler_params=pltpu.CompilerParams(has_side_effects=DATAFLOW),
    )(*srcs, *lands, send, recv, after)
    return list(outs[n:])


def _scatter_chips(arrs, *, name):
    n = len(arrs)

    def body(*refs):
        ins, outs = refs[:n], refs[n:2 * n]
        send, recv = refs[2 * n:]
        (x, y, c), chips = _chip_peers()
        copies = []
        for a in range(n):
            for j, (px, py) in enumerate(chips):
                cp = pltpu.make_async_remote_copy(
                    src_ref=ins[a].at[2 * px + py], dst_ref=outs[a].at[j], send_sem=send.at[a, j],
                    recv_sem=recv.at[a, j], device_id=(px, py, c), device_id_type=MESH)
                cp.start()
                copies.append(cp)
        for cp in copies:
            cp.wait()

    return pl.pallas_call(
        body, name=name, in_specs=[ANY] * n, out_specs=[ANY] * n,
        out_shape=[jax.ShapeDtypeStruct((3,) + a.shape[1:], a.dtype) for a in arrs],
        scratch_shapes=[pltpu.SemaphoreType.DMA((n, 3)), pltpu.SemaphoreType.DMA((n, 3))],
    )(*arrs)


def _send_other_halves(arrs, *, name):
    n = len(arrs)
    hs = [a.shape[1] // 2 for a in arrs]

    def body(*refs):
        ins, outs = refs[:n], refs[n:2 * n]
        send, recv = refs[2 * n:]
        x, y, c = lax.axis_index("x"), lax.axis_index("y"), lax.axis_index("c")
        copies = []
        for a in range(n):
            cp = pltpu.make_async_remote_copy(
                src_ref=ins[a].at[pl.ds(0, 4), pl.ds((1 - c) * hs[a], hs[a])], dst_ref=outs[a], send_sem=send.at[a],
                recv_sem=recv.at[a], device_id=(x, y, 1 - c), device_id_type=MESH)
            cp.start()
            copies.append(cp)
        for cp in copies:
            cp.wait()

    return pl.pallas_call(
        body, name=name, in_specs=[ANY] * n, out_specs=[ANY] * n,
        out_shape=[jax.ShapeDtypeStruct((4, h) + a.shape[2:], a.dtype) for a, h in zip(arrs, hs)],
        scratch_shapes=[pltpu.SemaphoreType.DMA((n,)), pltpu.SemaphoreType.DMA((n,))],
    )(*arrs)


def _join_halves(arrs, *, name):
    n = len(arrs)

    def body(*refs):
        ins, outs = refs[:n], refs[n:2 * n]
        send, recv = refs[2 * n:]
        x, y, c = lax.axis_index("x"), lax.axis_index("y"), lax.axis_index("c")
        copies = []
        for a in range(n):
            h = ins[a].shape[0]
            cp = pltpu.make_async_remote_copy(
                src_ref=ins[a], dst_ref=outs[a].at[pl.ds(c * h, h)], send_sem=send.at[a], recv_sem=recv.at[a],
                device_id=(x, y, 1 - c), device_id_type=MESH)
            cp.start()
            copies.append(cp)
        for cp in copies:
            cp.wait()

    outs = pl.pallas_call(
        body, name=name, in_specs=[ANY] * n, out_specs=[ANY] * n,
        out_shape=[jax.ShapeDtypeStruct((2 * a.shape[0],) + a.shape[1:], a.dtype) for a in arrs],
        scratch_shapes=[pltpu.SemaphoreType.DMA((n,)), pltpu.SemaphoreType.DMA((n,))],
    )(*arrs)
    core = lax.axis_index("c")
    return [lax.dynamic_update_slice_in_dim(o, a, core * a.shape[0], 0) for o, a in zip(outs, arrs)]


def _gather_all(a, *, name):
    def body(in_ref, out_ref, send, recv):
        x, y, c = lax.axis_index("x"), lax.axis_index("y"), lax.axis_index("c")
        me = 4 * x + 2 * y + c
        copies = []
        for j in range(1, 8):
            fx, fy, fc = (j >> 2) & 1, (j >> 1) & 1, j & 1
            peer = (x ^ fx, y ^ fy, c ^ fc)
            cp = pltpu.make_async_remote_copy(
                src_ref=in_ref, dst_ref=out_ref.at[me], send_sem=send.at[j - 1], recv_sem=recv.at[j - 1],
                device_id=peer, device_id_type=MESH)
            cp.start()
            copies.append(cp)
        for cp in copies:
            cp.wait()

    out = pl.pallas_call(
        body, name=name, in_specs=[ANY], out_specs=ANY,
        out_shape=jax.ShapeDtypeStruct((8,) + a.shape, a.dtype),
        scratch_shapes=[pltpu.SemaphoreType.DMA((7,)), pltpu.SemaphoreType.DMA((7,))],
    )(a)
    me = 4 * lax.axis_index("x") + 2 * lax.axis_index("y") + lax.axis_index("c")
    return lax.dynamic_update_index_in_dim(out, a, me, 0)


def _pack(arrs):
    flat = jnp.concatenate([a.astype(F32).reshape(-1) for a in arrs])
    n = flat.shape[0]
    pad = (-n) % (16 * LANE)
    return jnp.pad(flat, (0, pad)).reshape(-1, LANE)


def _unpack(buf, shapes):
    flat = buf.reshape(-1)
    out, off = [], 0
    for s in shapes:
        n = int(np.prod(s))
        out.append(flat[off:off + n].reshape(s))
        off += n
    return out


def _to_shards(g, axis):
    parts = jnp.split(g, 4, axis=axis)
    return jnp.stack([p.reshape(-1, p.shape[-1]) for p in parts])


def kernel(x, meta, ln_g, ln_b, gla_w_in, gla_w_g2, gla_b_g2, gla_norm_g, gla_w_out, kv_w, kv_bf, fox_w_in, fox_w_out, ffn_w_up, ffn_conv_w, ffn_conv_b, ffn_w_down, loss_target, m_meta, m_ln_g, m_ln_b, m_gla_w_in, m_gla_w_g2, m_gla_b_g2, m_gla_norm_g, m_gla_w_out, m_kv_w, m_kv_bf, m_fox_w_in, m_fox_w_out, m_ffn_w_up, m_ffn_conv_w, m_ffn_conv_b, m_ffn_w_down, v_meta, v_ln_g, v_ln_b, v_gla_w_in, v_gla_w_g2, v_gla_b_g2, v_gla_norm_g, v_gla_w_out, v_kv_w, v_kv_bf, v_fox_w_in, v_fox_w_out, v_ffn_w_up, v_ffn_conv_w, v_ffn_conv_b, v_ffn_w_down):
    D = x.shape[-1]
    L = x.shape[1] + FRONT
    HG = GLA_HEADS
    DK, DV = D // 2, D
    HK, HV = DK // HG, DV // HG
    HF = D // FOX_HD
    DFF = ffn_w_down.shape[1] * 4
    chip = 2 * lax.axis_index("x") + lax.axis_index("y")

    big_names = ["gla_w_in", "gla_w_out", "kv_w", "fox_w_in", "fox_w_out", "ffn_w_up", "ffn_w_down"]
    big = dict(gla_w_in=gla_w_in, gla_w_out=gla_w_out, kv_w=kv_w, fox_w_in=fox_w_in, fox_w_out=fox_w_out,
               ffn_w_up=ffn_w_up, ffn_w_down=ffn_w_down)
    big_axis = dict(gla_w_in=2, gla_w_out=1, kv_w=1, fox_w_in=2, fox_w_out=1, ffn_w_up=2, ffn_w_down=1)
    small_names = ["meta", "ln_g", "ln_b", "gla_w_g2", "gla_b_g2", "gla_norm_g", "ffn_conv_w"]
    small = dict(meta=meta, ln_g=ln_g, ln_b=ln_b, gla_w_g2=gla_w_g2, gla_b_g2=gla_b_g2, gla_norm_g=gla_norm_g,
                 ffn_conv_w=ffn_conv_w)
    small_shapes = [small[k].shape for k in small_names]
    bf = lambda a: a.astype(BF16)
    g0 = _gather_chips([bf(gla_w_in[0]), bf(gla_w_out[0]), _pack([small[k] for k in small_names])], name="gather_weights")
    src0 = [bf(ffn_w_up[0]), bf(ffn_w_down[0])]
    src1 = [bf(gla_w_in[1]), bf(gla_w_out[1]), bf(ffn_w_up[1]), bf(ffn_w_down[1])]
    src2 = [bf(kv_w), bf(fox_w_in), bf(fox_w_out), bf(ffn_w_up[2:]), bf(ffn_w_down[2:])]
    late0 = _gather_chips_start(src0, g0[-1], name="gather_ffn0_start")
    late1 = _gather_chips_start(src1, late0[4], name="gather_layer1_start")
    late2 = _gather_chips_start(src2, late1[4], name="gather_layer23_start")
    sm_sh = [_unpack(g0[-1][s], small_shapes) for s in range(4)]
    fs = {k: jnp.concatenate([sm_sh[s][i] for s in range(4)], axis=-1) for i, k in enumerate(small_names)}

    pad_cols = lambda w: jnp.pad(w, ((0, 0), (0, LANE - w.shape[1])))
    cat = lambda parts, axis: jnp.concatenate(parts, axis=axis)

    def gla_in_matrix(w):
        return cat([w[:, :2 * DK + DV], w[:, 2 * DK + DV + GLA_RANK:], pad_cols(w[:, 2 * DK + DV:2 * DK + DV + GLA_RANK])], 1)

    def ffn_matrices(up, down):
        return dict(u=cat([up(0), up(1)], 1), g=cat([up(2), up(3)], 1), d=cat([down(s) for s in range(4)], 0))

    WL = [None] * DEPTH
    WL[0] = dict(P=gla_in_matrix(cat([g0[0][s] for s in range(4)], 1)), go=cat([g0[1][s] for s in range(4)], 0))
    w2p = [jnp.pad(fs["gla_w_g2"][l], ((0, LANE - GLA_RANK), (0, 0))).astype(BF16) for l in range(N_A_LAYERS)]
    bf_pad = jnp.pad(kv_bf, (0, LANE - HF)).reshape(1, LANE)
    cw_u = [fs["ffn_conv_w"][l][:, :DFF] for l in range(DEPTH)]
    cw_g = [fs["ffn_conv_w"][l][:, DFF:] for l in range(DEPTH)]
    cb_u = [ffn_conv_b[l][None, :DFF] for l in range(DEPTH)]
    cb_g = [ffn_conv_b[l][None, DFF:] for l in range(DEPTH)]
    gl_blk = (2 * DK + 2 * DV) // LANE
    r_blk = (2 * DK + DV) // DV

    h = jnp.concatenate([jnp.concatenate([jnp.zeros((N_PAD, D), F32), fs["meta"]], axis=0), x[0]], axis=0)
    hb = (h + late2[4][0, 0]).astype(BF16)
    saved = []
    kvs = None
    for l in range(DEPTH):
        if l == 1:
            lands = _gather_chips_wait(late1, h, name="gather_layer1_wait")
            Lgi, Lgo, Lup, Ldn = [lax.dynamic_update_index_in_dim(o, a, chip, 0) for o, a in zip(lands, src1)]
            WL[1] = dict(P=gla_in_matrix(cat([Lgi[s] for s in range(4)], 1)), go=cat([Lgo[s] for s in range(4)], 0),
                         **ffn_matrices(lambda s: Lup[s], lambda s: Ldn[s]))
        if l == N_A_LAYERS:
            lands = _gather_chips_wait(late2, h, name="gather_layer23_wait")
            Lkv, Lfi, Lfo, Lup2, Ldn2 = [lax.dynamic_update_index_in_dim(o, a, chip, 0) for o, a in zip(lands, src2)]
            for ll in range(N_A_LAYERS, DEPTH):
                jj = ll - N_A_LAYERS
                WL[ll] = dict(fi=cat([Lfi[s][jj] for s in range(4)], 1), fo=cat([Lfo[s][jj] for s in range(4)], 0),
                              **ffn_matrices(lambda s, jj=jj: Lup2[s][jj], lambda s, jj=jj: Ldn2[s][jj]))
            kv_full = cat([Lkv[s] for s in range(4)], 1)
            W_kv = kv_full[:, :2 * D]
            W_f = pad_cols(kv_full[:, 2 * D:])
            W_kvf = cat([W_kv, W_f], 1)
        s = dict(h=h, hb=hb)
        if l < N_A_LAYERS:
            s["P"] = _mm(hb, WL[l]["P"], name=f"gla_in_{l}")
            s["la"] = _gla_gate_fwd(s["P"], w2p[l], fs["gla_b_g2"][l][None], gl_blk, name=f"gla_gate_{l}")
            s["o"], s["S"] = _gla_chunk_fwd(s["P"], s["la"], HG, HK, HV, name=f"gla_chunk_{l}")
            s["gated"] = _gla_post_fwd(s["o"], s["P"], fs["gla_norm_g"][l][None], HG, HV, r_blk, name=f"gla_post_{l}")
            s["mix"] = _mm(s["gated"], WL[l]["go"], name=f"gla_out_{l}")
        else:
            j = l - N_A_LAYERS
            if kvs is None:
                KV = _mm(hb, W_kv, out_dtype=BF16, name="kv_proj")
                f = _mm(hb, W_f, name="kv_gate_proj")
                c = _fox_c_fwd(f, bf_pad, name="fox_c")
                KA, VA = _fox_prep_kv(KV, c, HF, name="fox_prep_kv")
                kvs = dict(KA=KA, KT=KA.T, VA=VA, f=f, c=c, hb=hb)
            s["QO"] = _mm(hb, WL[l]["fi"], name=f"fox_in_{j}")
            s["QA"] = _fox_prep_q(s["QO"], kvs["c"], HF, name=f"fox_prep_q_{j}")
            s["o"], s["lse"] = _fox_attn_fwd(s["QA"], kvs["KT"], kvs["VA"], HF, name=f"fox_attn_{j}")
            s["gated"] = _fox_gate_fwd(s["o"], s["QO"], name=f"fox_gate_{j}")
            s["mix"] = _mm(s["gated"], WL[l]["fo"], name=f"fox_out_{j}")
        if l == 0:
            lands = _gather_chips_wait(late0, s["mix"], name="gather_ffn0_wait")
            Lup0, Ldn0 = [lax.dynamic_update_index_in_dim(o, a, chip, 0) for o, a in zip(lands, src0)]
            WL[0].update(ffn_matrices(lambda s_: Lup0[s_], lambda s_: Ldn0[s_]))
        s["h1"], s["h1b"] = _ln_fwd(h, s["mix"], fs["ln_g"][l, 0][None], fs["ln_b"][l, 0][None], name=f"ln_a_{l}")
        s["Uu"] = _mm(s["h1b"], WL[l]["u"], name=f"ffn_up_u_{l}")
        s["Ug"] = _mm(s["h1b"], WL[l]["g"], name=f"ffn_up_g_{l}")
        s["a"] = _conv_act_fwd(s["Uu"], s["Ug"], cw_u[l], cw_g[l], cb_u[l], cb_g[l], name=f"ffn_conv_{l}")
        s["ffn"] = _mm(s["a"], WL[l]["d"], name=f"ffn_down_{l}")
        h, hb = _ln_fwd(s["h1"], s["ffn"], fs["ln_g"][l, 1][None], fs["ln_b"][l, 1][None], name=f"ln_b_{l}")
        saved.append(s)

    loss_acc, dh = _loss(h, loss_target[0], name="loss")

    gW = {}
    d_ln_g = [[None, None] for _ in range(DEPTH)]
    d_ln_b = [[None, None] for _ in range(DEPTH)]
    d_cw, d_cb = [None] * DEPTH, [None] * DEPTH
    d_wg2, d_bg2, d_ng = [None] * N_A_LAYERS, [None] * N_A_LAYERS, [None] * N_A_LAYERS
    dkv = (jnp.zeros((L, HF * AUG), F32), jnp.zeros((L, HF * AUG), F32))
    dcqs = []
    core = lax.axis_index("c")
    stack = lambda k, lo, hi: jnp.stack([gW[(k, i)] for i in range(lo, hi)])

    def pair_sums(entries, tag):
        g4 = [_to_shards(g, ax) for _, g, ax in entries]
        sib = _send_other_halves(g4, name=f"pair_exchange_{tag}")
        out = []
        for (key, _, _), a, b in zip(entries, g4, sib):
            _, R, C = a.shape
            mine = lax.dynamic_slice_in_dim(a, core * (R // 2), R // 2, axis=1)
            out.append(_sum_parts([mine.reshape(2 * R, C), b.reshape(2 * R, C)], out_dtype=BF16,
                                  name=f"sum_pair_{key}").reshape(4, R // 2, C))
        return out

    def chip_sums(entries, pair, recv):
        return [_sum_parts([lax.dynamic_index_in_dim(p, chip, axis=0, keepdims=False), r[0], r[1], r[2]],
                           name=f"sum_chips_{key}") for (key, _, _), p, r in zip(entries, pair, recv)]

    def layer_entries(l):
        return [(f"gla_w_in_{l}", gW[("gla_w_in", l)][None], 2), (f"gla_w_out_{l}", gW[("gla_w_out", l)][None], 1),
                (f"ffn_w_up_{l}", gW[("ffn_w_up", l)][None], 2), (f"ffn_w_down_{l}", gW[("ffn_w_down", l)][None], 1)]

    ln_tok = 0.0
    for l in reversed(range(DEPTH)):
        s = saved[l]
        dz, dzb, d_ln_g[l][1], d_ln_b[l][1] = _ln_bwd(dh, s["h1"], s["ffn"], fs["ln_g"][l, 1][None] + ln_tok,
                                                      name=f"ln_b_bwd_{l}")
        dA = _mm(dzb, WL[l]["d"], tb=True, name=f"ffn_down_dx_{l}")
        gW[("ffn_w_down", l)] = _mm(s["a"], dzb, ta=True, out_dtype=BF16, name=f"ffn_down_dw_{l}")
        dcu, dcg, dwu, dwg, dbu, dbg = _conv_act_bwd(s["Uu"], s["Ug"], cw_u[l], cw_g[l], cb_u[l], cb_g[l], dA,
                                                     name=f"ffn_conv_bwd_{l}")
        d_cw[l] = jnp.concatenate([dwu, dwg], axis=1)
        d_cb[l] = jnp.concatenate([dbu, dbg], axis=1)[0]
        dUu = _conv_in_bwd(dcu, cw_u[l], name=f"ffn_conv_dx_u_{l}")
        dUg = _conv_in_bwd(dcg, cw_g[l], name=f"ffn_conv_dx_g_{l}")
        gW[("ffn_w_up", l)] = jnp.concatenate(
            [_mm(s["h1b"], dUu, ta=True, out_dtype=BF16, name=f"ffn_up_dw_u_{l}"),
             _mm(s["h1b"], dUg, ta=True, out_dtype=BF16, name=f"ffn_up_dw_g_{l}")], axis=1)
        dh1 = _mm(dUu, WL[l]["u"], tb=True, add=dz, add_scale=ALPHA, name=f"ffn_up_dx_u_{l}")
        dh1 = _mm(dUg, WL[l]["g"], tb=True, add=dh1, name=f"ffn_up_dx_g_{l}")
        dz, dzb, d_ln_g[l][0], d_ln_b[l][0] = _ln_bwd(dh1, s["h"], s["mix"], fs["ln_g"][l, 0][None], name=f"ln_a_bwd_{l}")
        if l < N_A_LAYERS:
            dgated = _mm(dzb, WL[l]["go"], tb=True, name=f"gla_out_dx_{l}")
            gW[("gla_w_out", l)] = _mm(s["gated"], dzb, ta=True, out_dtype=BF16, name=f"gla_out_dw_{l}")
            do, drb, d_ng[l] = _gla_post_bwd(dgated, s["o"], s["P"], fs["gla_norm_g"][l][None], HG, HV, r_blk,
                                             name=f"gla_post_bwd_{l}")
            dq, dk, dvb, dla = _gla_chunk_bwd(s["P"], s["la"], s["S"], do, HG, HK, HV, name=f"gla_chunk_bwd_{l}")
            dglb, dw2, d_bg2[l] = _gla_gate_bwd(dla, s["P"], w2p[l], fs["gla_b_g2"][l][None], gl_blk,
                                               name=f"gla_gate_bwd_{l}")
            d_wg2[l] = dw2[:GLA_RANK]
            dP = jnp.concatenate([dq, dk, dvb, drb, dglb], axis=1)
            gP = _mm(s["hb"], dP, ta=True, out_dtype=BF16, name=f"gla_in_dw_{l}")
            gW[("gla_w_in", l)] = jnp.concatenate(
                [gP[:, :2 * DK + DV], gP[:, 2 * DK + 2 * DV:2 * DK + 2 * DV + GLA_RANK], gP[:, 2 * DK + DV:2 * DK + 2 * DV]],
                axis=1)
            dh = _mm(dP, WL[l]["P"], tb=True, add=dz, add_scale=ALPHA, name=f"gla_in_dx_{l}")
            if l == 1:
                mid_entries = layer_entries(1)
                pair_mid = pair_sums(mid_entries, "mid")
                sc_mid = _gather_chips_start(pair_mid, dh, name="scatter_mid_start", scatter=True)
                ln_tok = sc_mid[4][0, 0]
        else:
            j = l - N_A_LAYERS
            dgo = _mm(dzb, WL[l]["fo"], tb=True, name=f"fox_out_dx_{j}")
            gW[("fox_w_out", j)] = _mm(s["gated"], dzb, ta=True, out_dtype=BF16, name=f"fox_out_dw_{j}")
            DOA, dogb = _fox_gate_bwd(dgo, s["o"], s["QO"], HF, name=f"fox_gate_bwd_{j}")
            DQA, DKA, DVA = _fox_attn_bwd(s["QA"], kvs["KA"], kvs["VA"], DOA, s["lse"], dkv, HF, name=f"fox_attn_bwd_{j}")
            dkv = (DKA, DVA)
            dqb, dcq = _fox_post_q(DQA, HF, name=f"fox_post_q_{j}")
            dcqs.append(dcq)
            dQO = jnp.concatenate([dqb, dogb], axis=1)
            gW[("fox_w_in", j)] = _mm(s["hb"], dQO, ta=True, out_dtype=BF16, name=f"fox_in_dw_{j}")
            dh = _mm(dQO, WL[l]["fi"], tb=True, add=dz, add_scale=ALPHA, name=f"fox_in_dx_{j}")
            if j == 0:
                dkvb, dck = _fox_post_kv(DKA, DVA, HF, name="fox_post_kv")
                dfb, d_bf = _fox_c_bwd(dcqs + [dck], kvs["f"], bf_pad, name="fox_c_bwd")
                dKVF = jnp.concatenate([dkvb, dfb], axis=1)
                gkv = _mm(kvs["hb"], dKVF, ta=True, out_dtype=BF16, name="kv_dw")
                gW[("kv_w", 0)] = gkv[:, :2 * D + HF]
                dh = _mm(dKVF, W_kvf, tb=True, add=dh, name="kv_dx")
                late_entries = [("fox_w_in", stack("fox_w_in", 0, 2), 2), ("fox_w_out", stack("fox_w_out", 0, 2), 1),
                                ("kv_w", gW[("kv_w", 0)], 1), ("ffn_w_up_hi", stack("ffn_w_up", N_A_LAYERS, DEPTH), 2),
                                ("ffn_w_down_hi", stack("ffn_w_down", N_A_LAYERS, DEPTH), 1)]
                pair_late = pair_sums(late_entries, "late")
                sc_late = _gather_chips_start(pair_late, dh, name="scatter_late_start", scatter=True)
                ln_tok = sc_late[4][0, 0]

    early_entries = layer_entries(0)
    pair_early = pair_sums(early_entries, "early")
    sc_early = _gather_chips_start(pair_early, dh, name="scatter_early_start", scatter=True)
    recv_late = _gather_chips_wait(sc_late, sc_early[4], name="scatter_late_wait")
    joined_late = _join_halves(chip_sums(late_entries, pair_late, recv_late), name="join_halves_late")
    gsum = {key: g for (key, _, _), g in zip(late_entries, joined_late)}
    recv_mid = _gather_chips_wait(sc_mid, joined_late[0], name="scatter_mid_wait")
    joined_mid = _join_halves(chip_sums(mid_entries, pair_mid, recv_mid), name="join_halves_mid")
    gsum.update({key: g for (key, _, _), g in zip(mid_entries, joined_mid)})
    moments = dict(gla_w_in=(m_gla_w_in, v_gla_w_in), gla_w_out=(m_gla_w_out, v_gla_w_out), kv_w=(m_kv_w, v_kv_w),
                   fox_w_in=(m_fox_w_in, v_fox_w_in), fox_w_out=(m_fox_w_out, v_fox_w_out),
                   ffn_w_up=(m_ffn_w_up, v_ffn_w_up), ffn_w_down=(m_ffn_w_down, v_ffn_w_down))
    res = {}

    def apply_adamw(k):
        w = big[k]
        sh = w.shape
        flat = lambda a: a.reshape(-1, sh[-1])
        parts = [gsum[f"{k}_{p}"] for p in ("0", "1", "hi") if f"{k}_{p}" in gsum] or [gsum[k]]
        outs = _adamw(flat(w), flat(moments[k][0]), flat(moments[k][1]), parts, stacked=len(parts) > 1, name=f"adamw_{k}")
        res[k] = [o.reshape(sh) for o in outs]

    for k in ("fox_w_in", "fox_w_out", "kv_w"):
        apply_adamw(k)

    dmeta = dh[N_PAD:FRONT]
    sg = dict(meta=dmeta,
              ln_g=jnp.stack([jnp.concatenate(d_ln_g[l], axis=0) for l in range(DEPTH)]),
              ln_b=jnp.stack([jnp.concatenate(d_ln_b[l], axis=0) for l in range(DEPTH)]),
              gla_w_g2=jnp.stack(d_wg2), gla_b_g2=jnp.stack([d[0] for d in d_bg2]),
              gla_norm_g=jnp.stack([d[0] for d in d_ng]), ffn_conv_w=jnp.stack(d_cw),
              kv_bf=d_bf[0, :HF], ffn_conv_b=jnp.stack(d_cb), loss=loss_acc[0, :1])
    sg_names = small_names + ["kv_bf", "ffn_conv_b", "loss"]
    sg_shapes = [sg[k].shape for k in sg_names]
    red = _sum_slots(_gather_all(_pack([sg[k] for k in sg_names]), name="gather_small_grads"), name="sum_small_grads")
    red = dict(zip(sg_names, _unpack(red, sg_shapes)))
    loss = red["loss"][0]
    loc = {}
    for k in small_names:
        wdt = small[k].shape[-1]
        loc[k] = lax.dynamic_slice_in_dim(red[k], chip * wdt, wdt, axis=red[k].ndim - 1)
    loc["kv_bf"] = red["kv_bf"]
    loc["ffn_conv_b"] = red["ffn_conv_b"]
    sm_all = small_names + ["kv_bf", "ffn_conv_b"]
    sw = dict(small, kv_bf=kv_bf, ffn_conv_b=ffn_conv_b)
    sm_m = dict(meta=m_meta, ln_g=m_ln_g, ln_b=m_ln_b, gla_w_g2=m_gla_w_g2, gla_b_g2=m_gla_b_g2, gla_norm_g=m_gla_norm_g,
                ffn_conv_w=m_ffn_conv_w, kv_bf=m_kv_bf, ffn_conv_b=m_ffn_conv_b)
    sm_v = dict(meta=v_meta, ln_g=v_ln_g, ln_b=v_ln_b, gla_w_g2=v_gla_w_g2, gla_b_g2=v_gla_b_g2, gla_norm_g=v_gla_norm_g,
                ffn_conv_w=v_ffn_conv_w, kv_bf=v_kv_bf, ffn_conv_b=v_ffn_conv_b)
    shapes_loc = [sw[k].shape for k in sm_all]
    outs = _adamw(_pack([sw[k] for k in sm_all]), _pack([sm_m[k] for k in sm_all]), _pack([sm_v[k] for k in sm_all]),
                  [_pack([loc[k] for k in sm_all])], name="adamw_small")
    outs = [_unpack(o, shapes_loc) for o in outs]
    for i, k in enumerate(sm_all):
        res[k] = [outs[q][i] for q in range(4)]

    behind = jnp.zeros((8, LANE), F32) + res["kv_w"][0][:1, :1] + res["kv_bf"][0][:1]
    recv_early = _gather_chips_wait(sc_early, behind, name="scatter_early_wait")
    joined_early = _join_halves(chip_sums(early_entries, pair_early, recv_early), name="join_halves_early")
    gsum.update({key: g for (key, _, _), g in zip(early_entries, joined_early)})
    for k in ("gla_w_in", "gla_w_out", "ffn_w_up", "ffn_w_down"):
        apply_adamw(k)

    order = ["meta", "ln_g", "ln_b", "gla_w_in", "gla_w_g2", "gla_b_g2", "gla_norm_g", "gla_w_out", "kv_w", "kv_bf",
             "fox_w_in", "fox_w_out", "ffn_w_up", "ffn_conv_w", "ffn_conv_b", "ffn_w_down"]
    grad_x = dh[FRONT:][None]
    return (loss, grad_x, *[res[k][0] for k in order], *[res[k][1] for k in order], *[res[k][2] for k in order],
            *[res[k][3] for k in order])
```

```python
import functools
import math

import numpy as np
import jax
import jax.numpy as jnp
from jax import lax
from jax.experimental import pallas as pl
from jax.experimental.pallas import tpu as pltpu

F32 = jnp.float32
BF16 = jnp.bfloat16
HIGHEST = lax.Precision.HIGHEST

DEPTH = 4
N_A_LAYERS = DEPTH // 2
N_META = 16
FRONT = 128
N_PAD = FRONT - N_META
ALPHA = (2.0 * DEPTH) ** 0.25
LN_EPS = 1e-5
GLA_HEADS = 4
GLA_RANK = 16
GLA_TAU = 16.0
GLA_CHUNK = 64
GLA_GROUP = 2
FOX_HD = 128
LANE = 128
ADAM_LR = 0.001
ADAM_B1 = 0.9
ADAM_B2 = 0.999
ADAM_EPS = 1e-08
ADAM_WD = 0.01
ADAM_STEP = 10
NEG = -(2.0 ** 100)
VMEM_LIMIT = 50 * 1024 * 1024
MESH = pl.DeviceIdType.MESH
ANY = pl.BlockSpec(memory_space=pl.ANY)


def _tile(n, pref, align):
    best = None
    t = align
    while t <= min(n, pref):
        if n % t == 0:
            best = t
        t += align
    return best if best is not None else n


def _cp(*sem):
    return pltpu.CompilerParams(dimension_semantics=sem, vmem_limit_bytes=VMEM_LIMIT)


def _dot(a, b, ca, cb, precision=None):
    return lax.dot_general(a, b, (((ca,), (cb,)), ((), ())), precision=precision,
                           preferred_element_type=F32)


def _sigmoid(x):
    return 1.0 / (1.0 + jnp.exp(-x))


def _log_sigmoid(z):
    return jnp.minimum(z, 0.0) - jnp.log(1.0 + jnp.exp(-jnp.abs(z)))


def _rows(i, tr, n=None):
    n = tr if n is None else n
    return i * tr + lax.broadcasted_iota(jnp.int32, (n, 1), 0)


def _mm(a, b, *, ta=False, tb=False, out_dtype=F32, add=None, add_scale=1.0, name):
    if ta:
        K, M = a.shape
    else:
        M, K = a.shape
    if tb:
        N, K2 = b.shape
    else:
        K2, N = b.shape
    assert K == K2, (a.shape, b.shape, ta, tb)
    tm = _tile(M, 1024, LANE) if ta else _tile(M, 1040, 16)
    tn = _tile(N, 1024, LANE)
    tk = _tile(K, 2048, LANE if ((not ta) or tb) else 16)
    nk = K // tk
    ca = 0 if ta else 1
    cb = 1 if tb else 0

    def body(*refs):
        if add is None:
            a_ref, b_ref, o_ref = refs[:3]
            add_ref = None
        else:
            a_ref, b_ref, add_ref, o_ref = refs[:4]
        part = _dot(a_ref[...].astype(BF16), b_ref[...].astype(BF16), ca, cb)

        def finish(acc):
            if add_ref is not None:
                acc = acc + add_scale * add_ref[...]
            o_ref[...] = acc.astype(out_dtype)

        if nk == 1:
            finish(part)
        else:
            acc_ref = refs[-1]
            k = pl.program_id(2)

            @pl.when(k == 0)
            def _():
                acc_ref[...] = part

            @pl.when(k > 0)
            def _():
                acc_ref[...] += part

            @pl.when(k == nk - 1)
            def _():
                finish(acc_ref[...])

    a_spec = pl.BlockSpec((tk, tm), lambda i, j, k: (k, i)) if ta else pl.BlockSpec((tm, tk), lambda i, j, k: (i, k))
    b_spec = pl.BlockSpec((tn, tk), lambda i, j, k: (j, k)) if tb else pl.BlockSpec((tk, tn), lambda i, j, k: (k, j))
    o_spec = pl.BlockSpec((tm, tn), lambda i, j, k: (i, j))
    in_specs = [a_spec, b_spec] + ([o_spec] if add is not None else [])
    args = (a, b) + ((add,) if add is not None else ())
    return pl.pallas_call(
        body, name=name, grid=(M // tm, N // tn, nk), in_specs=in_specs, out_specs=o_spec,
        out_shape=jax.ShapeDtypeStruct((M, N), out_dtype),
        scratch_shapes=[pltpu.VMEM((tm, tn), F32)] if nk > 1 else [],
        compiler_params=_cp("parallel", "parallel", "arbitrary"),
    )(*args)


def _ln_stats(h, mix):
    z = ALPHA * h + mix
    mu = jnp.mean(z, axis=-1, keepdims=True)
    zc = z - mu
    var = jnp.mean(zc * zc, axis=-1, keepdims=True)
    rstd = lax.rsqrt(var + LN_EPS)
    return zc * rstd, rstd


def _ln_fwd(h, mix, g, b, *, name):
    L, D = h.shape
    tr = _tile(L, 160, 16)

    def body(h_ref, m_ref, g_ref, b_ref, o_ref, ob_ref):
        xhat, _ = _ln_stats(h_ref[...], m_ref[...])
        y = xhat * g_ref[...] + b_ref[...]
        o_ref[...] = y
        ob_ref[...] = y.astype(BF16)

    row = pl.BlockSpec((tr, D), lambda i: (i, 0))
    vec = pl.BlockSpec((1, D), lambda i: (0, 0))
    return pl.pallas_call(
        body, name=name, grid=(L // tr,), in_specs=[row, row, vec, vec], out_specs=[row, row],
        out_shape=[jax.ShapeDtypeStruct((L, D), F32), jax.ShapeDtypeStruct((L, D), BF16)],
        compiler_params=_cp("parallel"),
    )(h, mix, g, b)


def _ln_bwd(dy, h, mix, g, *, name):
    L, D = h.shape
    tr = _tile(L, 160, 16)

    def body(dy_ref, h_ref, m_ref, g_ref, dz_ref, dzb_ref, dg_ref, db_ref):
        i = pl.program_id(0)
        xhat, rstd = _ln_stats(h_ref[...], m_ref[...])
        dy = dy_ref[...]
        dxh = dy * g_ref[...]
        m1 = jnp.mean(dxh, axis=-1, keepdims=True)
        m2 = jnp.mean(dxh * xhat, axis=-1, keepdims=True)
        dz = rstd * (dxh - m1 - xhat * m2)
        dz_ref[...] = dz
        dzb_ref[...] = dz.astype(BF16)
        pg = jnp.sum(dy * xhat, axis=0, keepdims=True)
        pb = jnp.sum(dy, axis=0, keepdims=True)

        @pl.when(i == 0)
        def _():
            dg_ref[...] = pg
            db_ref[...] = pb

        @pl.when(i > 0)
        def _():
            dg_ref[...] += pg
            db_ref[...] += pb

    row = pl.BlockSpec((tr, D), lambda i: (i, 0))
    vec = pl.BlockSpec((1, D), lambda i: (0, 0))
    return pl.pallas_call(
        body, name=name, grid=(L // tr,), in_specs=[row, row, row, vec], out_specs=[row, row, vec, vec],
        out_shape=[jax.ShapeDtypeStruct((L, D), F32), jax.ShapeDtypeStruct((L, D), BF16),
                   jax.ShapeDtypeStruct((1, D), F32), jax.ShapeDtypeStruct((1, D), F32)],
        compiler_params=_cp("arbitrary"),
    )(dy, h, mix, g)


def _loss(h, target, *, name):
    L, D = h.shape
    tr = FRONT

    def body(h_ref, t_ref, acc_ref, dy_ref):
        i = pl.program_id(0)
        e = jnp.where(i >= 1, h_ref[...] - t_ref[...], 0.0)
        dy_ref[...] = e * (1.0 / D)
        part = 0.5 * jnp.sum(jnp.sum(e * e, axis=-1, keepdims=True) * (1.0 / D), axis=0, keepdims=True)

        @pl.when(i == 0)
        def _():
            acc_ref[...] = jnp.zeros_like(acc_ref)

        acc_ref[...] += jnp.broadcast_to(part, acc_ref.shape)

    return pl.pallas_call(
        body, name=name, grid=(L // tr,),
        in_specs=[pl.BlockSpec((tr, D), lambda i: (i, 0)),
                  pl.BlockSpec((tr, D), lambda i: (jnp.maximum(i - 1, 0), 0))],
        out_specs=[pl.BlockSpec((8, LANE), lambda i: (0, 0)), pl.BlockSpec((tr, D), lambda i: (i, 0))],
        out_shape=[jax.ShapeDtypeStruct((8, LANE), F32), jax.ShapeDtypeStruct((L, D), F32)],
        compiler_params=_cp("arbitrary"),
    )(h, target)


def _gla_gate_fwd(P, w2p, b2, gl_blk, *, name):
    L = P.shape[0]
    DK = w2p.shape[1]
    tr = _tile(L, 640, 16)

    def body(gl_ref, w_ref, b_ref, o_ref):
        i = pl.program_id(0)
        z = _dot(gl_ref[...].astype(BF16), w_ref[...], 1, 0) + b_ref[...]
        la = _log_sigmoid(z) * (1.0 / GLA_TAU)
        o_ref[...] = jnp.where(_rows(i, tr) >= N_PAD, la, 0.0)

    return pl.pallas_call(
        body, name=name, grid=(L // tr,),
        in_specs=[pl.BlockSpec((tr, LANE), lambda i: (i, gl_blk)),
                  pl.BlockSpec((LANE, DK), lambda i: (0, 0)), pl.BlockSpec((1, DK), lambda i: (0, 0))],
        out_specs=pl.BlockSpec((tr, DK), lambda i: (i, 0)),
        out_shape=jax.ShapeDtypeStruct((L, DK), F32), compiler_params=_cp("parallel"),
    )(P, w2p, b2)


def _gla_gate_bwd(dla, P, w2p, b2, gl_blk, *, name):
    L = P.shape[0]
    DK = w2p.shape[1]
    tr = _tile(L, 640, 16)

    def body(dla_ref, gl_ref, w_ref, b_ref, dgl_ref, dw_ref, db_ref):
        i = pl.program_id(0)
        glb = gl_ref[...].astype(BF16)
        z = _dot(glb, w_ref[...], 1, 0) + b_ref[...]
        dz = jnp.where(_rows(i, tr) >= N_PAD, dla_ref[...], 0.0) * (1.0 / GLA_TAU) * _sigmoid(-z)
        dzb = dz.astype(BF16)
        dgl_ref[...] = _dot(dzb, w_ref[...], 1, 1).astype(BF16)
        pw = _dot(glb, dzb, 0, 0)
        pb = jnp.sum(dz, axis=0, keepdims=True)

        @pl.when(i == 0)
        def _():
            dw_ref[...] = pw
            db_ref[...] = pb

        @pl.when(i > 0)
        def _():
            dw_ref[...] += pw
            db_ref[...] += pb

    return pl.pallas_call(
        body, name=name, grid=(L // tr,),
        in_specs=[pl.BlockSpec((tr, DK), lambda i: (i, 0)), pl.BlockSpec((tr, LANE), lambda i: (i, gl_blk)),
                  pl.BlockSpec((LANE, DK), lambda i: (0, 0)), pl.BlockSpec((1, DK), lambda i: (0, 0))],
        out_specs=[pl.BlockSpec((tr, LANE), lambda i: (i, 0)), pl.BlockSpec((LANE, DK), lambda i: (0, 0)),
                   pl.BlockSpec((1, DK), lambda i: (0, 0))],
        out_shape=[jax.ShapeDtypeStruct((L, LANE), BF16), jax.ShapeDtypeStruct((LANE, DK), F32),
                   jax.ShapeDtypeStruct((1, DK), F32)],
        compiler_params=_cp("arbitrary"),
    )(dla, P, w2p, b2)


def _chunk_terms(q, k, g, n, scale, HV):
    C = q.shape[0]
    ri = lax.broadcasted_iota(jnp.int32, (C, C), 0)
    ci = lax.broadcasted_iota(jnp.int32, (C, C), 1)
    tri = ri >= ci
    valid = _rows(n, C) >= N_PAD
    km = jnp.where(valid, k, 0.0)
    b = _dot(tri.astype(F32), g, 1, 0, precision=HIGHEST)
    bl_row = jnp.sum(g, axis=0, keepdims=True)
    bl_col = _dot(g, jnp.ones((C, HV), F32), 0, 0, precision=HIGHEST)
    eb = jnp.exp(b)
    enb = jnp.exp(-b)
    qe = q * scale * eb
    ke = km * enb
    ebl_row = jnp.exp(bl_row)
    kl = ke * ebl_row
    return dict(tri=tri, valid=valid, eb=eb, enb=enb, qe=qe, ke=ke, kl=kl, ebl_row=ebl_row,
                ebl_col=jnp.exp(bl_col), ri=ri, ci=ci)


def _gla_chunk_fwd(P, la, H, HK, HV, *, name):
    L = P.shape[0]
    C = GLA_CHUNK
    N = L // C
    scale = HK ** -0.5

    G = GLA_GROUP if H % GLA_GROUP == 0 else 1
    HG = H // G

    def body(q_ref, k_ref, v_ref, g_ref, o_ref, s_ref, S):
        n = pl.program_id(1)

        @pl.when(n == 0)
        def _():
            S[...] = jnp.zeros_like(S)

        for g in range(G):
            ks, vs = slice(g * HK, (g + 1) * HK), slice(g * HV, (g + 1) * HV)
            S0 = S[g]
            s_ref[g, 0] = S0
            t = _chunk_terms(q_ref[:, ks], k_ref[:, ks], g_ref[:, ks], n, scale, HV)
            vb = v_ref[:, vs].astype(BF16)
            qeb = t["qe"].astype(BF16)
            inter = _dot(qeb, S0.astype(BF16), 1, 0)
            att = jnp.where(t["tri"], _dot(qeb, t["ke"].astype(BF16), 1, 1), 0.0)
            o_ref[:, vs] = inter + _dot(att.astype(BF16), vb, 1, 0)
            S[g] = t["ebl_col"] * S0 + _dot(t["kl"].astype(BF16), vb, 0, 0)

    return pl.pallas_call(
        body, name=name, grid=(HG, N),
        in_specs=[pl.BlockSpec((C, G * HK), lambda h, n: (n, h)), pl.BlockSpec((C, G * HK), lambda h, n: (n, HG + h)),
                  pl.BlockSpec((C, G * HV), lambda h, n: (n, HG + h)), pl.BlockSpec((C, G * HK), lambda h, n: (n, h))],
        out_specs=[pl.BlockSpec((C, G * HV), lambda h, n: (n, h)),
                   pl.BlockSpec((G, 1, HK, HV), lambda h, n: (h, n, 0, 0))],
        out_shape=[jax.ShapeDtypeStruct((L, H * HV), F32), jax.ShapeDtypeStruct((H, N, HK, HV), F32)],
        scratch_shapes=[pltpu.VMEM((G, HK, HV), F32)],
        compiler_params=_cp("parallel", "arbitrary"),
    )(P, P, P, la)


def _gla_chunk_bwd(P, la, S_all, do, H, HK, HV, *, name):
    L = P.shape[0]
    C = GLA_CHUNK
    N = L // C
    scale = HK ** -0.5

    G = 1
    HG = H // G

    def body(q_ref, k_ref, v_ref, g_ref, s_ref, do_ref, dq_ref, dk_ref, dv_ref, dg_ref, dS):
        step = pl.program_id(1)
        n = N - 1 - step

        @pl.when(step == 0)
        def _():
            dS[...] = jnp.zeros_like(dS)

        for g in range(G):
            ks, vs = slice(g * HK, (g + 1) * HK), slice(g * HV, (g + 1) * HV)
            dS1 = dS[g]
            S0 = s_ref[g, 0]
            t = _chunk_terms(q_ref[:, ks], k_ref[:, ks], g_ref[:, ks], n, scale, HV)
            tri, qe, ke, kl = t["tri"], t["qe"], t["ke"], t["kl"]
            vb = v_ref[:, vs].astype(BF16)
            dob = do_ref[:, vs].astype(BF16)
            qeb, keb, dSb = qe.astype(BF16), ke.astype(BF16), dS1.astype(BF16)
            dA = jnp.where(tri, _dot(dob, vb, 1, 1), 0.0).astype(BF16)
            A = jnp.where(tri, _dot(qeb, keb, 1, 1), 0.0).astype(BF16)
            dqe = _dot(dob, S0.astype(BF16), 1, 1) + _dot(dA, keb, 1, 0)
            dkl = _dot(vb, dSb, 1, 1)
            dke = _dot(dA, qeb, 0, 0) + dkl * t["ebl_row"]
            dv_ref[:, vs] = (_dot(A, dob, 0, 0) + _dot(kl.astype(BF16), dSb, 1, 0)).astype(BF16)
            debl = (jnp.sum(_dot(jnp.ones((8, HV), F32), dS1 * S0, 1, 1, precision=HIGHEST), axis=0, keepdims=True) * 0.125
                    + jnp.sum(dkl * ke, axis=0, keepdims=True))
            dbl = debl * t["ebl_row"]
            db = dqe * qe - dke * ke + jnp.where(lax.broadcasted_iota(jnp.int32, (C, 1), 0) == C - 1, dbl, 0.0)
            triu = (t["ci"] >= t["ri"]).astype(F32)
            dq_ref[:, ks] = (dqe * t["eb"] * scale).astype(BF16)
            dk_ref[:, ks] = jnp.where(t["valid"], dke * t["enb"], 0.0).astype(BF16)
            dg_ref[:, ks] = _dot(triu, db, 1, 0, precision=HIGHEST)
            dS[g] = t["ebl_col"] * dS1 + _dot(qeb, dob, 0, 0)

    rev = lambda h, s: (N - 1 - s, h)
    return pl.pallas_call(
        body, name=name, grid=(HG, N),
        in_specs=[pl.BlockSpec((C, G * HK), rev), pl.BlockSpec((C, G * HK), lambda h, s: (N - 1 - s, HG + h)),
                  pl.BlockSpec((C, G * HV), lambda h, s: (N - 1 - s, HG + h)), pl.BlockSpec((C, G * HK), rev),
                  pl.BlockSpec((G, 1, HK, HV), lambda h, s: (h, N - 1 - s, 0, 0)), pl.BlockSpec((C, G * HV), rev)],
        out_specs=[pl.BlockSpec((C, G * HK), rev), pl.BlockSpec((C, G * HK), rev),
                   pl.BlockSpec((C, G * HV), rev), pl.BlockSpec((C, G * HK), rev)],
        out_shape=[jax.ShapeDtypeStruct((L, H * HK), BF16), jax.ShapeDtypeStruct((L, H * HK), BF16),
                   jax.ShapeDtypeStruct((L, H * HV), BF16), jax.ShapeDtypeStruct((L, H * HK), F32)],
        scratch_shapes=[pltpu.VMEM((G, HK, HV), F32)],
        compiler_params=_cp("parallel", "arbitrary"),
    )(P, P, P, la, S_all, do)


def _silu_parts(x):
    s = _sigmoid(x)
    return x * s, s * (1.0 + x * (1.0 - s))


def _gla_post_fwd(o, P, ng, H, HV, r_blk, *, name):
    L, DV = o.shape
    tr = _tile(L, 320, 16)

    def body(o_ref, r_ref, g_ref, out_ref):
        for hd in range(H):
            sl = slice(hd * HV, (hd + 1) * HV)
            oh = o_ref[:, sl]
            rr = lax.rsqrt(jnp.mean(oh * oh, axis=-1, keepdims=True) + LN_EPS)
            silu, _ = _silu_parts(r_ref[:, sl])
            out_ref[:, sl] = (oh * rr * g_ref[...] * silu).astype(BF16)

    return pl.pallas_call(
        body, name=name, grid=(L // tr,),
        in_specs=[pl.BlockSpec((tr, DV), lambda i: (i, 0)), pl.BlockSpec((tr, DV), lambda i: (i, r_blk)),
                  pl.BlockSpec((1, HV), lambda i: (0, 0))],
        out_specs=pl.BlockSpec((tr, DV), lambda i: (i, 0)),
        out_shape=jax.ShapeDtypeStruct((L, DV), BF16), compiler_params=_cp("parallel"),
    )(o, P, ng)


def _gla_post_bwd(dgated, o, P, ng, H, HV, r_blk, *, name):
    L, DV = o.shape
    tr = _tile(L, 320, 16)

    def body(d_ref, o_ref, r_ref, g_ref, do_ref, dr_ref, dng_ref):
        i = pl.program_id(0)
        png = jnp.zeros((1, HV), F32)
        for hd in range(H):
            sl = slice(hd * HV, (hd + 1) * HV)
            oh = o_ref[:, sl]
            d = d_ref[:, sl]
            rr = lax.rsqrt(jnp.mean(oh * oh, axis=-1, keepdims=True) + LN_EPS)
            yh = oh * rr
            silu, dsilu = _silu_parts(r_ref[:, sl])
            dn = d * silu
            dr_ref[:, sl] = (d * yh * g_ref[...] * dsilu).astype(BF16)
            png = png + jnp.sum(dn * yh, axis=0, keepdims=True)
            dyh = dn * g_ref[...]
            do_ref[:, sl] = rr * (dyh - yh * jnp.mean(dyh * yh, axis=-1, keepdims=True))

        @pl.when(i == 0)
        def _():
            dng_ref[...] = png

        @pl.when(i > 0)
        def _():
            dng_ref[...] += png

    row = pl.BlockSpec((tr, DV), lambda i: (i, 0))
    return pl.pallas_call(
        body, name=name, grid=(L // tr,),
        in_specs=[row, row, pl.BlockSpec((tr, DV), lambda i: (i, r_blk)), pl.BlockSpec((1, HV), lambda i: (0, 0))],
        out_specs=[row, row, pl.BlockSpec((1, HV), lambda i: (0, 0))],
        out_shape=[jax.ShapeDtypeStruct((L, DV), F32), jax.ShapeDtypeStruct((L, DV), BF16),
                   jax.ShapeDtypeStruct((1, HV), F32)],
        compiler_params=_cp("arbitrary"),
    )(dgated, o, P, ng)


def _shift_down(x, halo, s):
    if s == 0:
        return x
    tr = x.shape[0]
    xx = jnp.concatenate([halo, x], axis=0)
    return pltpu.roll(xx, s, axis=0)[8:8 + tr]


def _shift_up(x, halo, s):
    if s == 0:
        return x
    tr = x.shape[0]
    xx = jnp.concatenate([x, halo], axis=0)
    return pltpu.roll(xx, tr + 8 - s, axis=0)[0:tr]


def _conv_taps(x_ref, halo_ref, i, tr):
    x = jnp.where(_rows(i, tr) >= N_PAD, x_ref[...], 0.0)
    halo = jnp.where(i * tr - 8 + lax.broadcasted_iota(jnp.int32, (8, 1), 0) >= N_PAD, halo_ref[...], 0.0)
    return [_shift_down(x, halo, s) for s in range(3)]


def _conv_apply(taps, w_ref, b_ref):
    return taps[2] * w_ref[0:1, :] + taps[1] * w_ref[1:2, :] + taps[0] * w_ref[2:3, :] + b_ref[...]


def _conv_specs(tr, tc):
    blk = pl.BlockSpec((tr, tc), lambda j, i: (i, j))
    halo = pl.BlockSpec((8, tc), lambda j, i: (jnp.maximum(i * (tr // 8) - 1, 0), j))
    w = pl.BlockSpec((3, tc), lambda j, i: (0, j))
    b = pl.BlockSpec((1, tc), lambda j, i: (0, j))
    return blk, halo, w, b


def _conv_act_fwd(Uu, Ug, wu, wg, bu, bg, *, name):
    L, DFF = Uu.shape
    tr = _tile(L, 320, 16)
    tc = _tile(DFF, 1408, LANE)

    def body(xu_ref, hu_ref, xg_ref, hg_ref, wu_ref, wg_ref, bu_ref, bg_ref, o_ref):
        i = pl.program_id(1)
        u = _conv_apply(_conv_taps(xu_ref, hu_ref, i, tr), wu_ref, bu_ref)
        g = _conv_apply(_conv_taps(xg_ref, hg_ref, i, tr), wg_ref, bg_ref)
        o_ref[...] = (_silu_parts(g)[0] * u).astype(BF16)

    blk, halo, w, b = _conv_specs(tr, tc)
    return pl.pallas_call(
        body, name=name, grid=(DFF // tc, L // tr),
        in_specs=[blk, halo, blk, halo, w, w, b, b], out_specs=blk,
        out_shape=jax.ShapeDtypeStruct((L, DFF), BF16), compiler_params=_cp("parallel", "parallel"),
    )(Uu, Uu, Ug, Ug, wu, wg, bu, bg)


def _conv_act_bwd(Uu, Ug, wu, wg, bu, bg, dA, *, name):
    L, DFF = Uu.shape
    tr = _tile(L, 320, 16)
    tc = _tile(DFF, 512, LANE)

    def body(xu_ref, hu_ref, xg_ref, hg_ref, wu_ref, wg_ref, bu_ref, bg_ref, da_ref,
             du_ref, dg_ref, dwu_ref, dwg_ref, dbu_ref, dbg_ref):
        i = pl.program_id(1)
        tu = _conv_taps(xu_ref, hu_ref, i, tr)
        tg = _conv_taps(xg_ref, hg_ref, i, tr)
        u = _conv_apply(tu, wu_ref, bu_ref)
        g = _conv_apply(tg, wg_ref, bg_ref)
        silu, dsilu = _silu_parts(g)
        da = da_ref[...]
        du = da * silu
        dg = da * u * dsilu
        du_ref[...] = du
        dg_ref[...] = dg

        @pl.when(i == 0)
        def _():
            dwu_ref[...] = jnp.zeros_like(dwu_ref)
            dwg_ref[...] = jnp.zeros_like(dwg_ref)
            dbu_ref[...] = jnp.zeros_like(dbu_ref)
            dbg_ref[...] = jnp.zeros_like(dbg_ref)

        for j in range(3):
            dwu_ref[j:j + 1, :] += jnp.sum(du * tu[2 - j], axis=0, keepdims=True)
            dwg_ref[j:j + 1, :] += jnp.sum(dg * tg[2 - j], axis=0, keepdims=True)
        dbu_ref[...] += jnp.sum(du, axis=0, keepdims=True)
        dbg_ref[...] += jnp.sum(dg, axis=0, keepdims=True)

    blk, halo, w, b = _conv_specs(tr, tc)
    return pl.pallas_call(
        body, name=name, grid=(DFF // tc, L // tr),
        in_specs=[blk, halo, blk, halo, w, w, b, b, blk], out_specs=[blk, blk, w, w, b, b],
        out_shape=[jax.ShapeDtypeStruct((L, DFF), F32), jax.ShapeDtypeStruct((L, DFF), F32),
                   jax.ShapeDtypeStruct((3, DFF), F32), jax.ShapeDtypeStruct((3, DFF), F32),
                   jax.ShapeDtypeStruct((1, DFF), F32), jax.ShapeDtypeStruct((1, DFF), F32)],
        compiler_params=_cp("parallel", "arbitrary"),
    )(Uu, Uu, Ug, Ug, wu, wg, bu, bg, dA)


def _conv_in_bwd(dh, w, *, name):
    L, DFF = dh.shape
    tr = _tile(L, 640, 16)
    tc = _tile(DFF, 1408, LANE)
    nb8 = L // 8

    def body(x_ref, halo_ref, w_ref, o_ref):
        i = pl.program_id(1)
        x = x_ref[...]
        halo = jnp.where((i + 1) * tr + lax.broadcasted_iota(jnp.int32, (8, 1), 0) < L, halo_ref[...], 0.0)
        d = (x * w_ref[2:3, :] + _shift_up(x, halo, 1) * w_ref[1:2, :] + _shift_up(x, halo, 2) * w_ref[0:1, :])
        o_ref[...] = jnp.where(_rows(i, tr) >= N_PAD, d, 0.0).astype(BF16)

    blk = pl.BlockSpec((tr, tc), lambda j, i: (i, j))
    halo = pl.BlockSpec((8, tc), lambda j, i: (jnp.minimum((i + 1) * (tr // 8), nb8 - 1), j))
    return pl.pallas_call(
        body, name=name, grid=(DFF // tc, L // tr),
        in_specs=[blk, halo, pl.BlockSpec((3, tc), lambda j, i: (0, j))], out_specs=blk,
        out_shape=jax.ShapeDtypeStruct((L, DFF), BF16), compiler_params=_cp("parallel", "parallel"),
    )(dh, dh, w)


def _fox_c_fwd(f, bf, *, name):
    L = f.shape[0]
    tr = _tile(L, 320, 16)

    def body(f_ref, b_ref, c_ref, carry):
        i = pl.program_id(0)

        @pl.when(i == 0)
        def _():
            carry[...] = jnp.zeros_like(carry)

        lf = jnp.where(_rows(i, tr) >= N_PAD, _log_sigmoid(f_ref[...] + b_ref[...]), 0.0)
        tri = (lax.broadcasted_iota(jnp.int32, (tr, tr), 0) >= lax.broadcasted_iota(jnp.int32, (tr, tr), 1)).astype(F32)
        c_ref[...] = _dot(tri, lf, 1, 0, precision=HIGHEST) + carry[...]
        carry[...] += jnp.sum(lf, axis=0, keepdims=True)

    return pl.pallas_call(
        body, name=name, grid=(L // tr,),
        in_specs=[pl.BlockSpec((tr, LANE), lambda i: (i, 0)), pl.BlockSpec((1, LANE), lambda i: (0, 0))],
        out_specs=pl.BlockSpec((tr, LANE), lambda i: (i, 0)),
        out_shape=jax.ShapeDtypeStruct((L, LANE), F32), scratch_shapes=[pltpu.VMEM((1, LANE), F32)],
        compiler_params=_cp("arbitrary"),
    )(f, bf)


def _fox_c_bwd(dcs, f, bf, *, name):
    L = f.shape[0]
    tr = _tile(L, 320, 16)
    nb = L // tr
    nd = len(dcs)

    def body(*refs):
        f_ref, b_ref, df_ref, db_ref, carry = refs[nd:]
        s = pl.program_id(0)
        i = nb - 1 - s

        @pl.when(s == 0)
        def _():
            carry[...] = jnp.zeros_like(carry)
            db_ref[...] = jnp.zeros_like(db_ref)

        dc = refs[0][...]
        for r in refs[1:nd]:
            dc = dc + r[...]
        triu = (lax.broadcasted_iota(jnp.int32, (tr, tr), 1) >= lax.broadcasted_iota(jnp.int32, (tr, tr), 0)).astype(F32)
        dlf = _dot(triu, dc, 1, 0, precision=HIGHEST) + carry[...]
        carry[...] += jnp.sum(dc, axis=0, keepdims=True)
        df = jnp.where(_rows(i, tr) >= N_PAD, dlf, 0.0) * _sigmoid(-(f_ref[...] + b_ref[...]))
        df_ref[...] = df.astype(BF16)
        db_ref[...] += jnp.sum(df, axis=0, keepdims=True)

    rev = pl.BlockSpec((tr, LANE), lambda s: (nb - 1 - s, 0))
    vec = pl.BlockSpec((1, LANE), lambda s: (0, 0))
    return pl.pallas_call(
        body, name=name, grid=(nb,), in_specs=[rev] * (nd + 1) + [vec], out_specs=[rev, vec],
        out_shape=[jax.ShapeDtypeStruct((L, LANE), BF16), jax.ShapeDtypeStruct((1, LANE), F32)],
        scratch_shapes=[pltpu.VMEM((1, LANE), F32)], compiler_params=_cp("arbitrary"),
    )(*dcs, f, bf)


AUG = 2 * FOX_HD
FOX_GROUP = 4
FOX_ROW_SPLIT = 5


def _split3(x):
    hi = x.astype(BF16).astype(F32)
    r = x - hi
    mid = r.astype(BF16).astype(F32)
    return hi, mid, (r - mid).astype(BF16).astype(F32)


def _aug_lanes(n, vals):
    lane = lax.broadcasted_iota(jnp.int32, (n, FOX_HD), 1)
    out = jnp.zeros((n, FOX_HD), F32)
    for j, v in enumerate(vals):
        out = jnp.where(lane == j, v, out)
    return out


def _lane_col(x, j):
    lane = lax.broadcasted_iota(jnp.int32, x.shape, 1)
    return jnp.sum(jnp.where(lane == j, x, 0.0), axis=-1, keepdims=True)


def _fox_prep_q(QO, c, H, *, name):
    L = QO.shape[0]
    hd = FOX_HD
    tr = _tile(L, 320, 16)

    def body(q_ref, c_ref, o_ref):
        c = c_ref[...]
        for h in range(H):
            hi, mid, lo = _split3(_lane_col(c, h))
            o_ref[:, h * AUG:h * AUG + hd] = (q_ref[:, h * hd:(h + 1) * hd] * (hd ** -0.5)).astype(BF16)
            o_ref[:, h * AUG + hd:(h + 1) * AUG] = _aug_lanes(tr, [hi, mid, lo, 1.0, 1.0, 1.0]).astype(BF16)

    return pl.pallas_call(
        body, name=name, grid=(L // tr,),
        in_specs=[pl.BlockSpec((tr, H * hd), lambda i: (i, 0)), pl.BlockSpec((tr, LANE), lambda i: (i, 0))],
        out_specs=pl.BlockSpec((tr, H * AUG), lambda i: (i, 0)),
        out_shape=jax.ShapeDtypeStruct((L, H * AUG), BF16), compiler_params=_cp("parallel"),
    )(QO, c)


def _fox_prep_kv(KV, c, H, *, name):
    L = KV.shape[0]
    hd = FOX_HD
    tr = _tile(L, 320, 16)

    def body(k_ref, v_ref, c_ref, ko_ref, vo_ref):
        i = pl.program_id(0)
        c = c_ref[...]
        pad = _rows(i, tr) < N_PAD
        for h in range(H):
            hi, mid, lo = _split3(_lane_col(c, h))
            aug = _aug_lanes(tr, [1.0, 1.0, 1.0, jnp.where(pad, NEG, -hi), jnp.where(pad, 0.0, -mid),
                                  jnp.where(pad, 0.0, -lo)])
            ko_ref[:, h * AUG:h * AUG + hd] = k_ref[:, h * hd:(h + 1) * hd]
            ko_ref[:, h * AUG + hd:(h + 1) * AUG] = aug.astype(BF16)
            vo_ref[:, h * AUG:h * AUG + hd] = v_ref[:, h * hd:(h + 1) * hd]
            vo_ref[:, h * AUG + hd:(h + 1) * AUG] = jnp.ones((tr, hd), BF16)

    wide = pl.BlockSpec((tr, H * AUG), lambda i: (i, 0))
    return pl.pallas_call(
        body, name=name, grid=(L // tr,),
        in_specs=[pl.BlockSpec((tr, H * hd), lambda i: (i, 0)), pl.BlockSpec((tr, H * hd), lambda i: (i, 1)),
                  pl.BlockSpec((tr, LANE), lambda i: (i, 0))],
        out_specs=[wide, wide],
        out_shape=[jax.ShapeDtypeStruct((L, H * AUG), BF16)] * 2, compiler_params=_cp("parallel"),
    )(KV, KV, c)


def _fox_mask(qi, kj, t):
    ti = qi * t + lax.broadcasted_iota(jnp.int32, (t, t), 0)
    si = kj * t + lax.broadcasted_iota(jnp.int32, (t, t), 1)
    return (si <= ti) & ((si >= N_PAD) | (si == ti))


def _fox_attn_fwd(QA, KT, VA, H, *, name):
    L = QA.shape[0]
    hd = FOX_HD
    t = _tile(L, 640, LANE)
    nb = L // t
    G = FOX_GROUP if H % FOX_GROUP == 0 else 1
    nr = FOX_ROW_SPLIT if t % (8 * FOX_ROW_SPLIT) == 0 else 1
    tr = t // nr

    def body(q_ref, k_ref, v_ref, o_ref, lse_ref, m_s, acc):
        qi, kj = pl.program_id(1), pl.program_id(2)

        @pl.when(kj == 0)
        def _():
            m_s[...] = jnp.full_like(m_s, NEG)
            acc[...] = jnp.zeros_like(acc)

        def step(masked):
            for g in range(G):
                cs = slice(g * AUG, (g + 1) * AUG)
                for r in range(nr):
                    rows = slice(r * tr, (r + 1) * tr)
                    s = _dot(q_ref[rows, cs], k_ref[cs, :], 1, 0)
                    if masked:
                        ti = qi * t + r * tr + lax.broadcasted_iota(jnp.int32, (tr, t), 0)
                        si = kj * t + lax.broadcasted_iota(jnp.int32, (tr, t), 1)
                        mask = (si <= ti) & ((si >= N_PAD) | (si == ti))
                        s = jnp.where(mask, s, NEG)
                    m_old = m_s[g, rows]
                    m_new = jnp.maximum(m_old, jnp.max(s, axis=-1, keepdims=True))
                    p = jnp.exp(s - m_new)
                    if masked:
                        p = jnp.where(mask, p, 0.0)
                    acc[g, rows] = jnp.exp(m_old - m_new) * acc[g, rows] + _dot(p.astype(BF16), v_ref[:, cs], 1, 0)
                    m_s[g, rows] = m_new

        @pl.when(kj < qi)
        def _():
            step(False)

        @pl.when(kj == qi)
        def _():
            step(True)

        @pl.when(kj == nb - 1)
        def _():
            for g in range(G):
                a = acc[g]
                l = a[:, hd:]
                o_ref[:, g * hd:(g + 1) * hd] = a[:, :hd] / l
                lse_ref[g] = m_s[g] + jnp.log(jnp.max(l, axis=-1, keepdims=True))

    return pl.pallas_call(
        body, name=name, grid=(H // G, nb, nb),
        in_specs=[pl.BlockSpec((t, G * AUG), lambda h, qi, kj: (qi, h)),
                  pl.BlockSpec((G * AUG, t), lambda h, qi, kj: (h, jnp.minimum(kj, qi))),
                  pl.BlockSpec((t, G * AUG), lambda h, qi, kj: (jnp.minimum(kj, qi), h))],
        out_specs=[pl.BlockSpec((t, G * hd), lambda h, qi, kj: (qi, h)),
                   pl.BlockSpec((G, t, 1), lambda h, qi, kj: (h, qi, 0))],
        out_shape=[jax.ShapeDtypeStruct((L, H * hd), F32), jax.ShapeDtypeStruct((H, L, 1), F32)],
        scratch_shapes=[pltpu.VMEM((G, t, 1), F32), pltpu.VMEM((G, t, AUG), F32)],
        compiler_params=_cp("parallel", "parallel", "arbitrary"),
    )(QA, KT, VA)


def _fox_attn_bwd(QA, KA, VA, DOA, lse, init, H, *, name):
    L = QA.shape[0]
    t = _tile(L, 640, LANE)
    nb = L // t

    def body(q_ref, k_ref, v_ref, do_ref, lse_ref, dk0_ref, dv0_ref, dq_ref, dk_ref, dv_ref):
        kj, qi = pl.program_id(1), pl.program_id(2)

        @pl.when((kj == 0) & (qi == 0))
        def _():
            dq_ref[...] = jnp.zeros_like(dq_ref)

        @pl.when(qi == 0)
        def _():
            dk_ref[...] = dk0_ref[...]
            dv_ref[...] = dv0_ref[...]

        def step(masked):
            q, k, doa = q_ref[...], k_ref[...], do_ref[...]
            p = jnp.exp(_dot(q, k, 1, 1) - lse_ref[0])
            if masked:
                p = jnp.where(_fox_mask(qi, kj, t), p, 0.0)
            pb = p.astype(BF16)
            ds = (p * _dot(doa, v_ref[...], 1, 1)).astype(BF16)
            dv_ref[...] += _dot(pb, doa, 0, 0)
            dk_ref[...] += _dot(ds, q, 0, 0)
            rows = pl.ds(pl.multiple_of(qi * t, t), t)
            dq_ref[rows, :] += _dot(ds, k, 1, 0)

        @pl.when(qi > kj)
        def _():
            step(False)

        @pl.when(qi == kj)
        def _():
            step(True)

    qb = pl.BlockSpec((t, AUG), lambda h, kj, qi: (jnp.maximum(qi, kj), h))
    kb = pl.BlockSpec((t, AUG), lambda h, kj, qi: (kj, h))
    return pl.pallas_call(
        body, name=name, grid=(H, nb, nb),
        in_specs=[qb, kb, kb, qb, pl.BlockSpec((1, t, 1), lambda h, kj, qi: (h, jnp.maximum(qi, kj), 0)), kb, kb],
        out_specs=[pl.BlockSpec((L, AUG), lambda h, kj, qi: (0, h)), kb, kb],
        out_shape=[jax.ShapeDtypeStruct((L, H * AUG), F32)] * 3,
        compiler_params=_cp("parallel", "arbitrary", "arbitrary"),
    )(QA, KA, VA, DOA, lse, *init)


def _fox_post_q(DQA, H, *, name):
    L = DQA.shape[0]
    hd = FOX_HD
    tr = _tile(L, 320, 16)

    def body(x_ref, dq_ref, dc_ref):
        lane = lax.broadcasted_iota(jnp.int32, (tr, LANE), 1)
        dc = jnp.zeros((tr, LANE), F32)
        for h in range(H):
            dq_ref[:, h * hd:(h + 1) * hd] = (x_ref[:, h * AUG:h * AUG + hd] * (hd ** -0.5)).astype(BF16)
            dc = jnp.where(lane == h, _lane_col(x_ref[:, h * AUG + hd:(h + 1) * AUG], 0), dc)
        dc_ref[...] = dc

    return pl.pallas_call(
        body, name=name, grid=(L // tr,), in_specs=[pl.BlockSpec((tr, H * AUG), lambda i: (i, 0))],
        out_specs=[pl.BlockSpec((tr, H * hd), lambda i: (i, 0)), pl.BlockSpec((tr, LANE), lambda i: (i, 0))],
        out_shape=[jax.ShapeDtypeStruct((L, H * hd), BF16), jax.ShapeDtypeStruct((L, LANE), F32)],
        compiler_params=_cp("parallel"),
    )(DQA)


def _fox_post_kv(DKA, DVA, H, *, name):
    L = DKA.shape[0]
    hd = FOX_HD
    tr = _tile(L, 320, 16)

    def body(k_ref, v_ref, o_ref, dc_ref):
        lane = lax.broadcasted_iota(jnp.int32, (tr, LANE), 1)
        dc = jnp.zeros((tr, LANE), F32)
        for h in range(H):
            o_ref[:, h * hd:(h + 1) * hd] = k_ref[:, h * AUG:h * AUG + hd].astype(BF16)
            o_ref[:, (H + h) * hd:(H + h + 1) * hd] = v_ref[:, h * AUG:h * AUG + hd].astype(BF16)
            dc = jnp.where(lane == h, -_lane_col(k_ref[:, h * AUG + hd:(h + 1) * AUG], 3), dc)
        dc_ref[...] = dc

    wide = pl.BlockSpec((tr, H * AUG), lambda i: (i, 0))
    return pl.pallas_call(
        body, name=name, grid=(L // tr,), in_specs=[wide, wide],
        out_specs=[pl.BlockSpec((tr, 2 * H * hd), lambda i: (i, 0)), pl.BlockSpec((tr, LANE), lambda i: (i, 0))],
        out_shape=[jax.ShapeDtypeStruct((L, 2 * H * hd), BF16), jax.ShapeDtypeStruct((L, LANE), F32)],
        compiler_params=_cp("parallel"),
    )(DKA, DVA)


def _fox_gate_fwd(o, QO, *, name):
    L, D = o.shape
    tr = _tile(L, 320, 16)

    def body(o_ref, g_ref, out_ref):
        out_ref[...] = (o_ref[...] * _sigmoid(g_ref[...])).astype(BF16)

    row = pl.BlockSpec((tr, D), lambda i: (i, 0))
    return pl.pallas_call(
        body, name=name, grid=(L // tr,), in_specs=[row, pl.BlockSpec((tr, D), lambda i: (i, 1))], out_specs=row,
        out_shape=jax.ShapeDtypeStruct((L, D), BF16), compiler_params=_cp("parallel"),
    )(o, QO)


def _fox_gate_bwd(d, o, QO, H, *, name):
    L, D = o.shape
    hd = FOX_HD
    tr = _tile(L, 320, 16)

    def body(d_ref, o_ref, g_ref, do_ref, dg_ref):
        for h in range(H):
            sl = slice(h * hd, (h + 1) * hd)
            s = _sigmoid(g_ref[:, sl])
            d = d_ref[:, sl]
            o = o_ref[:, sl]
            do = d * s
            dg_ref[:, sl] = (d * o * s * (1.0 - s)).astype(BF16)
            hi, mid, lo = _split3(-jnp.sum(do * o, axis=-1, keepdims=True))
            do_ref[:, h * AUG:h * AUG + hd] = do.astype(BF16)
            do_ref[:, h * AUG + hd:(h + 1) * AUG] = _aug_lanes(tr, [hi, mid, lo]).astype(BF16)

    row = pl.BlockSpec((tr, D), lambda i: (i, 0))
    return pl.pallas_call(
        body, name=name, grid=(L // tr,), in_specs=[row, row, pl.BlockSpec((tr, D), lambda i: (i, 1))],
        out_specs=[pl.BlockSpec((tr, H * AUG), lambda i: (i, 0)), row],
        out_shape=[jax.ShapeDtypeStruct((L, H * AUG), BF16), jax.ShapeDtypeStruct((L, D), BF16)],
        compiler_params=_cp("parallel"),
    )(d, o, QO)


def _row_tile(R, C, n_arrays):
    budget = VMEM_LIMIT // (3 * n_arrays * 4 * max(C, LANE))
    return _tile(R, max(16, budget // 16 * 16), 16)


def _sum_parts(parts, *, name, out_dtype=F32):
    R, C = parts[0].shape
    tr = _row_tile(R, C, len(parts) + 1)

    def body(*refs):
        acc = refs[0][...].astype(F32)
        for r in refs[1:-1]:
            acc = acc + r[...].astype(F32)
        refs[-1][...] = acc.astype(out_dtype)

    blk = pl.BlockSpec((tr, C), lambda i: (i, 0))
    return pl.pallas_call(
        body, name=name, grid=(R // tr,), in_specs=[blk] * len(parts), out_specs=blk,
        out_shape=jax.ShapeDtypeStruct((R, C), out_dtype), compiler_params=_cp("parallel"),
    )(*parts)


def _sum_slots(x, *, name):
    S, R, C = x.shape
    tr = _row_tile(R, C, S + 1)

    def body(x_ref, o_ref):
        acc = x_ref[0].astype(F32)
        for s in range(1, S):
            acc = acc + x_ref[s].astype(F32)
        o_ref[...] = acc

    return pl.pallas_call(
        body, name=name, grid=(R // tr,), in_specs=[pl.BlockSpec((S, tr, C), lambda i: (0, i, 0))],
        out_specs=pl.BlockSpec((tr, C), lambda i: (i, 0)),
        out_shape=jax.ShapeDtypeStruct((R, C), F32), compiler_params=_cp("parallel"),
    )(x)


def _adamw(w, m, v, gparts, *, name, stacked=False):
    R, C = w.shape
    ng = len(gparts)
    if stacked:
        rows = [g.shape[0] for g in gparts]
        assert sum(rows) == R, (rows, R)
        tr = _row_tile(math.gcd(*rows), C, 7 + ng)
        counts = [r // tr for r in rows]
        starts = [sum(counts[:p]) for p in range(ng)]
    else:
        tr = _row_tile(R, C, 7 + ng)

    def body(*refs):
        w_ref, m_ref, v_ref = refs[:3]
        if stacked:
            g = refs[3 + ng - 1][...]
            for p in range(ng - 2, -1, -1):
                g = jnp.where(pl.program_id(0) < starts[p + 1], refs[3 + p][...], g)
        else:
            g = refs[3][...]
            for r in refs[4:3 + ng]:
                g = g + r[...]
        g_ref, d_ref, nm_ref, nv_ref = refs[3 + ng:]
        nm = ADAM_B1 * m_ref[...] + (1.0 - ADAM_B1) * g
        nv = ADAM_B2 * v_ref[...] + (1.0 - ADAM_B2) * (g * g)
        m_hat = nm / (1.0 - ADAM_B1 ** ADAM_STEP)
        v_hat = nv / (1.0 - ADAM_B2 ** ADAM_STEP)
        g_ref[...] = g
        d_ref[...] = -ADAM_LR * (m_hat / (jnp.sqrt(v_hat) + ADAM_EPS) + ADAM_WD * w_ref[...])
        nm_ref[...] = nm
        nv_ref[...] = nv

    blk = pl.BlockSpec((tr, C), lambda i: (i, 0))
    if stacked:
        g_specs = [pl.BlockSpec((tr, C), lambda i, s=starts[p], n=counts[p]: (jnp.clip(i - s, 0, n - 1), 0))
                   for p in range(ng)]
    else:
        g_specs = [blk] * ng
    return pl.pallas_call(
        body, name=name, grid=(R // tr,), in_specs=[blk] * 3 + g_specs, out_specs=[blk] * 4,
        out_shape=[jax.ShapeDtypeStruct((R, C), F32)] * 4, compiler_params=_cp("parallel"),
    )(w, m, v, *gparts)


def _chip_peers():
    x, y, c = lax.axis_index("x"), lax.axis_index("y"), lax.axis_index("c")
    return (x, y, c), [(1 - x, y), (x, 1 - y), (1 - x, 1 - y)]


def _gather_chips(arrs, *, name):
    n = len(arrs)
    hs = [a.shape[0] // 2 for a in arrs]

    def body(*refs):
        ins, outs = refs[:n], refs[n:2 * n]
        send, recv = refs[2 * n:]
        (x, y, c), chips = _chip_peers()
        me = 2 * x + y

        def landing(a, chip_idx):
            return outs[a].at[chip_idx, pl.ds(c * hs[a], hs[a])]

        ici, passed = [], []
        for a in range(n):
            for j, (px, py) in enumerate(chips):
                cp = pltpu.make_async_remote_copy(
                    src_ref=ins[a].at[pl.ds(c * hs[a], hs[a])], dst_ref=landing(a, me), send_sem=send.at[a, j],
                    recv_sem=recv.at[a, j], device_id=(px, py, c), device_id_type=MESH)
                cp.start()
                ici.append(cp)
        for a in range(n):
            for j, (px, py) in enumerate(chips):
                ici[3 * a + j].wait_recv()
                src = landing(a, 2 * px + py)
                cp = pltpu.make_async_remote_copy(
                    src_ref=src, dst_ref=src, send_sem=send.at[a, 3 + j], recv_sem=recv.at[a, 3 + j],
                    device_id=(x, y, 1 - c), device_id_type=MESH)
                cp.start()
                passed.append(cp)
        for cp in ici:
            cp.wait_send()
        for cp in passed:
            cp.wait()

    chip = 2 * lax.axis_index("x") + lax.axis_index("y")
    outs = pl.pallas_call(
        body, name=name, in_specs=[ANY] * n, out_specs=[ANY] * n,
        out_shape=[jax.ShapeDtypeStruct((4,) + a.shape, a.dtype) for a in arrs],
        scratch_shapes=[pltpu.SemaphoreType.DMA((n, 6)), pltpu.SemaphoreType.DMA((n, 6))],
    )(*arrs)
    return [lax.dynamic_update_index_in_dim(o, a, chip, 0) for o, a in zip(outs, arrs)]


HBM_SPEC = pl.BlockSpec(memory_space=pltpu.HBM)
SEM_SPEC = pl.BlockSpec(memory_space=pltpu.SEMAPHORE)
DATAFLOW = pltpu.SideEffectType.DATAFLOW_SIDE_EFFECTING


def _late_copies(srcs, lands, send, recv, scatter):
    (x, y, c), chips = _chip_peers()
    out = []
    for a in range(len(srcs)):
        for j, (px, py) in enumerate(chips):
            src, dst = (srcs[a].at[2 * px + py], lands[a].at[j]) if scatter else (srcs[a], lands[a].at[2 * x + y])
            out.append(pltpu.make_async_remote_copy(
                src_ref=src, dst_ref=dst, send_sem=send.at[3 * a + j], recv_sem=recv.at[3 * a + j],
                device_id=(px, py, c), device_id_type=MESH))
    return out


def _gather_chips_start(arrs, after, *, name, scatter=False):
    n = len(arrs)
    land_shapes = [((3,) + a.shape[1:]) if scatter else ((4,) + a.shape) for a in arrs]

    def body(*refs):
        srcs, lands = refs[:n], refs[n:2 * n]
        send, recv = refs[2 * n + 1], refs[2 * n + 2]
        for cp in _late_copies(srcs, lands, send, recv, scatter):
            cp.start()
        refs[-1][...] = jnp.zeros_like(refs[-1])

    hbm = lambda a: pltpu.with_memory_space_constraint(a, pltpu.HBM)
    outs = pl.pallas_call(
        body, name=name,
        out_shape=(pltpu.SemaphoreType.DMA((3 * n,)), pltpu.SemaphoreType.DMA((3 * n,)),
                   *[pltpu.HBM(a.shape, a.dtype) for a in arrs],
                   *[pltpu.HBM(s, a.dtype) for s, a in zip(land_shapes, arrs)], jax.ShapeDtypeStruct((8, LANE), F32)),
        in_specs=[HBM_SPEC] * (2 * n) + [ANY],
        out_specs=(SEM_SPEC, SEM_SPEC, *[HBM_SPEC] * (2 * n), pl.BlockSpec(memory_space=pltpu.VMEM)),
        input_output_aliases={i: 2 + i for i in range(2 * n)},
        compiler_params=pltpu.CompilerParams(has_side_effects=DATAFLOW),
    )(*[hbm(a) for a in arrs], *[hbm(lax.empty(s, a.dtype)) for s, a in zip(land_shapes, arrs)], after)
    return outs[0], outs[1], list(outs[2:2 + n]), list(outs[2 + n:2 + 2 * n]), outs[-1], scatter


def _gather_chips_wait(started, after, *, name):
    send, recv, srcs, lands, _, scatter = started
    n = len(srcs)

    def body(*refs):
        for cp in _late_copies(refs[:n], refs[n:2 * n], refs[2 * n], refs[2 * n + 1], scatter):
            cp.wait_send()
            cp.wait_recv()

    outs = pl.pallas_call(
        body, name=name,
        out_shape=tuple(pltpu.HBM(a.shape, a.dtype) for a in srcs + lands),
        in_specs=[HBM_SPEC] * (2 * n) + [SEM_SPEC, SEM_SPEC, ANY], out_specs=tuple([HBM_SPEC] * (2 * n)),
        input_output_aliases={i: i for i in range(2 * n)},
        compiler_params=pltpu.CompilerParams(has_side_effects=DATAFLOW),
    )(*srcs, *lands, send, recv, after)
    return list(outs[n:])


def _scatter_chips(arrs, *, name):
    n = len(arrs)

    def body(*refs):
        ins, outs = refs[:n], refs[n:2 * n]
        send, recv = refs[2 * n:]
        (x, y, c), chips = _chip_peers()
        copies = []
        for a in range(n):
            for j, (px, py) in enumerate(chips):
                cp = pltpu.make_async_remote_copy(
                    src_ref=ins[a].at[2 * px + py], dst_ref=outs[a].at[j], send_sem=send.at[a, j],
                    recv_sem=recv.at[a, j], device_id=(px, py, c), device_id_type=MESH)
                cp.start()
                copies.append(cp)
        for cp in copies:
            cp.wait()

    return pl.pallas_call(
        body, name=name, in_specs=[ANY] * n, out_specs=[ANY] * n,
        out_shape=[jax.ShapeDtypeStruct((3,) + a.shape[1:], a.dtype) for a in arrs],
        scratch_shapes=[pltpu.SemaphoreType.DMA((n, 3)), pltpu.SemaphoreType.DMA((n, 3))],
    )(*arrs)


def _send_other_halves(arrs, *, name):
    n = len(arrs)
    hs = [a.shape[1] // 2 for a in arrs]

    def body(*refs):
        ins, outs = refs[:n], refs[n:2 * n]
        send, recv = refs[2 * n:]
        x, y, c = lax.axis_index("x"), lax.axis_index("y"), lax.axis_index("c")
        copies = []
        for a in range(n):
            cp = pltpu.make_async_remote_copy(
                src_ref=ins[a].at[pl.ds(0, 4), pl.ds((1 - c) * hs[a], hs[a])], dst_ref=outs[a], send_sem=send.at[a],
                recv_sem=recv.at[a], device_id=(x, y, 1 - c), device_id_type=MESH)
            cp.start()
            copies.append(cp)
        for cp in copies:
            cp.wait()

    return pl.pallas_call(
        body, name=name, in_specs=[ANY] * n, out_specs=[ANY] * n,
        out_shape=[jax.ShapeDtypeStruct((4, h) + a.shape[2:], a.dtype) for a, h in zip(arrs, hs)],
        scratch_shapes=[pltpu.SemaphoreType.DMA((n,)), pltpu.SemaphoreType.DMA((n,))],
    )(*arrs)


def _join_halves(arrs, *, name):
    n = len(arrs)

    def body(*refs):
        ins, outs = refs[:n], refs[n:2 * n]
        send, recv = refs[2 * n:]
        x, y, c = lax.axis_index("x"), lax.axis_index("y"), lax.axis_index("c")
        copies = []
        for a in range(n):
            h = ins[a].shape[0]
            cp = pltpu.make_async_remote_copy(
                src_ref=ins[a], dst_ref=outs[a].at[pl.ds(c * h, h)], send_sem=send.at[a], recv_sem=recv.at[a],
                device_id=(x, y, 1 - c), device_id_type=MESH)
            cp.start()
            copies.append(cp)
        for cp in copies:
            cp.wait()

    outs = pl.pallas_call(
        body, name=name, in_specs=[ANY] * n, out_specs=[ANY] * n,
        out_shape=[jax.ShapeDtypeStruct((2 * a.shape[0],) + a.shape[1:], a.dtype) for a in arrs],
        scratch_shapes=[pltpu.SemaphoreType.DMA((n,)), pltpu.SemaphoreType.DMA((n,))],
    )(*arrs)
    core = lax.axis_index("c")
    return [lax.dynamic_update_slice_in_dim(o, a, core * a.shape[0], 0) for o, a in zip(outs, arrs)]


def _gather_all(a, *, name):
    def body(in_ref, out_ref, send, recv):
        x, y, c = lax.axis_index("x"), lax.axis_index("y"), lax.axis_index("c")
        me = 4 * x + 2 * y + c
        copies = []
        for j in range(1, 8):
            fx, fy, fc = (j >> 2) & 1, (j >> 1) & 1, j & 1
            peer = (x ^ fx, y ^ fy, c ^ fc)
            cp = pltpu.make_async_remote_copy(
                src_ref=in_ref, dst_ref=out_ref.at[me], send_sem=send.at[j - 1], recv_sem=recv.at[j - 1],
                device_id=peer, device_id_type=MESH)
            cp.start()
            copies.append(cp)
        for cp in copies:
            cp.wait()

    out = pl.pallas_call(
        body, name=name, in_specs=[ANY], out_specs=ANY,
        out_shape=jax.ShapeDtypeStruct((8,) + a.shape, a.dtype),
        scratch_shapes=[pltpu.SemaphoreType.DMA((7,)), pltpu.SemaphoreType.DMA((7,))],
    )(a)
    me = 4 * lax.axis_index("x") + 2 * lax.axis_index("y") + lax.axis_index("c")
    return lax.dynamic_update_index_in_dim(out, a, me, 0)


def _pack(arrs):
    flat = jnp.concatenate([a.astype(F32).reshape(-1) for a in arrs])
    n = flat.shape[0]
    pad = (-n) % (16 * LANE)
    return jnp.pad(flat, (0, pad)).reshape(-1, LANE)


def _unpack(buf, shapes):
    flat = buf.reshape(-1)
    out, off = [], 0
    for s in shapes:
        n = int(np.prod(s))
        out.append(flat[off:off + n].reshape(s))
        off += n
    return out


def _to_shards(g, axis):
    parts = jnp.split(g, 4, axis=axis)
    return jnp.stack([p.reshape(-1, p.shape[-1]) for p in parts])


def kernel(x, meta, ln_g, ln_b, gla_w_in, gla_w_g2, gla_b_g2, gla_norm_g, gla_w_out, kv_w, kv_bf, fox_w_in, fox_w_out, ffn_w_up, ffn_conv_w, ffn_conv_b, ffn_w_down, loss_target, m_meta, m_ln_g, m_ln_b, m_gla_w_in, m_gla_w_g2, m_gla_b_g2, m_gla_norm_g, m_gla_w_out, m_kv_w, m_kv_bf, m_fox_w_in, m_fox_w_out, m_ffn_w_up, m_ffn_conv_w, m_ffn_conv_b, m_ffn_w_down, v_meta, v_ln_g, v_ln_b, v_gla_w_in, v_gla_w_g2, v_gla_b_g2, v_gla_norm_g, v_gla_w_out, v_kv_w, v_kv_bf, v_fox_w_in, v_fox_w_out, v_ffn_w_up, v_ffn_conv_w, v_ffn_conv_b, v_ffn_w_down):
    D = x.shape[-1]
    L = x.shape[1] + FRONT
    HG = GLA_HEADS
    DK, DV = D // 2, D
    HK, HV = DK // HG, DV // HG
    HF = D // FOX_HD
    DFF = ffn_w_down.shape[1] * 4
    chip = 2 * lax.axis_index("x") + lax.axis_index("y")

    big_names = ["gla_w_in", "gla_w_out", "kv_w", "fox_w_in", "fox_w_out", "ffn_w_up", "ffn_w_down"]
    big = dict(gla_w_in=gla_w_in, gla_w_out=gla_w_out, kv_w=kv_w, fox_w_in=fox_w_in, fox_w_out=fox_w_out,
               ffn_w_up=ffn_w_up, ffn_w_down=ffn_w_down)
    big_axis = dict(gla_w_in=2, gla_w_out=1, kv_w=1, fox_w_in=2, fox_w_out=1, ffn_w_up=2, ffn_w_down=1)
    small_names = ["meta", "ln_g", "ln_b", "gla_w_g2", "gla_b_g2", "gla_norm_g", "ffn_conv_w"]
    small = dict(meta=meta, ln_g=ln_g, ln_b=ln_b, gla_w_g2=gla_w_g2, gla_b_g2=gla_b_g2, gla_norm_g=gla_norm_g,
                 ffn_conv_w=ffn_conv_w)
    small_shapes = [small[k].shape for k in small_names]
    bf = lambda a: a.astype(BF16)
    g0 = _gather_chips([bf(gla_w_in[0]), bf(gla_w_out[0]), _pack([small[k] for k in small_names])], name="gather_weights")
    src0 = [bf(ffn_w_up[0]), bf(ffn_w_down[0])]
    src1 = [bf(gla_w_in[1]), bf(gla_w_out[1]), bf(ffn_w_up[1]), bf(ffn_w_down[1])]
    src2 = [bf(kv_w), bf(fox_w_in), bf(fox_w_out), bf(ffn_w_up[2:]), bf(ffn_w_down[2:])]
    late0 = _gather_chips_start(src0, g0[-1], name="gather_ffn0_start")
    late1 = _gather_chips_start(src1, late0[4], name="gather_layer1_start")
    late2 = _gather_chips_start(src2, late1[4], name="gather_layer23_start")
    sm_sh = [_unpack(g0[-1][s], small_shapes) for s in range(4)]
    fs = {k: jnp.concatenate([sm_sh[s][i] for s in range(4)], axis=-1) for i, k in enumerate(small_names)}

    pad_cols = lambda w: jnp.pad(w, ((0, 0), (0, LANE - w.shape[1])))
    cat = lambda parts, axis: jnp.concatenate(parts, axis=axis)

    def gla_in_matrix(w):
        return cat([w[:, :2 * DK + DV], w[:, 2 * DK + DV + GLA_RANK:], pad_cols(w[:, 2 * DK + DV:2 * DK + DV + GLA_RANK])], 1)

    def ffn_matrices(up, down):
        return dict(u=cat([up(0), up(1)], 1), g=cat([up(2), up(3)], 1), d=cat([down(s) for s in range(4)], 0))

    WL = [None] * DEPTH
    WL[0] = dict(P=gla_in_matrix(cat([g0[0][s] for s in range(4)], 1)), go=cat([g0[1][s] for s in range(4)], 0))
    w2p = [jnp.pad(fs["gla_w_g2"][l], ((0, LANE - GLA_RANK), (0, 0))).astype(BF16) for l in range(N_A_LAYERS)]
    bf_pad = jnp.pad(kv_bf, (0, LANE - HF)).reshape(1, LANE)
    cw_u = [fs["ffn_conv_w"][l][:, :DFF] for l in range(DEPTH)]
    cw_g = [fs["ffn_conv_w"][l][:, DFF:] for l in range(DEPTH)]
    cb_u = [ffn_conv_b[l][None, :DFF] for l in range(DEPTH)]
    cb_g = [ffn_conv_b[l][None, DFF:] for l in range(DEPTH)]
    gl_blk = (2 * DK + 2 * DV) // LANE
    r_blk = (2 * DK + DV) // DV

    h = jnp.concatenate([jnp.concatenate([jnp.zeros((N_PAD, D), F32), fs["meta"]], axis=0), x[0]], axis=0)
    hb = (h + late2[4][0, 0]).astype(BF16)
    saved = []
    kvs = None
    for l in range(DEPTH):
        if l == 1:
            lands = _gather_chips_wait(late1, h, name="gather_layer1_wait")
            Lgi, Lgo, Lup, Ldn = [lax.dynamic_update_index_in_dim(o, a, chip, 0) for o, a in zip(lands, src1)]
            WL[1] = dict(P=gla_in_matrix(cat([Lgi[s] for s in range(4)], 1)), go=cat([Lgo[s] for s in range(4)], 0),
                         **ffn_matrices(lambda s: Lup[s], lambda s: Ldn[s]))
        if l == N_A_LAYERS:
            lands = _gather_chips_wait(late2, h, name="gather_layer23_wait")
            Lkv, Lfi, Lfo, Lup2, Ldn2 = [lax.dynamic_update_index_in_dim(o, a, chip, 0) for o, a in zip(lands, src2)]
            for ll in range(N_A_LAYERS, DEPTH):
                jj = ll - N_A_LAYERS
                WL[ll] = dict(fi=cat([Lfi[s][jj] for s in range(4)], 1), fo=cat([Lfo[s][jj] for s in range(4)], 0),
                              **ffn_matrices(lambda s, jj=jj: Lup2[s][jj], lambda s, jj=jj: Ldn2[s][jj]))
            kv_full = cat([Lkv[s] for s in range(4)], 1)
            W_kv = kv_full[:, :2 * D]
            W_f = pad_cols(kv_full[:, 2 * D:])
            W_kvf = cat([W_kv, W_f], 1)
        s = dict(h=h, hb=hb)
        if l < N_A_LAYERS:
            s["P"] = _mm(hb, WL[l]["P"], name=f"gla_in_{l}")
            s["la"] = _gla_gate_fwd(s["P"], w2p[l], fs["gla_b_g2"][l][None], gl_blk, name=f"gla_gate_{l}")
            s["o"], s["S"] = _gla_chunk_fwd(s["P"], s["la"], HG, HK, HV, name=f"gla_chunk_{l}")
            s["gated"] = _gla_post_fwd(s["o"], s["P"], fs["gla_norm_g"][l][None], HG, HV, r_blk, name=f"gla_post_{l}")
            s["mix"] = _mm(s["gated"], WL[l]["go"], name=f"gla_out_{l}")
        else:
            j = l - N_A_LAYERS
            if kvs is None:
                KV = _mm(hb, W_kv, out_dtype=BF16, name="kv_proj")
                f = _mm(hb, W_f, name="kv_gate_proj")
                c = _fox_c_fwd(f, bf_pad, name="fox_c")
                KA, VA = _fox_prep_kv(KV, c, HF, name="fox_prep_kv")
                kvs = dict(KA=KA, KT=KA.T, VA=VA, f=f, c=c, hb=hb)
            s["QO"] = _mm(hb, WL[l]["fi"], name=f"fox_in_{j}")
            s["QA"] = _fox_prep_q(s["QO"], kvs["c"], HF, name=f"fox_prep_q_{j}")
            s["o"], s["lse"] = _fox_attn_fwd(s["QA"], kvs["KT"], kvs["VA"], HF, name=f"fox_attn_{j}")
            s["gated"] = _fox_gate_fwd(s["o"], s["QO"], name=f"fox_gate_{j}")
            s["mix"] = _mm(s["gated"], WL[l]["fo"], name=f"fox_out_{j}")
        if l == 0:
            lands = _gather_chips_wait(late0, s["mix"], name="gather_ffn0_wait")
            Lup0, Ldn0 = [lax.dynamic_update_index_in_dim(o, a, chip, 0) for o, a in zip(lands, src0)]
            WL[0].update(ffn_matrices(lambda s_: Lup0[s_], lambda s_: Ldn0[s_]))
        s["h1"], s["h1b"] = _ln_fwd(h, s["mix"], fs["ln_g"][l, 0][None], fs["ln_b"][l, 0][None], name=f"ln_a_{l}")
        s["Uu"] = _mm(s["h1b"], WL[l]["u"], name=f"ffn_up_u_{l}")
        s["Ug"] = _mm(s["h1b"], WL[l]["g"], name=f"ffn_up_g_{l}")
        s["a"] = _conv_act_fwd(s["Uu"], s["Ug"], cw_u[l], cw_g[l], cb_u[l], cb_g[l], name=f"ffn_conv_{l}")
        s["ffn"] = _mm(s["a"], WL[l]["d"], name=f"ffn_down_{l}")
        h, hb = _ln_fwd(s["h1"], s["ffn"], fs["ln_g"][l, 1][None], fs["ln_b"][l, 1][None], name=f"ln_b_{l}")
        saved.append(s)

    loss_acc, dh = _loss(h, loss_target[0], name="loss")

    gW = {}
    d_ln_g = [[None, None] for _ in range(DEPTH)]
    d_ln_b = [[None, None] for _ in range(DEPTH)]
    d_cw, d_cb = [None] * DEPTH, [None] * DEPTH
    d_wg2, d_bg2, d_ng = [None] * N_A_LAYERS, [None] * N_A_LAYERS, [None] * N_A_LAYERS
    dkv = (jnp.zeros((L, HF * AUG), F32), jnp.zeros((L, HF * AUG), F32))
    dcqs = []
    core = lax.axis_index("c")
    stack = lambda k, lo, hi: jnp.stack([gW[(k, i)] for i in range(lo, hi)])

    def pair_sums(entries, tag):
        g4 = [_to_shards(g, ax) for _, g, ax in entries]
        sib = _send_other_halves(g4, name=f"pair_exchange_{tag}")
        out = []
        for (key, _, _), a, b in zip(entries, g4, sib):
            _, R, C = a.shape
            mine = lax.dynamic_slice_in_dim(a, core * (R // 2), R // 2, axis=1)
            out.append(_sum_parts([mine.reshape(2 * R, C), b.reshape(2 * R, C)], out_dtype=BF16,
                                  name=f"sum_pair_{key}").reshape(4, R // 2, C))
        return out

    def chip_sums(entries, pair, recv):
        return [_sum_parts([lax.dynamic_index_in_dim(p, chip, axis=0, keepdims=False), r[0], r[1], r[2]],
                           name=f"sum_chips_{key}") for (key, _, _), p, r in zip(entries, pair, recv)]

    def layer_entries(l):
        return [(f"gla_w_in_{l}", gW[("gla_w_in", l)][None], 2), (f"gla_w_out_{l}", gW[("gla_w_out", l)][None], 1),
                (f"ffn_w_up_{l}", gW[("ffn_w_up", l)][None], 2), (f"ffn_w_down_{l}", gW[("ffn_w_down", l)][None], 1)]

    ln_tok = 0.0
    for l in reversed(range(DEPTH)):
        s = saved[l]
        dz, dzb, d_ln_g[l][1], d_ln_b[l][1] = _ln_bwd(dh, s["h1"], s["ffn"], fs["ln_g"][l, 1][None] + ln_tok,
                                                      name=f"ln_b_bwd_{l}")
        dA = _mm(dzb, WL[l]["d"], tb=True, name=f"ffn_down_dx_{l}")
        gW[("ffn_w_down", l)] = _mm(s["a"], dzb, ta=True, out_dtype=BF16, name=f"ffn_down_dw_{l}")
        dcu, dcg, dwu, dwg, dbu, dbg = _conv_act_bwd(s["Uu"], s["Ug"], cw_u[l], cw_g[l], cb_u[l], cb_g[l], dA,
                                                     name=f"ffn_conv_bwd_{l}")
        d_cw[l] = jnp.concatenate([dwu, dwg], axis=1)
        d_cb[l] = jnp.concatenate([dbu, dbg], axis=1)[0]
        dUu = _conv_in_bwd(dcu, cw_u[l], name=f"ffn_conv_dx_u_{l}")
        dUg = _conv_in_bwd(dcg, cw_g[l], name=f"ffn_conv_dx_g_{l}")
        gW[("ffn_w_up", l)] = jnp.concatenate(
            [_mm(s["h1b"], dUu, ta=True, out_dtype=BF16, name=f"ffn_up_dw_u_{l}"),
             _mm(s["h1b"], dUg, ta=True, out_dtype=BF16, name=f"ffn_up_dw_g_{l}")], axis=1)
        dh1 = _mm(dUu, WL[l]["u"], tb=True, add=dz, add_scale=ALPHA, name=f"ffn_up_dx_u_{l}")
        dh1 = _mm(dUg, WL[l]["g"], tb=True, add=dh1, name=f"ffn_up_dx_g_{l}")
        dz, dzb, d_ln_g[l][0], d_ln_b[l][0] = _ln_bwd(dh1, s["h"], s["mix"], fs["ln_g"][l, 0][None], name=f"ln_a_bwd_{l}")
        if l < N_A_LAYERS:
            dgated = _mm(dzb, WL[l]["go"], tb=True, name=f"gla_out_dx_{l}")
            gW[("gla_w_out", l)] = _mm(s["gated"], dzb, ta=True, out_dtype=BF16, name=f"gla_out_dw_{l}")
            do, drb, d_ng[l] = _gla_post_bwd(dgated, s["o"], s["P"], fs["gla_norm_g"][l][None], HG, HV, r_blk,
                                             name=f"gla_post_bwd_{l}")
            dq, dk, dvb, dla = _gla_chunk_bwd(s["P"], s["la"], s["S"], do, HG, HK, HV, name=f"gla_chunk_bwd_{l}")
            dglb, dw2, d_bg2[l] = _gla_gate_bwd(dla, s["P"], w2p[l], fs["gla_b_g2"][l][None], gl_blk,
                                               name=f"gla_gate_bwd_{l}")
            d_wg2[l] = dw2[:GLA_RANK]
            dP = jnp.concatenate([dq, dk, dvb, drb, dglb], axis=1)
            gP = _mm(s["hb"], dP, ta=True, out_dtype=BF16, name=f"gla_in_dw_{l}")
            gW[("gla_w_in", l)] = jnp.concatenate(
                [gP[:, :2 * DK + DV], gP[:, 2 * DK + 2 * DV:2 * DK + 2 * DV + GLA_RANK], gP[:, 2 * DK + DV:2 * DK + 2 * DV]],
                axis=1)
            dh = _mm(dP, WL[l]["P"], tb=True, add=dz, add_scale=ALPHA, name=f"gla_in_dx_{l}")
            if l == 1:
                mid_entries = layer_entries(1)
                pair_mid = pair_sums(mid_entries, "mid")
                sc_mid = _gather_chips_start(pair_mid, dh, name="scatter_mid_start", scatter=True)
                ln_tok = sc_mid[4][0, 0]
        else:
            j = l - N_A_LAYERS
            dgo = _mm(dzb, WL[l]["fo"], tb=True, name=f"fox_out_dx_{j}")
            gW[("fox_w_out", j)] = _mm(s["gated"], dzb, ta=True, out_dtype=BF16, name=f"fox_out_dw_{j}")
            DOA, dogb = _fox_gate_bwd(dgo, s["o"], s["QO"], HF, name=f"fox_gate_bwd_{j}")
            DQA, DKA, DVA = _fox_attn_bwd(s["QA"], kvs["KA"], kvs["VA"], DOA, s["lse"], dkv, HF, name=f"fox_attn_bwd_{j}")
            dkv = (DKA, DVA)
            dqb, dcq = _fox_post_q(DQA, HF, name=f"fox_post_q_{j}")
            dcqs.append(dcq)
            dQO = jnp.concatenate([dqb, dogb], axis=1)
            gW[("fox_w_in", j)] = _mm(s["hb"], dQO, ta=True, out_dtype=BF16, name=f"fox_in_dw_{j}")
            dh = _mm(dQO, WL[l]["fi"], tb=True, add=dz, add_scale=ALPHA, name=f"fox_in_dx_{j}")
            if j == 0:
                dkvb, dck = _fox_post_kv(DKA, DVA, HF, name="fox_post_kv")
                dfb, d_bf = _fox_c_bwd(dcqs + [dck], kvs["f"], bf_pad, name="fox_c_bwd")
                dKVF = jnp.concatenate([dkvb, dfb], axis=1)
                gkv = _mm(kvs["hb"], dKVF, ta=True, out_dtype=BF16, name="kv_dw")
                gW[("kv_w", 0)] = gkv[:, :2 * D + HF]
                dh = _mm(dKVF, W_kvf, tb=True, add=dh, name="kv_dx")
                late_entries = [("fox_w_in", stack("fox_w_in", 0, 2), 2), ("fox_w_out", stack("fox_w_out", 0, 2), 1),
                                ("kv_w", gW[("kv_w", 0)], 1), ("ffn_w_up_hi", stack("ffn_w_up", N_A_LAYERS, DEPTH), 2),
                                ("ffn_w_down_hi", stack("ffn_w_down", N_A_LAYERS, DEPTH), 1)]
                pair_late = pair_sums(late_entries, "late")
                sc_late = _gather_chips_start(pair_late, dh, name="scatter_late_start", scatter=True)
                ln_tok = sc_late[4][0, 0]

    early_entries = layer_entries(0)
    pair_early = pair_sums(early_entries, "early")
    sc_early = _gather_chips_start(pair_early, dh, name="scatter_early_start", scatter=True)
    recv_late = _gather_chips_wait(sc_late, sc_early[4], name="scatter_late_wait")
    joined_late = _join_halves(chip_sums(late_entries, pair_late, recv_late), name="join_halves_late")
    gsum = {key: g for (key, _, _), g in zip(late_entries, joined_late)}
    recv_mid = _gather_chips_wait(sc_mid, joined_late[0], name="scatter_mid_wait")
    joined_mid = _join_halves(chip_sums(mid_entries, pair_mid, recv_mid), name="join_halves_mid")
    gsum.update({key: g for (key, _, _), g in zip(mid_entries, joined_mid)})
    moments = dict(gla_w_in=(m_gla_w_in, v_gla_w_in), gla_w_out=(m_gla_w_out, v_gla_w_out), kv_w=(m_kv_w, v_kv_w),
                   fox_w_in=(m_fox_w_in, v_fox_w_in), fox_w_out=(m_fox_w_out, v_fox_w_out),
                   ffn_w_up=(m_ffn_w_up, v_ffn_w_up), ffn_w_down=(m_ffn_w_down, v_ffn_w_down))
    res = {}

    def apply_adamw(k):
        w = big[k]
        sh = w.shape
        flat = lambda a: a.reshape(-1, sh[-1])
        parts = [gsum[f"{k}_{p}"] for p in ("0", "1", "hi") if f"{k}_{p}" in gsum] or [gsum[k]]
        outs = _adamw(flat(w), flat(moments[k][0]), flat(moments[k][1]), parts, stacked=len(parts) > 1, name=f"adamw_{k}")
        res[k] = [o.reshape(sh) for o in outs]

    for k in ("fox_w_in", "fox_w_out", "kv_w"):
        apply_adamw(k)

    dmeta = dh[N_PAD:FRONT]
    sg = dict(meta=dmeta,
              ln_g=jnp.stack([jnp.concatenate(d_ln_g[l], axis=0) for l in range(DEPTH)]),
              ln_b=jnp.stack([jnp.concatenate(d_ln_b[l], axis=0) for l in range(DEPTH)]),
              gla_w_g2=jnp.stack(d_wg2), gla_b_g2=jnp.stack([d[0] for d in d_bg2]),
              gla_norm_g=jnp.stack([d[0] for d in d_ng]), ffn_conv_w=jnp.stack(d_cw),
              kv_bf=d_bf[0, :HF], ffn_conv_b=jnp.stack(d_cb), loss=loss_acc[0, :1])
    sg_names = small_names + ["kv_bf", "ffn_conv_b", "loss"]
    sg_shapes = [sg[k].shape for k in sg_names]
    red = _sum_slots(_gather_all(_pack([sg[k] for k in sg_names]), name="gather_small_grads"), name="sum_small_grads")
    red = dict(zip(sg_names, _unpack(red, sg_shapes)))
    loss = red["loss"][0]
    loc = {}
    for k in small_names:
        wdt = small[k].shape[-1]
        loc[k] = lax.dynamic_slice_in_dim(red[k], chip * wdt, wdt, axis=red[k].ndim - 1)
    loc["kv_bf"] = red["kv_bf"]
    loc["ffn_conv_b"] = red["ffn_conv_b"]
    sm_all = small_names + ["kv_bf", "ffn_conv_b"]
    sw = dict(small, kv_bf=kv_bf, ffn_conv_b=ffn_conv_b)
    sm_m = dict(meta=m_meta, ln_g=m_ln_g, ln_b=m_ln_b, gla_w_g2=m_gla_w_g2, gla_b_g2=m_gla_b_g2, gla_norm_g=m_gla_norm_g,
                ffn_conv_w=m_ffn_conv_w, kv_bf=m_kv_bf, ffn_conv_b=m_ffn_conv_b)
    sm_v = dict(meta=v_meta, ln_g=v_ln_g, ln_b=v_ln_b, gla_w_g2=v_gla_w_g2, gla_b_g2=v_gla_b_g2, gla_norm_g=v_gla_norm_g,
                ffn_conv_w=v_ffn_conv_w, kv_bf=v_kv_bf, ffn_conv_b=v_ffn_conv_b)
    shapes_loc = [sw[k].shape for k in sm_all]
    outs = _adamw(_pack([sw[k] for k in sm_all]), _pack([sm_m[k] for k in sm_all]), _pack([sm_v[k] for k in sm_all]),
                  [_pack([loc[k] for k in sm_all])], name="adamw_small")
    outs = [_unpack(o, shapes_loc) for o in outs]
    for i, k in enumerate(sm_all):
        res[k] = [outs[q][i] for q in range(4)]

    behind = jnp.zeros((8, LANE), F32) + res["kv_w"][0][:1, :1] + res["kv_bf"][0][:1]
    recv_early = _gather_chips_wait(sc_early, behind, name="scatter_early_wait")
    joined_early = _join_halves(chip_sums(early_entries, pair_early, recv_early), name="join_halves_early")
    gsum.update({key: g for (key, _, _), g in zip(early_entries, joined_early)})
    for k in ("gla_w_in", "gla_w_out", "ffn_w_up", "ffn_w_down"):
        apply_adamw(k)

    order = ["meta", "ln_g", "ln_b", "gla_w_in", "gla_w_g2", "gla_b_g2", "gla_norm_g", "gla_w_out", "kv_w", "kv_bf",
             "fox_w_in", "fox_w_out", "ffn_w_up", "ffn_conv_w", "ffn_conv_b", "ffn_w_down"]
    grad_x = dh[FRONT:][None]
    return (loss, grad_x, *[res[k][0] for k in order], *[res[k][1] for k in order], *[res[k][2] for k in order],
            *[res[k][3] for k in order])
```
